```python
import jax
import jax.numpy as jnp
from jax import lax
import numpy as np

D_MODEL = 1024
BATCH = 8
SEQ = 4096
DEPTH = 2
DEC_BATCH = 32
DEC_SEQ = 4
PAST_LEN = 16384
PAGE_SIZE = 128

HEAD_DIM = 64
NSA_W = D_MODEL // 2
GM_W = D_MODEL // 4
POOL_W = D_MODEL // 4
MIX_W = NSA_W + GM_W + POOL_W
N_HEADS = NSA_W // HEAD_DIM
N_KV_HEADS = 2
GROUP = N_HEADS // N_KV_HEADS
CMP_STRIDE = 16
CMP_LEN = 2 * CMP_STRIDE
CMP_HID = HEAD_DIM
SLC_BLOCK = 64
SLC_TOPK = 16
WINDOW = 512
WIN_QB = 128
SLC_QC = 32
FORCE_SCORE = 1.0e4
ROPE_THETA = 10000.0
SCALE = HEAD_DIM ** -0.5
GM_HEADS = GM_W // HEAD_DIM
GM_CHUNK = 128
POOL_GROUPS = 4
POOL_GW = POOL_W // POOL_GROUPS
POOL_WINDOWS = (2, 4, 8, 16)
POOL_HIST = max(POOL_WINDOWS) - 1
D_FF = 256 * ((8 * D_MODEL // 3 + 255) // 256)
ALPHA = (2 * DEPTH) ** 0.25
BETA = (8 * DEPTH) ** -0.25
LN_EPS = 1e-5
Q_W = N_HEADS * HEAD_DIM
KV_W = 2 * N_KV_HEADS * HEAD_DIM
GATE_W = 3 * N_HEADS
SPLITS = [Q_W, Q_W + KV_W, Q_W + 2 * KV_W, Q_W + 3 * KV_W, Q_W + 3 * KV_W + GATE_W, Q_W + 3 * KV_W + GATE_W + 2 * GM_W]
N_IN = SPLITS[-1] + POOL_W

kernel_name = 'nsa_gmlp_pool_hybrid_step'


def _ln(x, g, b):
    xf = x.astype(jnp.float32)
    mu = jnp.mean(xf, -1, keepdims=True)
    var = jnp.mean(jnp.square(xf - mu), -1, keepdims=True)
    return ((xf - mu) * lax.rsqrt(var + LN_EPS) * g + b).astype(x.dtype)


def _swiglu(x, w_in, w_out):
    gate, up = jnp.split(x @ w_in, 2, axis=-1)
    return (jax.nn.silu(gate) * up) @ w_out


def _rope(x, pos):
    half = HEAD_DIM // 2
    inv = ROPE_THETA ** (-jnp.arange(half, dtype=jnp.float32) / half)
    ang = pos.astype(jnp.float32)[:, None] * inv[None, :]
    cos = jnp.cos(ang)[None, :, None, :]
    sin = jnp.sin(ang)[None, :, None, :]
    xf = x.astype(jnp.float32)
    x1, x2 = xf[..., :half], xf[..., half:]
    return jnp.concatenate([x1 * cos - x2 * sin, x1 * sin + x2 * cos], -1).astype(x.dtype)


def _masked_softmax(s, mask):
    s = jnp.where(mask, s.astype(jnp.float32), -1e30)
    return jnp.where(mask, jax.nn.softmax(s, axis=-1), 0.0)


def _attend(q, k, v, mask):
    s = jnp.einsum('btkgd,bskd->bkgts', q, k) * SCALE
    p = _masked_softmax(s, mask)
    return jnp.einsum('bkgts,bskd->btkgd', p.astype(v.dtype), v)


def _win_mask(qpos, kpos):
    d = qpos[:, None] - kpos[None, :]
    return (d >= 0) & (d <= WINDOW) & (kpos[None, :] >= 0)


def _pad_seq(x, mult):
    pad = (-x.shape[1]) % mult
    return jnp.pad(x, [(0, 0), (0, pad)] + [(0, 0)] * (x.ndim - 2))


def _compress(x, pe, w1, w2):
    B, L, K, D = x.shape
    sub = x.reshape(B, L // CMP_STRIDE, CMP_STRIDE, K, D)
    first = jnp.einsum('bnjkd,jde->bnke', sub, w1[:CMP_STRIDE])
    second = jnp.einsum('bnjkd,jde->bnke', sub, w1[CMP_STRIDE:])
    pe_term = jnp.einsum('jd,jde->e', pe, w1)
    hid = jax.nn.gelu(first[:, :-1] + second[:, 1:] + pe_term)
    return jnp.einsum('bnke,ed->bnkd', hid, w2)


def _sel_map(nc, ns):
    c0 = CMP_STRIDE * np.arange(nc)[:, None]
    s0 = SLC_BLOCK * np.arange(ns)[None, :]
    ov = np.clip(np.minimum(c0 + CMP_LEN, s0 + SLC_BLOCK) - np.maximum(c0, s0), 0, None) / CMP_LEN
    return jnp.asarray(ov, dtype=jnp.float32)


def _nsa_keys(k_c, v_c, k_s, v_s, pe, w1, w2):
    B, L = k_c.shape[:2]
    kc = _compress(k_c, pe[0], w1[0], w2[0])
    vc = _compress(v_c, pe[1], w1[1], w2[1])
    nc = kc.shape[1]
    cmp_end = CMP_STRIDE * jnp.arange(nc) + (CMP_LEN - 1)
    ns = L // SLC_BLOCK
    ksb = k_s.reshape(B, ns, SLC_BLOCK, N_KV_HEADS, HEAD_DIM).transpose(0, 3, 1, 2, 4)
    vsb = v_s.reshape(B, ns, SLC_BLOCK, N_KV_HEADS, HEAD_DIM).transpose(0, 3, 1, 2, 4)
    return (kc, vc, cmp_end, ksb, vsb, _sel_map(nc, ns))


def _cmp_slc_attend(q, qpos, kc, vc, cmp_end, ksb, vsb, sel_map):
    B, Tq = q.shape[:2]
    s = jnp.einsum('btkgd,bckd->bkgtc', q, kc) * SCALE
    p = _masked_softmax(s, cmp_end[None, :] <= qpos[:, None])
    o_c = jnp.einsum('bkgtc,bckd->btkgd', p.astype(vc.dtype), vc)
    imp = jnp.einsum('bkgtc,cs->bkts', p, sel_map)
    ns = ksb.shape[2]
    blk = jnp.arange(ns)[None, :]
    cur = (qpos // SLC_BLOCK)[:, None]
    forced = (blk == 0) | (blk == cur) | (blk == cur - 1)
    score = jnp.where(forced, FORCE_SCORE, jnp.where(blk <= cur, imp, -1.0))
    n_sel = min(SLC_TOPK, ns)
    _, idx = lax.top_k(score, n_sel)
    bi = jnp.arange(B)[:, None, None, None]
    ki = jnp.arange(N_KV_HEADS)[None, :, None, None]
    kg = ksb[bi, ki, idx].reshape(B, N_KV_HEADS, Tq, n_sel * SLC_BLOCK, HEAD_DIM)
    vg = vsb[bi, ki, idx].reshape(B, N_KV_HEADS, Tq, n_sel * SLC_BLOCK, HEAD_DIM)
    kpos = (idx[..., None] * SLC_BLOCK + jnp.arange(SLC_BLOCK)).reshape(B, N_KV_HEADS, Tq, n_sel * SLC_BLOCK)
    mask = kpos <= qpos[None, None, :, None]
    s2 = jnp.einsum('btkgd,bktnd->bkgtn', q, kg) * SCALE
    p2 = _masked_softmax(s2, mask[:, :, None])
    o_s = jnp.einsum('bkgtn,bktnd->btkgd', p2.astype(vg.dtype), vg)
    return o_c, o_s


def _cmp_slc_prompt(q, keys):
    B, T = q.shape[:2]
    nq = T // SLC_QC
    qc = jnp.moveaxis(q.reshape(B, nq, SLC_QC, N_KV_HEADS, GROUP, HEAD_DIM), 1, 0)
    qpos = jnp.arange(T).reshape(nq, SLC_QC)
    o_c, o_s = lax.map(lambda a: _cmp_slc_attend(a[0], a[1], *keys), (qc, qpos))
    back = lambda o: jnp.moveaxis(o, 0, 1).reshape(B, T, N_KV_HEADS, GROUP, HEAD_DIM)
    return back(o_c), back(o_s)


def _window_prompt(q, k, v):
    B, T = q.shape[:2]
    nb = T // WIN_QB
    span = WINDOW + WIN_QB
    padw = [(0, 0), (WINDOW, 0), (0, 0), (0, 0)]
    kp = jnp.pad(k, padw)
    vp = jnp.pad(v, padw)
    qb = jnp.moveaxis(q.reshape(B, nb, WIN_QB, N_KV_HEADS, GROUP, HEAD_DIM), 1, 0)

    def block(args):
        qi, i = args
        start = i * WIN_QB
        ks = lax.dynamic_slice_in_dim(kp, start, span, axis=1)
        vs = lax.dynamic_slice_in_dim(vp, start, span, axis=1)
        qpos = start + jnp.arange(WIN_QB)
        kpos = start - WINDOW + jnp.arange(span)
        return _attend(qi, ks, vs, _win_mask(qpos, kpos))

    o = lax.map(block, (qb, jnp.arange(nb)))
    return jnp.moveaxis(o, 0, 1).reshape(B, T, N_KV_HEADS, GROUP, HEAD_DIM)


def _gmlp(u, v, ws, bias):
    B, n = v.shape[:2]
    c = min(n, GM_CHUNK)
    vc = v.reshape(B, n // c, c, GM_HEADS, HEAD_DIM)
    wm = jnp.tril(ws[:, :c, :c])
    s = jnp.einsum('hts,bnshd->bnthd', wm, vc) + bias[:, :c].T[None, None, :, :, None]
    return u * s.reshape(B, n, GM_W)


def _pool(z_ext, pos0, pw, ps):
    n = z_ext.shape[1] - POOL_HIST
    zf = z_ext.astype(jnp.float32)
    cs = jnp.concatenate([jnp.zeros_like(zf[:, :1]), jnp.cumsum(zf, axis=1)], axis=1)
    pos = pos0 + jnp.arange(n)
    cur = zf[:, POOL_HIST:]
    outs = []
    for g, w in enumerate(POOL_WINDOWS):
        sl = slice(g * POOL_GW, (g + 1) * POOL_GW)
        wsum = cs[:, POOL_HIST + 1:POOL_HIST + 1 + n, sl] - cs[:, POOL_HIST + 1 - w:POOL_HIST + 1 - w + n, sl]
        cnt = jnp.minimum(w, pos + 1).astype(jnp.float32)[None, :, None]
        d = (wsum / cnt - cur[..., sl]).astype(z_ext.dtype)
        outs.append(jnp.einsum('btc,ce->bte', d, pw[g]))
    return jnp.concatenate(outs, -1) * ps


def _project(h, w_in, pos):
    B, T = h.shape[:2]
    q, kvc, kvs, kvw, gt, uv, p = jnp.split(h @ w_in, SPLITS, axis=-1)
    q = _rope(q.reshape(B, T, N_HEADS, HEAD_DIM), pos).reshape(B, T, N_KV_HEADS, GROUP, HEAD_DIM)

    def kv(t):
        t = t.reshape(B, T, 2, N_KV_HEADS, HEAD_DIM)
        return _rope(t[:, :, 0], pos), t[:, :, 1]

    gates = jax.nn.sigmoid(gt).reshape(B, T, N_HEADS, 3)
    gu, gv = jnp.split(jax.nn.gelu(uv), 2, axis=-1)
    return q, kv(kvc), kv(kvs), kv(kvw), gates, gu, gv, p


def _gate_merge(gates, o_c, o_s, o_w):
    B, T = gates.shape[:2]
    f = lambda o: o.reshape(B, T, N_HEADS, HEAD_DIM)
    o = gates[..., 0:1] * f(o_c) + gates[..., 1:2] * f(o_s) + gates[..., 2:3] * f(o_w)
    return o.reshape(B, T, NSA_W)


def _mixer_prompt(h, P):
    w_in, w_o, pe, w1, w2, gmg, gmb, ws, gb, pw, ps = P
    B, T = h.shape[:2]
    pos = jnp.arange(T)
    q, (kc_, vc_), (ks_, vs_), (kw_, vw_), gates, gu, gv, p = _project(h, w_in, pos)
    gv = _ln(gv, gmg, gmb)
    keys = _nsa_keys(_pad_seq(kc_, SLC_BLOCK), _pad_seq(vc_, SLC_BLOCK), _pad_seq(ks_, SLC_BLOCK), _pad_seq(vs_, SLC_BLOCK), pe, w1, w2)
    o_c, o_s = _cmp_slc_prompt(q, keys)
    o_w = _window_prompt(q, kw_, vw_)
    o_nsa = _gate_merge(gates, o_c, o_s, o_w)
    o_gm = _gmlp(gu, gv, ws, gb)
    z_ext = jnp.concatenate([jnp.zeros((B, POOL_HIST, POOL_W), p.dtype), p], axis=1)
    o_pool = _pool(z_ext, 0, pw, ps)
    y = jnp.concatenate([o_nsa, o_gm, o_pool], -1) @ w_o
    wb = min(WINDOW, T)
    new = (jnp.stack([kc_, vc_], 2), jnp.stack([ks_, vs_], 2),
           jnp.stack([kw_[:, T - wb:], vw_[:, T - wb:]], 2), p[:, T - POOL_HIST:])
    return y, new


def _mixer_sample(h, pool_c, pool_s, win_buf, pool_buf, page_table, P):
    w_in, w_o, pe, w1, w2, gmg, gmb, ws, gb, pw, ps = P
    B, T = h.shape[:2]
    pos = PAST_LEN + jnp.arange(T)
    q, (kc_, vc_), (ks_, vs_), (kw_, vw_), gates, gu, gv, p = _project(h, w_in, pos)
    gv = _ln(gv, gmg, gmb)

    def full(pool_kv, k_new, v_new):
        past = pool_kv[page_table].reshape(B, -1, 2, N_KV_HEADS, HEAD_DIM)
        k = _pad_seq(jnp.concatenate([past[:, :, 0], k_new], axis=1), SLC_BLOCK)
        v = _pad_seq(jnp.concatenate([past[:, :, 1], v_new], axis=1), SLC_BLOCK)
        return k, v

    kfc, vfc = full(pool_c, kc_, vc_)
    kfs, vfs = full(pool_s, ks_, vs_)
    keys = _nsa_keys(kfc, vfc, kfs, vfs, pe, w1, w2)
    o_c, o_s = _cmp_slc_attend(q, pos, *keys)
    kwb = jnp.concatenate([win_buf[:, :, 0], kw_], axis=1)
    vwb = jnp.concatenate([win_buf[:, :, 1], vw_], axis=1)
    wb = win_buf.shape[1]
    kpos = PAST_LEN - wb + jnp.arange(wb + T)
    o_w = _attend(q, kwb, vwb, _win_mask(pos, kpos))
    o_nsa = _gate_merge(gates, o_c, o_s, o_w)
    o_gm = _gmlp(gu, gv, ws, gb)
    z_ext = jnp.concatenate([pool_buf, p], axis=1)
    o_pool = _pool(z_ext, PAST_LEN, pw, ps)
    y = jnp.concatenate([o_nsa, o_gm, o_pool], -1) @ w_o
    new = (jnp.stack([kc_, vc_], 2), jnp.stack([ks_, vs_], 2),
           jnp.stack([kwb[:, T:], vwb[:, T:]], 2), z_ext[:, T:], gv)
    return y, new


def _layer(x, f_in, f_out, g, b, mixer):
    x = _ln(ALPHA * x + 0.5 * _swiglu(x, f_in[0], f_out[0]), g[0], b[0])
    m, new = mixer(x)
    x = _ln(ALPHA * x + m, g[1], b[1])
    x = _ln(ALPHA * x + 0.5 * _swiglu(x, f_in[1], f_out[1]), g[2], b[2])
    return x, new


def setup_inputs(seed: int = 0) -> dict:
    key = jax.random.key(seed)
    ks = jax.random.split(key, 24)

    def nrm(i, shape, scale):
        return scale * jax.random.normal(ks[i], shape, jnp.float32)

    n_pages = PAST_LEN // PAGE_SIZE
    n_phys = (DEC_BATCH * n_pages * 5) // 4
    w_buf = min(WINDOW, PAST_LEN)
    kv_page = (DEPTH, n_phys, PAGE_SIZE, 2, N_KV_HEADS, HEAD_DIM)
    perm = jax.random.permutation(ks[5], n_phys)
    page_table = perm[:DEC_BATCH * n_pages].reshape(DEC_BATCH, n_pages).astype(jnp.int32)
    return {
        'x_prompt': nrm(0, (BATCH, SEQ, D_MODEL), 1.0),
        'x_sample': nrm(1, (DEC_BATCH, DEC_SEQ, D_MODEL), 1.0),
        'cache_kv_cmp': nrm(2, kv_page, 1.0),
        'cache_kv_slc': nrm(3, kv_page, 1.0),
        'state_kv_win': nrm(4, (DEPTH, DEC_BATCH, w_buf, 2, N_KV_HEADS, HEAD_DIM), 1.0),
        'state_pool': nrm(6, (DEPTH, DEC_BATCH, POOL_HIST, POOL_W), 1.0),
        'page_table': page_table,
        'ln_g': 1.0 + nrm(7, (DEPTH, 3, D_MODEL), 0.02),
        'ln_b': nrm(8, (DEPTH, 3, D_MODEL), 0.02),
        'ffn_w_in': nrm(9, (DEPTH, 2, D_MODEL, 2 * D_FF), D_MODEL ** -0.5),
        'ffn_w_out': nrm(10, (DEPTH, 2, D_FF, D_MODEL), BETA * D_FF ** -0.5),
        'w_in': nrm(11, (DEPTH, D_MODEL, N_IN), D_MODEL ** -0.5),
        'w_o': nrm(12, (DEPTH, MIX_W, D_MODEL), BETA * MIX_W ** -0.5),
        'cmp_pe': nrm(13, (DEPTH, 2, CMP_LEN, HEAD_DIM), 0.1),
        'cmp_w1': nrm(14, (DEPTH, 2, CMP_LEN, HEAD_DIM, CMP_HID), (CMP_LEN * HEAD_DIM) ** -0.5),
        'cmp_w2': nrm(15, (DEPTH, 2, CMP_HID, HEAD_DIM), CMP_HID ** -0.5),
        'gm_ln_g': 1.0 + nrm(16, (DEPTH, GM_W), 0.02),
        'gm_ln_b': nrm(17, (DEPTH, GM_W), 0.02),
        'gm_ws': nrm(18, (DEPTH, GM_HEADS, GM_CHUNK, GM_CHUNK), GM_CHUNK ** -0.5),
        'gm_b': 1.0 + nrm(19, (DEPTH, GM_HEADS, GM_CHUNK), 0.02),
        'pool_w': nrm(20, (DEPTH, POOL_GROUPS, POOL_GW, POOL_GW), POOL_GW ** -0.5),
        'pool_scale': 1.0 + nrm(21, (DEPTH, POOL_W), 0.02),
    }


def reference(x_prompt, x_sample, cache_kv_cmp, cache_kv_slc, state_kv_win, state_pool, page_table,
              ln_g, ln_b, ffn_w_in, ffn_w_out, w_in, w_o, cmp_pe, cmp_w1, cmp_w2,
              gm_ln_g, gm_ln_b, gm_ws, gm_b, pool_w, pool_scale):
    xp, xs = x_prompt, x_sample
    new_p, new_s = [], []
    for l in range(DEPTH):
        P = (w_in[l], w_o[l], cmp_pe[l], cmp_w1[l], cmp_w2[l], gm_ln_g[l], gm_ln_b[l],
             gm_ws[l], gm_b[l], pool_w[l], pool_scale[l])
        xp, st_p = _layer(xp, ffn_w_in[l], ffn_w_out[l], ln_g[l], ln_b[l],
                          lambda h: _mixer_prompt(h, P))
        xs, st_s = _layer(xs, ffn_w_in[l], ffn_w_out[l], ln_g[l], ln_b[l],
                          lambda h: _mixer_sample(h, cache_kv_cmp[l], cache_kv_slc[l], state_kv_win[l],
                                                  state_pool[l], page_table, P))
        new_p.append(st_p)
        new_s.append(st_s)
    stk = lambda lst, i: jnp.stack([t[i] for t in lst])
    return (xp, xs, stk(new_p, 0), stk(new_s, 0), stk(new_p, 1), stk(new_s, 1),
            stk(new_p, 2), stk(new_s, 2), stk(new_p, 3), stk(new_s, 3), stk(new_s, 4))
```

```python
import functools

import numpy as np
import jax
import jax.numpy as jnp
from jax import lax
from jax.experimental import pallas as pl
from jax.experimental.pallas import tpu as pltpu

F32 = jnp.float32
BF16 = jnp.bfloat16

D_MODEL = 1024
DEPTH = 2
PAST_LEN = 16384
PAGE_SIZE = 128
HEAD_DIM = 64
NSA_W = D_MODEL // 2
GM_W = D_MODEL // 4
POOL_W = D_MODEL // 4
N_HEADS = NSA_W // HEAD_DIM
N_KV_HEADS = 2
GROUP = N_HEADS // N_KV_HEADS
CMP_STRIDE = 16
CMP_LEN = 2 * CMP_STRIDE
SLC_BLOCK = 64
SLC_TOPK = 16
WINDOW = 512
FORCE_SCORE = 1.0e4
ROPE_THETA = 10000.0
SCALE = HEAD_DIM ** -0.5
GM_HEADS = GM_W // HEAD_DIM
GM_CHUNK = 128
POOL_GROUPS = 4
POOL_GW = POOL_W // POOL_GROUPS
POOL_WINDOWS = (2, 4, 8, 16)
POOL_HIST = max(POOL_WINDOWS) - 1
D_FF = 256 * ((8 * D_MODEL // 3 + 255) // 256)
ALPHA = (2 * DEPTH) ** 0.25
LN_EPS = 1e-5
Q_W = N_HEADS * HEAD_DIM
KV_W = 2 * N_KV_HEADS * HEAD_DIM
GATE_W = 3 * N_HEADS
N_IN = Q_W + 3 * KV_W + GATE_W + 2 * GM_W + POOL_W

LANES = 128
KVP = N_KV_HEADS * HEAD_DIM
VMEM_LIMIT = 56 * 1024 * 1024
NEG = -1e30
HALO = 16

_OFF_Q = 0
_OFF_QR = Q_W
_OFF_KV = 2 * Q_W
_OFF_GATE = _OFF_KV + 3 * 3 * KVP
_OFF_UV = _OFF_GATE + LANES
_OFF_P = _OFF_UV + 2 * GM_W
N_EXT = _OFF_P + POOL_W


def _ln_rows(y, g, b):
    mu = jnp.mean(y, axis=-1, keepdims=True)
    d = y - mu
    var = jnp.mean(d * d, axis=-1, keepdims=True)
    return d * lax.rsqrt(var + LN_EPS) * g + b


def _dot(a, b):
    return jnp.dot(a, b, preferred_element_type=F32)


def _dot_t(a, b):
    return lax.dot_general(a, b, (((1,), (1,)), ((), ())), preferred_element_type=F32)


def _ffn_kernel(x_ref, wg_ref, wu_ref, wo_ref, g_ref, b_ref, o_ref, xb_ref, acc_ref, *, n_chunks):
    j = pl.program_id(1)

    @pl.when(j == 0)
    def _():
        xb_ref[...] = x_ref[...].astype(BF16)
        acc_ref[...] = jnp.zeros_like(acc_ref)

    xb = xb_ref[...]
    gate = _dot(xb, wg_ref[...])
    up = _dot(xb, wu_ref[...])
    hid = (gate * jax.nn.sigmoid(gate)) * up
    acc_ref[...] += _dot(hid.astype(BF16), wo_ref[...])

    @pl.when(j == n_chunks - 1)
    def _():
        y = ALPHA * x_ref[...] + 0.5 * acc_ref[...]
        o_ref[...] = _ln_rows(y, g_ref[...], b_ref[...])


def _ffn_ln(x, w_in_b, w_out_b, g, b, *, tm):
    rows = x.shape[0]
    n_chunks = 2
    fc = D_FF // n_chunks
    return pl.pallas_call(
        functools.partial(_ffn_kernel, n_chunks=n_chunks),
        grid=(rows // tm, n_chunks),
        in_specs=[
            pl.BlockSpec((tm, D_MODEL), lambda i, j: (i, 0)),
            pl.BlockSpec((D_MODEL, fc), lambda i, j: (0, j)),
            pl.BlockSpec((D_MODEL, fc), lambda i, j: (0, n_chunks + j)),
            pl.BlockSpec((fc, D_MODEL), lambda i, j: (j, 0)),
            pl.BlockSpec((1, D_MODEL), lambda i, j: (0, 0)),
            pl.BlockSpec((1, D_MODEL), lambda i, j: (0, 0)),
        ],
        out_specs=pl.BlockSpec((tm, D_MODEL), lambda i, j: (i, 0)),
        out_shape=jax.ShapeDtypeStruct((rows, D_MODEL), F32),
        scratch_shapes=[pltpu.VMEM((tm, D_MODEL), BF16), pltpu.VMEM((tm, D_MODEL), F32)],
        compiler_params=pltpu.CompilerParams(
            dimension_semantics=("parallel", "arbitrary"), vmem_limit_bytes=VMEM_LIMIT),
        name="ffn_ln",
    )(x, w_in_b, w_in_b, w_out_b, g.reshape(1, D_MODEL), b.reshape(1, D_MODEL))


def _inproj_kernel(h_ref, w_ref, cos_ref, sin_ref, gmg_ref, gmb_ref,
                   q_ref, kvc_ref, kvs_ref, kvw_ref, kvsb_ref, kvwb_ref, gate_ref, gu_ref, gv_ref, p_ref):
    hb = h_ref[...].astype(BF16)
    cos = cos_ref[...]
    sin = sin_ref[...]
    lane = lax.broadcasted_iota(jnp.int32, (1, LANES), 1)
    low = lane < HEAD_DIM

    for m in range(N_HEADS // 2):
        c0 = m * LANES
        qa = _dot(hb, w_ref[:, _OFF_Q + c0:_OFF_Q + c0 + LANES])
        qb = _dot(hb, w_ref[:, _OFF_QR + c0:_OFF_QR + c0 + LANES])
        pair = (qa * cos + qb * sin) * SCALE
        swapped = pltpu.roll(pair, HEAD_DIM, 1)
        kvh = (2 * m) // GROUP
        for e in range(2):
            h = 2 * m + e
            src = pair if e == kvh else swapped
            keep = low if kvh == 0 else jnp.logical_not(low)
            q_ref[h] = jnp.where(keep, src, 0.0).astype(BF16)

    for br, (f_ref, b_ref) in enumerate(((kvc_ref, None), (kvs_ref, kvsb_ref), (kvw_ref, kvwb_ref))):
        c0 = _OFF_KV + br * 3 * KVP
        ka = _dot(hb, w_ref[:, c0:c0 + KVP])
        kb = _dot(hb, w_ref[:, c0 + KVP:c0 + 2 * KVP])
        v = _dot(hb, w_ref[:, c0 + 2 * KVP:c0 + 3 * KVP])
        k = ka * cos + kb * sin
        f_ref[:, 0:KVP] = k
        f_ref[:, KVP:2 * KVP] = v
        if b_ref is not None:
            b_ref[:, 0:KVP] = k.astype(BF16)
            b_ref[:, KVP:2 * KVP] = v.astype(BF16)

    gate_ref[...] = jax.nn.sigmoid(_dot(hb, w_ref[:, _OFF_GATE:_OFF_GATE + LANES]))
    gu_ref[...] = jax.nn.gelu(_dot(hb, w_ref[:, _OFF_UV:_OFF_UV + GM_W]))
    gv = jax.nn.gelu(_dot(hb, w_ref[:, _OFF_UV + GM_W:_OFF_UV + 2 * GM_W]))
    gv_ref[...] = _ln_rows(gv, gmg_ref[...], gmb_ref[...])
    p_ref[...] = _dot(hb, w_ref[:, _OFF_P:_OFF_P + POOL_W])


def _inproj(h, w_ext, cos, sin, gmg, gmb, *, tm):
    rows = h.shape[0]
    n_tab = cos.shape[0] // tm
    row_spec = lambda w: pl.BlockSpec((tm, w), lambda i: (i, 0))
    tab_spec = pl.BlockSpec((tm, LANES), lambda i: (i % n_tab, 0))
    vec_spec = pl.BlockSpec((1, GM_W), lambda i: (0, 0))
    sds = jax.ShapeDtypeStruct
    return pl.pallas_call(
        _inproj_kernel,
        grid=(rows // tm,),
        in_specs=[row_spec(D_MODEL), pl.BlockSpec((D_MODEL, N_EXT), lambda i: (0, 0)),
                  tab_spec, tab_spec, vec_spec, vec_spec],
        out_specs=[pl.BlockSpec((N_HEADS, tm, LANES), lambda i: (0, i, 0)),
                   row_spec(KV_W), row_spec(KV_W), row_spec(KV_W), row_spec(KV_W), row_spec(KV_W),
                   row_spec(LANES), row_spec(GM_W), row_spec(GM_W), row_spec(POOL_W)],
        out_shape=[sds((N_HEADS, rows, LANES), BF16),
                   sds((rows, KV_W), F32), sds((rows, KV_W), F32), sds((rows, KV_W), F32),
                   sds((rows, KV_W), BF16), sds((rows, KV_W), BF16),
                   sds((rows, LANES), F32), sds((rows, GM_W), F32), sds((rows, GM_W), F32),
                   sds((rows, POOL_W), F32)],
        compiler_params=pltpu.CompilerParams(dimension_semantics=("parallel",), vmem_limit_bytes=VMEM_LIMIT),
        name="inproj",
    )(h, w_ext, cos, sin, gmg.reshape(1, GM_W), gmb.reshape(1, GM_W))


def _build_w_ext(w_in):
    half = HEAD_DIM // 2

    def rot(w):
        n = w.shape[1] // HEAD_DIM
        w3 = w.reshape(D_MODEL, n, 2, half)
        return jnp.stack([-w3[:, :, 1], w3[:, :, 0]], axis=2).reshape(D_MODEL, n * HEAD_DIM)

    q = w_in[:, :Q_W]
    cols = [q, rot(q)]
    for br in range(3):
        kv = w_in[:, Q_W + br * KV_W:Q_W + (br + 1) * KV_W]
        k, v = kv[:, :KVP], kv[:, KVP:]
        cols += [k, rot(k), v]
    g0 = Q_W + 3 * KV_W
    cols.append(jnp.pad(w_in[:, g0:g0 + GATE_W], ((0, 0), (0, LANES - GATE_W))))
    cols.append(w_in[:, g0 + GATE_W:])
    return jnp.concatenate(cols, axis=1).astype(BF16)


def _rope_tables(pos):
    half = HEAD_DIM // 2
    inv = ROPE_THETA ** (-jnp.arange(half, dtype=F32) / half)
    ang = pos.astype(F32)[:, None] * inv[None, :]
    cos = jnp.tile(jnp.cos(ang), (1, LANES // half))
    sin = jnp.tile(jnp.sin(ang), (1, LANES // half))
    return cos, sin


def _compress_kernel(x_ref, pe_ref, w1_ref, w2_ref, o_ref):
    w1 = w1_ref[...]
    fs = _dot(x_ref[...].astype(BF16), w1)
    per = _dot(pe_ref[...].astype(BF16), w1)
    pe_term = per[0:1, 0:2 * KVP] + per[1:2, 2 * KVP:4 * KVP]
    first = fs[:, 0:2 * KVP]
    second = fs[:, 2 * KVP:4 * KVP]
    nxt = jnp.concatenate([second[1:], jnp.zeros((1, 2 * KVP), F32)], axis=0)
    hid = jax.nn.gelu(first + nxt + pe_term)
    o_ref[...] = _dot(hid.astype(BF16), w2_ref[...]).astype(BF16)


def _compress_weights(pe, w1, w2):
    eye = jnp.eye(N_KV_HEADS, dtype=F32)
    w1r = w1.reshape(2, 2, CMP_STRIDE, HEAD_DIM, HEAD_DIM)
    w1big = jnp.einsum('ksjde,kq,hg->jkhdsqge', w1r, eye, eye).reshape(CMP_STRIDE * 2 * KVP, 4 * KVP)
    w2big = jnp.einsum('ked,kq,hg->kheqgd', w2, eye, eye).reshape(2 * KVP, 2 * KVP)
    per = pe.reshape(2, 2, CMP_STRIDE, HEAD_DIM).transpose(1, 2, 0, 3)
    per = jnp.broadcast_to(per[:, :, :, None, :], (2, CMP_STRIDE, 2, N_KV_HEADS, HEAD_DIM))
    pe_rows = jnp.pad(per.reshape(2, CMP_STRIDE * 2 * KVP), ((0, 6), (0, 0)))
    return pe_rows, w1big.astype(BF16), w2big.astype(BF16)


def _compress_prompt(kvc, pe_rows, w1big, w2big, *, batch):
    nsub = kvc.shape[0] // batch // CMP_STRIDE
    width = CMP_STRIDE * 2 * KVP
    x = kvc.reshape(batch, nsub, width)
    return pl.pallas_call(
        _compress_kernel,
        grid=(batch,),
        in_specs=[pl.BlockSpec((None, nsub, width), lambda b: (b, 0, 0)),
                  pl.BlockSpec((8, width), lambda b: (0, 0)),
                  pl.BlockSpec((width, 4 * KVP), lambda b: (0, 0)),
                  pl.BlockSpec((2 * KVP, 2 * KVP), lambda b: (0, 0))],
        out_specs=pl.BlockSpec((None, nsub, 2 * KVP), lambda b: (b, 0, 0)),
        out_shape=jax.ShapeDtypeStruct((batch, nsub, 2 * KVP), BF16),
        compiler_params=pltpu.CompilerParams(dimension_semantics=("parallel",), vmem_limit_bytes=VMEM_LIMIT),
        name="compress_prompt",
    )(x, pe_rows, w1big, w2big)


def _top_blocks(score, blk):
    return _top_blocks_idx(score, blk)[0]


def _top_blocks_idx(score, blk):
    sel = jnp.zeros(score.shape, F32)
    idx = jnp.zeros((score.shape[0], LANES), jnp.int32)
    lane = lax.broadcasted_iota(jnp.int32, (1, LANES), 1)
    big = jnp.int32(1 << 20)
    for it in range(SLC_TOPK):
        m = jnp.max(score, axis=-1, keepdims=True)
        first = jnp.min(jnp.where(score == m, blk, big), axis=-1, keepdims=True)
        hit = blk == first
        sel = jnp.where(hit, 1.0, sel)
        idx = jnp.where(lane == it, first, idx)
        score = jnp.where(hit, -jnp.inf, score)
    return sel, idx


def _softmax_rows(s, mask):
    s = jnp.where(mask, s, NEG)
    m = jnp.max(s, axis=-1, keepdims=True)
    e = jnp.exp(s - m)
    return jnp.where(mask, e / jnp.sum(e, axis=-1, keepdims=True), 0.0)


def _nsa_prompt_kernel(q_ref, kcvc_ref, kvs_ref, kvw_ref, gate_ref, selmap_ref, expand_ref, o_ref,
                       *, tq, tk, seq, n_sel_blocks):
    t0 = pl.program_id(1) * tq
    nsub = kcvc_ref.shape[0]
    qpos = t0 + lax.broadcasted_iota(jnp.int32, (tq, 1), 0)
    lane = lax.broadcasted_iota(jnp.int32, (1, LANES), 1)
    low = lane < HEAD_DIM
    gates = gate_ref[...]
    span = min(WINDOW + tq, seq)
    outs = []
    for k in range(N_KV_HEADS):
        qk = q_ref[GROUP * k:GROUP * (k + 1)].reshape(GROUP * tq, LANES)

        s = _dot_t(qk, kcvc_ref[:, 0:KVP]).reshape(GROUP, tq, nsub)
        cmp_end = CMP_STRIDE * lax.broadcasted_iota(jnp.int32, (1, nsub), 1) + (CMP_LEN - 1)
        p = _softmax_rows(s, (cmp_end <= qpos)[None]).astype(BF16).reshape(GROUP * tq, nsub)
        o_c = _dot(p, kcvc_ref[:, KVP:2 * KVP])
        imp_g = _dot(p, selmap_ref[...]).reshape(GROUP, tq, LANES)
        imp = imp_g[0]
        for g in range(1, GROUP):
            imp = imp + imp_g[g]

        cur = qpos // SLC_BLOCK
        forced = (lane == 0) | (lane == cur) | (lane == cur - 1)
        score = jnp.where(forced, FORCE_SCORE, jnp.where(lane <= cur, imp, -1.0))
        score = jnp.where(lane < n_sel_blocks, score, -jnp.inf)
        sel = _top_blocks(score, lane).astype(BF16)

        def body(kt, carry):
            m_i, l_i, acc = carry
            r0 = pl.multiple_of(kt * tk, tk)
            s = _dot_t(qk, kvs_ref[pl.ds(r0, tk), 0:KVP]).reshape(GROUP, tq, tk)
            chosen = _dot(sel, expand_ref[kt])
            kpos = r0 + lax.broadcasted_iota(jnp.int32, (1, tk), 1)
            mask = ((chosen > 0.5) & (kpos <= qpos))[None]
            s = jnp.where(mask, s, NEG)
            m_new = jnp.maximum(m_i, jnp.max(s, axis=-1, keepdims=True))
            a = jnp.exp(m_i - m_new)
            e = jnp.where(mask, jnp.exp(s - m_new), 0.0)
            l_new = a * l_i + jnp.sum(e, axis=-1, keepdims=True)
            pv = _dot(e.astype(BF16).reshape(GROUP * tq, tk), kvs_ref[pl.ds(r0, tk), KVP:2 * KVP])
            return m_new, l_new, a * acc + pv.reshape(GROUP, tq, LANES)

        n_kt = (t0 + tq + tk - 1) // tk
        init = (jnp.full((GROUP, tq, 1), NEG, F32), jnp.zeros((GROUP, tq, 1), F32),
                jnp.zeros((GROUP, tq, LANES), F32))
        _, l_s, acc_s = lax.fori_loop(0, n_kt, body, init)
        o_s = acc_s / l_s

        start = pl.multiple_of(jnp.maximum(t0 + tq - span, 0), tq)
        s = _dot_t(qk, kvw_ref[pl.ds(start, span), 0:KVP]).reshape(GROUP, tq, span)
        dist = qpos - (start + lax.broadcasted_iota(jnp.int32, (1, span), 1))
        p = _softmax_rows(s, ((dist >= 0) & (dist <= WINDOW))[None]).astype(BF16).reshape(GROUP * tq, span)
        o_w = _dot(p, kvw_ref[pl.ds(start, span), KVP:2 * KVP])

        o_c = o_c.reshape(GROUP, tq, LANES)
        o_w = o_w.reshape(GROUP, tq, LANES)
        for g in range(GROUP):
            c = 3 * (GROUP * k + g)
            outs.append(gates[:, c:c + 1] * o_c[g] + gates[:, c + 1:c + 2] * o_s[g] + gates[:, c + 2:c + 3] * o_w[g])

    for m in range(N_HEADS // 2):
        k = (2 * m) // GROUP
        a, b = outs[2 * m], outs[2 * m + 1]
        if k == 0:
            b = pltpu.roll(b, HEAD_DIM, 1)
        else:
            a = pltpu.roll(a, HEAD_DIM, 1)
        o_ref[:, m * LANES:(m + 1) * LANES] = jnp.where(low, a, b).astype(o_ref.dtype)


def _sel_map(nc_rows, n_cmp, ns):
    c0 = CMP_STRIDE * np.arange(nc_rows)[:, None]
    s0 = SLC_BLOCK * np.arange(LANES)[None, :]
    ov = np.clip(np.minimum(c0 + CMP_LEN, s0 + SLC_BLOCK) - np.maximum(c0, s0), 0, None) / CMP_LEN
    ov = ov * (np.arange(nc_rows)[:, None] < n_cmp) * (np.arange(LANES)[None, :] < ns)
    return jnp.asarray(ov, dtype=BF16)


def _expand_map(seq, tk):
    key_blk = np.arange(seq) // SLC_BLOCK
    e = (np.arange(LANES)[:, None] == key_blk[None, :]).astype(np.float32)
    return jnp.asarray(e.reshape(LANES, seq // tk, tk).transpose(1, 0, 2), dtype=BF16)


def _nsa_prompt(q, kcvc, kvs_b, kvw_b, gates, *, batch, seq, tq=128, tk=512):
    tk = min(tk, seq)
    nq = seq // tq
    nsub = seq // CMP_STRIDE
    ns = seq // SLC_BLOCK
    assert ns <= LANES and seq % tk == 0 and seq % tq == 0
    selmap = _sel_map(nsub, nsub - 1, ns)
    expand = _expand_map(seq, tk)
    kern = functools.partial(_nsa_prompt_kernel, tq=tq, tk=tk, seq=seq, n_sel_blocks=ns)
    return pl.pallas_call(
        kern,
        grid=(batch, nq),
        in_specs=[pl.BlockSpec((N_HEADS, tq, LANES), lambda b, i: (0, b * nq + i, 0)),
                  pl.BlockSpec((None, nsub, 2 * KVP), lambda b, i: (b, 0, 0)),
                  pl.BlockSpec((seq, KV_W), lambda b, i: (b, 0)),
                  pl.BlockSpec((seq, KV_W), lambda b, i: (b, 0)),
                  pl.BlockSpec((tq, LANES), lambda b, i: (b * nq + i, 0)),
                  pl.BlockSpec((nsub, LANES), lambda b, i: (0, 0)),
                  pl.BlockSpec((seq // tk, LANES, tk), lambda b, i: (0, 0, 0))],
        out_specs=pl.BlockSpec((tq, NSA_W), lambda b, i: (b * nq + i, 0)),
        out_shape=jax.ShapeDtypeStruct((batch * seq, NSA_W), BF16),
        compiler_params=pltpu.CompilerParams(
            dimension_semantics=("parallel", "arbitrary"), vmem_limit_bytes=VMEM_LIMIT),
        name="nsa_prompt",
    )(q, kcvc, kvs_b, kvw_b, gates, selmap, expand)


def _pool_windows(z_ext, tm):
    s2 = z_ext[1:] + z_ext[:-1]
    s4 = s2[2:] + s2[:-2]
    s8 = s4[4:] + s4[:-4]
    s16 = s8[8:] + s8[:-8]
    return (s2[HALO - 1:HALO - 1 + tm], s4[HALO - 3:HALO - 3 + tm], s8[HALO - 7:HALO - 7 + tm],
            s16[HALO - 15:HALO - 15 + tm])


def _mix_out_kernel(*refs, tm, tiles_per_seq, pool_in_kernel):
    if pool_in_kernel:
        (x_ref, nsa_ref, gu_ref, gv_ref, p_ref, halo_ref, ws_ref, gb_ref, pw_ref, ps_ref, wo_ref,
         g_ref, b_ref, o_ref) = refs
    else:
        (x_ref, nsa_ref, gu_ref, gv_ref, d_ref, ws_ref, gb_ref, pw_ref, ps_ref, wo_ref,
         g_ref, b_ref, o_ref) = refs
    lane = lax.broadcasted_iota(jnp.int32, (1, GM_W), 1)

    parts = []
    for c in range(tm // GM_CHUNK):
        v = gv_ref[c * GM_CHUNK:(c + 1) * GM_CHUNK, :]
        stacked = jnp.concatenate(
            [jnp.where(lane // HEAD_DIM == h, v, 0.0) for h in range(GM_HEADS)], axis=0).astype(BF16)
        s = _dot(ws_ref[...], stacked) + gb_ref[...]
        parts.append(gu_ref[c * GM_CHUNK:(c + 1) * GM_CHUNK, :] * s)
    o_gm = parts[0] if len(parts) == 1 else jnp.concatenate(parts, axis=0)

    if pool_in_kernel:
        first_tile = (pl.program_id(0) % tiles_per_seq) == 0
        halo = jnp.where(first_tile, 0.0, halo_ref[...])
        z = p_ref[...]
        wins = _pool_windows(jnp.concatenate([halo, z], axis=0), tm)
        pos = (pl.program_id(0) % tiles_per_seq) * tm + lax.broadcasted_iota(jnp.int32, (tm, 1), 0)
        grp = lane // POOL_GW
        wsum = jnp.where(grp == 0, wins[0], jnp.where(grp == 1, wins[1], jnp.where(grp == 2, wins[2], wins[3])))
        width = jnp.where(grp == 0, POOL_WINDOWS[0], jnp.where(grp == 1, POOL_WINDOWS[1],
                          jnp.where(grp == 2, POOL_WINDOWS[2], POOL_WINDOWS[3])))
        cnt = jnp.minimum(width, pos + 1).astype(F32)
        d = wsum / cnt - z
    else:
        d = d_ref[...]
    o_pool = _dot(d.astype(BF16), pw_ref[...]) * ps_ref[...]

    mixed = jnp.concatenate([nsa_ref[...], o_gm.astype(BF16), o_pool.astype(BF16)], axis=1)
    y = ALPHA * x_ref[...] + _dot(mixed, wo_ref[...])
    o_ref[...] = _ln_rows(y, g_ref[...], b_ref[...])


def _mix_out(x, o_nsa, gu, gv, p_or_d, ws_cat, gb_full, pw_big, ps, w_o_b, g, b, *, tm, seq, pool_in_kernel):
    rows = x.shape[0]
    tiles_per_seq = max(seq // tm, 1)
    row_spec = lambda w: pl.BlockSpec((tm, w), lambda i: (i, 0))
    const = lambda shp: pl.BlockSpec(shp, lambda i: (0,) * len(shp))
    in_specs = [row_spec(D_MODEL), row_spec(NSA_W), row_spec(GM_W), row_spec(GM_W), row_spec(POOL_W)]
    args = [x, o_nsa, gu, gv, p_or_d]
    if pool_in_kernel:
        in_specs.append(pl.BlockSpec((HALO, POOL_W), lambda i: (jnp.maximum(i * (tm // HALO) - 1, 0), 0)))
        args.append(p_or_d)
    in_specs += [const((GM_CHUNK, GM_HEADS * GM_CHUNK)), const((GM_CHUNK, GM_W)), const((POOL_W, POOL_W)),
                 const((1, POOL_W)), const((D_MODEL, D_MODEL)), const((1, D_MODEL)), const((1, D_MODEL))]
    args += [ws_cat, gb_full, pw_big, ps.reshape(1, POOL_W), w_o_b, g.reshape(1, D_MODEL), b.reshape(1, D_MODEL)]
    kern = functools.partial(_mix_out_kernel, tm=tm, tiles_per_seq=tiles_per_seq, pool_in_kernel=pool_in_kernel)
    return pl.pallas_call(
        kern,
        grid=(rows // tm,),
        in_specs=in_specs,
        out_specs=row_spec(D_MODEL),
        out_shape=jax.ShapeDtypeStruct((rows, D_MODEL), F32),
        compiler_params=pltpu.CompilerParams(dimension_semantics=("parallel",), vmem_limit_bytes=VMEM_LIMIT),
        name="mix_out_prompt" if pool_in_kernel else "mix_out_sample",
    )(*args)


def _gmlp_weights(ws, gb, chunk_rows, reps):
    wm = jnp.tril(ws[:, :chunk_rows, :chunk_rows])
    bias = gb[:, :chunk_rows]
    if reps > 1:
        eye = jnp.eye(reps, dtype=F32)
        wm = jnp.einsum('hts,ab->hatbs', wm, eye).reshape(GM_HEADS, reps * chunk_rows, reps * chunk_rows)
        bias = jnp.tile(bias, (1, reps))
    ws_cat = wm.transpose(1, 0, 2).reshape(GM_CHUNK, GM_HEADS * GM_CHUNK).astype(BF16)
    gb_full = jnp.repeat(bias.T, HEAD_DIM, axis=1)
    return ws_cat, gb_full


def _pool_weights(pw):
    eye = jnp.eye(POOL_GROUPS, dtype=F32)
    return jnp.einsum('gce,gq->gcqe', pw, eye).reshape(POOL_W, POOL_W).astype(BF16)


PAGES_PER_STEP = 32
SUBS_PER_PAGE = PAGE_SIZE // CMP_STRIDE


def _cmp_sample_kernel(pt_ref, cache_ref, q_ref, pe_ref, w1_ref, w2_ref, selmap_ref, oc_ref, idx_ref,
                       pbuf, sem, xt_ref, fs_ref, *, layer, n_chunks, n_seq, dec_seq, past_len, n_sel_blocks):
    b = pl.program_id(0)
    c = pl.program_id(1)
    step = b * n_chunks + c
    slot = lax.rem(step, 2)
    pps = PAGES_PER_STEP

    def page_copies(sb, sc, sl):
        return [pltpu.make_async_copy(cache_ref.at[layer, pt_ref[sb, sc * pps + p]], pbuf.at[sl, p], sem.at[sl])
                for p in range(pps)]

    @pl.when(step == 0)
    def _():
        for cp in page_copies(b, c, slot):
            cp.start()

    @pl.when(step + 1 < n_seq * n_chunks)
    def _():
        wrap = c + 1 == n_chunks
        for cp in page_copies(jnp.where(wrap, b + 1, b), jnp.where(wrap, 0, c + 1), 1 - slot):
            cp.start()

    for cp in page_copies(b, c, slot):
        cp.wait()

    def to_rows(p, carry):
        r0 = pl.multiple_of(p * PAGE_SIZE, PAGE_SIZE)
        for kv in range(2):
            xt_ref[kv, pl.ds(r0, PAGE_SIZE), :] = pbuf[slot, p, kv].T
        return carry

    lax.fori_loop(0, pps, to_rows, 0)

    subs = pps * SUBS_PER_PAGE
    s0 = pl.multiple_of(c * subs, subs)
    for kv in range(2):
        xr = jnp.concatenate(
            [xt_ref[kv, pl.ds(j, subs, stride=CMP_STRIDE), :].astype(BF16) for j in range(CMP_STRIDE)], axis=1)
        fs_ref[kv, pl.ds(s0, subs), :] = _dot(xr, w1_ref[kv])

    @pl.when(c == n_chunks - 1)
    def _():
        nsub = n_chunks * subs
        kcv = []
        for kv in range(2):
            f = fs_ref[kv]
            per = _dot(pe_ref[kv].astype(BF16), w1_ref[kv])
            pe_term = per[0:1, 0:KVP] + per[1:2, KVP:2 * KVP]
            nxt = jnp.concatenate([f[1:, KVP:2 * KVP], jnp.zeros((1, KVP), F32)], axis=0)
            hid = jax.nn.gelu(f[:, 0:KVP] + nxt + pe_term)
            kcv.append(_dot(hid.astype(BF16), w2_ref[kv]).astype(BF16))
        n_kt = N_KV_HEADS * dec_seq
        rows = GROUP * n_kt
        q = q_ref[...]
        qpos = past_len + lax.rem(lax.broadcasted_iota(jnp.int32, (rows, 1), 0), dec_seq)
        cmp_end = CMP_STRIDE * lax.broadcasted_iota(jnp.int32, (1, nsub), 1) + (CMP_LEN - 1)
        p = _softmax_rows(_dot_t(q, kcv[0]), cmp_end <= qpos).astype(BF16)
        oc_ref[...] = _dot(p, kcv[1])
        imp_g = _dot(p, selmap_ref[...])
        imp = imp_g[0:n_kt]
        for g in range(1, GROUP):
            imp = imp + imp_g[g * n_kt:(g + 1) * n_kt]
        blk = lax.broadcasted_iota(jnp.int32, (1, imp.shape[1]), 1)
        cur = qpos[0:n_kt] // SLC_BLOCK
        forced = (blk == 0) | (blk == cur) | (blk == cur - 1)
        score = jnp.where(forced, FORCE_SCORE, jnp.where(blk <= cur, imp, -1.0))
        score = jnp.where(blk < n_sel_blocks, score, -jnp.inf)
        idx_ref[...] = _top_blocks_idx(score, blk)[1]


def _cmp_sample(page_table, cache_t, q_gkt, pe2, w1kv, w2kv, *, layer, dec_seq, past_len):
    n_seq, n_pages = page_table.shape
    n_chunks = n_pages // PAGES_PER_STEP
    nsub = n_pages * SUBS_PER_PAGE
    ns = (past_len + dec_seq + SLC_BLOCK - 1) // SLC_BLOCK
    ns_pad = LANES * ((ns + LANES - 1) // LANES)
    n_kt = N_KV_HEADS * dec_seq
    rows = GROUP * n_kt
    c0 = CMP_STRIDE * np.arange(nsub)[:, None]
    s0 = SLC_BLOCK * np.arange(ns_pad)[None, :]
    ov = np.clip(np.minimum(c0 + CMP_LEN, s0 + SLC_BLOCK) - np.maximum(c0, s0), 0, None) / CMP_LEN
    ov = ov * (np.arange(nsub)[:, None] < nsub - 1) * (np.arange(ns_pad)[None, :] < ns)
    selmap = jnp.asarray(ov, dtype=BF16)
    width = CMP_STRIDE * KVP
    kern = functools.partial(_cmp_sample_kernel, layer=layer, n_chunks=n_chunks, n_seq=n_seq, dec_seq=dec_seq,
                             past_len=past_len, n_sel_blocks=ns)
    const = lambda shp: pl.BlockSpec(shp, lambda b, c, pt: (0,) * len(shp))
    grid_spec = pltpu.PrefetchScalarGridSpec(
        num_scalar_prefetch=1,
        grid=(n_seq, n_chunks),
        in_specs=[pl.BlockSpec(memory_space=pl.ANY),
                  pl.BlockSpec((None, rows, LANES), lambda b, c, pt: (b, 0, 0)),
                  const((2, 8, width)), const((2, width, 2 * KVP)), const((2, KVP, KVP)), const((nsub, ns_pad))],
        out_specs=[pl.BlockSpec((None, rows, LANES), lambda b, c, pt: (b, 0, 0)),
                   pl.BlockSpec((None, n_kt, LANES), lambda b, c, pt: (b, 0, 0))],
        scratch_shapes=[pltpu.VMEM((2, PAGES_PER_STEP, 2, KVP, PAGE_SIZE), F32),
                        pltpu.SemaphoreType.DMA((2,)),
                        pltpu.VMEM((2, PAGES_PER_STEP * PAGE_SIZE, KVP), F32),
                        pltpu.VMEM((2, nsub, 2 * KVP), F32)])
    return pl.pallas_call(
        kern,
        grid_spec=grid_spec,
        out_shape=[jax.ShapeDtypeStruct((n_seq, rows, LANES), F32),
                   jax.ShapeDtypeStruct((n_seq, n_kt, LANES), jnp.int32)],
        compiler_params=pltpu.CompilerParams(
            dimension_semantics=("arbitrary", "arbitrary"), vmem_limit_bytes=VMEM_LIMIT),
        name="cmp_sample",
    )(page_table, cache_t, q_gkt, pe2, w1kv, w2kv, selmap)


def _compress_weights_kv(pe, w1, w2):
    eye = jnp.eye(N_KV_HEADS, dtype=F32)
    w1r = w1.reshape(2, 2, CMP_STRIDE, HEAD_DIM, HEAD_DIM)
    w1kv = jnp.einsum('ksjde,hg->kjhdsge', w1r, eye).reshape(2, CMP_STRIDE * KVP, 2 * KVP)
    w2kv = jnp.einsum('ked,hg->khegd', w2, eye).reshape(2, KVP, KVP)
    per = pe.reshape(2, 2, CMP_STRIDE, HEAD_DIM)
    per = jnp.broadcast_to(per[:, :, :, None, :], (2, 2, CMP_STRIDE, N_KV_HEADS, HEAD_DIM))
    pe2 = jnp.pad(per.reshape(2, 2, CMP_STRIDE * KVP), ((0, 0), (0, 6), (0, 0)))
    return pe2, w1kv.astype(BF16), w2kv.astype(BF16)


def _slc_sample_kernel(pt_ref, idx_sm_ref, cache_ref, q_ref, idxv_ref, knew_ref, win_ref, wnew_ref, oc_ref,
                       gate_ref, expand_ref, o_ref, kbuf, vbuf, sem,
                       *, layer, n_seq, n_pages, dec_seq, past_len):
    b = pl.program_id(0)
    slot = lax.rem(b, 2)
    n_kt = N_KV_HEADS * dec_seq
    rows = GROUP * n_kt
    n_past_blocks = past_len // SLC_BLOCK
    per_head = dec_seq * SLC_TOPK

    def tile_copies(sb, sl, k, i):
        kt = k * dec_seq + i // SLC_TOPK
        s = lax.rem(i, SLC_TOPK)
        j = idx_sm_ref[(sb * n_kt + kt) * SLC_TOPK + s]
        phys = pt_ref[sb, jnp.minimum(lax.shift_right_logical(j, 1), n_pages - 1)]
        return [pltpu.make_async_copy(cache_ref.at[layer, phys, kv, pl.ds(k * HEAD_DIM, HEAD_DIM), :],
                                      buf.at[sl, kt, s], sem.at[sl]) for kv, buf in ((0, kbuf), (1, vbuf))]

    def start_all(sb, sl):
        for k in range(N_KV_HEADS):
            def body(i, carry):
                for cp in tile_copies(sb, sl, k, i):
                    cp.start()
                return carry
            lax.fori_loop(0, per_head, body, 0)

    @pl.when(b == 0)
    def _():
        start_all(b, slot)

    @pl.when(b + 1 < n_seq)
    def _():
        start_all(b + 1, 1 - slot)

    for k in range(N_KV_HEADS):
        def wait_body(i, carry):
            for cp in tile_copies(b, slot, k, i):
                cp.wait()
            return carry
        lax.fori_loop(0, per_head, wait_body, 0)

    q = q_ref[...]
    qb = q.astype(BF16)
    gates = gate_ref[...]
    t_row = lax.rem(lax.broadcasted_iota(jnp.int32, (rows, 1), 0) // GROUP, dec_seq)
    t_new = lax.broadcasted_iota(jnp.int32, (1, dec_seq), 1)
    new_ok = t_new <= t_row

    wb = win_ref.shape[2]
    kpos = past_len - wb + lax.broadcasted_iota(jnp.int32, (1, wb), 1)
    dist = past_len + t_row - kpos
    w_ok = (dist >= 0) & (dist <= WINDOW) & (kpos >= 0)
    s_w = jnp.where(w_ok, _dot(qb, win_ref[0].astype(BF16)), NEG)
    s_n = jnp.where(new_ok, _dot_t(qb, wnew_ref[:, 0:KVP].astype(BF16)), NEG)
    m = jnp.maximum(jnp.max(s_w, axis=-1, keepdims=True), jnp.max(s_n, axis=-1, keepdims=True))
    e_w = jnp.where(w_ok, jnp.exp(s_w - m), 0.0)
    e_n = jnp.where(new_ok, jnp.exp(s_n - m), 0.0)
    den = jnp.sum(e_w, axis=-1, keepdims=True) + jnp.sum(e_n, axis=-1, keepdims=True)
    o_w = (_dot_t(e_w.astype(BF16), win_ref[1].astype(BF16))
           + _dot(e_n.astype(BF16), wnew_ref[:, KVP:2 * KVP].astype(BF16))) / den

    idxv = idxv_ref[...]
    lane16 = lax.broadcasted_iota(jnp.int32, (1, LANES), 1) < SLC_TOPK
    half = jnp.where(lane16 & ((idxv & 1) == 1), 1.0, 0.0).astype(BF16)
    live = jnp.where(lane16 & (idxv < n_past_blocks), 1.0, 0.0).astype(BF16)
    half_x = _dot(half, expand_ref[...])
    live_x = _dot(live, expand_ref[...])
    col = lax.broadcasted_iota(jnp.int32, (1, SLC_TOPK * PAGE_SIZE), 1)
    col_half = (lax.rem(col, PAGE_SIZE) // SLC_BLOCK).astype(F32)
    tile_ok = (live_x > 0.5) & (half_x == col_half)
    s_new = _dot_t(qb, knew_ref[:, 0:KVP].astype(BF16))
    zeros_half = jnp.zeros((GROUP, HEAD_DIM), F32)
    o_parts = []
    for kt in range(n_kt):
        k = kt // dec_seq
        r0 = kt * GROUP
        qk = q[r0:r0 + GROUP, k * HEAD_DIM:(k + 1) * HEAD_DIM].astype(BF16)
        kcat = jnp.concatenate([kbuf[slot, kt, s] for s in range(SLC_TOPK)], axis=1).astype(BF16)
        vcat = jnp.concatenate([vbuf[slot, kt, s] for s in range(SLC_TOPK)], axis=1).astype(BF16)
        ok = tile_ok[kt:kt + 1]
        nok = new_ok[r0:r0 + GROUP]
        s_s = jnp.where(ok, _dot(qk, kcat), NEG)
        s_n = jnp.where(nok, s_new[r0:r0 + GROUP], NEG)
        m = jnp.maximum(jnp.max(s_s, axis=-1, keepdims=True), jnp.max(s_n, axis=-1, keepdims=True))
        e_s = jnp.where(ok, jnp.exp(s_s - m), 0.0)
        e_n = jnp.where(nok, jnp.exp(s_n - m), 0.0)
        den = jnp.sum(e_s, axis=-1, keepdims=True) + jnp.sum(e_n, axis=-1, keepdims=True)
        v_new = knew_ref[:, KVP + k * HEAD_DIM:KVP + (k + 1) * HEAD_DIM].astype(BF16)
        o = (_dot_t(e_s.astype(BF16), vcat) + _dot(e_n.astype(BF16), v_new)) / den
        o_parts.append(jnp.concatenate([o, zeros_half] if k == 0 else [zeros_half, o], axis=1))
    o_s = jnp.concatenate(o_parts, axis=0)

    o_ref[...] = gates[:, 0:1] * oc_ref[...] + gates[:, 1:2] * o_s + gates[:, 2:3] * o_w


def _slc_sample(page_table, idx_flat, cache_t, q_ktg, idxv, kvs_new, win_t, kvw_new, o_c, gates_r,
                *, layer, dec_seq, past_len):
    n_seq, n_pages = page_table.shape
    n_kt = N_KV_HEADS * dec_seq
    rows = GROUP * n_kt
    wb = win_t.shape[-1]
    cols = SLC_TOPK * PAGE_SIZE
    expand = jnp.asarray((np.arange(LANES)[:, None] == (np.arange(cols) // PAGE_SIZE)[None, :]).astype(np.float32),
                         dtype=BF16)
    kern = functools.partial(_slc_sample_kernel, layer=layer, n_seq=n_seq, n_pages=n_pages, dec_seq=dec_seq,
                             past_len=past_len)
    per_seq = lambda r, w: pl.BlockSpec((None, r, w), lambda b, pt, ix: (b, 0, 0))
    grid_spec = pltpu.PrefetchScalarGridSpec(
        num_scalar_prefetch=2,
        grid=(n_seq,),
        in_specs=[pl.BlockSpec(memory_space=pl.ANY),
                  per_seq(rows, LANES), per_seq(n_kt, LANES), per_seq(dec_seq, KV_W),
                  pl.BlockSpec((None, None, 2, KVP, wb), lambda b, pt, ix: (layer, b, 0, 0, 0)),
                  per_seq(dec_seq, KV_W), per_seq(rows, LANES), per_seq(rows, LANES),
                  pl.BlockSpec((LANES, cols), lambda b, pt, ix: (0, 0))],
        out_specs=per_seq(rows, LANES),
        scratch_shapes=[pltpu.VMEM((2, n_kt, SLC_TOPK, HEAD_DIM, PAGE_SIZE), F32),
                        pltpu.VMEM((2, n_kt, SLC_TOPK, HEAD_DIM, PAGE_SIZE), F32),
                        pltpu.SemaphoreType.DMA((2,))])
    return pl.pallas_call(
        kern,
        grid_spec=grid_spec,
        out_shape=jax.ShapeDtypeStruct((n_seq, rows, LANES), F32),
        compiler_params=pltpu.CompilerParams(dimension_semantics=("arbitrary",), vmem_limit_bytes=VMEM_LIMIT),
        name="slc_sample",
    )(page_table, idx_flat, cache_t, q_ktg, idxv, kvs_new, win_t, kvw_new, o_c, gates_r, expand)


def _pool_sample_kernel(z_ref, d_ref, *, dec_seq, pos0):
    lane = lax.broadcasted_iota(jnp.int32, (1, POOL_W), 1)
    grp = lane // POOL_GW
    for t in range(dec_seq):
        cur = z_ref[:, POOL_HIST + t, :]
        acc = cur
        sums = {}
        for back in range(1, max(POOL_WINDOWS)):
            acc = acc + z_ref[:, POOL_HIST + t - back, :]
            if back + 1 in POOL_WINDOWS:
                sums[back + 1] = acc
        d = None
        for g, w in enumerate(POOL_WINDOWS):
            val = sums[w] / float(min(w, pos0 + t + 1)) - cur
            d = val if d is None else jnp.where(grp == g, val, d)
        d_ref[:, t, :] = d


def _pool_sample(z_ext, *, dec_seq, pos0):
    n_seq = z_ext.shape[0]
    return pl.pallas_call(
        functools.partial(_pool_sample_kernel, dec_seq=dec_seq, pos0=pos0),
        out_shape=jax.ShapeDtypeStruct((n_seq, dec_seq, POOL_W), F32),
        name="pool_sample",
    )(z_ext)


def _sample_layer(x, lw, cmp_t, slc_t, win_t, win_state, pool_state, page_table, *, layer, n_seq, dec_seq):
    rows = n_seq * dec_seq
    n_kt = N_KV_HEADS * dec_seq
    x = _ffn_ln(x, lw['ffn_in'][0], lw['ffn_out'][0], lw['ln_g'][0], lw['ln_b'][0], tm=rows)
    cos, sin = _rope_tables(PAST_LEN + jnp.arange(rows) % dec_seq)
    q, kvc, kvs, kvw, _, _, gates, gu, gv, p = _inproj(
        x, lw['w_ext'], cos, sin, lw['gm_ln_g'], lw['gm_ln_b'], tm=rows)

    qf = q.astype(F32).reshape(N_KV_HEADS, GROUP, n_seq, dec_seq, LANES)
    q_gkt = qf.transpose(2, 1, 0, 3, 4).reshape(n_seq, GROUP * n_kt, LANES).astype(BF16)
    q_ktg = qf.transpose(2, 0, 3, 1, 4).reshape(n_seq, GROUP * n_kt, LANES)
    o_c, idxv = _cmp_sample(page_table, cmp_t, q_gkt, lw['pe2'], lw['w1kv'], lw['w2kv'],
                            layer=layer, dec_seq=dec_seq, past_len=PAST_LEN)
    o_c = o_c.reshape(n_seq, GROUP, N_KV_HEADS, dec_seq, LANES).transpose(0, 2, 3, 1, 4).reshape(n_seq, GROUP * n_kt, LANES)
    gates_r = gates[:, :GATE_W].reshape(n_seq, dec_seq, N_KV_HEADS, GROUP, 3).transpose(0, 2, 1, 3, 4)
    gates_r = jnp.pad(gates_r.reshape(n_seq, GROUP * n_kt, 3), ((0, 0), (0, 0), (0, LANES - 3)))
    idx_flat = idxv[:, :, :SLC_TOPK].reshape(-1)
    o = _slc_sample(page_table, idx_flat, slc_t, q_ktg, idxv, kvs.reshape(n_seq, dec_seq, KV_W), win_t,
                    kvw.reshape(n_seq, dec_seq, KV_W), o_c, gates_r, layer=layer, dec_seq=dec_seq, past_len=PAST_LEN)
    o = o.reshape(n_seq, N_KV_HEADS, dec_seq, GROUP, LANES)
    o_nsa = jnp.stack([o[:, k, :, :, k * HEAD_DIM:(k + 1) * HEAD_DIM] for k in range(N_KV_HEADS)], axis=2)
    o_nsa = o_nsa.reshape(rows, NSA_W).astype(BF16)

    z_ext = jnp.concatenate([pool_state, p.reshape(n_seq, dec_seq, POOL_W)], axis=1)
    d = _pool_sample(z_ext, dec_seq=dec_seq, pos0=PAST_LEN).reshape(rows, POOL_W)
    ws_cat, gb_full = _gmlp_weights(lw['gm_ws'], lw['gm_b'], dec_seq, GM_CHUNK // dec_seq)
    x = _mix_out(x, o_nsa, gu, gv, d, ws_cat, gb_full, lw['pw_big'], lw['pool_scale'], lw['w_o'],
                 lw['ln_g'][1], lw['ln_b'][1], tm=rows, seq=dec_seq, pool_in_kernel=False)
    x = _ffn_ln(x, lw['ffn_in'][1], lw['ffn_out'][1], lw['ln_g'][2], lw['ln_b'][2], tm=rows)
    shp = (n_seq, dec_seq, 2, N_KV_HEADS, HEAD_DIM)
    new = (kvc.reshape(shp), kvs.reshape(shp),
           jnp.concatenate([win_state[:, dec_seq:], kvw.reshape(shp)], axis=1),
           z_ext[:, dec_seq:], gv.reshape(n_seq, dec_seq, GM_W))
    return x, new


def _pages_by_channel(cache):
    nd = cache.ndim
    t = jnp.transpose(cache, tuple(range(nd - 4)) + (nd - 3, nd - 2, nd - 1, nd - 4))
    return t.reshape(t.shape[:-3] + (KVP, t.shape[-1]))
def _prompt_layer(x, lw, *, batch, seq, tm):
    x = _ffn_ln(x, lw['ffn_in'][0], lw['ffn_out'][0], lw['ln_g'][0], lw['ln_b'][0], tm=tm)
    cos, sin = _rope_tables(jnp.arange(seq))
    q, kvc, kvs, kvw, kvs_b, kvw_b, gates, gu, gv, p = _inproj(
        x, lw['w_ext'], cos, sin, lw['gm_ln_g'], lw['gm_ln_b'], tm=tm)
    kcvc = _compress_prompt(kvc, lw['pe_rows'], lw['w1big'], lw['w2big'], batch=batch)
    o_nsa = _nsa_prompt(q, kcvc, kvs_b, kvw_b, gates, batch=batch, seq=seq)
    ws_cat, gb_full = _gmlp_weights(lw['gm_ws'], lw['gm_b'], GM_CHUNK, 1)
    x = _mix_out(x, o_nsa, gu, gv, p, ws_cat, gb_full, lw['pw_big'], lw['pool_scale'], lw['w_o'],
                 lw['ln_g'][1], lw['ln_b'][1], tm=tm, seq=seq, pool_in_kernel=True)
    x = _ffn_ln(x, lw['ffn_in'][1], lw['ffn_out'][1], lw['ln_g'][2], lw['ln_b'][2], tm=tm)
    wb = min(WINDOW, seq)
    shp = (batch, seq, 2, N_KV_HEADS, HEAD_DIM)
    new = (kvc.reshape(shp), kvs.reshape(shp), kvw.reshape(shp)[:, seq - wb:],
           p.reshape(batch, seq, POOL_W)[:, seq - POOL_HIST:])
    return x, new


def _layer_weights(l, ffn_in_b, ffn_out_b, ln_g, ln_b, w_in, w_o, cmp_pe, cmp_w1, cmp_w2,
                   gm_ln_g, gm_ln_b, gm_ws, gm_b, pool_w, pool_scale):
    pe_rows, w1big, w2big = _compress_weights(cmp_pe[l], cmp_w1[l], cmp_w2[l])
    pe2, w1kv, w2kv = _compress_weights_kv(cmp_pe[l], cmp_w1[l], cmp_w2[l])
    return dict(ffn_in=ffn_in_b[l], ffn_out=ffn_out_b[l], ln_g=ln_g[l], ln_b=ln_b[l],
                w_ext=_build_w_ext(w_in[l]), w_o=w_o[l].astype(BF16),
                pe_rows=pe_rows, w1big=w1big, w2big=w2big, pe2=pe2, w1kv=w1kv, w2kv=w2kv,
                gm_ln_g=gm_ln_g[l], gm_ln_b=gm_ln_b[l], gm_ws=gm_ws[l], gm_b=gm_b[l],
                pw_big=_pool_weights(pool_w[l]), pool_scale=pool_scale[l])


def kernel(x_prompt, x_sample, cache_kv_cmp, cache_kv_slc, state_kv_win, state_pool, page_table, ln_g, ln_b, ffn_w_in, ffn_w_out, w_in, w_o, cmp_pe, cmp_w1, cmp_w2, gm_ln_g, gm_ln_b, gm_ws, gm_b, pool_w, pool_scale):
    batch, seq, _ = x_prompt.shape
    fi = ffn_w_in.astype(BF16)
    fo = ffn_w_out.astype(BF16)
    n_seq, dec_seq, _ = x_sample.shape
    xp = x_prompt.reshape(batch * seq, D_MODEL)
    xs = x_sample.reshape(n_seq * dec_seq, D_MODEL)
    cmp_t = _pages_by_channel(cache_kv_cmp)
    slc_t = _pages_by_channel(cache_kv_slc)
    win_t = _pages_by_channel(state_kv_win)
    new_p, new_s = [], []
    for l in range(DEPTH):
        lw = _layer_weights(l, fi, fo, ln_g, ln_b, w_in, w_o, cmp_pe, cmp_w1, cmp_w2, gm_ln_g, gm_ln_b, gm_ws, gm_b, pool_w, pool_scale)
        xp, st_p = _prompt_layer(xp, lw, batch=batch, seq=seq, tm=512)
        xs, st_s = _sample_layer(xs, lw, cmp_t, slc_t, win_t, state_kv_win[l], state_pool[l], page_table,
                                 layer=l, n_seq=n_seq, dec_seq=dec_seq)
        new_p.append(st_p)
        new_s.append(st_s)
    stk = lambda lst, i: jnp.stack([t[i] for t in lst])
    return (xp.reshape(batch, seq, D_MODEL), xs.reshape(n_seq, dec_seq, D_MODEL),
            stk(new_p, 0), stk(new_s, 0), stk(new_p, 1), stk(new_s, 1),
            stk(new_p, 2), stk(new_s, 2), stk(new_p, 3), stk(new_s, 3), stk(new_s, 4))
```

```python
import functools

import numpy as np
import jax
import jax.numpy as jnp
from jax import lax
from jax.experimental import pallas as pl
from jax.experimental.pallas import tpu as pltpu

F32 = jnp.float32
BF16 = jnp.bfloat16

D_MODEL = 1024
DEPTH = 2
PAST_LEN = 16384
PAGE_SIZE = 128
HEAD_DIM = 64
NSA_W = D_MODEL // 2
GM_W = D_MODEL // 4
POOL_W = D_MODEL // 4
N_HEADS = NSA_W // HEAD_DIM
N_KV_HEADS = 2
GROUP = N_HEADS // N_KV_HEADS
CMP_STRIDE = 16
CMP_LEN = 2 * CMP_STRIDE
SLC_BLOCK = 64
SLC_TOPK = 16
WINDOW = 512
FORCE_SCORE = 1.0e4
ROPE_THETA = 10000.0
SCALE = HEAD_DIM ** -0.5
GM_HEADS = GM_W // HEAD_DIM
GM_CHUNK = 128
POOL_GROUPS = 4
POOL_GW = POOL_W // POOL_GROUPS
POOL_WINDOWS = (2, 4, 8, 16)
POOL_HIST = max(POOL_WINDOWS) - 1
D_FF = 256 * ((8 * D_MODEL // 3 + 255) // 256)
ALPHA = (2 * DEPTH) ** 0.25
LN_EPS = 1e-5
Q_W = N_HEADS * HEAD_DIM
KV_W = 2 * N_KV_HEADS * HEAD_DIM
GATE_W = 3 * N_HEADS
N_IN = Q_W + 3 * KV_W + GATE_W + 2 * GM_W + POOL_W

LANES = 128
KVP = N_KV_HEADS * HEAD_DIM
VMEM_LIMIT = 56 * 1024 * 1024
NEG = -1e30
HALO = 16

_OFF_Q = 0
_OFF_QR = Q_W
_OFF_KV = 2 * Q_W
_OFF_GATE = _OFF_KV + 3 * 3 * KVP
_OFF_UV = _OFF_GATE + LANES
_OFF_P = _OFF_UV + 2 * GM_W
N_EXT = _OFF_P + POOL_W


def _ln_rows(y, g, b):
    mu = jnp.mean(y, axis=-1, keepdims=True)
    d = y - mu
    var = jnp.mean(d * d, axis=-1, keepdims=True)
    return d * lax.rsqrt(var + LN_EPS) * g + b


def _dot(a, b):
    return jnp.dot(a, b, preferred_element_type=F32)


def _dot_t(a, b):
    return lax.dot_general(a, b, (((1,), (1,)), ((), ())), preferred_element_type=F32)


def _ffn_kernel(x_ref, wg_ref, wu_ref, wo_ref, g_ref, b_ref, o_ref, xb_ref, *, n_chunks):
    j = pl.program_id(1)

    @pl.when(j == 0)
    def _():
        xb_ref[...] = x_ref[...].astype(BF16)

    xb = xb_ref[...]
    gate = _dot(xb, wg_ref[...])
    up = _dot(xb, wu_ref[...])
    hid = (gate * jax.nn.sigmoid(gate)) * up
    part = _dot(hid.astype(BF16), wo_ref[...])

    @pl.when(j == 0)
    def _():
        o_ref[...] = part

    if n_chunks > 2:
        @pl.when((j > 0) & (j < n_chunks - 1))
        def _():
            o_ref[...] += part

    @pl.when(j == n_chunks - 1)
    def _():
        y = ALPHA * x_ref[...] + 0.5 * (o_ref[...] + part)
        o_ref[...] = _ln_rows(y, g_ref[...], b_ref[...])


def _ffn_ln(x, w_in_b, w_out_b, g, b, *, tm):
    rows = x.shape[0]
    n_chunks = 2
    fc = D_FF // n_chunks
    return pl.pallas_call(
        functools.partial(_ffn_kernel, n_chunks=n_chunks),
        grid=(rows // tm, n_chunks),
        in_specs=[
            pl.BlockSpec((tm, D_MODEL), lambda i, j: (i, 0)),
            pl.BlockSpec((D_MODEL, fc), lambda i, j: (0, j)),
            pl.BlockSpec((D_MODEL, fc), lambda i, j: (0, n_chunks + j)),
            pl.BlockSpec((fc, D_MODEL), lambda i, j: (j, 0)),
            pl.BlockSpec((1, D_MODEL), lambda i, j: (0, 0)),
            pl.BlockSpec((1, D_MODEL), lambda i, j: (0, 0)),
        ],
        out_specs=pl.BlockSpec((tm, D_MODEL), lambda i, j: (i, 0)),
        out_shape=jax.ShapeDtypeStruct((rows, D_MODEL), F32),
        scratch_shapes=[pltpu.VMEM((tm, D_MODEL), BF16)],
        compiler_params=pltpu.CompilerParams(
            dimension_semantics=("parallel", "arbitrary"), vmem_limit_bytes=VMEM_LIMIT),
        name="ffn_ln",
    )(x, w_in_b, w_in_b, w_out_b, g.reshape(1, D_MODEL), b.reshape(1, D_MODEL))


def _inproj_kernel(h_ref, w_ref, cos_ref, sin_ref, gmg_ref, gmb_ref,
                   q_ref, kvc_ref, kvs_ref, kvw_ref, kvsb_ref, kvwb_ref, gate_ref, gu_ref, gv_ref, p_ref):
    hb = h_ref[...].astype(BF16)
    cos = cos_ref[...]
    sin = sin_ref[...]
    lane = lax.broadcasted_iota(jnp.int32, (1, LANES), 1)
    low = lane < HEAD_DIM

    zq = _dot(hb, w_ref[:, _OFF_Q:_OFF_KV])
    zk = _dot(hb, w_ref[:, _OFF_KV:_OFF_UV])
    zu = _dot(hb, w_ref[:, _OFF_UV:N_EXT])

    for m in range(N_HEADS // 2):
        c0 = m * LANES
        pair = (zq[:, c0:c0 + LANES] * cos + zq[:, Q_W + c0:Q_W + c0 + LANES] * sin) * SCALE
        swapped = pltpu.roll(pair, HEAD_DIM, 1)
        kvh = (2 * m) // GROUP
        for e in range(2):
            h = 2 * m + e
            src = pair if e == kvh else swapped
            keep = low if kvh == 0 else jnp.logical_not(low)
            q_ref[h] = jnp.where(keep, src, 0.0).astype(BF16)

    for br, (f_ref, b_ref) in enumerate(((kvc_ref, None), (kvs_ref, kvsb_ref), (kvw_ref, kvwb_ref))):
        c0 = br * 3 * KVP
        k = zk[:, c0:c0 + KVP] * cos + zk[:, c0 + KVP:c0 + 2 * KVP] * sin
        v = zk[:, c0 + 2 * KVP:c0 + 3 * KVP]
        f_ref[:, 0:KVP] = k
        f_ref[:, KVP:2 * KVP] = v
        if b_ref is not None:
            b_ref[:, 0:KVP] = k.astype(BF16)
            b_ref[:, KVP:2 * KVP] = v.astype(BF16)

    gate_ref[...] = jax.nn.sigmoid(zk[:, _OFF_GATE - _OFF_KV:_OFF_UV - _OFF_KV])
    gu_ref[...] = jax.nn.gelu(zu[:, 0:GM_W])
    gv_ref[...] = _ln_rows(jax.nn.gelu(zu[:, GM_W:2 * GM_W]), gmg_ref[...], gmb_ref[...])
    p_ref[...] = zu[:, 2 * GM_W:2 * GM_W + POOL_W]


def _inproj(h, w_ext, cos, sin, gmg, gmb, *, tm):
    rows = h.shape[0]
    n_tab = cos.shape[0] // tm
    row_spec = lambda w: pl.BlockSpec((tm, w), lambda i: (i, 0))
    tab_spec = pl.BlockSpec((tm, LANES), lambda i: (i % n_tab, 0))
    vec_spec = pl.BlockSpec((1, GM_W), lambda i: (0, 0))
    sds = jax.ShapeDtypeStruct
    return pl.pallas_call(
        _inproj_kernel,
        grid=(rows // tm,),
        in_specs=[row_spec(D_MODEL), pl.BlockSpec((D_MODEL, N_EXT), lambda i: (0, 0)),
                  tab_spec, tab_spec, vec_spec, vec_spec],
        out_specs=[pl.BlockSpec((N_HEADS, tm, LANES), lambda i: (0, i, 0)),
                   row_spec(KV_W), row_spec(KV_W), row_spec(KV_W), row_spec(KV_W), row_spec(KV_W),
                   row_spec(LANES), row_spec(GM_W), row_spec(GM_W), row_spec(POOL_W)],
        out_shape=[sds((N_HEADS, rows, LANES), BF16),
                   sds((rows, KV_W), F32), sds((rows, KV_W), F32), sds((rows, KV_W), F32),
                   sds((rows, KV_W), BF16), sds((rows, KV_W), BF16),
                   sds((rows, LANES), F32), sds((rows, GM_W), F32), sds((rows, GM_W), F32),
                   sds((rows, POOL_W), F32)],
        compiler_params=pltpu.CompilerParams(dimension_semantics=("parallel",), vmem_limit_bytes=VMEM_LIMIT),
        name="inproj",
    )(h, w_ext, cos, sin, gmg.reshape(1, GM_W), gmb.reshape(1, GM_W))


def _build_w_ext(w_in):
    half = HEAD_DIM // 2

    def rot(w):
        n = w.shape[1] // HEAD_DIM
        w3 = w.reshape(D_MODEL, n, 2, half)
        return jnp.stack([-w3[:, :, 1], w3[:, :, 0]], axis=2).reshape(D_MODEL, n * HEAD_DIM)

    q = w_in[:, :Q_W]
    cols = [q, rot(q)]
    for br in range(3):
        kv = w_in[:, Q_W + br * KV_W:Q_W + (br + 1) * KV_W]
        k, v = kv[:, :KVP], kv[:, KVP:]
        cols += [k, rot(k), v]
    g0 = Q_W + 3 * KV_W
    cols.append(jnp.pad(w_in[:, g0:g0 + GATE_W], ((0, 0), (0, LANES - GATE_W))))
    cols.append(w_in[:, g0 + GATE_W:])
    return jnp.concatenate(cols, axis=1).astype(BF16)


def _rope_tables(pos):
    half = HEAD_DIM // 2
    inv = ROPE_THETA ** (-jnp.arange(half, dtype=F32) / half)
    ang = pos.astype(F32)[:, None] * inv[None, :]
    cos = jnp.tile(jnp.cos(ang), (1, LANES // half))
    sin = jnp.tile(jnp.sin(ang), (1, LANES // half))
    return cos, sin


def _compress_kernel(x_ref, pe_ref, w1_ref, w2_ref, o_ref):
    w1 = w1_ref[...]
    fs = _dot(x_ref[...].astype(BF16), w1)
    per = _dot(pe_ref[...].astype(BF16), w1)
    pe_term = per[0:1, 0:2 * KVP] + per[1:2, 2 * KVP:4 * KVP]
    first = fs[:, 0:2 * KVP]
    second = fs[:, 2 * KVP:4 * KVP]
    nxt = jnp.concatenate([second[1:], jnp.zeros((1, 2 * KVP), F32)], axis=0)
    hid = jax.nn.gelu(first + nxt + pe_term)
    o_ref[...] = _dot(hid.astype(BF16), w2_ref[...]).astype(BF16)


def _compress_weights(pe, w1, w2):
    eye = jnp.eye(N_KV_HEADS, dtype=F32)
    w1r = w1.reshape(2, 2, CMP_STRIDE, HEAD_DIM, HEAD_DIM)
    w1big = jnp.einsum('ksjde,kq,hg->jkhdsqge', w1r, eye, eye).reshape(CMP_STRIDE * 2 * KVP, 4 * KVP)
    w2big = jnp.einsum('ked,kq,hg->kheqgd', w2, eye, eye).reshape(2 * KVP, 2 * KVP)
    per = pe.reshape(2, 2, CMP_STRIDE, HEAD_DIM).transpose(1, 2, 0, 3)
    per = jnp.broadcast_to(per[:, :, :, None, :], (2, CMP_STRIDE, 2, N_KV_HEADS, HEAD_DIM))
    pe_rows = jnp.pad(per.reshape(2, CMP_STRIDE * 2 * KVP), ((0, 6), (0, 0)))
    return pe_rows, w1big.astype(BF16), w2big.astype(BF16)


def _compress_prompt(kvc, pe_rows, w1big, w2big, *, batch):
    nsub = kvc.shape[0] // batch // CMP_STRIDE
    width = CMP_STRIDE * 2 * KVP
    x = kvc.reshape(batch, nsub, width)
    return pl.pallas_call(
        _compress_kernel,
        grid=(batch,),
        in_specs=[pl.BlockSpec((None, nsub, width), lambda b: (b, 0, 0)),
                  pl.BlockSpec((8, width), lambda b: (0, 0)),
                  pl.BlockSpec((width, 4 * KVP), lambda b: (0, 0)),
                  pl.BlockSpec((2 * KVP, 2 * KVP), lambda b: (0, 0))],
        out_specs=pl.BlockSpec((None, nsub, 2 * KVP), lambda b: (b, 0, 0)),
        out_shape=jax.ShapeDtypeStruct((batch, nsub, 2 * KVP), BF16),
        compiler_params=pltpu.CompilerParams(dimension_semantics=("parallel",), vmem_limit_bytes=VMEM_LIMIT),
        name="compress_prompt",
    )(x, pe_rows, w1big, w2big)


def _top_blocks_cols(score, blk):
    sel = jnp.zeros(score.shape, F32)
    for _ in range(SLC_TOPK):
        m = jnp.max(score, axis=0, keepdims=True)
        first = jnp.min(jnp.where(score == m, blk, 1e9), axis=0, keepdims=True)
        hit = blk == first
        sel = jnp.where(hit, 1.0, sel)
        score = jnp.where(hit, -jnp.inf, score)
    return sel


def _top_blocks_idx(score, blk):
    sel = jnp.zeros(score.shape, F32)
    idx = jnp.zeros((score.shape[0], LANES), jnp.int32)
    lane = lax.broadcasted_iota(jnp.int32, (1, LANES), 1)
    big = jnp.int32(1 << 20)
    for it in range(SLC_TOPK):
        m = jnp.max(score, axis=-1, keepdims=True)
        first = jnp.min(jnp.where(score == m, blk, big), axis=-1, keepdims=True)
        hit = blk == first
        sel = jnp.where(hit, 1.0, sel)
        idx = jnp.where(lane == it, first, idx)
        score = jnp.where(hit, -jnp.inf, score)
    return sel, idx


def _softmax_rows(s, mask):
    s = jnp.where(mask, s, NEG)
    m = jnp.max(s, axis=-1, keepdims=True)
    e = jnp.exp(s - m)
    return jnp.where(mask, e / jnp.sum(e, axis=-1, keepdims=True), 0.0)


def _nsa_prompt_kernel(q_ref, kcvc_ref, kvs_ref, kvw_ref, gate_ref, selmap_ref, expand_ref, o_ref,
                       *, tq, tk, seq, n_sel_blocks):
    t0 = pl.program_id(1) * tq
    nsub = kcvc_ref.shape[0]
    nb = 8 * ((n_sel_blocks + 7) // 8)
    qpos = t0 + lax.broadcasted_iota(jnp.int32, (tq, 1), 0)
    qpos_l = t0 + lax.broadcasted_iota(jnp.int32, (1, tq), 1)
    lane = lax.broadcasted_iota(jnp.int32, (1, LANES), 1)
    low = lane < HEAD_DIM
    gates = gate_ref[...]
    span = min(WINDOW + tq, seq)

    cmp_end = CMP_STRIDE * lax.broadcasted_iota(jnp.int32, (1, nsub), 1) + (CMP_LEN - 1)
    c_bias = jnp.where(cmp_end <= qpos, 0.0, NEG)
    any_cmp = qpos >= CMP_LEN - 1
    any_cmp_l = qpos_l >= CMP_LEN - 1
    start = pl.multiple_of(jnp.maximum(t0 + tq - span, 0), tq)
    dist = qpos - (start + lax.broadcasted_iota(jnp.int32, (1, span), 1))
    w_bias = jnp.where(dist >= 0, jnp.where(dist <= WINDOW, 0.0, NEG), NEG)
    blk = lax.broadcasted_iota(jnp.int32, (nb, 1), 0)
    blk_f = blk.astype(F32)
    cur_l = qpos_l // SLC_BLOCK
    forced = (blk == 0) | (blk == cur_l) | (blk == cur_l - 1)
    n_kt = (t0 + tq + tk - 1) // tk

    outs = []
    for k in range(N_KV_HEADS):
        qk = q_ref[GROUP * k:GROUP * (k + 1)].reshape(GROUP * tq, LANES)

        s = _dot_t(qk, kcvc_ref[:, 0:KVP]).reshape(GROUP, tq, nsub) + c_bias[None]
        e = jnp.exp(s - jnp.max(s, axis=-1, keepdims=True))
        inv_c = jnp.where(any_cmp, 1.0 / jnp.sum(e, axis=-1, keepdims=True), 0.0)
        eb = e.astype(BF16).reshape(GROUP * tq, nsub)
        o_c = _dot(eb, kcvc_ref[:, KVP:2 * KVP]).reshape(GROUP, tq, LANES) * inv_c

        imp_t = _dot_t(selmap_ref[...], eb)
        imp = None
        for g in range(GROUP):
            part = imp_t[:, g * tq:(g + 1) * tq]
            part = part[0:nb] * (1.0 / part[LANES - 1:LANES])
            imp = part if imp is None else imp + part
        imp = jnp.where(any_cmp_l, imp, 0.0)
        score = jnp.where(forced, FORCE_SCORE, jnp.where(blk <= cur_l, imp, -1.0))
        if nb > n_sel_blocks:
            score = jnp.where(blk < n_sel_blocks, score, -jnp.inf)
        sel_t = _top_blocks_cols(score, blk_f)
        sel_m1 = jnp.concatenate([sel_t - 1.0, jnp.zeros((LANES - nb, tq), F32)], axis=0).T.astype(BF16)

        def tile(kt, carry, diagonal):
            m_i, l_i, acc = carry
            r0 = pl.multiple_of(kt * tk, tk)
            bias = _dot(sel_m1, expand_ref[kt])
            if diagonal:
                kpos = r0 + lax.broadcasted_iota(jnp.int32, (1, tk), 1)
                bias = jnp.where(kpos <= qpos, bias, NEG)
            s = _dot_t(qk, kvs_ref[pl.ds(r0, tk), 0:KVP]).reshape(GROUP, tq, tk) + bias[None]
            m_new = jnp.maximum(m_i, jnp.max(s, axis=-1, keepdims=True))
            a = jnp.exp(m_i - m_new)
            e = jnp.exp(s - m_new)
            l_new = a * l_i + jnp.sum(e, axis=-1, keepdims=True)
            pv = _dot(e.astype(BF16).reshape(GROUP * tq, tk), kvs_ref[pl.ds(r0, tk), KVP:2 * KVP])
            return m_new, l_new, a * acc + pv.reshape(GROUP, tq, LANES)

        init = (jnp.full((GROUP, tq, 1), NEG, F32), jnp.zeros((GROUP, tq, 1), F32),
                jnp.zeros((GROUP, tq, LANES), F32))
        carry = lax.fori_loop(0, n_kt - 1, lambda kt, c: tile(kt, c, False), init)
        _, l_s, acc_s = tile(n_kt - 1, carry, True)
        o_s = acc_s * (1.0 / l_s)

        s = _dot_t(qk, kvw_ref[pl.ds(start, span), 0:KVP]).reshape(GROUP, tq, span) + w_bias[None]
        e = jnp.exp(s - jnp.max(s, axis=-1, keepdims=True))
        inv_w = 1.0 / jnp.sum(e, axis=-1, keepdims=True)
        o_w = _dot(e.astype(BF16).reshape(GROUP * tq, span),
                   kvw_ref[pl.ds(start, span), KVP:2 * KVP]).reshape(GROUP, tq, LANES) * inv_w

        for g in range(GROUP):
            c = 3 * (GROUP * k + g)
            outs.append(gates[:, c:c + 1] * o_c[g] + gates[:, c + 1:c + 2] * o_s[g] + gates[:, c + 2:c + 3] * o_w[g])

    for m in range(N_HEADS // 2):
        k = (2 * m) // GROUP
        a, b = outs[2 * m], outs[2 * m + 1]
        if k == 0:
            b = pltpu.roll(b, HEAD_DIM, 1)
        else:
            a = pltpu.roll(a, HEAD_DIM, 1)
        o_ref[:, m * LANES:(m + 1) * LANES] = jnp.where(low, a, b).astype(o_ref.dtype)


def _sel_map_t(nc_rows, n_cmp, ns):
    c0 = CMP_STRIDE * np.arange(nc_rows)[None, :]
    s0 = SLC_BLOCK * np.arange(LANES)[:, None]
    ov = np.clip(np.minimum(c0 + CMP_LEN, s0 + SLC_BLOCK) - np.maximum(c0, s0), 0, None) / CMP_LEN
    ov = ov * (np.arange(nc_rows)[None, :] < n_cmp) * (np.arange(LANES)[:, None] < ns)
    ov[LANES - 1, :] = 1.0
    return jnp.asarray(ov, dtype=BF16)


def _expand_map(seq, tk):
    key_blk = np.arange(seq) // SLC_BLOCK
    e = (np.arange(LANES)[:, None] == key_blk[None, :]).astype(np.float32) * -NEG
    return jnp.asarray(e.reshape(LANES, seq // tk, tk).transpose(1, 0, 2), dtype=BF16)


def _nsa_prompt(q, kcvc, kvs_b, kvw_b, gates, *, batch, seq, tq=128, tk=512):
    tk = min(tk, seq)
    nq = seq // tq
    nsub = seq // CMP_STRIDE
    ns = seq // SLC_BLOCK
    assert ns < LANES and seq % tk == 0 and tk % tq == 0
    selmap = _sel_map_t(nsub, nsub - 1, ns)
    expand = _expand_map(seq, tk)
    kern = functools.partial(_nsa_prompt_kernel, tq=tq, tk=tk, seq=seq, n_sel_blocks=ns)
    return pl.pallas_call(
        kern,
        grid=(batch, nq),
        in_specs=[pl.BlockSpec((N_HEADS, tq, LANES), lambda b, i: (0, b * nq + i, 0)),
                  pl.BlockSpec((None, nsub, 2 * KVP), lambda b, i: (b, 0, 0)),
                  pl.BlockSpec((seq, KV_W), lambda b, i: (b, 0)),
                  pl.BlockSpec((seq, KV_W), lambda b, i: (b, 0)),
                  pl.BlockSpec((tq, LANES), lambda b, i: (b * nq + i, 0)),
                  pl.BlockSpec((LANES, nsub), lambda b, i: (0, 0)),
                  pl.BlockSpec((seq // tk, LANES, tk), lambda b, i: (0, 0, 0))],
        out_specs=pl.BlockSpec((tq, NSA_W), lambda b, i: (b * nq + i, 0)),
        out_shape=jax.ShapeDtypeStruct((batch * seq, NSA_W), BF16),
        compiler_params=pltpu.CompilerParams(
            dimension_semantics=("parallel", "arbitrary"), vmem_limit_bytes=VMEM_LIMIT),
        name="nsa_prompt",
    )(q, kcvc, kvs_b, kvw_b, gates, selmap, expand)


def _pool_windows(z_ext, tm):
    s2 = z_ext[1:] + z_ext[:-1]
    s4 = s2[2:] + s2[:-2]
    s8 = s4[4:] + s4[:-4]
    s16 = s8[8:] + s8[:-8]
    return (s2[HALO - 1:HALO - 1 + tm], s4[HALO - 3:HALO - 3 + tm], s8[HALO - 7:HALO - 7 + tm],
            s16[HALO - 15:HALO - 15 + tm])


def _mix_out_kernel(*refs, tm, tiles_per_seq, pool_in_kernel):
    if pool_in_kernel:
        (x_ref, nsa_ref, gu_ref, gv_ref, p_ref, halo_ref, ws_ref, gb_ref, pw_ref, ps_ref, wo_ref,
         g_ref, b_ref, o_ref) = refs
    else:
        (x_ref, nsa_ref, gu_ref, gv_ref, d_ref, ws_ref, gb_ref, pw_ref, ps_ref, wo_ref,
         g_ref, b_ref, o_ref) = refs
    lane = lax.broadcasted_iota(jnp.int32, (1, GM_W), 1)

    parts = []
    for c in range(tm // GM_CHUNK):
        v = gv_ref[c * GM_CHUNK:(c + 1) * GM_CHUNK, :]
        stacked = jnp.concatenate(
            [jnp.where(lane // HEAD_DIM == h, v, 0.0) for h in range(GM_HEADS)], axis=0).astype(BF16)
        s = _dot(ws_ref[...], stacked) + gb_ref[...]
        parts.append(gu_ref[c * GM_CHUNK:(c + 1) * GM_CHUNK, :] * s)
    o_gm = parts[0] if len(parts) == 1 else jnp.concatenate(parts, axis=0)

    if pool_in_kernel:
        first_tile = (pl.program_id(0) % tiles_per_seq) == 0
        halo = jnp.where(first_tile, 0.0, halo_ref[...])
        z = p_ref[...]
        wins = _pool_windows(jnp.concatenate([halo, z], axis=0), tm)
        pos = (pl.program_id(0) % tiles_per_seq) * tm + lax.broadcasted_iota(jnp.int32, (tm, 1), 0)
        grp = lane // POOL_GW
        wsum = jnp.where(grp == 0, wins[0], jnp.where(grp == 1, wins[1], jnp.where(grp == 2, wins[2], wins[3])))
        width = jnp.where(grp == 0, POOL_WINDOWS[0], jnp.where(grp == 1, POOL_WINDOWS[1],
                          jnp.where(grp == 2, POOL_WINDOWS[2], POOL_WINDOWS[3])))
        cnt = jnp.minimum(width, pos + 1).astype(F32)
        d = wsum / cnt - z
    else:
        d = d_ref[...]
    o_pool = _dot(d.astype(BF16), pw_ref[...]) * ps_ref[...]

    mixed = jnp.concatenate([nsa_ref[...], o_gm.astype(BF16), o_pool.astype(BF16)], axis=1)
    y = ALPHA * x_ref[...] + _dot(mixed, wo_ref[...])
    o_ref[...] = _ln_rows(y, g_ref[...], b_ref[...])


def _mix_out(x, o_nsa, gu, gv, p_or_d, ws_cat, gb_full, pw_big, ps, w_o_b, g, b, *, tm, seq, pool_in_kernel):
    rows = x.shape[0]
    tiles_per_seq = max(seq // tm, 1)
    row_spec = lambda w: pl.BlockSpec((tm, w), lambda i: (i, 0))
    const = lambda shp: pl.BlockSpec(shp, lambda i: (0,) * len(shp))
    in_specs = [row_spec(D_MODEL), row_spec(NSA_W), row_spec(GM_W), row_spec(GM_W), row_spec(POOL_W)]
    args = [x, o_nsa, gu, gv, p_or_d]
    if pool_in_kernel:
        in_specs.append(pl.BlockSpec((HALO, POOL_W), lambda i: (jnp.maximum(i * (tm // HALO) - 1, 0), 0)))
        args.append(p_or_d)
    in_specs += [const((GM_CHUNK, GM_HEADS * GM_CHUNK)), const((GM_CHUNK, GM_W)), const((POOL_W, POOL_W)),
                 const((1, POOL_W)), const((D_MODEL, D_MODEL)), const((1, D_MODEL)), const((1, D_MODEL))]
    args += [ws_cat, gb_full, pw_big, ps.reshape(1, POOL_W), w_o_b, g.reshape(1, D_MODEL), b.reshape(1, D_MODEL)]
    kern = functools.partial(_mix_out_kernel, tm=tm, tiles_per_seq=tiles_per_seq, pool_in_kernel=pool_in_kernel)
    return pl.pallas_call(
        kern,
        grid=(rows // tm,),
        in_specs=in_specs,
        out_specs=row_spec(D_MODEL),
        out_shape=jax.ShapeDtypeStruct((rows, D_MODEL), F32),
        compiler_params=pltpu.CompilerParams(dimension_semantics=("parallel",), vmem_limit_bytes=VMEM_LIMIT),
        name="mix_out_prompt" if pool_in_kernel else "mix_out_sample",
    )(*args)


def _gmlp_weights(ws, gb, chunk_rows, reps):
    wm = jnp.tril(ws[:, :chunk_rows, :chunk_rows])
    bias = gb[:, :chunk_rows]
    if reps > 1:
        eye = jnp.eye(reps, dtype=F32)
        wm = jnp.einsum('hts,ab->hatbs', wm, eye).reshape(GM_HEADS, reps * chunk_rows, reps * chunk_rows)
        bias = jnp.tile(bias, (1, reps))
    ws_cat = wm.transpose(1, 0, 2).reshape(GM_CHUNK, GM_HEADS * GM_CHUNK).astype(BF16)
    gb_full = jnp.repeat(bias.T, HEAD_DIM, axis=1)
    return ws_cat, gb_full


def _pool_weights(pw):
    eye = jnp.eye(POOL_GROUPS, dtype=F32)
    return jnp.einsum('gce,gq->gcqe', pw, eye).reshape(POOL_W, POOL_W).astype(BF16)


PAGES_PER_STEP = 32
SUBS_PER_PAGE = PAGE_SIZE // CMP_STRIDE


def _cmp_sample_kernel(pt_ref, cache_ref, q_ref, pe_ref, w1_ref, w2_ref, selmap_ref, oc_ref, idx_ref,
                       pbuf, sem, xt_ref, fs_ref, *, layer, n_chunks, n_seq, dec_seq, past_len, n_sel_blocks):
    b = pl.program_id(0)
    c = pl.program_id(1)
    step = b * n_chunks + c
    slot = lax.rem(step, 2)
    pps = PAGES_PER_STEP

    def page_copies(sb, sc, sl):
        return [pltpu.make_async_copy(cache_ref.at[layer, pt_ref[sb, sc * pps + p]], pbuf.at[sl, p], sem.at[sl])
                for p in range(pps)]

    @pl.when(step == 0)
    def _():
        for cp in page_copies(b, c, slot):
            cp.start()

    @pl.when(step + 1 < n_seq * n_chunks)
    def _():
        wrap = c + 1 == n_chunks
        for cp in page_copies(jnp.where(wrap, b + 1, b), jnp.where(wrap, 0, c + 1), 1 - slot):
            cp.start()

    for cp in page_copies(b, c, slot):
        cp.wait()

    def to_rows(p, carry):
        r0 = pl.multiple_of(p * PAGE_SIZE, PAGE_SIZE)
        for kv in range(2):
            xt_ref[kv, pl.ds(r0, PAGE_SIZE), :] = pbuf[slot, p, kv].T
        return carry

    lax.fori_loop(0, pps, to_rows, 0, unroll=8)

    subs = pps * SUBS_PER_PAGE
    s0 = pl.multiple_of(c * subs, subs)
    for kv in range(2):
        xr = jnp.concatenate(
            [xt_ref[kv, pl.ds(j, subs, stride=CMP_STRIDE), :].astype(BF16) for j in range(CMP_STRIDE)], axis=1)
        fs_ref[kv, pl.ds(s0, subs), :] = _dot(xr, w1_ref[kv])

    @pl.when(c == n_chunks - 1)
    def _():
        nsub = n_chunks * subs
        kcv = []
        for kv in range(2):
            f = fs_ref[kv]
            per = _dot(pe_ref[kv].astype(BF16), w1_ref[kv])
            pe_term = per[0:1, 0:KVP] + per[1:2, KVP:2 * KVP]
            nxt = jnp.concatenate([f[1:, KVP:2 * KVP], jnp.zeros((1, KVP), F32)], axis=0)
            hid = jax.nn.gelu(f[:, 0:KVP] + nxt + pe_term)
            kcv.append(_dot(hid.astype(BF16), w2_ref[kv]).astype(BF16))
        n_kt = N_KV_HEADS * dec_seq
        rows = GROUP * n_kt
        q = q_ref[...]
        qpos = past_len + lax.rem(lax.broadcasted_iota(jnp.int32, (rows, 1), 0), dec_seq)
        cmp_end = CMP_STRIDE * lax.broadcasted_iota(jnp.int32, (1, nsub), 1) + (CMP_LEN - 1)
        p = _softmax_rows(_dot_t(q, kcv[0]), cmp_end <= qpos).astype(BF16)
        oc_ref[...] = _dot(p, kcv[1])
        imp_g = _dot(p, selmap_ref[...])
        imp = imp_g[0:n_kt]
        for g in range(1, GROUP):
            imp = imp + imp_g[g * n_kt:(g + 1) * n_kt]
        blk = lax.broadcasted_iota(jnp.int32, (1, imp.shape[1]), 1)
        cur = qpos[0:n_kt] // SLC_BLOCK
        forced = (blk == 0) | (blk == cur) | (blk == cur - 1)
        score = jnp.where(forced, FORCE_SCORE, jnp.where(blk <= cur, imp, -1.0))
        score = jnp.where(blk < n_sel_blocks, score, -jnp.inf)
        idx_ref[...] = _top_blocks_idx(score, blk)[1]


def _cmp_sample(page_table, cache_t, q_gkt, pe2, w1kv, w2kv, *, layer, dec_seq, past_len):
    n_seq, n_pages = page_table.shape
    n_chunks = n_pages // PAGES_PER_STEP
    nsub = n_pages * SUBS_PER_PAGE
    ns = (past_len + dec_seq + SLC_BLOCK - 1) // SLC_BLOCK
    ns_pad = LANES * ((ns + LANES - 1) // LANES)
    n_kt = N_KV_HEADS * dec_seq
    rows = GROUP * n_kt
    c0 = CMP_STRIDE * np.arange(nsub)[:, None]
    s0 = SLC_BLOCK * np.arange(ns_pad)[None, :]
    ov = np.clip(np.minimum(c0 + CMP_LEN, s0 + SLC_BLOCK) - np.maximum(c0, s0), 0, None) / CMP_LEN
    ov = ov * (np.arange(nsub)[:, None] < nsub - 1) * (np.arange(ns_pad)[None, :] < ns)
    selmap = jnp.asarray(ov, dtype=BF16)
    width = CMP_STRIDE * KVP
    kern = functools.partial(_cmp_sample_kernel, layer=layer, n_chunks=n_chunks, n_seq=n_seq, dec_seq=dec_seq,
                             past_len=past_len, n_sel_blocks=ns)
    const = lambda shp: pl.BlockSpec(shp, lambda b, c, pt: (0,) * len(shp))
    grid_spec = pltpu.PrefetchScalarGridSpec(
        num_scalar_prefetch=1,
        grid=(n_seq, n_chunks),
        in_specs=[pl.BlockSpec(memory_space=pl.ANY),
                  pl.BlockSpec((None, rows, LANES), lambda b, c, pt: (b, 0, 0)),
                  const((2, 8, width)), const((2, width, 2 * KVP)), const((2, KVP, KVP)), const((nsub, ns_pad))],
        out_specs=[pl.BlockSpec((None, rows, LANES), lambda b, c, pt: (b, 0, 0)),
                   pl.BlockSpec((None, n_kt, LANES), lambda b, c, pt: (b, 0, 0))],
        scratch_shapes=[pltpu.VMEM((2, PAGES_PER_STEP, 2, KVP, PAGE_SIZE), F32),
                        pltpu.SemaphoreType.DMA((2,)),
                        pltpu.VMEM((2, PAGES_PER_STEP * PAGE_SIZE, KVP), F32),
                        pltpu.VMEM((2, nsub, 2 * KVP), F32)])
    return pl.pallas_call(
        kern,
        grid_spec=grid_spec,
        out_shape=[jax.ShapeDtypeStruct((n_seq, rows, LANES), F32),
                   jax.ShapeDtypeStruct((n_seq, n_kt, LANES), jnp.int32)],
        compiler_params=pltpu.CompilerParams(
            dimension_semantics=("arbitrary", "arbitrary"), vmem_limit_bytes=VMEM_LIMIT),
        name="cmp_sample",
    )(page_table, cache_t, q_gkt, pe2, w1kv, w2kv, selmap)


def _compress_weights_kv(pe, w1, w2):
    eye = jnp.eye(N_KV_HEADS, dtype=F32)
    w1r = w1.reshape(2, 2, CMP_STRIDE, HEAD_DIM, HEAD_DIM)
    w1kv = jnp.einsum('ksjde,hg->kjhdsge', w1r, eye).reshape(2, CMP_STRIDE * KVP, 2 * KVP)
    w2kv = jnp.einsum('ked,hg->khegd', w2, eye).reshape(2, KVP, KVP)
    per = pe.reshape(2, 2, CMP_STRIDE, HEAD_DIM)
    per = jnp.broadcast_to(per[:, :, :, None, :], (2, 2, CMP_STRIDE, N_KV_HEADS, HEAD_DIM))
    pe2 = jnp.pad(per.reshape(2, 2, CMP_STRIDE * KVP), ((0, 0), (0, 6), (0, 0)))
    return pe2, w1kv.astype(BF16), w2kv.astype(BF16)


def _slc_sample_kernel(pt_ref, idx_sm_ref, cache_ref, q_ref, idxv_ref, knew_ref, win_ref, wnew_ref, oc_ref,
                       gate_ref, expand_ref, o_ref, kbuf, vbuf, sem,
                       *, layer, n_seq, n_pages, dec_seq, past_len):
    b = pl.program_id(0)
    slot = lax.rem(b, 2)
    n_kt = N_KV_HEADS * dec_seq
    rows = GROUP * n_kt
    n_past_blocks = past_len // SLC_BLOCK
    per_head = dec_seq * SLC_TOPK

    def tile_copies(sb, sl, k, i):
        kt = k * dec_seq + i // SLC_TOPK
        s = lax.rem(i, SLC_TOPK)
        j = idx_sm_ref[(sb * n_kt + kt) * SLC_TOPK + s]
        phys = pt_ref[sb, jnp.minimum(lax.shift_right_logical(j, 1), n_pages - 1)]
        return [pltpu.make_async_copy(cache_ref.at[layer, phys, kv, pl.ds(k * HEAD_DIM, HEAD_DIM), :],
                                      buf.at[sl, kt, s], sem.at[sl]) for kv, buf in ((0, kbuf), (1, vbuf))]

    def start_all(sb, sl):
        for k in range(N_KV_HEADS):
            def body(i, carry):
                for cp in tile_copies(sb, sl, k, i):
                    cp.start()
                return carry
            lax.fori_loop(0, per_head, body, 0)

    @pl.when(b == 0)
    def _():
        start_all(b, slot)

    @pl.when(b + 1 < n_seq)
    def _():
        start_all(b + 1, 1 - slot)

    for k in range(N_KV_HEADS):
        def wait_body(i, carry):
            for cp in tile_copies(b, slot, k, i):
                cp.wait()
            return carry
        lax.fori_loop(0, per_head, wait_body, 0)

    q = q_ref[...]
    qb = q.astype(BF16)
    gates = gate_ref[...]
    t_row = lax.rem(lax.broadcasted_iota(jnp.int32, (rows, 1), 0) // GROUP, dec_seq)
    t_new = lax.broadcasted_iota(jnp.int32, (1, dec_seq), 1)
    new_ok = t_new <= t_row

    wb = win_ref.shape[2]
    kpos = past_len - wb + lax.broadcasted_iota(jnp.int32, (1, wb), 1)
    dist = past_len + t_row - kpos
    w_ok = (dist >= 0) & (dist <= WINDOW) & (kpos >= 0)
    s_w = jnp.where(w_ok, _dot(qb, win_ref[0].astype(BF16)), NEG)
    s_n = jnp.where(new_ok, _dot_t(qb, wnew_ref[:, 0:KVP].astype(BF16)), NEG)
    m = jnp.maximum(jnp.max(s_w, axis=-1, keepdims=True), jnp.max(s_n, axis=-1, keepdims=True))
    e_w = jnp.where(w_ok, jnp.exp(s_w - m), 0.0)
    e_n = jnp.where(new_ok, jnp.exp(s_n - m), 0.0)
    den = jnp.sum(e_w, axis=-1, keepdims=True) + jnp.sum(e_n, axis=-1, keepdims=True)
    o_w = (_dot_t(e_w.astype(BF16), win_ref[1].astype(BF16))
           + _dot(e_n.astype(BF16), wnew_ref[:, KVP:2 * KVP].astype(BF16))) / den

    idxv = idxv_ref[...]
    lane16 = lax.broadcasted_iota(jnp.int32, (1, LANES), 1) < SLC_TOPK
    half = jnp.where(lane16 & ((idxv & 1) == 1), 1.0, 0.0).astype(BF16)
    live = jnp.where(lane16 & (idxv < n_past_blocks), 1.0, 0.0).astype(BF16)
    half_x = _dot(half, expand_ref[...])
    live_x = _dot(live, expand_ref[...])
    col = lax.broadcasted_iota(jnp.int32, (1, SLC_TOPK * PAGE_SIZE), 1)
    col_half = (lax.rem(col, PAGE_SIZE) // SLC_BLOCK).astype(F32)
    tile_ok = (live_x > 0.5) & (half_x == col_half)
    s_new = _dot_t(qb, knew_ref[:, 0:KVP].astype(BF16))
    zeros_half = jnp.zeros((GROUP, HEAD_DIM), F32)
    o_parts = []
    for kt in range(n_kt):
        k = kt // dec_seq
        r0 = kt * GROUP
        qk = q[r0:r0 + GROUP, k * HEAD_DIM:(k + 1) * HEAD_DIM].astype(BF16)
        kcat = jnp.concatenate([kbuf[slot, kt, s] for s in range(SLC_TOPK)], axis=1).astype(BF16)
        vcat = jnp.concatenate([vbuf[slot, kt, s] for s in range(SLC_TOPK)], axis=1).astype(BF16)
        ok = tile_ok[kt:kt + 1]
        nok = new_ok[r0:r0 + GROUP]
        s_s = jnp.where(ok, _dot(qk, kcat), NEG)
        s_n = jnp.where(nok, s_new[r0:r0 + GROUP], NEG)
        m = jnp.maximum(jnp.max(s_s, axis=-1, keepdims=True), jnp.max(s_n, axis=-1, keepdims=True))
        e_s = jnp.where(ok, jnp.exp(s_s - m), 0.0)
        e_n = jnp.where(nok, jnp.exp(s_n - m), 0.0)
        den = jnp.sum(e_s, axis=-1, keepdims=True) + jnp.sum(e_n, axis=-1, keepdims=True)
        v_new = knew_ref[:, KVP + k * HEAD_DIM:KVP + (k + 1) * HEAD_DIM].astype(BF16)
        o = (_dot_t(e_s.astype(BF16), vcat) + _dot(e_n.astype(BF16), v_new)) / den
        o_parts.append(jnp.concatenate([o, zeros_half] if k == 0 else [zeros_half, o], axis=1))
    o_s = jnp.concatenate(o_parts, axis=0)

    o_ref[...] = gates[:, 0:1] * oc_ref[...] + gates[:, 1:2] * o_s + gates[:, 2:3] * o_w


def _slc_sample(page_table, idx_flat, cache_t, q_ktg, idxv, kvs_new, win_t, kvw_new, o_c, gates_r,
                *, layer, dec_seq, past_len):
    n_seq, n_pages = page_table.shape
    n_kt = N_KV_HEADS * dec_seq
    rows = GROUP * n_kt
    wb = win_t.shape[-1]
    cols = SLC_TOPK * PAGE_SIZE
    expand = jnp.asarray((np.arange(LANES)[:, None] == (np.arange(cols) // PAGE_SIZE)[None, :]).astype(np.float32),
                         dtype=BF16)
    kern = functools.partial(_slc_sample_kernel, layer=layer, n_seq=n_seq, n_pages=n_pages, dec_seq=dec_seq,
                             past_len=past_len)
    per_seq = lambda r, w: pl.BlockSpec((None, r, w), lambda b, pt, ix: (b, 0, 0))
    grid_spec = pltpu.PrefetchScalarGridSpec(
        num_scalar_prefetch=2,
        grid=(n_seq,),
        in_specs=[pl.BlockSpec(memory_space=pl.ANY),
                  per_seq(rows, LANES), per_seq(n_kt, LANES), per_seq(dec_seq, KV_W),
                  pl.BlockSpec((None, None, 2, KVP, wb), lambda b, pt, ix: (layer, b, 0, 0, 0)),
                  per_seq(dec_seq, KV_W), per_seq(rows, LANES), per_seq(rows, LANES),
                  pl.BlockSpec((LANES, cols), lambda b, pt, ix: (0, 0))],
        out_specs=per_seq(rows, LANES),
        scratch_shapes=[pltpu.VMEM((2, n_kt, SLC_TOPK, HEAD_DIM, PAGE_SIZE), F32),
                        pltpu.VMEM((2, n_kt, SLC_TOPK, HEAD_DIM, PAGE_SIZE), F32),
                        pltpu.SemaphoreType.DMA((2,))])
    return pl.pallas_call(
        kern,
        grid_spec=grid_spec,
        out_shape=jax.ShapeDtypeStruct((n_seq, rows, LANES), F32),
        compiler_params=pltpu.CompilerParams(dimension_semantics=("arbitrary",), vmem_limit_bytes=VMEM_LIMIT),
        name="slc_sample",
    )(page_table, idx_flat, cache_t, q_ktg, idxv, kvs_new, win_t, kvw_new, o_c, gates_r, expand)


def _pool_sample_kernel(z_ref, d_ref, *, dec_seq, pos0):
    lane = lax.broadcasted_iota(jnp.int32, (1, POOL_W), 1)
    grp = lane // POOL_GW
    for t in range(dec_seq):
        cur = z_ref[:, POOL_HIST + t, :]
        acc = cur
        sums = {}
        for back in range(1, max(POOL_WINDOWS)):
            acc = acc + z_ref[:, POOL_HIST + t - back, :]
            if back + 1 in POOL_WINDOWS:
                sums[back + 1] = acc
        d = None
        for g, w in enumerate(POOL_WINDOWS):
            val = sums[w] / float(min(w, pos0 + t + 1)) - cur
            d = val if d is None else jnp.where(grp == g, val, d)
        d_ref[:, t, :] = d


def _pool_sample(z_ext, *, dec_seq, pos0):
    n_seq = z_ext.shape[0]
    return pl.pallas_call(
        functools.partial(_pool_sample_kernel, dec_seq=dec_seq, pos0=pos0),
        out_shape=jax.ShapeDtypeStruct((n_seq, dec_seq, POOL_W), F32),
        name="pool_sample",
    )(z_ext)


def _sample_layer(x, lw, cmp_t, slc_t, win_t, win_state, pool_state, page_table, *, layer, n_seq, dec_seq):
    rows = n_seq * dec_seq
    n_kt = N_KV_HEADS * dec_seq
    x = _ffn_ln(x, lw['ffn_in'][0], lw['ffn_out'][0], lw['ln_g'][0], lw['ln_b'][0], tm=rows)
    cos, sin = _rope_tables(PAST_LEN + jnp.arange(rows) % dec_seq)
    q, kvc, kvs, kvw, _, _, gates, gu, gv, p = _inproj(
        x, lw['w_ext'], cos, sin, lw['gm_ln_g'], lw['gm_ln_b'], tm=rows)

    qf = q.astype(F32).reshape(N_KV_HEADS, GROUP, n_seq, dec_seq, LANES)
    q_gkt = qf.transpose(2, 1, 0, 3, 4).reshape(n_seq, GROUP * n_kt, LANES).astype(BF16)
    q_ktg = qf.transpose(2, 0, 3, 1, 4).reshape(n_seq, GROUP * n_kt, LANES)
    o_c, idxv = _cmp_sample(page_table, cmp_t, q_gkt, lw['pe2'], lw['w1kv'], lw['w2kv'],
                            layer=layer, dec_seq=dec_seq, past_len=PAST_LEN)
    o_c = o_c.reshape(n_seq, GROUP, N_KV_HEADS, dec_seq, LANES).transpose(0, 2, 3, 1, 4).reshape(n_seq, GROUP * n_kt, LANES)
    gates_r = gates[:, :GATE_W].reshape(n_seq, dec_seq, N_KV_HEADS, GROUP, 3).transpose(0, 2, 1, 3, 4)
    gates_r = jnp.pad(gates_r.reshape(n_seq, GROUP * n_kt, 3), ((0, 0), (0, 0), (0, LANES - 3)))
    idx_flat = idxv[:, :, :SLC_TOPK].reshape(-1)
    o = _slc_sample(page_table, idx_flat, slc_t, q_ktg, idxv, kvs.reshape(n_seq, dec_seq, KV_W), win_t,
                    kvw.reshape(n_seq, dec_seq, KV_W), o_c, gates_r, layer=layer, dec_seq=dec_seq, past_len=PAST_LEN)
    o = o.reshape(n_seq, N_KV_HEADS, dec_seq, GROUP, LANES)
    o_nsa = jnp.stack([o[:, k, :, :, k * HEAD_DIM:(k + 1) * HEAD_DIM] for k in range(N_KV_HEADS)], axis=2)
    o_nsa = o_nsa.reshape(rows, NSA_W).astype(BF16)

    z_ext = jnp.concatenate([pool_state, p.reshape(n_seq, dec_seq, POOL_W)], axis=1)
    d = _pool_sample(z_ext, dec_seq=dec_seq, pos0=PAST_LEN).reshape(rows, POOL_W)
    ws_cat, gb_full = _gmlp_weights(lw['gm_ws'], lw['gm_b'], dec_seq, GM_CHUNK // dec_seq)
    x = _mix_out(x, o_nsa, gu, gv, d, ws_cat, gb_full, lw['pw_big'], lw['pool_scale'], lw['w_o'],
                 lw['ln_g'][1], lw['ln_b'][1], tm=rows, seq=dec_seq, pool_in_kernel=False)
    x = _ffn_ln(x, lw['ffn_in'][1], lw['ffn_out'][1], lw['ln_g'][2], lw['ln_b'][2], tm=rows)
    shp = (n_seq, dec_seq, 2, N_KV_HEADS, HEAD_DIM)
    new = (kvc.reshape(shp), kvs.reshape(shp),
           jnp.concatenate([win_state[:, dec_seq:], kvw.reshape(shp)], axis=1),
           z_ext[:, dec_seq:], gv.reshape(n_seq, dec_seq, GM_W))
    return x, new


def _pages_by_channel(cache):
    nd = cache.ndim
    t = jnp.transpose(cache, tuple(range(nd - 4)) + (nd - 3, nd - 2, nd - 1, nd - 4))
    return t.reshape(t.shape[:-3] + (KVP, t.shape[-1]))
def _prompt_layer(x, lw, *, batch, seq, tm, tm_ffn):
    x = _ffn_ln(x, lw['ffn_in'][0], lw['ffn_out'][0], lw['ln_g'][0], lw['ln_b'][0], tm=tm_ffn)
    cos, sin = _rope_tables(jnp.arange(seq))
    q, kvc, kvs, kvw, kvs_b, kvw_b, gates, gu, gv, p = _inproj(
        x, lw['w_ext'], cos, sin, lw['gm_ln_g'], lw['gm_ln_b'], tm=tm)
    kcvc = _compress_prompt(kvc, lw['pe_rows'], lw['w1big'], lw['w2big'], batch=batch)
    o_nsa = _nsa_prompt(q, kcvc, kvs_b, kvw_b, gates, batch=batch, seq=seq)
    ws_cat, gb_full = _gmlp_weights(lw['gm_ws'], lw['gm_b'], GM_CHUNK, 1)
    x = _mix_out(x, o_nsa, gu, gv, p, ws_cat, gb_full, lw['pw_big'], lw['pool_scale'], lw['w_o'],
                 lw['ln_g'][1], lw['ln_b'][1], tm=tm, seq=seq, pool_in_kernel=True)
    x = _ffn_ln(x, lw['ffn_in'][1], lw['ffn_out'][1], lw['ln_g'][2], lw['ln_b'][2], tm=tm_ffn)
    wb = min(WINDOW, seq)
    shp = (batch, seq, 2, N_KV_HEADS, HEAD_DIM)
    new = (kvc.reshape(shp), kvs.reshape(shp), kvw.reshape(shp)[:, seq - wb:],
           p.reshape(batch, seq, POOL_W)[:, seq - POOL_HIST:])
    return x, new


def _layer_weights(l, ffn_in_b, ffn_out_b, ln_g, ln_b, w_in, w_o, cmp_pe, cmp_w1, cmp_w2,
                   gm_ln_g, gm_ln_b, gm_ws, gm_b, pool_w, pool_scale):
    pe_rows, w1big, w2big = _compress_weights(cmp_pe[l], cmp_w1[l], cmp_w2[l])
    pe2, w1kv, w2kv = _compress_weights_kv(cmp_pe[l], cmp_w1[l], cmp_w2[l])
    return dict(ffn_in=ffn_in_b[l], ffn_out=ffn_out_b[l], ln_g=ln_g[l], ln_b=ln_b[l],
                w_ext=_build_w_ext(w_in[l]), w_o=w_o[l].astype(BF16),
                pe_rows=pe_rows, w1big=w1big, w2big=w2big, pe2=pe2, w1kv=w1kv, w2kv=w2kv,
                gm_ln_g=gm_ln_g[l], gm_ln_b=gm_ln_b[l], gm_ws=gm_ws[l], gm_b=gm_b[l],
                pw_big=_pool_weights(pool_w[l]), pool_scale=pool_scale[l])


def kernel(x_prompt, x_sample, cache_kv_cmp, cache_kv_slc, state_kv_win, state_pool, page_table, ln_g, ln_b, ffn_w_in, ffn_w_out, w_in, w_o, cmp_pe, cmp_w1, cmp_w2, gm_ln_g, gm_ln_b, gm_ws, gm_b, pool_w, pool_scale):
    batch, seq, _ = x_prompt.shape
    fi = ffn_w_in.astype(BF16)
    fo = ffn_w_out.astype(BF16)
    n_seq, dec_seq, _ = x_sample.shape
    xp = x_prompt.reshape(batch * seq, D_MODEL)
    xs = x_sample.reshape(n_seq * dec_seq, D_MODEL)
    cmp_t = _pages_by_channel(cache_kv_cmp)
    slc_t = _pages_by_channel(cache_kv_slc)
    win_t = _pages_by_channel(state_kv_win)
    new_p, new_s = [], []
    for l in range(DEPTH):
        lw = _layer_weights(l, fi, fo, ln_g, ln_b, w_in, w_o, cmp_pe, cmp_w1, cmp_w2, gm_ln_g, gm_ln_b, gm_ws, gm_b, pool_w, pool_scale)
        xp, st_p = _prompt_layer(xp, lw, batch=batch, seq=seq, tm=512, tm_ffn=1024)
        xs, st_s = _sample_layer(xs, lw, cmp_t, slc_t, win_t, state_kv_win[l], state_pool[l], page_table,
                                 layer=l, n_seq=n_seq, dec_seq=dec_seq)
        new_p.append(st_p)
        new_s.append(st_s)
    stk = lambda lst, i: jnp.stack([t[i] for t in lst])
    return (xp.reshape(batch, seq, D_MODEL), xs.reshape(n_seq, dec_seq, D_MODEL),
            stk(new_p, 0), stk(new_s, 0), stk(new_p, 1), stk(new_s, 1),
            stk(new_p, 2), stk(new_s, 2), stk(new_p, 3), stk(new_s, 3), stk(new_s, 4))
```

```python
import functools

import numpy as np
import jax
import jax.numpy as jnp
from jax import lax
from jax.experimental import pallas as pl
from jax.experimental.pallas import tpu as pltpu

F32 = jnp.float32
BF16 = jnp.bfloat16

D_MODEL = 1024
DEPTH = 2
PAST_LEN = 16384
PAGE_SIZE = 128
HEAD_DIM = 64
NSA_W = D_MODEL // 2
GM_W = D_MODEL // 4
POOL_W = D_MODEL // 4
N_HEADS = NSA_W // HEAD_DIM
N_KV_HEADS = 2
GROUP = N_HEADS // N_KV_HEADS
CMP_STRIDE = 16
CMP_LEN = 2 * CMP_STRIDE
SLC_BLOCK = 64
SLC_TOPK = 16
WINDOW = 512
FORCE_SCORE = 1.0e4
ROPE_THETA = 10000.0
SCALE = HEAD_DIM ** -0.5
GM_HEADS = GM_W // HEAD_DIM
GM_CHUNK = 128
POOL_GROUPS = 4
POOL_GW = POOL_W // POOL_GROUPS
POOL_WINDOWS = (2, 4, 8, 16)
POOL_HIST = max(POOL_WINDOWS) - 1
D_FF = 256 * ((8 * D_MODEL // 3 + 255) // 256)
ALPHA = (2 * DEPTH) ** 0.25
LN_EPS = 1e-5
Q_W = N_HEADS * HEAD_DIM
KV_W = 2 * N_KV_HEADS * HEAD_DIM
GATE_W = 3 * N_HEADS
N_IN = Q_W + 3 * KV_W + GATE_W + 2 * GM_W + POOL_W

LANES = 128
KVP = N_KV_HEADS * HEAD_DIM
VMEM_LIMIT = 56 * 1024 * 1024
NEG = -1e30
HALO = 16

_OFF_Q = 0
_OFF_QR = Q_W
_OFF_KV = 2 * Q_W
_OFF_GATE = _OFF_KV + 3 * 3 * KVP
_OFF_UV = _OFF_GATE + LANES
_OFF_P = _OFF_UV + 2 * GM_W
N_EXT = _OFF_P + POOL_W


def _ln_rows(y, g, b):
    mu = jnp.mean(y, axis=-1, keepdims=True)
    d = y - mu
    var = jnp.mean(d * d, axis=-1, keepdims=True)
    return d * lax.rsqrt(var + LN_EPS) * g + b


def _dot(a, b):
    return jnp.dot(a, b, preferred_element_type=F32)


def _dot_t(a, b):
    return lax.dot_general(a, b, (((1,), (1,)), ((), ())), preferred_element_type=F32)


def _ffn_kernel(x_ref, wg_ref, wu_ref, wo_ref, g_ref, b_ref, o_ref, xb_ref, *, n_chunks):
    j = pl.program_id(1)

    @pl.when(j == 0)
    def _():
        xb_ref[...] = x_ref[...].astype(BF16)

    xb = xb_ref[...]
    gate = _dot(xb, wg_ref[...])
    up = _dot(xb, wu_ref[...])
    hid = (gate * jax.nn.sigmoid(gate)) * up
    part = _dot(hid.astype(BF16), wo_ref[...])

    @pl.when(j == 0)
    def _():
        o_ref[...] = part

    if n_chunks > 2:
        @pl.when((j > 0) & (j < n_chunks - 1))
        def _():
            o_ref[...] += part

    @pl.when(j == n_chunks - 1)
    def _():
        y = ALPHA * x_ref[...] + 0.5 * (o_ref[...] + part)
        o_ref[...] = _ln_rows(y, g_ref[...], b_ref[...])


def _ffn_ln(x, w_in_b, w_out_b, g, b, *, tm):
    rows = x.shape[0]
    n_chunks = 2
    fc = D_FF // n_chunks
    return pl.pallas_call(
        functools.partial(_ffn_kernel, n_chunks=n_chunks),
        grid=(rows // tm, n_chunks),
        in_specs=[
            pl.BlockSpec((tm, D_MODEL), lambda i, j: (i, 0)),
            pl.BlockSpec((D_MODEL, fc), lambda i, j: (0, j)),
            pl.BlockSpec((D_MODEL, fc), lambda i, j: (0, n_chunks + j)),
            pl.BlockSpec((fc, D_MODEL), lambda i, j: (j, 0)),
            pl.BlockSpec((1, D_MODEL), lambda i, j: (0, 0)),
            pl.BlockSpec((1, D_MODEL), lambda i, j: (0, 0)),
        ],
        out_specs=pl.BlockSpec((tm, D_MODEL), lambda i, j: (i, 0)),
        out_shape=jax.ShapeDtypeStruct((rows, D_MODEL), F32),
        scratch_shapes=[pltpu.VMEM((tm, D_MODEL), BF16)],
        compiler_params=pltpu.CompilerParams(
            dimension_semantics=("parallel", "arbitrary"), vmem_limit_bytes=VMEM_LIMIT),
        name="ffn_ln",
    )(x, w_in_b, w_in_b, w_out_b, g.reshape(1, D_MODEL), b.reshape(1, D_MODEL))


def _inproj_kernel(h_ref, w_ref, cos_ref, sin_ref, gmg_ref, gmb_ref,
                   qt_ref, kvc_ref, kvs_ref, kvw_ref, ks_ref, vts_ref, kw_ref, vtw_ref,
                   gate_ref, gu_ref, gv_ref, p_ref):
    hb = h_ref[...].astype(BF16)
    cos = cos_ref[...]
    sin = sin_ref[...]

    zq = _dot(hb, w_ref[:, _OFF_Q:_OFF_KV])
    zk = _dot(hb, w_ref[:, _OFF_KV:_OFF_UV])
    zu = _dot(hb, w_ref[:, _OFF_UV:N_EXT])

    n_sq = hb.shape[0] // LANES
    zeros_half = jnp.zeros((HEAD_DIM, LANES), F32)

    for m in range(N_HEADS // 2):
        c0 = m * LANES
        pair = (zq[:, c0:c0 + LANES] * cos + zq[:, Q_W + c0:Q_W + c0 + LANES] * sin) * SCALE
        kvh = (2 * m) // GROUP
        for c in range(n_sq):
            pt = pair[c * LANES:(c + 1) * LANES].T
            for e in range(2):
                piece = pt[e * HEAD_DIM:(e + 1) * HEAD_DIM]
                both = [piece, zeros_half] if kvh == 0 else [zeros_half, piece]
                qt_ref[2 * m + e, :, c * LANES:(c + 1) * LANES] = jnp.concatenate(both, axis=0).astype(BF16)

    for br, (f_ref, k_ref, vt_ref) in enumerate(((kvc_ref, None, None), (kvs_ref, ks_ref, vts_ref),
                                                 (kvw_ref, kw_ref, vtw_ref))):
        c0 = br * 3 * KVP
        k = zk[:, c0:c0 + KVP] * cos + zk[:, c0 + KVP:c0 + 2 * KVP] * sin
        v = zk[:, c0 + 2 * KVP:c0 + 3 * KVP]
        f_ref[:, 0:KVP] = k
        f_ref[:, KVP:2 * KVP] = v
        if k_ref is not None:
            k_ref[...] = k.astype(BF16)
            for c in range(n_sq):
                vt_ref[c] = v[c * LANES:(c + 1) * LANES].T.astype(BF16)

    gate_ref[...] = jax.nn.sigmoid(zk[:, _OFF_GATE - _OFF_KV:_OFF_UV - _OFF_KV])
    gu_ref[...] = jax.nn.gelu(zu[:, 0:GM_W])
    gv_ref[...] = _ln_rows(jax.nn.gelu(zu[:, GM_W:2 * GM_W]), gmg_ref[...], gmb_ref[...])
    p_ref[...] = zu[:, 2 * GM_W:2 * GM_W + POOL_W]


def _inproj(h, w_ext, cos, sin, gmg, gmb, *, tm):
    rows = h.shape[0]
    n_tab = cos.shape[0] // tm
    row_spec = lambda w: pl.BlockSpec((tm, w), lambda i: (i, 0))
    tab_spec = pl.BlockSpec((tm, LANES), lambda i: (i % n_tab, 0))
    vec_spec = pl.BlockSpec((1, GM_W), lambda i: (0, 0))
    sq_spec = pl.BlockSpec((tm // LANES, KVP, LANES), lambda i: (i, 0, 0))
    sds = jax.ShapeDtypeStruct
    return pl.pallas_call(
        _inproj_kernel,
        grid=(rows // tm,),
        in_specs=[row_spec(D_MODEL), pl.BlockSpec((D_MODEL, N_EXT), lambda i: (0, 0)),
                  tab_spec, tab_spec, vec_spec, vec_spec],
        out_specs=[pl.BlockSpec((N_HEADS, LANES, tm), lambda i: (0, 0, i)),
                   row_spec(KV_W), row_spec(KV_W), row_spec(KV_W),
                   row_spec(KVP), sq_spec, row_spec(KVP), sq_spec,
                   row_spec(LANES), row_spec(GM_W), row_spec(GM_W), row_spec(POOL_W)],
        out_shape=[sds((N_HEADS, LANES, rows), BF16),
                   sds((rows, KV_W), F32), sds((rows, KV_W), F32), sds((rows, KV_W), F32),
                   sds((rows, KVP), BF16), sds((rows // LANES, KVP, LANES), BF16),
                   sds((rows, KVP), BF16), sds((rows // LANES, KVP, LANES), BF16),
                   sds((rows, LANES), F32), sds((rows, GM_W), F32), sds((rows, GM_W), F32),
                   sds((rows, POOL_W), F32)],
        compiler_params=pltpu.CompilerParams(dimension_semantics=("parallel",), vmem_limit_bytes=VMEM_LIMIT),
        name="inproj",
    )(h, w_ext, cos, sin, gmg.reshape(1, GM_W), gmb.reshape(1, GM_W))


def _build_w_ext(w_in):
    half = HEAD_DIM // 2

    def rot(w):
        n = w.shape[1] // HEAD_DIM
        w3 = w.reshape(D_MODEL, n, 2, half)
        return jnp.stack([-w3[:, :, 1], w3[:, :, 0]], axis=2).reshape(D_MODEL, n * HEAD_DIM)

    q = w_in[:, :Q_W]
    cols = [q, rot(q)]
    for br in range(3):
        kv = w_in[:, Q_W + br * KV_W:Q_W + (br + 1) * KV_W]
        k, v = kv[:, :KVP], kv[:, KVP:]
        cols += [k, rot(k), v]
    g0 = Q_W + 3 * KV_W
    cols.append(jnp.pad(w_in[:, g0:g0 + GATE_W], ((0, 0), (0, LANES - GATE_W))))
    cols.append(w_in[:, g0 + GATE_W:])
    return jnp.concatenate(cols, axis=1).astype(BF16)


def _rope_tables(pos):
    half = HEAD_DIM // 2
    inv = ROPE_THETA ** (-jnp.arange(half, dtype=F32) / half)
    ang = pos.astype(F32)[:, None] * inv[None, :]
    cos = jnp.tile(jnp.cos(ang), (1, LANES // half))
    sin = jnp.tile(jnp.sin(ang), (1, LANES // half))
    return cos, sin


def _compress_kernel(x_ref, pe_ref, w1_ref, w2_ref, kc_ref, vct_ref):
    w1 = w1_ref[...]
    fs = _dot(x_ref[...].astype(BF16), w1)
    per = _dot(pe_ref[...].astype(BF16), w1)
    pe_term = per[0:1, 0:2 * KVP] + per[1:2, 2 * KVP:4 * KVP]
    first = fs[:, 0:2 * KVP]
    second = fs[:, 2 * KVP:4 * KVP]
    nxt = jnp.concatenate([second[1:], jnp.zeros((1, 2 * KVP), F32)], axis=0)
    hid = jax.nn.gelu(first + nxt + pe_term)
    out = _dot(hid.astype(BF16), w2_ref[...])
    kc_ref[...] = out[:, 0:KVP].astype(BF16)
    for c in range(out.shape[0] // LANES):
        vct_ref[:, c * LANES:(c + 1) * LANES] = out[c * LANES:(c + 1) * LANES, KVP:2 * KVP].T.astype(BF16)


def _compress_weights(pe, w1, w2):
    eye = jnp.eye(N_KV_HEADS, dtype=F32)
    w1r = w1.reshape(2, 2, CMP_STRIDE, HEAD_DIM, HEAD_DIM)
    w1big = jnp.einsum('ksjde,kq,hg->jkhdsqge', w1r, eye, eye).reshape(CMP_STRIDE * 2 * KVP, 4 * KVP)
    w2big = jnp.einsum('ked,kq,hg->kheqgd', w2, eye, eye).reshape(2 * KVP, 2 * KVP)
    per = pe.reshape(2, 2, CMP_STRIDE, HEAD_DIM).transpose(1, 2, 0, 3)
    per = jnp.broadcast_to(per[:, :, :, None, :], (2, CMP_STRIDE, 2, N_KV_HEADS, HEAD_DIM))
    pe_rows = jnp.pad(per.reshape(2, CMP_STRIDE * 2 * KVP), ((0, 6), (0, 0)))
    return pe_rows, w1big.astype(BF16), w2big.astype(BF16)


def _compress_prompt(kvc, pe_rows, w1big, w2big, *, batch):
    nsub = kvc.shape[0] // batch // CMP_STRIDE
    width = CMP_STRIDE * 2 * KVP
    x = kvc.reshape(batch, nsub, width)
    return pl.pallas_call(
        _compress_kernel,
        grid=(batch,),
        in_specs=[pl.BlockSpec((None, nsub, width), lambda b: (b, 0, 0)),
                  pl.BlockSpec((8, width), lambda b: (0, 0)),
                  pl.BlockSpec((width, 4 * KVP), lambda b: (0, 0)),
                  pl.BlockSpec((2 * KVP, 2 * KVP), lambda b: (0, 0))],
        out_specs=[pl.BlockSpec((None, nsub, KVP), lambda b: (b, 0, 0)),
                   pl.BlockSpec((None, KVP, nsub), lambda b: (b, 0, 0))],
        out_shape=[jax.ShapeDtypeStruct((batch, nsub, KVP), BF16), jax.ShapeDtypeStruct((batch, KVP, nsub), BF16)],
        compiler_params=pltpu.CompilerParams(dimension_semantics=("parallel",), vmem_limit_bytes=VMEM_LIMIT),
        name="compress_prompt",
    )(x, pe_rows, w1big, w2big)


def _top_blocks_cols(score, blk):
    sel = jnp.zeros(score.shape, F32)
    for _ in range(SLC_TOPK):
        m = jnp.max(score, axis=0, keepdims=True)
        first = jnp.min(jnp.where(score == m, blk, 1e9), axis=0, keepdims=True)
        hit = blk == first
        sel = jnp.where(hit, 1.0, sel)
        score = jnp.where(hit, -jnp.inf, score)
    return sel


def _top_blocks_idx(score, blk):
    sel = jnp.zeros(score.shape, F32)
    idx = jnp.zeros((score.shape[0], LANES), jnp.int32)
    lane = lax.broadcasted_iota(jnp.int32, (1, LANES), 1)
    big = jnp.int32(1 << 20)
    for it in range(SLC_TOPK):
        m = jnp.max(score, axis=-1, keepdims=True)
        first = jnp.min(jnp.where(score == m, blk, big), axis=-1, keepdims=True)
        hit = blk == first
        sel = jnp.where(hit, 1.0, sel)
        idx = jnp.where(lane == it, first, idx)
        score = jnp.where(hit, -jnp.inf, score)
    return sel, idx


def _softmax_rows(s, mask):
    s = jnp.where(mask, s, NEG)
    m = jnp.max(s, axis=-1, keepdims=True)
    e = jnp.exp(s - m)
    return jnp.where(mask, e / jnp.sum(e, axis=-1, keepdims=True), 0.0)


def _nsa_prompt_kernel(qt_ref, kc_ref, vct_ref, ks_ref, vts_ref, kw_ref, vtw_ref, gate_ref, selmap_ref, expand_ref,
                       o_ref, *, tq, tk, seq, n_sel_blocks):
    t0 = pl.program_id(1) * tq
    nsub = kc_ref.shape[0]
    nb = 8 * ((n_sel_blocks + 7) // 8)
    span = min(WINDOW + tq, seq)
    qpos = t0 + lax.broadcasted_iota(jnp.int32, (1, tq), 1)
    rep = lambda x, n: jnp.concatenate([x] * n, axis=1)
    qt = jnp.concatenate([qt_ref[h] for h in range(N_HEADS)], axis=1)

    cmp_end = CMP_STRIDE * lax.broadcasted_iota(jnp.int32, (nsub, 1), 0) + (CMP_LEN - 1)
    c_bias = jnp.where(cmp_end <= qpos, 0.0, NEG)
    s = _dot(kc_ref[...], qt) + rep(c_bias, N_HEADS)
    e = jnp.exp(s - jnp.max(s, axis=0, keepdims=True))
    inv_c = jnp.where(rep(qpos >= CMP_LEN - 1, N_HEADS), 1.0 / jnp.sum(e, axis=0, keepdims=True), 0.0)
    eb = e.astype(BF16)
    o_c = _dot(vct_ref[...], eb) * inv_c
    imp_h = _dot(selmap_ref[...], eb)[0:nb] * inv_c

    imp = []
    for k in range(N_KV_HEADS):
        acc = None
        for g in range(GROUP):
            part = imp_h[:, (GROUP * k + g) * tq:(GROUP * k + g + 1) * tq]
            acc = part if acc is None else acc + part
        imp.append(acc)
    imp = jnp.concatenate(imp, axis=1)
    blk = lax.broadcasted_iota(jnp.int32, (nb, 1), 0)
    cur = rep(qpos // SLC_BLOCK, N_KV_HEADS)
    forced = (blk == 0) | (blk == cur) | (blk == cur - 1)
    score = jnp.where(forced, FORCE_SCORE, jnp.where(blk <= cur, imp, -1.0))
    if nb > n_sel_blocks:
        score = jnp.where(blk < n_sel_blocks, score, -jnp.inf)
    sel = _top_blocks_cols(score, blk.astype(F32))
    sel_m1 = jnp.concatenate([sel - 1.0, jnp.zeros((LANES - nb, N_KV_HEADS * tq), F32)], axis=0).astype(BF16)

    def tile(kt, carry, diagonal):
        m_i, l_i, acc = carry
        r0 = pl.multiple_of(kt * tk, tk)
        bias = _dot(expand_ref[pl.ds(r0, tk), :], sel_m1)
        if diagonal:
            kpos = r0 + lax.broadcasted_iota(jnp.int32, (tk, 1), 0)
            bias = jnp.where(kpos <= rep(qpos, N_KV_HEADS), bias, NEG)
        s = _dot(ks_ref[pl.ds(r0, tk), :], qt)
        s = jnp.concatenate([s[:, h * tq:(h + 1) * tq] + bias[:, (h // GROUP) * tq:(h // GROUP + 1) * tq]
                             for h in range(N_HEADS)], axis=1)
        m_new = jnp.maximum(m_i, jnp.max(s, axis=0, keepdims=True))
        a = jnp.exp(m_i - m_new)
        e = jnp.exp(s - m_new)
        l_new = a * l_i + jnp.sum(e, axis=0, keepdims=True)
        c0 = kt * (tk // LANES)
        vt = jnp.concatenate([vts_ref[c0 + c] for c in range(tk // LANES)], axis=1)
        return m_new, l_new, a * acc + _dot(vt, e.astype(BF16))

    n_kt = (t0 + tq + tk - 1) // tk
    init = (jnp.full((1, N_HEADS * tq), NEG, F32), jnp.zeros((1, N_HEADS * tq), F32),
            jnp.zeros((KVP, N_HEADS * tq), F32))
    carry = lax.fori_loop(0, n_kt - 1, lambda kt, c: tile(kt, c, False), init)
    _, l_s, acc_s = tile(n_kt - 1, carry, True)
    o_s = acc_s * (1.0 / l_s)

    start = pl.multiple_of(jnp.maximum(t0 + tq - span, 0), tq)
    dist = qpos - (start + lax.broadcasted_iota(jnp.int32, (span, 1), 0))
    w_bias = jnp.where(dist >= 0, jnp.where(dist <= WINDOW, 0.0, NEG), NEG)
    s = _dot(kw_ref[pl.ds(start, span), :], qt) + rep(w_bias, N_HEADS)
    e = jnp.exp(s - jnp.max(s, axis=0, keepdims=True))
    inv_w = 1.0 / jnp.sum(e, axis=0, keepdims=True)
    c0 = start // LANES
    vt = jnp.concatenate([vtw_ref[c0 + c] for c in range(span // LANES)], axis=1)
    o_w = _dot(vt, e.astype(BF16)) * inv_w

    gt = gate_ref[...].T
    parts = []
    for h in range(N_HEADS):
        rows = slice((h // GROUP) * HEAD_DIM, (h // GROUP + 1) * HEAD_DIM)
        cols = slice(h * tq, (h + 1) * tq)
        parts.append(gt[3 * h:3 * h + 1] * o_c[rows, cols] + gt[3 * h + 1:3 * h + 2] * o_s[rows, cols]
                     + gt[3 * h + 2:3 * h + 3] * o_w[rows, cols])
    ot = jnp.concatenate(parts, axis=0)
    for m in range(NSA_W // LANES):
        o_ref[:, m * LANES:(m + 1) * LANES] = ot[m * LANES:(m + 1) * LANES].T.astype(o_ref.dtype)


def _sel_map_t(nc_rows, n_cmp, ns):
    c0 = CMP_STRIDE * np.arange(nc_rows)[None, :]
    s0 = SLC_BLOCK * np.arange(LANES)[:, None]
    ov = np.clip(np.minimum(c0 + CMP_LEN, s0 + SLC_BLOCK) - np.maximum(c0, s0), 0, None) / CMP_LEN
    ov = ov * (np.arange(nc_rows)[None, :] < n_cmp) * (np.arange(LANES)[:, None] < ns)
    return jnp.asarray(ov, dtype=BF16)


def _expand_map(seq):
    e = (np.arange(LANES)[None, :] == (np.arange(seq) // SLC_BLOCK)[:, None]).astype(np.float32) * -NEG
    return jnp.asarray(e, dtype=BF16)


def _nsa_prompt(qt, kc, vct, ks, vts, kw, vtw, gates, *, batch, seq, tq=128, tk=512):
    tk = min(tk, seq)
    nq = seq // tq
    nsub = seq // CMP_STRIDE
    ns = seq // SLC_BLOCK
    assert ns <= LANES and seq % tk == 0 and tk % tq == 0 and tq == LANES
    kern = functools.partial(_nsa_prompt_kernel, tq=tq, tk=tk, seq=seq, n_sel_blocks=ns)
    per_batch = lambda shp: pl.BlockSpec(shp, lambda b, i: (b,) + (0,) * (len(shp) - 1))
    const = lambda shp: pl.BlockSpec(shp, lambda b, i: (0,) * len(shp))
    return pl.pallas_call(
        kern,
        grid=(batch, nq),
        in_specs=[pl.BlockSpec((N_HEADS, KVP, tq), lambda b, i: (0, 0, b * nq + i)),
                  per_batch((None, nsub, KVP)), per_batch((None, KVP, nsub)),
                  per_batch((seq, KVP)), per_batch((seq // LANES, KVP, LANES)),
                  per_batch((seq, KVP)), per_batch((seq // LANES, KVP, LANES)),
                  pl.BlockSpec((tq, LANES), lambda b, i: (b * nq + i, 0)),
                  const((LANES, nsub)), const((seq, LANES))],
        out_specs=pl.BlockSpec((tq, NSA_W), lambda b, i: (b * nq + i, 0)),
        out_shape=jax.ShapeDtypeStruct((batch * seq, NSA_W), BF16),
        compiler_params=pltpu.CompilerParams(
            dimension_semantics=("parallel", "arbitrary"), vmem_limit_bytes=VMEM_LIMIT),
        name="nsa_prompt",
    )(qt, kc, vct, ks, vts, kw, vtw, gates, _sel_map_t(nsub, nsub - 1, ns), _expand_map(seq))


def _pool_windows(z_ext, tm):
    s2 = z_ext[1:] + z_ext[:-1]
    s4 = s2[2:] + s2[:-2]
    s8 = s4[4:] + s4[:-4]
    s16 = s8[8:] + s8[:-8]
    return (s2[HALO - 1:HALO - 1 + tm], s4[HALO - 3:HALO - 3 + tm], s8[HALO - 7:HALO - 7 + tm],
            s16[HALO - 15:HALO - 15 + tm])


def _mix_out_kernel(*refs, tm, tiles_per_seq, pool_in_kernel):
    if pool_in_kernel:
        (x_ref, nsa_ref, gu_ref, gv_ref, p_ref, halo_ref, ws_ref, gb_ref, pw_ref, ps_ref, wo_ref,
         g_ref, b_ref, o_ref) = refs
    else:
        (x_ref, nsa_ref, gu_ref, gv_ref, d_ref, ws_ref, gb_ref, pw_ref, ps_ref, wo_ref,
         g_ref, b_ref, o_ref) = refs
    lane = lax.broadcasted_iota(jnp.int32, (1, GM_W), 1)

    parts = []
    for c in range(tm // GM_CHUNK):
        v = gv_ref[c * GM_CHUNK:(c + 1) * GM_CHUNK, :]
        stacked = jnp.concatenate(
            [jnp.where(lane // HEAD_DIM == h, v, 0.0) for h in range(GM_HEADS)], axis=0).astype(BF16)
        s = _dot(ws_ref[...], stacked) + gb_ref[...]
        parts.append(gu_ref[c * GM_CHUNK:(c + 1) * GM_CHUNK, :] * s)
    o_gm = parts[0] if len(parts) == 1 else jnp.concatenate(parts, axis=0)

    if pool_in_kernel:
        first_tile = (pl.program_id(0) % tiles_per_seq) == 0
        halo = jnp.where(first_tile, 0.0, halo_ref[...])
        z = p_ref[...]
        wins = _pool_windows(jnp.concatenate([halo, z], axis=0), tm)
        pos = (pl.program_id(0) % tiles_per_seq) * tm + lax.broadcasted_iota(jnp.int32, (tm, 1), 0)
        grp = lane // POOL_GW
        wsum = jnp.where(grp == 0, wins[0], jnp.where(grp == 1, wins[1], jnp.where(grp == 2, wins[2], wins[3])))
        width = jnp.where(grp == 0, POOL_WINDOWS[0], jnp.where(grp == 1, POOL_WINDOWS[1],
                          jnp.where(grp == 2, POOL_WINDOWS[2], POOL_WINDOWS[3])))
        cnt = jnp.minimum(width, pos + 1).astype(F32)
        d = wsum / cnt - z
    else:
        d = d_ref[...]
    o_pool = _dot(d.astype(BF16), pw_ref[...]) * ps_ref[...]

    mixed = jnp.concatenate([nsa_ref[...], o_gm.astype(BF16), o_pool.astype(BF16)], axis=1)
    y = ALPHA * x_ref[...] + _dot(mixed, wo_ref[...])
    o_ref[...] = _ln_rows(y, g_ref[...], b_ref[...])


def _mix_out(x, o_nsa, gu, gv, p_or_d, ws_cat, gb_full, pw_big, ps, w_o_b, g, b, *, tm, seq, pool_in_kernel):
    rows = x.shape[0]
    tiles_per_seq = max(seq // tm, 1)
    row_spec = lambda w: pl.BlockSpec((tm, w), lambda i: (i, 0))
    const = lambda shp: pl.BlockSpec(shp, lambda i: (0,) * len(shp))
    in_specs = [row_spec(D_MODEL), row_spec(NSA_W), row_spec(GM_W), row_spec(GM_W), row_spec(POOL_W)]
    args = [x, o_nsa, gu, gv, p_or_d]
    if pool_in_kernel:
        in_specs.append(pl.BlockSpec((HALO, POOL_W), lambda i: (jnp.maximum(i * (tm // HALO) - 1, 0), 0)))
        args.append(p_or_d)
    in_specs += [const((GM_CHUNK, GM_HEADS * GM_CHUNK)), const((GM_CHUNK, GM_W)), const((POOL_W, POOL_W)),
                 const((1, POOL_W)), const((D_MODEL, D_MODEL)), const((1, D_MODEL)), const((1, D_MODEL))]
    args += [ws_cat, gb_full, pw_big, ps.reshape(1, POOL_W), w_o_b, g.reshape(1, D_MODEL), b.reshape(1, D_MODEL)]
    kern = functools.partial(_mix_out_kernel, tm=tm, tiles_per_seq=tiles_per_seq, pool_in_kernel=pool_in_kernel)
    return pl.pallas_call(
        kern,
        grid=(rows // tm,),
        in_specs=in_specs,
        out_specs=row_spec(D_MODEL),
        out_shape=jax.ShapeDtypeStruct((rows, D_MODEL), F32),
        compiler_params=pltpu.CompilerParams(dimension_semantics=("parallel",), vmem_limit_bytes=VMEM_LIMIT),
        name="mix_out_prompt" if pool_in_kernel else "mix_out_sample",
    )(*args)


def _gmlp_weights(ws, gb, chunk_rows, reps):
    wm = jnp.tril(ws[:, :chunk_rows, :chunk_rows])
    bias = gb[:, :chunk_rows]
    if reps > 1:
        eye = jnp.eye(reps, dtype=F32)
        wm = jnp.einsum('hts,ab->hatbs', wm, eye).reshape(GM_HEADS, reps * chunk_rows, reps * chunk_rows)
        bias = jnp.tile(bias, (1, reps))
    ws_cat = wm.transpose(1, 0, 2).reshape(GM_CHUNK, GM_HEADS * GM_CHUNK).astype(BF16)
    gb_full = jnp.repeat(bias.T, HEAD_DIM, axis=1)
    return ws_cat, gb_full


def _pool_weights(pw):
    eye = jnp.eye(POOL_GROUPS, dtype=F32)
    return jnp.einsum('gce,gq->gcqe', pw, eye).reshape(POOL_W, POOL_W).astype(BF16)


PAGES_PER_STEP = 32
SUBS_PER_PAGE = PAGE_SIZE // CMP_STRIDE


def _cmp_sample_kernel(pt_ref, cache_ref, q_ref, pe_ref, w1_ref, w2_ref, selmap_ref, oc_ref, idx_ref,
                       pbuf, sem, xt_ref, fs_ref, *, layer, n_chunks, n_seq, dec_seq, past_len, n_sel_blocks):
    b = pl.program_id(0)
    c = pl.program_id(1)
    step = b * n_chunks + c
    slot = lax.rem(step, 2)
    pps = PAGES_PER_STEP

    def page_copies(sb, sc, sl):
        return [pltpu.make_async_copy(cache_ref.at[layer, pt_ref[sb, sc * pps + p]], pbuf.at[sl, p], sem.at[sl])
                for p in range(pps)]

    @pl.when(step == 0)
    def _():
        for cp in page_copies(b, c, slot):
            cp.start()

    @pl.when(step + 1 < n_seq * n_chunks)
    def _():
        wrap = c + 1 == n_chunks
        for cp in page_copies(jnp.where(wrap, b + 1, b), jnp.where(wrap, 0, c + 1), 1 - slot):
            cp.start()

    for cp in page_copies(b, c, slot):
        cp.wait()

    def to_rows(p, carry):
        r0 = pl.multiple_of(p * PAGE_SIZE, PAGE_SIZE)
        for kv in range(2):
            xt_ref[kv, pl.ds(r0, PAGE_SIZE), :] = pbuf[slot, p, kv].T
        return carry

    lax.fori_loop(0, pps, to_rows, 0, unroll=8)

    subs = pps * SUBS_PER_PAGE
    s0 = pl.multiple_of(c * subs, subs)
    for kv in range(2):
        xr = jnp.concatenate(
            [xt_ref[kv, pl.ds(j, subs, stride=CMP_STRIDE), :].astype(BF16) for j in range(CMP_STRIDE)], axis=1)
        fs_ref[kv, pl.ds(s0, subs), :] = _dot(xr, w1_ref[kv])

    @pl.when(c == n_chunks - 1)
    def _():
        nsub = n_chunks * subs
        kcv = []
        for kv in range(2):
            f = fs_ref[kv]
            per = _dot(pe_ref[kv].astype(BF16), w1_ref[kv])
            pe_term = per[0:1, 0:KVP] + per[1:2, KVP:2 * KVP]
            nxt = jnp.concatenate([f[1:, KVP:2 * KVP], jnp.zeros((1, KVP), F32)], axis=0)
            hid = jax.nn.gelu(f[:, 0:KVP] + nxt + pe_term)
            kcv.append(_dot(hid.astype(BF16), w2_ref[kv]).astype(BF16))
        n_kt = N_KV_HEADS * dec_seq
        rows = GROUP * n_kt
        q = q_ref[...]
        qpos = past_len + lax.rem(lax.broadcasted_iota(jnp.int32, (rows, 1), 0), dec_seq)
        cmp_end = CMP_STRIDE * lax.broadcasted_iota(jnp.int32, (1, nsub), 1) + (CMP_LEN - 1)
        p = _softmax_rows(_dot_t(q, kcv[0]), cmp_end <= qpos).astype(BF16)
        oc_ref[...] = _dot(p, kcv[1])
        imp_g = _dot(p, selmap_ref[...])
        imp = imp_g[0:n_kt]
        for g in range(1, GROUP):
            imp = imp + imp_g[g * n_kt:(g + 1) * n_kt]
        blk = lax.broadcasted_iota(jnp.int32, (1, imp.shape[1]), 1)
        cur = qpos[0:n_kt] // SLC_BLOCK
        forced = (blk == 0) | (blk == cur) | (blk == cur - 1)
        score = jnp.where(forced, FORCE_SCORE, jnp.where(blk <= cur, imp, -1.0))
        score = jnp.where(blk < n_sel_blocks, score, -jnp.inf)
        idx_ref[...] = _top_blocks_idx(score, blk)[1]


def _cmp_sample(page_table, cache_t, q_gkt, pe2, w1kv, w2kv, *, layer, dec_seq, past_len):
    n_seq, n_pages = page_table.shape
    n_chunks = n_pages // PAGES_PER_STEP
    nsub = n_pages * SUBS_PER_PAGE
    ns = (past_len + dec_seq + SLC_BLOCK - 1) // SLC_BLOCK
    ns_pad = LANES * ((ns + LANES - 1) // LANES)
    n_kt = N_KV_HEADS * dec_seq
    rows = GROUP * n_kt
    c0 = CMP_STRIDE * np.arange(nsub)[:, None]
    s0 = SLC_BLOCK * np.arange(ns_pad)[None, :]
    ov = np.clip(np.minimum(c0 + CMP_LEN, s0 + SLC_BLOCK) - np.maximum(c0, s0), 0, None) / CMP_LEN
    ov = ov * (np.arange(nsub)[:, None] < nsub - 1) * (np.arange(ns_pad)[None, :] < ns)
    selmap = jnp.asarray(ov, dtype=BF16)
    width = CMP_STRIDE * KVP
    kern = functools.partial(_cmp_sample_kernel, layer=layer, n_chunks=n_chunks, n_seq=n_seq, dec_seq=dec_seq,
                             past_len=past_len, n_sel_blocks=ns)
    const = lambda shp: pl.BlockSpec(shp, lambda b, c, pt: (0,) * len(shp))
    grid_spec = pltpu.PrefetchScalarGridSpec(
        num_scalar_prefetch=1,
        grid=(n_seq, n_chunks),
        in_specs=[pl.BlockSpec(memory_space=pl.ANY),
                  pl.BlockSpec((None, rows, LANES), lambda b, c, pt: (b, 0, 0)),
                  const((2, 8, width)), const((2, width, 2 * KVP)), const((2, KVP, KVP)), const((nsub, ns_pad))],
        out_specs=[pl.BlockSpec((None, rows, LANES), lambda b, c, pt: (b, 0, 0)),
                   pl.BlockSpec((None, n_kt, LANES), lambda b, c, pt: (b, 0, 0))],
        scratch_shapes=[pltpu.VMEM((2, PAGES_PER_STEP, 2, KVP, PAGE_SIZE), F32),
                        pltpu.SemaphoreType.DMA((2,)),
                        pltpu.VMEM((2, PAGES_PER_STEP * PAGE_SIZE, KVP), F32),
                        pltpu.VMEM((2, nsub, 2 * KVP), F32)])
    return pl.pallas_call(
        kern,
        grid_spec=grid_spec,
        out_shape=[jax.ShapeDtypeStruct((n_seq, rows, LANES), F32),
                   jax.ShapeDtypeStruct((n_seq, n_kt, LANES), jnp.int32)],
        compiler_params=pltpu.CompilerParams(
            dimension_semantics=("arbitrary", "arbitrary"), vmem_limit_bytes=VMEM_LIMIT),
        name="cmp_sample",
    )(page_table, cache_t, q_gkt, pe2, w1kv, w2kv, selmap)


def _compress_weights_kv(pe, w1, w2):
    eye = jnp.eye(N_KV_HEADS, dtype=F32)
    w1r = w1.reshape(2, 2, CMP_STRIDE, HEAD_DIM, HEAD_DIM)
    w1kv = jnp.einsum('ksjde,hg->kjhdsge', w1r, eye).reshape(2, CMP_STRIDE * KVP, 2 * KVP)
    w2kv = jnp.einsum('ked,hg->khegd', w2, eye).reshape(2, KVP, KVP)
    per = pe.reshape(2, 2, CMP_STRIDE, HEAD_DIM)
    per = jnp.broadcast_to(per[:, :, :, None, :], (2, 2, CMP_STRIDE, N_KV_HEADS, HEAD_DIM))
    pe2 = jnp.pad(per.reshape(2, 2, CMP_STRIDE * KVP), ((0, 0), (0, 6), (0, 0)))
    return pe2, w1kv.astype(BF16), w2kv.astype(BF16)


def _slc_sample_kernel(pt_ref, idx_sm_ref, cache_ref, q_ref, idxv_ref, knew_ref, win_ref, wnew_ref, oc_ref,
                       gate_ref, expand_ref, o_ref, kbuf, vbuf, sem,
                       *, layer, n_seq, n_pages, dec_seq, past_len):
    b = pl.program_id(0)
    slot = lax.rem(b, 2)
    n_kt = N_KV_HEADS * dec_seq
    rows = GROUP * n_kt
    n_past_blocks = past_len // SLC_BLOCK
    per_head = dec_seq * SLC_TOPK

    def tile_copies(sb, sl, k, i):
        kt = k * dec_seq + i // SLC_TOPK
        s = lax.rem(i, SLC_TOPK)
        j = idx_sm_ref[(sb * n_kt + kt) * SLC_TOPK + s]
        phys = pt_ref[sb, jnp.minimum(lax.shift_right_logical(j, 1), n_pages - 1)]
        return [pltpu.make_async_copy(cache_ref.at[layer, phys, kv, pl.ds(k * HEAD_DIM, HEAD_DIM), :],
                                      buf.at[sl, kt, s], sem.at[sl]) for kv, buf in ((0, kbuf), (1, vbuf))]

    def start_all(sb, sl):
        for k in range(N_KV_HEADS):
            def body(i, carry):
                for cp in tile_copies(sb, sl, k, i):
                    cp.start()
                return carry
            lax.fori_loop(0, per_head, body, 0)

    @pl.when(b == 0)
    def _():
        start_all(b, slot)

    @pl.when(b + 1 < n_seq)
    def _():
        start_all(b + 1, 1 - slot)

    for k in range(N_KV_HEADS):
        def wait_body(i, carry):
            for cp in tile_copies(b, slot, k, i):
                cp.wait()
            return carry
        lax.fori_loop(0, per_head, wait_body, 0)

    q = q_ref[...]
    qb = q.astype(BF16)
    gates = gate_ref[...]
    t_row = lax.rem(lax.broadcasted_iota(jnp.int32, (rows, 1), 0) // GROUP, dec_seq)
    t_new = lax.broadcasted_iota(jnp.int32, (1, dec_seq), 1)
    new_ok = t_new <= t_row

    wb = win_ref.shape[2]
    kpos = past_len - wb + lax.broadcasted_iota(jnp.int32, (1, wb), 1)
    dist = past_len + t_row - kpos
    w_ok = (dist >= 0) & (dist <= WINDOW) & (kpos >= 0)
    s_w = jnp.where(w_ok, _dot(qb, win_ref[0].astype(BF16)), NEG)
    s_n = jnp.where(new_ok, _dot_t(qb, wnew_ref[:, 0:KVP].astype(BF16)), NEG)
    m = jnp.maximum(jnp.max(s_w, axis=-1, keepdims=True), jnp.max(s_n, axis=-1, keepdims=True))
    e_w = jnp.where(w_ok, jnp.exp(s_w - m), 0.0)
    e_n = jnp.where(new_ok, jnp.exp(s_n - m), 0.0)
    den = jnp.sum(e_w, axis=-1, keepdims=True) + jnp.sum(e_n, axis=-1, keepdims=True)
    o_w = (_dot_t(e_w.astype(BF16), win_ref[1].astype(BF16))
           + _dot(e_n.astype(BF16), wnew_ref[:, KVP:2 * KVP].astype(BF16))) / den

    idxv = idxv_ref[...]
    lane16 = lax.broadcasted_iota(jnp.int32, (1, LANES), 1) < SLC_TOPK
    half = jnp.where(lane16 & ((idxv & 1) == 1), 1.0, 0.0).astype(BF16)
    live = jnp.where(lane16 & (idxv < n_past_blocks), 1.0, 0.0).astype(BF16)
    half_x = _dot(half, expand_ref[...])
    live_x = _dot(live, expand_ref[...])
    col = lax.broadcasted_iota(jnp.int32, (1, SLC_TOPK * PAGE_SIZE), 1)
    col_half = (lax.rem(col, PAGE_SIZE) // SLC_BLOCK).astype(F32)
    tile_ok = (live_x > 0.5) & (half_x == col_half)
    s_new = _dot_t(qb, knew_ref[:, 0:KVP].astype(BF16))
    zeros_half = jnp.zeros((GROUP, HEAD_DIM), F32)
    o_parts = []
    for kt in range(n_kt):
        k = kt // dec_seq
        r0 = kt * GROUP
        qk = q[r0:r0 + GROUP, k * HEAD_DIM:(k + 1) * HEAD_DIM].astype(BF16)
        kcat = jnp.concatenate([kbuf[slot, kt, s] for s in range(SLC_TOPK)], axis=1).astype(BF16)
        vcat = jnp.concatenate([vbuf[slot, kt, s] for s in range(SLC_TOPK)], axis=1).astype(BF16)
        ok = tile_ok[kt:kt + 1]
        nok = new_ok[r0:r0 + GROUP]
        s_s = jnp.where(ok, _dot(qk, kcat), NEG)
        s_n = jnp.where(nok, s_new[r0:r0 + GROUP], NEG)
        m = jnp.maximum(jnp.max(s_s, axis=-1, keepdims=True), jnp.max(s_n, axis=-1, keepdims=True))
        e_s = jnp.where(ok, jnp.exp(s_s - m), 0.0)
        e_n = jnp.where(nok, jnp.exp(s_n - m), 0.0)
        den = jnp.sum(e_s, axis=-1, keepdims=True) + jnp.sum(e_n, axis=-1, keepdims=True)
        v_new = knew_ref[:, KVP + k * HEAD_DIM:KVP + (k + 1) * HEAD_DIM].astype(BF16)
        o = (_dot_t(e_s.astype(BF16), vcat) + _dot(e_n.astype(BF16), v_new)) / den
        o_parts.append(jnp.concatenate([o, zeros_half] if k == 0 else [zeros_half, o], axis=1))
    o_s = jnp.concatenate(o_parts, axis=0)

    o_ref[...] = gates[:, 0:1] * oc_ref[...] + gates[:, 1:2] * o_s + gates[:, 2:3] * o_w


def _slc_sample(page_table, idx_flat, cache_t, q_ktg, idxv, kvs_new, win_t, kvw_new, o_c, gates_r,
                *, layer, dec_seq, past_len):
    n_seq, n_pages = page_table.shape
    n_kt = N_KV_HEADS * dec_seq
    rows = GROUP * n_kt
    wb = win_t.shape[-1]
    cols = SLC_TOPK * PAGE_SIZE
    expand = jnp.asarray((np.arange(LANES)[:, None] == (np.arange(cols) // PAGE_SIZE)[None, :]).astype(np.float32),
                         dtype=BF16)
    kern = functools.partial(_slc_sample_kernel, layer=layer, n_seq=n_seq, n_pages=n_pages, dec_seq=dec_seq,
                             past_len=past_len)
    per_seq = lambda r, w: pl.BlockSpec((None, r, w), lambda b, pt, ix: (b, 0, 0))
    grid_spec = pltpu.PrefetchScalarGridSpec(
        num_scalar_prefetch=2,
        grid=(n_seq,),
        in_specs=[pl.BlockSpec(memory_space=pl.ANY),
                  per_seq(rows, LANES), per_seq(n_kt, LANES), per_seq(dec_seq, KV_W),
                  pl.BlockSpec((None, None, 2, KVP, wb), lambda b, pt, ix: (layer, b, 0, 0, 0)),
                  per_seq(dec_seq, KV_W), per_seq(rows, LANES), per_seq(rows, LANES),
                  pl.BlockSpec((LANES, cols), lambda b, pt, ix: (0, 0))],
        out_specs=per_seq(rows, LANES),
        scratch_shapes=[pltpu.VMEM((2, n_kt, SLC_TOPK, HEAD_DIM, PAGE_SIZE), F32),
                        pltpu.VMEM((2, n_kt, SLC_TOPK, HEAD_DIM, PAGE_SIZE), F32),
                        pltpu.SemaphoreType.DMA((2,))])
    return pl.pallas_call(
        kern,
        grid_spec=grid_spec,
        out_shape=jax.ShapeDtypeStruct((n_seq, rows, LANES), F32),
        compiler_params=pltpu.CompilerParams(dimension_semantics=("arbitrary",), vmem_limit_bytes=VMEM_LIMIT),
        name="slc_sample",
    )(page_table, idx_flat, cache_t, q_ktg, idxv, kvs_new, win_t, kvw_new, o_c, gates_r, expand)


def _pool_sample_kernel(z_ref, d_ref, *, dec_seq, pos0):
    lane = lax.broadcasted_iota(jnp.int32, (1, POOL_W), 1)
    grp = lane // POOL_GW
    for t in range(dec_seq):
        cur = z_ref[:, POOL_HIST + t, :]
        acc = cur
        sums = {}
        for back in range(1, max(POOL_WINDOWS)):
            acc = acc + z_ref[:, POOL_HIST + t - back, :]
            if back + 1 in POOL_WINDOWS:
                sums[back + 1] = acc
        d = None
        for g, w in enumerate(POOL_WINDOWS):
            val = sums[w] / float(min(w, pos0 + t + 1)) - cur
            d = val if d is None else jnp.where(grp == g, val, d)
        d_ref[:, t, :] = d


def _pool_sample(z_ext, *, dec_seq, pos0):
    n_seq = z_ext.shape[0]
    return pl.pallas_call(
        functools.partial(_pool_sample_kernel, dec_seq=dec_seq, pos0=pos0),
        out_shape=jax.ShapeDtypeStruct((n_seq, dec_seq, POOL_W), F32),
        name="pool_sample",
    )(z_ext)


def _sample_layer(x, lw, cmp_t, slc_t, win_t, win_state, pool_state, page_table, *, layer, n_seq, dec_seq):
    rows = n_seq * dec_seq
    n_kt = N_KV_HEADS * dec_seq
    x = _ffn_ln(x, lw['ffn_in'][0], lw['ffn_out'][0], lw['ln_g'][0], lw['ln_b'][0], tm=rows)
    cos, sin = _rope_tables(PAST_LEN + jnp.arange(rows) % dec_seq)
    qt, kvc, kvs, kvw, _, _, _, _, gates, gu, gv, p = _inproj(
        x, lw['w_ext'], cos, sin, lw['gm_ln_g'], lw['gm_ln_b'], tm=rows)

    qf = jnp.swapaxes(qt.astype(F32), 1, 2).reshape(N_KV_HEADS, GROUP, n_seq, dec_seq, LANES)
    q_gkt = qf.transpose(2, 1, 0, 3, 4).reshape(n_seq, GROUP * n_kt, LANES).astype(BF16)
    q_ktg = qf.transpose(2, 0, 3, 1, 4).reshape(n_seq, GROUP * n_kt, LANES)
    o_c, idxv = _cmp_sample(page_table, cmp_t, q_gkt, lw['pe2'], lw['w1kv'], lw['w2kv'],
                            layer=layer, dec_seq=dec_seq, past_len=PAST_LEN)
    o_c = o_c.reshape(n_seq, GROUP, N_KV_HEADS, dec_seq, LANES).transpose(0, 2, 3, 1, 4).reshape(n_seq, GROUP * n_kt, LANES)
    gates_r = gates[:, :GATE_W].reshape(n_seq, dec_seq, N_KV_HEADS, GROUP, 3).transpose(0, 2, 1, 3, 4)
    gates_r = jnp.pad(gates_r.reshape(n_seq, GROUP * n_kt, 3), ((0, 0), (0, 0), (0, LANES - 3)))
    idx_flat = idxv[:, :, :SLC_TOPK].reshape(-1)
    o = _slc_sample(page_table, idx_flat, slc_t, q_ktg, idxv, kvs.reshape(n_seq, dec_seq, KV_W), win_t,
                    kvw.reshape(n_seq, dec_seq, KV_W), o_c, gates_r, layer=layer, dec_seq=dec_seq, past_len=PAST_LEN)
    o = o.reshape(n_seq, N_KV_HEADS, dec_seq, GROUP, LANES)
    o_nsa = jnp.stack([o[:, k, :, :, k * HEAD_DIM:(k + 1) * HEAD_DIM] for k in range(N_KV_HEADS)], axis=2)
    o_nsa = o_nsa.reshape(rows, NSA_W).astype(BF16)

    z_ext = jnp.concatenate([pool_state, p.reshape(n_seq, dec_seq, POOL_W)], axis=1)
    d = _pool_sample(z_ext, dec_seq=dec_seq, pos0=PAST_LEN).reshape(rows, POOL_W)
    ws_cat, gb_full = _gmlp_weights(lw['gm_ws'], lw['gm_b'], dec_seq, GM_CHUNK // dec_seq)
    x = _mix_out(x, o_nsa, gu, gv, d, ws_cat, gb_full, lw['pw_big'], lw['pool_scale'], lw['w_o'],
                 lw['ln_g'][1], lw['ln_b'][1], tm=rows, seq=dec_seq, pool_in_kernel=False)
    x = _ffn_ln(x, lw['ffn_in'][1], lw['ffn_out'][1], lw['ln_g'][2], lw['ln_b'][2], tm=rows)
    shp = (n_seq, dec_seq, 2, N_KV_HEADS, HEAD_DIM)
    new = (kvc.reshape(shp), kvs.reshape(shp),
           jnp.concatenate([win_state[:, dec_seq:], kvw.reshape(shp)], axis=1),
           z_ext[:, dec_seq:], gv.reshape(n_seq, dec_seq, GM_W))
    return x, new


def _pages_by_channel(cache):
    nd = cache.ndim
    t = jnp.transpose(cache, tuple(range(nd - 4)) + (nd - 3, nd - 2, nd - 1, nd - 4))
    return t.reshape(t.shape[:-3] + (KVP, t.shape[-1]))
def _prompt_layer(x, lw, *, batch, seq, tm, tm_ffn):
    x = _ffn_ln(x, lw['ffn_in'][0], lw['ffn_out'][0], lw['ln_g'][0], lw['ln_b'][0], tm=tm_ffn)
    cos, sin = _rope_tables(jnp.arange(seq))
    qt, kvc, kvs, kvw, ks, vts, kw, vtw, gates, gu, gv, p = _inproj(
        x, lw['w_ext'], cos, sin, lw['gm_ln_g'], lw['gm_ln_b'], tm=tm)
    kc, vct = _compress_prompt(kvc, lw['pe_rows'], lw['w1big'], lw['w2big'], batch=batch)
    o_nsa = _nsa_prompt(qt, kc, vct, ks, vts, kw, vtw, gates, batch=batch, seq=seq)
    ws_cat, gb_full = _gmlp_weights(lw['gm_ws'], lw['gm_b'], GM_CHUNK, 1)
    x = _mix_out(x, o_nsa, gu, gv, p, ws_cat, gb_full, lw['pw_big'], lw['pool_scale'], lw['w_o'],
                 lw['ln_g'][1], lw['ln_b'][1], tm=tm, seq=seq, pool_in_kernel=True)
    x = _ffn_ln(x, lw['ffn_in'][1], lw['ffn_out'][1], lw['ln_g'][2], lw['ln_b'][2], tm=tm_ffn)
    wb = min(WINDOW, seq)
    shp = (batch, seq, 2, N_KV_HEADS, HEAD_DIM)
    new = (kvc.reshape(shp), kvs.reshape(shp), kvw.reshape(shp)[:, seq - wb:],
           p.reshape(batch, seq, POOL_W)[:, seq - POOL_HIST:])
    return x, new


def _layer_weights(l, ffn_in_b, ffn_out_b, ln_g, ln_b, w_in, w_o, cmp_pe, cmp_w1, cmp_w2,
                   gm_ln_g, gm_ln_b, gm_ws, gm_b, pool_w, pool_scale):
    pe_rows, w1big, w2big = _compress_weights(cmp_pe[l], cmp_w1[l], cmp_w2[l])
    pe2, w1kv, w2kv = _compress_weights_kv(cmp_pe[l], cmp_w1[l], cmp_w2[l])
    return dict(ffn_in=ffn_in_b[l], ffn_out=ffn_out_b[l], ln_g=ln_g[l], ln_b=ln_b[l],
                w_ext=_build_w_ext(w_in[l]), w_o=w_o[l].astype(BF16),
                pe_rows=pe_rows, w1big=w1big, w2big=w2big, pe2=pe2, w1kv=w1kv, w2kv=w2kv,
                gm_ln_g=gm_ln_g[l], gm_ln_b=gm_ln_b[l], gm_ws=gm_ws[l], gm_b=gm_b[l],
                pw_big=_pool_weights(pool_w[l]), pool_scale=pool_scale[l])


def kernel(x_prompt, x_sample, cache_kv_cmp, cache_kv_slc, state_kv_win, state_pool, page_table, ln_g, ln_b, ffn_w_in, ffn_w_out, w_in, w_o, cmp_pe, cmp_w1, cmp_w2, gm_ln_g, gm_ln_b, gm_ws, gm_b, pool_w, pool_scale):
    batch, seq, _ = x_prompt.shape
    fi = ffn_w_in.astype(BF16)
    fo = ffn_w_out.astype(BF16)
    n_seq, dec_seq, _ = x_sample.shape
    xp = x_prompt.reshape(batch * seq, D_MODEL)
    xs = x_sample.reshape(n_seq * dec_seq, D_MODEL)
    cmp_t = _pages_by_channel(cache_kv_cmp)
    slc_t = _pages_by_channel(cache_kv_slc)
    win_t = _pages_by_channel(state_kv_win)
    new_p, new_s = [], []
    for l in range(DEPTH):
        lw = _layer_weights(l, fi, fo, ln_g, ln_b, w_in, w_o, cmp_pe, cmp_w1, cmp_w2, gm_ln_g, gm_ln_b, gm_ws, gm_b, pool_w, pool_scale)
        xp, st_p = _prompt_layer(xp, lw, batch=batch, seq=seq, tm=512, tm_ffn=1024)
        xs, st_s = _sample_layer(xs, lw, cmp_t, slc_t, win_t, state_kv_win[l], state_pool[l], page_table,
                                 layer=l, n_seq=n_seq, dec_seq=dec_seq)
        new_p.append(st_p)
        new_s.append(st_s)
    stk = lambda lst, i: jnp.stack([t[i] for t in lst])
    return (xp.reshape(batch, seq, D_MODEL), xs.reshape(n_seq, dec_seq, D_MODEL),
            stk(new_p, 0), stk(new_s, 0), stk(new_p, 1), stk(new_s, 1),
            stk(new_p, 2), stk(new_s, 2), stk(new_p, 3), stk(new_s, 3), stk(new_s, 4))
```

```python
import functools

import numpy as np
import jax
import jax.numpy as jnp
from jax import lax
from jax.experimental import pallas as pl
from jax.experimental.pallas import tpu as pltpu

F32 = jnp.float32
BF16 = jnp.bfloat16

D_MODEL = 1024
DEPTH = 2
PAST_LEN = 16384
PAGE_SIZE = 128
HEAD_DIM = 64
NSA_W = D_MODEL // 2
GM_W = D_MODEL // 4
POOL_W = D_MODEL // 4
N_HEADS = NSA_W // HEAD_DIM
N_KV_HEADS = 2
GROUP = N_HEADS // N_KV_HEADS
CMP_STRIDE = 16
CMP_LEN = 2 * CMP_STRIDE
SLC_BLOCK = 64
SLC_TOPK = 16
WINDOW = 512
FORCE_SCORE = 1.0e4
ROPE_THETA = 10000.0
SCALE = HEAD_DIM ** -0.5
GM_HEADS = GM_W // HEAD_DIM
GM_CHUNK = 128
POOL_GROUPS = 4
POOL_GW = POOL_W // POOL_GROUPS
POOL_WINDOWS = (2, 4, 8, 16)
POOL_HIST = max(POOL_WINDOWS) - 1
D_FF = 256 * ((8 * D_MODEL // 3 + 255) // 256)
ALPHA = (2 * DEPTH) ** 0.25
LN_EPS = 1e-5
Q_W = N_HEADS * HEAD_DIM
KV_W = 2 * N_KV_HEADS * HEAD_DIM
GATE_W = 3 * N_HEADS
N_IN = Q_W + 3 * KV_W + GATE_W + 2 * GM_W + POOL_W

LANES = 128
KVP = N_KV_HEADS * HEAD_DIM
VMEM_LIMIT = 56 * 1024 * 1024
NEG = -1e30
HALO = 16

_OFF_Q = 0
_OFF_QR = Q_W
_OFF_KV = 2 * Q_W
_OFF_GATE = _OFF_KV + 3 * 3 * KVP
_OFF_UV = _OFF_GATE + LANES
_OFF_P = _OFF_UV + 2 * GM_W
N_EXT = _OFF_P + POOL_W


def _ln_rows(y, g, b):
    mu = jnp.mean(y, axis=-1, keepdims=True)
    d = y - mu
    var = jnp.mean(d * d, axis=-1, keepdims=True)
    return d * lax.rsqrt(var + LN_EPS) * g + b


def _dot(a, b):
    return jnp.dot(a, b, preferred_element_type=F32)


def _dot_t(a, b):
    return lax.dot_general(a, b, (((1,), (1,)), ((), ())), preferred_element_type=F32)


def _ffn_kernel(x_ref, wg_ref, wu_ref, wo_ref, g_ref, b_ref, o_ref, xb_ref, *, n_chunks):
    j = pl.program_id(1)

    @pl.when(j == 0)
    def _():
        xb_ref[...] = x_ref[...].astype(BF16)

    xb = xb_ref[...]
    gate = _dot(xb, wg_ref[...])
    up = _dot(xb, wu_ref[...])
    hid = (gate * jax.nn.sigmoid(gate)) * up
    part = _dot(hid.astype(BF16), wo_ref[...])

    @pl.when(j == 0)
    def _():
        o_ref[...] = part

    if n_chunks > 2:
        @pl.when((j > 0) & (j < n_chunks - 1))
        def _():
            o_ref[...] += part

    @pl.when(j == n_chunks - 1)
    def _():
        y = ALPHA * x_ref[...] + 0.5 * (o_ref[...] + part)
        o_ref[...] = _ln_rows(y, g_ref[...], b_ref[...])


def _ffn_ln(x, w_in_b, w_out_b, g, b, *, tm):
    rows = x.shape[0]
    n_chunks = 2
    fc = D_FF // n_chunks
    return pl.pallas_call(
        functools.partial(_ffn_kernel, n_chunks=n_chunks),
        grid=(rows // tm, n_chunks),
        in_specs=[
            pl.BlockSpec((tm, D_MODEL), lambda i, j: (i, 0)),
            pl.BlockSpec((D_MODEL, fc), lambda i, j: (0, j)),
            pl.BlockSpec((D_MODEL, fc), lambda i, j: (0, n_chunks + j)),
            pl.BlockSpec((fc, D_MODEL), lambda i, j: (j, 0)),
            pl.BlockSpec((1, D_MODEL), lambda i, j: (0, 0)),
            pl.BlockSpec((1, D_MODEL), lambda i, j: (0, 0)),
        ],
        out_specs=pl.BlockSpec((tm, D_MODEL), lambda i, j: (i, 0)),
        out_shape=jax.ShapeDtypeStruct((rows, D_MODEL), F32),
        scratch_shapes=[pltpu.VMEM((tm, D_MODEL), BF16)],
        compiler_params=pltpu.CompilerParams(
            dimension_semantics=("parallel", "arbitrary"), vmem_limit_bytes=VMEM_LIMIT),
        name="ffn_ln",
    )(x, w_in_b, w_in_b, w_out_b, g.reshape(1, D_MODEL), b.reshape(1, D_MODEL))


def _inproj_kernel(*refs, prompt, n_alias):
    h_ref, w_ref, cos_ref, sin_ref, gmg_ref, gmb_ref = refs[:6]
    outs = refs[6 + n_alias:]
    if prompt:
        (qt_ref, kvcb_ref, leafc_ref, leafs_ref, leafw_ref, ks_ref, vts_ref, kw_ref, vtw_ref,
         gate_ref, gu_ref, gv_ref, p_ref) = outs
        leaves = (leafc_ref, leafs_ref, leafw_ref)
        k_refs = (None, ks_ref, kw_ref)
        vt_refs = (None, vts_ref, vtw_ref)
    else:
        qt_ref, kvc_ref, kvs_ref, kvw_ref, gate_ref, gu_ref, gv_ref, p_ref = outs
        rows_out = (kvc_ref, kvs_ref, kvw_ref)
    hb = h_ref[...].astype(BF16)
    cos = cos_ref[...]
    sin = sin_ref[...]

    zq = _dot(hb, w_ref[:, _OFF_Q:_OFF_KV])
    zk = _dot(hb, w_ref[:, _OFF_KV:_OFF_UV])
    zu = _dot(hb, w_ref[:, _OFF_UV:N_EXT])

    n_sq = hb.shape[0] // LANES
    zeros_half = jnp.zeros((HEAD_DIM, LANES), F32)

    for m in range(N_HEADS // 2):
        c0 = m * LANES
        pair = (zq[:, c0:c0 + LANES] * cos + zq[:, Q_W + c0:Q_W + c0 + LANES] * sin) * SCALE
        kvh = (2 * m) // GROUP
        for c in range(n_sq):
            pt = pair[c * LANES:(c + 1) * LANES].T
            for e in range(2):
                piece = pt[e * HEAD_DIM:(e + 1) * HEAD_DIM]
                both = [piece, zeros_half] if kvh == 0 else [zeros_half, piece]
                qt_ref[2 * m + e, :, c * LANES:(c + 1) * LANES] = jnp.concatenate(both, axis=0).astype(BF16)

    for br in range(3):
        c0 = br * 3 * KVP
        k = zk[:, c0:c0 + KVP] * cos + zk[:, c0 + KVP:c0 + 2 * KVP] * sin
        v = zk[:, c0 + 2 * KVP:c0 + 3 * KVP]
        if not prompt:
            rows_out[br][:, 0:KVP] = k
            rows_out[br][:, KVP:2 * KVP] = v
            continue
        for c in range(n_sq):
            cols = slice(c * LANES, (c + 1) * LANES)
            vt = v[cols].T
            leaves[br][0:KVP, cols] = k[cols].T
            leaves[br][KVP:2 * KVP, cols] = vt
            if br > 0:
                vt_refs[br][c] = vt.astype(BF16)
        if br == 0:
            kvcb_ref[:, 0:KVP] = k.astype(BF16)
            kvcb_ref[:, KVP:2 * KVP] = v.astype(BF16)
        else:
            k_refs[br][...] = k.astype(BF16)

    gate_ref[...] = jax.nn.sigmoid(zk[:, _OFF_GATE - _OFF_KV:_OFF_UV - _OFF_KV])
    gu_ref[...] = jax.nn.gelu(zu[:, 0:GM_W])
    gv_ref[...] = _ln_rows(jax.nn.gelu(zu[:, GM_W:2 * GM_W]), gmg_ref[...], gmb_ref[...])
    p_ref[...] = zu[:, 2 * GM_W:2 * GM_W + POOL_W]


def _inproj(h, w_ext, cos, sin, gmg, gmb, *, tm, leaf_bufs=None, layer=0, batch=None):
    rows = h.shape[0]
    n_tab = cos.shape[0] // tm
    prompt = leaf_bufs is not None
    row_spec = lambda w: pl.BlockSpec((tm, w), lambda i: (i, 0))
    tab_spec = pl.BlockSpec((tm, LANES), lambda i: (i % n_tab, 0))
    vec_spec = pl.BlockSpec((1, GM_W), lambda i: (0, 0))
    sds = jax.ShapeDtypeStruct
    in_specs = [row_spec(D_MODEL), pl.BlockSpec((D_MODEL, N_EXT), lambda i: (0, 0)),
                tab_spec, tab_spec, vec_spec, vec_spec]
    args = [h, w_ext, cos, sin, gmg.reshape(1, GM_W), gmb.reshape(1, GM_W)]
    tail_specs = [row_spec(LANES), row_spec(GM_W), row_spec(GM_W), row_spec(POOL_W)]
    tail_shapes = [sds((rows, LANES), F32), sds((rows, GM_W), F32), sds((rows, GM_W), F32), sds((rows, POOL_W), F32)]
    qt_spec = pl.BlockSpec((N_HEADS, LANES, tm), lambda i: (0, 0, i))
    qt_shape = sds((N_HEADS, LANES, rows), BF16)
    aliases = {}
    if prompt:
        seq = rows // batch
        tiles = seq // tm
        leaf_spec = pl.BlockSpec((None, None, KV_W, tm), lambda i: (layer, i // tiles, 0, i % tiles))
        leaf_shape = sds((DEPTH, batch, KV_W, seq), F32)
        sq_spec = pl.BlockSpec((tm // LANES, KVP, LANES), lambda i: (i, 0, 0))
        sq_shape = sds((rows // LANES, KVP, LANES), BF16)
        out_specs = [qt_spec, row_spec(KV_W), leaf_spec, leaf_spec, leaf_spec,
                     row_spec(KVP), sq_spec, row_spec(KVP), sq_spec] + tail_specs
        out_shape = [qt_shape, sds((rows, KV_W), BF16), leaf_shape, leaf_shape, leaf_shape,
                     sds((rows, KVP), BF16), sq_shape, sds((rows, KVP), BF16), sq_shape] + tail_shapes
        for n, buf in enumerate(leaf_bufs):
            in_specs.append(pl.BlockSpec(memory_space=pl.ANY))
            args.append(buf)
            aliases[6 + n] = 2 + n
    else:
        out_specs = [qt_spec, row_spec(KV_W), row_spec(KV_W), row_spec(KV_W)] + tail_specs
        out_shape = [qt_shape, sds((rows, KV_W), F32), sds((rows, KV_W), F32), sds((rows, KV_W), F32)] + tail_shapes
    return pl.pallas_call(
        functools.partial(_inproj_kernel, prompt=prompt, n_alias=len(aliases)),
        grid=(rows // tm,),
        in_specs=in_specs,
        out_specs=out_specs,
        out_shape=out_shape,
        input_output_aliases=aliases,
        compiler_params=pltpu.CompilerParams(dimension_semantics=("parallel",), vmem_limit_bytes=VMEM_LIMIT),
        name="inproj",
    )(*args)


def _build_w_ext(w_in):
    half = HEAD_DIM // 2

    def rot(w):
        n = w.shape[1] // HEAD_DIM
        w3 = w.reshape(D_MODEL, n, 2, half)
        return jnp.stack([-w3[:, :, 1], w3[:, :, 0]], axis=2).reshape(D_MODEL, n * HEAD_DIM)

    q = w_in[:, :Q_W]
    cols = [q, rot(q)]
    for br in range(3):
        kv = w_in[:, Q_W + br * KV_W:Q_W + (br + 1) * KV_W]
        k, v = kv[:, :KVP], kv[:, KVP:]
        cols += [k, rot(k), v]
    g0 = Q_W + 3 * KV_W
    cols.append(jnp.pad(w_in[:, g0:g0 + GATE_W], ((0, 0), (0, LANES - GATE_W))))
    cols.append(w_in[:, g0 + GATE_W:])
    return jnp.concatenate(cols, axis=1).astype(BF16)


def _rope_tables(pos):
    half = HEAD_DIM // 2
    inv = ROPE_THETA ** (-jnp.arange(half, dtype=F32) / half)
    ang = pos.astype(F32)[:, None] * inv[None, :]
    cos = jnp.tile(jnp.cos(ang), (1, LANES // half))
    sin = jnp.tile(jnp.sin(ang), (1, LANES // half))
    return cos, sin


def _compress_kernel(x_ref, pe_ref, w1_ref, w2_ref, kc_ref, vct_ref):
    w1 = w1_ref[...]
    fs = _dot(x_ref[...].astype(BF16), w1)
    per = _dot(pe_ref[...].astype(BF16), w1)
    pe_term = per[0:1, 0:2 * KVP] + per[1:2, 2 * KVP:4 * KVP]
    first = fs[:, 0:2 * KVP]
    second = fs[:, 2 * KVP:4 * KVP]
    nxt = jnp.concatenate([second[1:], jnp.zeros((1, 2 * KVP), F32)], axis=0)
    hid = jax.nn.gelu(first + nxt + pe_term)
    out = _dot(hid.astype(BF16), w2_ref[...])
    kc_ref[...] = out[:, 0:KVP].astype(BF16)
    for c in range(out.shape[0] // LANES):
        vct_ref[:, c * LANES:(c + 1) * LANES] = out[c * LANES:(c + 1) * LANES, KVP:2 * KVP].T.astype(BF16)


def _compress_weights(pe, w1, w2):
    eye = jnp.eye(N_KV_HEADS, dtype=F32)
    w1r = w1.reshape(2, 2, CMP_STRIDE, HEAD_DIM, HEAD_DIM)
    w1big = jnp.einsum('ksjde,kq,hg->jkhdsqge', w1r, eye, eye).reshape(CMP_STRIDE * 2 * KVP, 4 * KVP)
    w2big = jnp.einsum('ked,kq,hg->kheqgd', w2, eye, eye).reshape(2 * KVP, 2 * KVP)
    per = pe.reshape(2, 2, CMP_STRIDE, HEAD_DIM).transpose(1, 2, 0, 3)
    per = jnp.broadcast_to(per[:, :, :, None, :], (2, CMP_STRIDE, 2, N_KV_HEADS, HEAD_DIM))
    pe_rows = jnp.pad(per.reshape(2, CMP_STRIDE * 2 * KVP), ((0, 6), (0, 0)))
    return pe_rows, w1big.astype(BF16), w2big.astype(BF16)


def _compress_prompt(kvc, pe_rows, w1big, w2big, *, batch):
    nsub = kvc.shape[0] // batch // CMP_STRIDE
    width = CMP_STRIDE * 2 * KVP
    x = kvc.reshape(batch, nsub, width)
    return pl.pallas_call(
        _compress_kernel,
        grid=(batch,),
        in_specs=[pl.BlockSpec((None, nsub, width), lambda b: (b, 0, 0)),
                  pl.BlockSpec((8, width), lambda b: (0, 0)),
                  pl.BlockSpec((width, 4 * KVP), lambda b: (0, 0)),
                  pl.BlockSpec((2 * KVP, 2 * KVP), lambda b: (0, 0))],
        out_specs=[pl.BlockSpec((None, nsub, KVP), lambda b: (b, 0, 0)),
                   pl.BlockSpec((None, KVP, nsub), lambda b: (b, 0, 0))],
        out_shape=[jax.ShapeDtypeStruct((batch, nsub, KVP), BF16), jax.ShapeDtypeStruct((batch, KVP, nsub), BF16)],
        compiler_params=pltpu.CompilerParams(dimension_semantics=("parallel",), vmem_limit_bytes=VMEM_LIMIT),
        name="compress_prompt",
    )(x, pe_rows, w1big, w2big)


def _top_blocks_cols(score, blk):
    sel = jnp.zeros(score.shape, F32)
    for _ in range(SLC_TOPK):
        m = jnp.max(score, axis=0, keepdims=True)
        first = jnp.min(jnp.where(score == m, blk, 1e9), axis=0, keepdims=True)
        hit = blk == first
        sel = jnp.where(hit, 1.0, sel)
        score = jnp.where(hit, -jnp.inf, score)
    return sel


def _top_blocks_idx(score, blk):
    sel = jnp.zeros(score.shape, F32)
    idx = jnp.zeros((score.shape[0], LANES), jnp.int32)
    lane = lax.broadcasted_iota(jnp.int32, (1, LANES), 1)
    big = jnp.int32(1 << 20)
    for it in range(SLC_TOPK):
        m = jnp.max(score, axis=-1, keepdims=True)
        first = jnp.min(jnp.where(score == m, blk, big), axis=-1, keepdims=True)
        hit = blk == first
        sel = jnp.where(hit, 1.0, sel)
        idx = jnp.where(lane == it, first, idx)
        score = jnp.where(hit, -jnp.inf, score)
    return sel, idx


def _softmax_rows(s, mask):
    s = jnp.where(mask, s, NEG)
    m = jnp.max(s, axis=-1, keepdims=True)
    e = jnp.exp(s - m)
    return jnp.where(mask, e / jnp.sum(e, axis=-1, keepdims=True), 0.0)


def _nsa_prompt_kernel(qt_ref, kc_ref, vct_ref, ks_ref, vts_ref, kw_ref, vtw_ref, gate_ref, selmap_ref, expand_ref,
                       o_ref, *, tq, tk, seq, n_sel_blocks):
    t0 = pl.program_id(1) * tq
    nsub = kc_ref.shape[0]
    nb = 8 * ((n_sel_blocks + 7) // 8)
    span = min(WINDOW + tq, seq)
    qpos = t0 + lax.broadcasted_iota(jnp.int32, (1, tq), 1)
    rep = lambda x, n: jnp.concatenate([x] * n, axis=1)
    qt = jnp.concatenate([qt_ref[h] for h in range(N_HEADS)], axis=1)

    cmp_end = CMP_STRIDE * lax.broadcasted_iota(jnp.int32, (nsub, 1), 0) + (CMP_LEN - 1)
    c_bias = jnp.where(cmp_end <= qpos, 0.0, NEG)
    s = _dot(kc_ref[...], qt) + rep(c_bias, N_HEADS)
    e = jnp.exp(s - jnp.max(s, axis=0, keepdims=True))
    inv_c = jnp.where(rep(qpos >= CMP_LEN - 1, N_HEADS), 1.0 / jnp.sum(e, axis=0, keepdims=True), 0.0)
    eb = e.astype(BF16)
    o_c = _dot(vct_ref[...], eb) * inv_c
    imp_h = _dot(selmap_ref[...], eb)[0:nb] * inv_c

    imp = []
    for k in range(N_KV_HEADS):
        acc = None
        for g in range(GROUP):
            part = imp_h[:, (GROUP * k + g) * tq:(GROUP * k + g + 1) * tq]
            acc = part if acc is None else acc + part
        imp.append(acc)
    imp = jnp.concatenate(imp, axis=1)
    blk = lax.broadcasted_iota(jnp.int32, (nb, 1), 0)
    cur = rep(qpos // SLC_BLOCK, N_KV_HEADS)
    forced = (blk == 0) | (blk == cur) | (blk == cur - 1)
    score = jnp.where(forced, FORCE_SCORE, jnp.where(blk <= cur, imp, -1.0))
    if nb > n_sel_blocks:
        score = jnp.where(blk < n_sel_blocks, score, -jnp.inf)
    sel = _top_blocks_cols(score, blk.astype(F32))
    sel_m1 = jnp.concatenate([sel - 1.0, jnp.zeros((LANES - nb, N_KV_HEADS * tq), F32)], axis=0).astype(BF16)

    start = pl.multiple_of(jnp.maximum(t0 + tq - span, 0), tq)
    dist = qpos - (start + lax.broadcasted_iota(jnp.int32, (span, 1), 0))
    w_bias = jnp.where(dist >= 0, jnp.where(dist <= WINDOW, 0.0, NEG), NEG)
    s = _dot(kw_ref[pl.ds(start, span), :], qt) + rep(w_bias, N_HEADS)
    e = jnp.exp(s - jnp.max(s, axis=0, keepdims=True))
    inv_w = 1.0 / jnp.sum(e, axis=0, keepdims=True)
    c0 = start // LANES
    vt = jnp.concatenate([vtw_ref[c0 + c] for c in range(span // LANES)], axis=1)
    o_w = _dot(vt, e.astype(BF16)) * inv_w

    def tile(kt, carry, diagonal):
        m_i, l_i, acc = carry
        r0 = pl.multiple_of(kt * tk, tk)
        bias = _dot(expand_ref[pl.ds(r0, tk), :], sel_m1)
        if diagonal:
            kpos = r0 + lax.broadcasted_iota(jnp.int32, (tk, 1), 0)
            bias = jnp.where(kpos <= rep(qpos, N_KV_HEADS), bias, NEG)
        s = _dot(ks_ref[pl.ds(r0, tk), :], qt)
        s = jnp.concatenate([s[:, h * tq:(h + 1) * tq] + bias[:, (h // GROUP) * tq:(h // GROUP + 1) * tq]
                             for h in range(N_HEADS)], axis=1)
        m_new = jnp.maximum(m_i, jnp.max(s, axis=0, keepdims=True))
        a = jnp.exp(m_i - m_new)
        e = jnp.exp(s - m_new)
        l_new = a * l_i + jnp.sum(e, axis=0, keepdims=True)
        c0 = kt * (tk // LANES)
        vt = jnp.concatenate([vts_ref[c0 + c] for c in range(tk // LANES)], axis=1)
        return m_new, l_new, a * acc + _dot(vt, e.astype(BF16))

    n_kt = (t0 + tq + tk - 1) // tk
    init = (jnp.full((1, N_HEADS * tq), NEG, F32), jnp.zeros((1, N_HEADS * tq), F32),
            jnp.zeros((KVP, N_HEADS * tq), F32))
    carry = lax.fori_loop(0, n_kt - 1, lambda kt, c: tile(kt, c, False), init)
    _, l_s, acc_s = tile(n_kt - 1, carry, True)
    o_s = acc_s * (1.0 / l_s)

    gt = gate_ref[...].T
    parts = []
    for h in range(N_HEADS):
        rows = slice((h // GROUP) * HEAD_DIM, (h // GROUP + 1) * HEAD_DIM)
        cols = slice(h * tq, (h + 1) * tq)
        parts.append(gt[3 * h:3 * h + 1] * o_c[rows, cols] + gt[3 * h + 1:3 * h + 2] * o_s[rows, cols]
                     + gt[3 * h + 2:3 * h + 3] * o_w[rows, cols])
    ot = jnp.concatenate(parts, axis=0)
    for m in range(NSA_W // LANES):
        o_ref[:, m * LANES:(m + 1) * LANES] = ot[m * LANES:(m + 1) * LANES].T.astype(o_ref.dtype)


def _sel_map_t(nc_rows, n_cmp, ns):
    c0 = CMP_STRIDE * np.arange(nc_rows)[None, :]
    s0 = SLC_BLOCK * np.arange(LANES)[:, None]
    ov = np.clip(np.minimum(c0 + CMP_LEN, s0 + SLC_BLOCK) - np.maximum(c0, s0), 0, None) / CMP_LEN
    ov = ov * (np.arange(nc_rows)[None, :] < n_cmp) * (np.arange(LANES)[:, None] < ns)
    return jnp.asarray(ov, dtype=BF16)


def _expand_map(seq):
    e = (np.arange(LANES)[None, :] == (np.arange(seq) // SLC_BLOCK)[:, None]).astype(np.float32) * -NEG
    return jnp.asarray(e, dtype=BF16)


def _nsa_prompt(qt, kc, vct, ks, vts, kw, vtw, gates, *, batch, seq, tq=128, tk=512):
    tk = min(tk, seq)
    nq = seq // tq
    nsub = seq // CMP_STRIDE
    ns = seq // SLC_BLOCK
    assert ns <= LANES and seq % tk == 0 and tk % tq == 0 and tq == LANES
    kern = functools.partial(_nsa_prompt_kernel, tq=tq, tk=tk, seq=seq, n_sel_blocks=ns)
    per_batch = lambda shp: pl.BlockSpec(shp, lambda b, i: (b,) + (0,) * (len(shp) - 1))
    const = lambda shp: pl.BlockSpec(shp, lambda b, i: (0,) * len(shp))
    return pl.pallas_call(
        kern,
        grid=(batch, nq),
        in_specs=[pl.BlockSpec((N_HEADS, KVP, tq), lambda b, i: (0, 0, b * nq + i)),
                  per_batch((None, nsub, KVP)), per_batch((None, KVP, nsub)),
                  per_batch((seq, KVP)), per_batch((seq // LANES, KVP, LANES)),
                  per_batch((seq, KVP)), per_batch((seq // LANES, KVP, LANES)),
                  pl.BlockSpec((tq, LANES), lambda b, i: (b * nq + i, 0)),
                  const((LANES, nsub)), const((seq, LANES))],
        out_specs=pl.BlockSpec((tq, NSA_W), lambda b, i: (b * nq + i, 0)),
        out_shape=jax.ShapeDtypeStruct((batch * seq, NSA_W), BF16),
        compiler_params=pltpu.CompilerParams(
            dimension_semantics=("parallel", "arbitrary"), vmem_limit_bytes=VMEM_LIMIT),
        name="nsa_prompt",
    )(qt, kc, vct, ks, vts, kw, vtw, gates, _sel_map_t(nsub, nsub - 1, ns), _expand_map(seq))


def _pool_windows(z_ext, tm):
    s2 = z_ext[1:] + z_ext[:-1]
    s4 = s2[2:] + s2[:-2]
    s8 = s4[4:] + s4[:-4]
    s16 = s8[8:] + s8[:-8]
    return (s2[HALO - 1:HALO - 1 + tm], s4[HALO - 3:HALO - 3 + tm], s8[HALO - 7:HALO - 7 + tm],
            s16[HALO - 15:HALO - 15 + tm])


def _mix_out_kernel(*refs, tm, tiles_per_seq, pool_in_kernel):
    if pool_in_kernel:
        (x_ref, nsa_ref, gu_ref, gv_ref, p_ref, halo_ref, ws_ref, gb_ref, pw_ref, ps_ref, wo_ref,
         g_ref, b_ref, o_ref) = refs
    else:
        (x_ref, nsa_ref, gu_ref, gv_ref, d_ref, ws_ref, gb_ref, pw_ref, ps_ref, wo_ref,
         g_ref, b_ref, o_ref) = refs
    lane = lax.broadcasted_iota(jnp.int32, (1, GM_W), 1)

    parts = []
    for c in range(tm // GM_CHUNK):
        v = gv_ref[c * GM_CHUNK:(c + 1) * GM_CHUNK, :]
        stacked = jnp.concatenate(
            [jnp.where(lane // HEAD_DIM == h, v, 0.0) for h in range(GM_HEADS)], axis=0).astype(BF16)
        s = _dot(ws_ref[...], stacked) + gb_ref[...]
        parts.append(gu_ref[c * GM_CHUNK:(c + 1) * GM_CHUNK, :] * s)
    o_gm = parts[0] if len(parts) == 1 else jnp.concatenate(parts, axis=0)

    if pool_in_kernel:
        first_tile = (pl.program_id(0) % tiles_per_seq) == 0
        halo = jnp.where(first_tile, 0.0, halo_ref[...])
        z = p_ref[...]
        wins = _pool_windows(jnp.concatenate([halo, z], axis=0), tm)
        pos = (pl.program_id(0) % tiles_per_seq) * tm + lax.broadcasted_iota(jnp.int32, (tm, 1), 0)
        grp = lane // POOL_GW
        wsum = jnp.where(grp == 0, wins[0], jnp.where(grp == 1, wins[1], jnp.where(grp == 2, wins[2], wins[3])))
        width = jnp.where(grp == 0, POOL_WINDOWS[0], jnp.where(grp == 1, POOL_WINDOWS[1],
                          jnp.where(grp == 2, POOL_WINDOWS[2], POOL_WINDOWS[3])))
        cnt = jnp.minimum(width, pos + 1).astype(F32)
        d = wsum / cnt - z
    else:
        d = d_ref[...]
    o_pool = _dot(d.astype(BF16), pw_ref[...]) * ps_ref[...]

    mixed = jnp.concatenate([nsa_ref[...], o_gm.astype(BF16), o_pool.astype(BF16)], axis=1)
    y = ALPHA * x_ref[...] + _dot(mixed, wo_ref[...])
    o_ref[...] = _ln_rows(y, g_ref[...], b_ref[...])


def _mix_out(x, o_nsa, gu, gv, p_or_d, ws_cat, gb_full, pw_big, ps, w_o_b, g, b, *, tm, seq, pool_in_kernel):
    rows = x.shape[0]
    tiles_per_seq = max(seq // tm, 1)
    row_spec = lambda w: pl.BlockSpec((tm, w), lambda i: (i, 0))
    const = lambda shp: pl.BlockSpec(shp, lambda i: (0,) * len(shp))
    in_specs = [row_spec(D_MODEL), row_spec(NSA_W), row_spec(GM_W), row_spec(GM_W), row_spec(POOL_W)]
    args = [x, o_nsa, gu, gv, p_or_d]
    if pool_in_kernel:
        in_specs.append(pl.BlockSpec((HALO, POOL_W), lambda i: (jnp.maximum(i * (tm // HALO) - 1, 0), 0)))
        args.append(p_or_d)
    in_specs += [const((GM_CHUNK, GM_HEADS * GM_CHUNK)), const((GM_CHUNK, GM_W)), const((POOL_W, POOL_W)),
                 const((1, POOL_W)), const((D_MODEL, D_MODEL)), const((1, D_MODEL)), const((1, D_MODEL))]
    args += [ws_cat, gb_full, pw_big, ps.reshape(1, POOL_W), w_o_b, g.reshape(1, D_MODEL), b.reshape(1, D_MODEL)]
    kern = functools.partial(_mix_out_kernel, tm=tm, tiles_per_seq=tiles_per_seq, pool_in_kernel=pool_in_kernel)
    return pl.pallas_call(
        kern,
        grid=(rows // tm,),
        in_specs=in_specs,
        out_specs=row_spec(D_MODEL),
        out_shape=jax.ShapeDtypeStruct((rows, D_MODEL), F32),
        compiler_params=pltpu.CompilerParams(dimension_semantics=("parallel",), vmem_limit_bytes=VMEM_LIMIT),
        name="mix_out_prompt" if pool_in_kernel else "mix_out_sample",
    )(*args)


def _gmlp_weights(ws, gb, chunk_rows, reps):
    wm = jnp.tril(ws[:, :chunk_rows, :chunk_rows])
    bias = gb[:, :chunk_rows]
    if reps > 1:
        eye = jnp.eye(reps, dtype=F32)
        wm = jnp.einsum('hts,ab->hatbs', wm, eye).reshape(GM_HEADS, reps * chunk_rows, reps * chunk_rows)
        bias = jnp.tile(bias, (1, reps))
    ws_cat = wm.transpose(1, 0, 2).reshape(GM_CHUNK, GM_HEADS * GM_CHUNK).astype(BF16)
    gb_full = jnp.repeat(bias.T, HEAD_DIM, axis=1)
    return ws_cat, gb_full


def _pool_weights(pw):
    eye = jnp.eye(POOL_GROUPS, dtype=F32)
    return jnp.einsum('gce,gq->gcqe', pw, eye).reshape(POOL_W, POOL_W).astype(BF16)


PAGES_PER_STEP = 32
SUBS_PER_PAGE = PAGE_SIZE // CMP_STRIDE


def _cmp_sample_kernel(pt_ref, cache_ref, q_ref, pe_ref, w1_ref, w2_ref, selmap_ref, oc_ref, idx_ref,
                       pbuf, sem, xt_ref, fs_ref, *, layer, n_chunks, n_seq, dec_seq, past_len, n_sel_blocks):
    b = pl.program_id(0)
    c = pl.program_id(1)
    step = b * n_chunks + c
    slot = lax.rem(step, 2)
    pps = PAGES_PER_STEP

    def page_copies(sb, sc, sl):
        return [pltpu.make_async_copy(cache_ref.at[layer, pt_ref[sb, sc * pps + p]], pbuf.at[sl, p], sem.at[sl])
                for p in range(pps)]

    @pl.when(step == 0)
    def _():
        for cp in page_copies(b, c, slot):
            cp.start()

    @pl.when(step + 1 < n_seq * n_chunks)
    def _():
        wrap = c + 1 == n_chunks
        for cp in page_copies(jnp.where(wrap, b + 1, b), jnp.where(wrap, 0, c + 1), 1 - slot):
            cp.start()

    for cp in page_copies(b, c, slot):
        cp.wait()

    def to_rows(p, carry):
        r0 = pl.multiple_of(p * PAGE_SIZE, PAGE_SIZE)
        for kv in range(2):
            xt_ref[kv, pl.ds(r0, PAGE_SIZE), :] = pbuf[slot, p, kv].T
        return carry

    lax.fori_loop(0, pps, to_rows, 0, unroll=8)

    subs = pps * SUBS_PER_PAGE
    s0 = pl.multiple_of(c * subs, subs)
    for kv in range(2):
        xr = jnp.concatenate(
            [xt_ref[kv, pl.ds(j, subs, stride=CMP_STRIDE), :].astype(BF16) for j in range(CMP_STRIDE)], axis=1)
        fs_ref[kv, pl.ds(s0, subs), :] = _dot(xr, w1_ref[kv])

    @pl.when(c == n_chunks - 1)
    def _():
        nsub = n_chunks * subs
        kcv = []
        for kv in range(2):
            f = fs_ref[kv]
            per = _dot(pe_ref[kv].astype(BF16), w1_ref[kv])
            pe_term = per[0:1, 0:KVP] + per[1:2, KVP:2 * KVP]
            nxt = jnp.concatenate([f[1:, KVP:2 * KVP], jnp.zeros((1, KVP), F32)], axis=0)
            hid = jax.nn.gelu(f[:, 0:KVP] + nxt + pe_term)
            kcv.append(_dot(hid.astype(BF16), w2_ref[kv]).astype(BF16))
        n_kt = N_KV_HEADS * dec_seq
        rows = GROUP * n_kt
        q = q_ref[...]
        qpos = past_len + lax.rem(lax.broadcasted_iota(jnp.int32, (rows, 1), 0), dec_seq)
        cmp_end = CMP_STRIDE * lax.broadcasted_iota(jnp.int32, (1, nsub), 1) + (CMP_LEN - 1)
        p = _softmax_rows(_dot_t(q, kcv[0]), cmp_end <= qpos).astype(BF16)
        oc_ref[...] = _dot(p, kcv[1])
        imp_g = _dot(p, selmap_ref[...])
        imp = imp_g[0:n_kt]
        for g in range(1, GROUP):
            imp = imp + imp_g[g * n_kt:(g + 1) * n_kt]
        blk = lax.broadcasted_iota(jnp.int32, (1, imp.shape[1]), 1)
        cur = qpos[0:n_kt] // SLC_BLOCK
        forced = (blk == 0) | (blk == cur) | (blk == cur - 1)
        score = jnp.where(forced, FORCE_SCORE, jnp.where(blk <= cur, imp, -1.0))
        score = jnp.where(blk < n_sel_blocks, score, -jnp.inf)
        idx_ref[...] = _top_blocks_idx(score, blk)[1]


def _cmp_sample(page_table, cache_t, q_gkt, pe2, w1kv, w2kv, *, layer, dec_seq, past_len):
    n_seq, n_pages = page_table.shape
    n_chunks = n_pages // PAGES_PER_STEP
    nsub = n_pages * SUBS_PER_PAGE
    ns = (past_len + dec_seq + SLC_BLOCK - 1) // SLC_BLOCK
    ns_pad = LANES * ((ns + LANES - 1) // LANES)
    n_kt = N_KV_HEADS * dec_seq
    rows = GROUP * n_kt
    c0 = CMP_STRIDE * np.arange(nsub)[:, None]
    s0 = SLC_BLOCK * np.arange(ns_pad)[None, :]
    ov = np.clip(np.minimum(c0 + CMP_LEN, s0 + SLC_BLOCK) - np.maximum(c0, s0), 0, None) / CMP_LEN
    ov = ov * (np.arange(nsub)[:, None] < nsub - 1) * (np.arange(ns_pad)[None, :] < ns)
    selmap = jnp.asarray(ov, dtype=BF16)
    width = CMP_STRIDE * KVP
    kern = functools.partial(_cmp_sample_kernel, layer=layer, n_chunks=n_chunks, n_seq=n_seq, dec_seq=dec_seq,
                             past_len=past_len, n_sel_blocks=ns)
    const = lambda shp: pl.BlockSpec(shp, lambda b, c, pt: (0,) * len(shp))
    grid_spec = pltpu.PrefetchScalarGridSpec(
        num_scalar_prefetch=1,
        grid=(n_seq, n_chunks),
        in_specs=[pl.BlockSpec(memory_space=pl.ANY),
                  pl.BlockSpec((None, rows, LANES), lambda b, c, pt: (b, 0, 0)),
                  const((2, 8, width)), const((2, width, 2 * KVP)), const((2, KVP, KVP)), const((nsub, ns_pad))],
        out_specs=[pl.BlockSpec((None, rows, LANES), lambda b, c, pt: (b, 0, 0)),
                   pl.BlockSpec((None, n_kt, LANES), lambda b, c, pt: (b, 0, 0))],
        scratch_shapes=[pltpu.VMEM((2, PAGES_PER_STEP, 2, KVP, PAGE_SIZE), F32),
                        pltpu.SemaphoreType.DMA((2,)),
                        pltpu.VMEM((2, PAGES_PER_STEP * PAGE_SIZE, KVP), F32),
                        pltpu.VMEM((2, nsub, 2 * KVP), F32)])
    return pl.pallas_call(
        kern,
        grid_spec=grid_spec,
        out_shape=[jax.ShapeDtypeStruct((n_seq, rows, LANES), F32),
                   jax.ShapeDtypeStruct((n_seq, n_kt, LANES), jnp.int32)],
        compiler_params=pltpu.CompilerParams(
            dimension_semantics=("arbitrary", "arbitrary"), vmem_limit_bytes=VMEM_LIMIT),
        name="cmp_sample",
    )(page_table, cache_t, q_gkt, pe2, w1kv, w2kv, selmap)


def _compress_weights_kv(pe, w1, w2):
    eye = jnp.eye(N_KV_HEADS, dtype=F32)
    w1r = w1.reshape(2, 2, CMP_STRIDE, HEAD_DIM, HEAD_DIM)
    w1kv = jnp.einsum('ksjde,hg->kjhdsge', w1r, eye).reshape(2, CMP_STRIDE * KVP, 2 * KVP)
    w2kv = jnp.einsum('ked,hg->khegd', w2, eye).reshape(2, KVP, KVP)
    per = pe.reshape(2, 2, CMP_STRIDE, HEAD_DIM)
    per = jnp.broadcast_to(per[:, :, :, None, :], (2, 2, CMP_STRIDE, N_KV_HEADS, HEAD_DIM))
    pe2 = jnp.pad(per.reshape(2, 2, CMP_STRIDE * KVP), ((0, 0), (0, 6), (0, 0)))
    return pe2, w1kv.astype(BF16), w2kv.astype(BF16)


def _slc_sample_kernel(*refs, layer, n_seq, n_pages, dec_seq, past_len, n_alias):
    (pt_ref, idx_sm_ref, cache_ref, q_ref, idxv_ref, knew_ref, win_ref, wnew_ref, oc_ref, gate_ref, expand_ref,
     wnewt_ref) = refs[:12]
    o_ref, wout_ref, kbuf, vbuf, sem = refs[12 + n_alias:]
    b = pl.program_id(0)
    slot = lax.rem(b, 2)
    n_kt = N_KV_HEADS * dec_seq
    rows = GROUP * n_kt
    n_past_blocks = past_len // SLC_BLOCK
    per_head = dec_seq * SLC_TOPK

    def tile_copies(sb, sl, k, i):
        kt = k * dec_seq + i // SLC_TOPK
        s = lax.rem(i, SLC_TOPK)
        j = idx_sm_ref[(sb * n_kt + kt) * SLC_TOPK + s]
        phys = pt_ref[sb, jnp.minimum(lax.shift_right_logical(j, 1), n_pages - 1)]
        return [pltpu.make_async_copy(cache_ref.at[layer, phys, kv, pl.ds(k * HEAD_DIM, HEAD_DIM), :],
                                      buf.at[sl, kt, s], sem.at[sl]) for kv, buf in ((0, kbuf), (1, vbuf))]

    def start_all(sb, sl):
        for k in range(N_KV_HEADS):
            def body(i, carry):
                for cp in tile_copies(sb, sl, k, i):
                    cp.start()
                return carry
            lax.fori_loop(0, per_head, body, 0)

    @pl.when(b == 0)
    def _():
        start_all(b, slot)

    @pl.when(b + 1 < n_seq)
    def _():
        start_all(b + 1, 1 - slot)

    for k in range(N_KV_HEADS):
        def wait_body(i, carry):
            for cp in tile_copies(b, slot, k, i):
                cp.wait()
            return carry
        lax.fori_loop(0, per_head, wait_body, 0)

    q = q_ref[...]
    qb = q.astype(BF16)
    gates = gate_ref[...]
    t_row = lax.rem(lax.broadcasted_iota(jnp.int32, (rows, 1), 0) // GROUP, dec_seq)
    t_new = lax.broadcasted_iota(jnp.int32, (1, dec_seq), 1)
    new_ok = t_new <= t_row

    wb = win_ref.shape[2]
    kpos = past_len - wb + lax.broadcasted_iota(jnp.int32, (1, wb), 1)
    dist = past_len + t_row - kpos
    w_ok = (dist >= 0) & (dist <= WINDOW) & (kpos >= 0)
    s_w = jnp.where(w_ok, _dot(qb, win_ref[0].astype(BF16)), NEG)
    s_n = jnp.where(new_ok, _dot_t(qb, wnew_ref[:, 0:KVP].astype(BF16)), NEG)
    m = jnp.maximum(jnp.max(s_w, axis=-1, keepdims=True), jnp.max(s_n, axis=-1, keepdims=True))
    e_w = jnp.where(w_ok, jnp.exp(s_w - m), 0.0)
    e_n = jnp.where(new_ok, jnp.exp(s_n - m), 0.0)
    den = jnp.sum(e_w, axis=-1, keepdims=True) + jnp.sum(e_n, axis=-1, keepdims=True)
    o_w = (_dot_t(e_w.astype(BF16), win_ref[1].astype(BF16))
           + _dot(e_n.astype(BF16), wnew_ref[:, KVP:2 * KVP].astype(BF16))) / den

    idxv = idxv_ref[...]
    lane16 = lax.broadcasted_iota(jnp.int32, (1, LANES), 1) < SLC_TOPK
    half = jnp.where(lane16 & ((idxv & 1) == 1), 1.0, 0.0).astype(BF16)
    live = jnp.where(lane16 & (idxv < n_past_blocks), 1.0, 0.0).astype(BF16)
    half_x = _dot(half, expand_ref[...])
    live_x = _dot(live, expand_ref[...])
    col = lax.broadcasted_iota(jnp.int32, (1, SLC_TOPK * PAGE_SIZE), 1)
    col_half = (lax.rem(col, PAGE_SIZE) // SLC_BLOCK).astype(F32)
    tile_ok = (live_x > 0.5) & (half_x == col_half)
    s_new = _dot_t(qb, knew_ref[:, 0:KVP].astype(BF16))
    zeros_half = jnp.zeros((GROUP, HEAD_DIM), F32)
    o_parts = []
    for kt in range(n_kt):
        k = kt // dec_seq
        r0 = kt * GROUP
        qk = q[r0:r0 + GROUP, k * HEAD_DIM:(k + 1) * HEAD_DIM].astype(BF16)
        kcat = jnp.concatenate([kbuf[slot, kt, s] for s in range(SLC_TOPK)], axis=1).astype(BF16)
        vcat = jnp.concatenate([vbuf[slot, kt, s] for s in range(SLC_TOPK)], axis=1).astype(BF16)
        ok = tile_ok[kt:kt + 1]
        nok = new_ok[r0:r0 + GROUP]
        s_s = jnp.where(ok, _dot(qk, kcat), NEG)
        s_n = jnp.where(nok, s_new[r0:r0 + GROUP], NEG)
        m = jnp.maximum(jnp.max(s_s, axis=-1, keepdims=True), jnp.max(s_n, axis=-1, keepdims=True))
        e_s = jnp.where(ok, jnp.exp(s_s - m), 0.0)
        e_n = jnp.where(nok, jnp.exp(s_n - m), 0.0)
        den = jnp.sum(e_s, axis=-1, keepdims=True) + jnp.sum(e_n, axis=-1, keepdims=True)
        v_new = knew_ref[:, KVP + k * HEAD_DIM:KVP + (k + 1) * HEAD_DIM].astype(BF16)
        o = (_dot_t(e_s.astype(BF16), vcat) + _dot(e_n.astype(BF16), v_new)) / den
        o_parts.append(jnp.concatenate([o, zeros_half] if k == 0 else [zeros_half, o], axis=1))
    o_s = jnp.concatenate(o_parts, axis=0)

    o_ref[...] = gates[:, 0:1] * oc_ref[...] + gates[:, 1:2] * o_s + gates[:, 2:3] * o_w

    lane_w = lax.broadcasted_iota(jnp.int32, (1, wb), 1)
    for kv in range(2):
        moved = pltpu.roll(win_ref[kv], wb - dec_seq, 1)
        for t in range(dec_seq):
            moved = jnp.where(lane_w == wb - dec_seq + t, wnewt_ref[kv * KVP:(kv + 1) * KVP, t:t + 1], moved)
        wout_ref[kv] = moved


def _slc_sample(page_table, idx_flat, cache_t, q_ktg, idxv, kvs_new, win_t, kvw_new, o_c, gates_r, win_out,
                *, layer, dec_seq, past_len):
    n_seq, n_pages = page_table.shape
    n_kt = N_KV_HEADS * dec_seq
    rows = GROUP * n_kt
    wb = win_t.shape[-1]
    cols = SLC_TOPK * PAGE_SIZE
    expand = jnp.asarray((np.arange(LANES)[:, None] == (np.arange(cols) // PAGE_SIZE)[None, :]).astype(np.float32),
                         dtype=BF16)
    n_alias = 0 if win_out is None else 1
    kern = functools.partial(_slc_sample_kernel, layer=layer, n_seq=n_seq, n_pages=n_pages, dec_seq=dec_seq,
                             past_len=past_len, n_alias=n_alias)
    per_seq = lambda r, w: pl.BlockSpec((None, r, w), lambda b, pt, ix: (b, 0, 0))
    state_spec = pl.BlockSpec((None, None, 2, KVP, wb), lambda b, pt, ix: (layer, b, 0, 0, 0))
    in_specs = [pl.BlockSpec(memory_space=pl.ANY),
                per_seq(rows, LANES), per_seq(n_kt, LANES), per_seq(dec_seq, KV_W), state_spec,
                per_seq(dec_seq, KV_W), per_seq(rows, LANES), per_seq(rows, LANES),
                pl.BlockSpec((LANES, cols), lambda b, pt, ix: (0, 0)), per_seq(KV_W, dec_seq)]
    args = [page_table, idx_flat, cache_t, q_ktg, idxv, kvs_new, win_t, kvw_new, o_c, gates_r, expand,
            jnp.swapaxes(kvw_new, 1, 2)]
    if n_alias:
        in_specs.append(pl.BlockSpec(memory_space=pl.ANY))
        args.append(win_out)
    grid_spec = pltpu.PrefetchScalarGridSpec(
        num_scalar_prefetch=2,
        grid=(n_seq,),
        in_specs=in_specs,
        out_specs=[per_seq(rows, LANES), state_spec],
        scratch_shapes=[pltpu.VMEM((2, n_kt, SLC_TOPK, HEAD_DIM, PAGE_SIZE), F32),
                        pltpu.VMEM((2, n_kt, SLC_TOPK, HEAD_DIM, PAGE_SIZE), F32),
                        pltpu.SemaphoreType.DMA((2,))])
    return pl.pallas_call(
        kern,
        grid_spec=grid_spec,
        out_shape=[jax.ShapeDtypeStruct((n_seq, rows, LANES), F32), jax.ShapeDtypeStruct(win_t.shape, F32)],
        input_output_aliases={len(args) - 1: 1} if n_alias else {},
        compiler_params=pltpu.CompilerParams(dimension_semantics=("arbitrary",), vmem_limit_bytes=VMEM_LIMIT),
        name="slc_sample",
    )(*args)


def _pool_sample_kernel(z_ref, d_ref, *, dec_seq, pos0):
    lane = lax.broadcasted_iota(jnp.int32, (1, POOL_W), 1)
    grp = lane // POOL_GW
    for t in range(dec_seq):
        cur = z_ref[:, POOL_HIST + t, :]
        acc = cur
        sums = {}
        for back in range(1, max(POOL_WINDOWS)):
            acc = acc + z_ref[:, POOL_HIST + t - back, :]
            if back + 1 in POOL_WINDOWS:
                sums[back + 1] = acc
        d = None
        for g, w in enumerate(POOL_WINDOWS):
            val = sums[w] / float(min(w, pos0 + t + 1)) - cur
            d = val if d is None else jnp.where(grp == g, val, d)
        d_ref[:, t, :] = d


def _pool_sample(z_ext, *, dec_seq, pos0):
    n_seq = z_ext.shape[0]
    return pl.pallas_call(
        functools.partial(_pool_sample_kernel, dec_seq=dec_seq, pos0=pos0),
        out_shape=jax.ShapeDtypeStruct((n_seq, dec_seq, POOL_W), F32),
        name="pool_sample",
    )(z_ext)


def _sample_layer(x, lw, cmp_t, slc_t, win_t, win_out, pool_state, page_table, *, layer, n_seq, dec_seq):
    rows = n_seq * dec_seq
    n_kt = N_KV_HEADS * dec_seq
    x = _ffn_ln(x, lw['ffn_in'][0], lw['ffn_out'][0], lw['ln_g'][0], lw['ln_b'][0], tm=rows)
    cos, sin = _rope_tables(PAST_LEN + jnp.arange(rows) % dec_seq)
    qt, kvc, kvs, kvw, gates, gu, gv, p = _inproj(
        x, lw['w_ext'], cos, sin, lw['gm_ln_g'], lw['gm_ln_b'], tm=rows)

    qf = jnp.swapaxes(qt.astype(F32), 1, 2).reshape(N_KV_HEADS, GROUP, n_seq, dec_seq, LANES)
    q_gkt = qf.transpose(2, 1, 0, 3, 4).reshape(n_seq, GROUP * n_kt, LANES).astype(BF16)
    q_ktg = qf.transpose(2, 0, 3, 1, 4).reshape(n_seq, GROUP * n_kt, LANES)
    o_c, idxv = _cmp_sample(page_table, cmp_t, q_gkt, lw['pe2'], lw['w1kv'], lw['w2kv'],
                            layer=layer, dec_seq=dec_seq, past_len=PAST_LEN)
    o_c = o_c.reshape(n_seq, GROUP, N_KV_HEADS, dec_seq, LANES).transpose(0, 2, 3, 1, 4).reshape(n_seq, GROUP * n_kt, LANES)
    gates_r = gates[:, :GATE_W].reshape(n_seq, dec_seq, N_KV_HEADS, GROUP, 3).transpose(0, 2, 1, 3, 4)
    gates_r = jnp.pad(gates_r.reshape(n_seq, GROUP * n_kt, 3), ((0, 0), (0, 0), (0, LANES - 3)))
    idx_flat = idxv[:, :, :SLC_TOPK].reshape(-1)
    o, win_out = _slc_sample(page_table, idx_flat, slc_t, q_ktg, idxv, kvs.reshape(n_seq, dec_seq, KV_W), win_t,
                             kvw.reshape(n_seq, dec_seq, KV_W), o_c, gates_r, win_out,
                             layer=layer, dec_seq=dec_seq, past_len=PAST_LEN)
    o = o.reshape(n_seq, N_KV_HEADS, dec_seq, GROUP, LANES)
    o_nsa = jnp.stack([o[:, k, :, :, k * HEAD_DIM:(k + 1) * HEAD_DIM] for k in range(N_KV_HEADS)], axis=2)
    o_nsa = o_nsa.reshape(rows, NSA_W).astype(BF16)

    z_ext = jnp.concatenate([pool_state, p.reshape(n_seq, dec_seq, POOL_W)], axis=1)
    d = _pool_sample(z_ext, dec_seq=dec_seq, pos0=PAST_LEN).reshape(rows, POOL_W)
    ws_cat, gb_full = _gmlp_weights(lw['gm_ws'], lw['gm_b'], dec_seq, GM_CHUNK // dec_seq)
    x = _mix_out(x, o_nsa, gu, gv, d, ws_cat, gb_full, lw['pw_big'], lw['pool_scale'], lw['w_o'],
                 lw['ln_g'][1], lw['ln_b'][1], tm=rows, seq=dec_seq, pool_in_kernel=False)
    x = _ffn_ln(x, lw['ffn_in'][1], lw['ffn_out'][1], lw['ln_g'][2], lw['ln_b'][2], tm=rows)
    shp = (n_seq, dec_seq, 2, N_KV_HEADS, HEAD_DIM)
    new = (kvc.reshape(shp), kvs.reshape(shp), z_ext[:, dec_seq:], gv.reshape(n_seq, dec_seq, GM_W))
    return x, new, win_out


def _pages_by_channel(cache):
    nd = cache.ndim
    t = jnp.transpose(cache, tuple(range(nd - 4)) + (nd - 3, nd - 2, nd - 1, nd - 4))
    return t.reshape(t.shape[:-3] + (KVP, t.shape[-1]))
def _prompt_layer(x, lw, leaf_bufs, *, layer, batch, seq, tm, tm_ffn):
    x = _ffn_ln(x, lw['ffn_in'][0], lw['ffn_out'][0], lw['ln_g'][0], lw['ln_b'][0], tm=tm_ffn)
    cos, sin = _rope_tables(jnp.arange(seq))
    qt, kvc, leaf_c, leaf_s, leaf_w, ks, vts, kw, vtw, gates, gu, gv, p = _inproj(
        x, lw['w_ext'], cos, sin, lw['gm_ln_g'], lw['gm_ln_b'], tm=tm, leaf_bufs=leaf_bufs, layer=layer, batch=batch)
    kc, vct = _compress_prompt(kvc, lw['pe_rows'], lw['w1big'], lw['w2big'], batch=batch)
    o_nsa = _nsa_prompt(qt, kc, vct, ks, vts, kw, vtw, gates, batch=batch, seq=seq)
    ws_cat, gb_full = _gmlp_weights(lw['gm_ws'], lw['gm_b'], GM_CHUNK, 1)
    x = _mix_out(x, o_nsa, gu, gv, p, ws_cat, gb_full, lw['pw_big'], lw['pool_scale'], lw['w_o'],
                 lw['ln_g'][1], lw['ln_b'][1], tm=tm, seq=seq, pool_in_kernel=True)
    x = _ffn_ln(x, lw['ffn_in'][1], lw['ffn_out'][1], lw['ln_g'][2], lw['ln_b'][2], tm=tm_ffn)
    return x, (leaf_c, leaf_s, leaf_w), p.reshape(batch, seq, POOL_W)[:, seq - POOL_HIST:]


def _leaf_rows(buf):
    d, b, _, t = buf.shape
    return buf.reshape(d, b, 2, N_KV_HEADS, HEAD_DIM, t).transpose(0, 1, 5, 2, 3, 4)


def _layer_weights(l, ffn_in_b, ffn_out_b, ln_g, ln_b, w_in, w_o, cmp_pe, cmp_w1, cmp_w2,
                   gm_ln_g, gm_ln_b, gm_ws, gm_b, pool_w, pool_scale):
    pe_rows, w1big, w2big = _compress_weights(cmp_pe[l], cmp_w1[l], cmp_w2[l])
    pe2, w1kv, w2kv = _compress_weights_kv(cmp_pe[l], cmp_w1[l], cmp_w2[l])
    return dict(ffn_in=ffn_in_b[l], ffn_out=ffn_out_b[l], ln_g=ln_g[l], ln_b=ln_b[l],
                w_ext=_build_w_ext(w_in[l]), w_o=w_o[l].astype(BF16),
                pe_rows=pe_rows, w1big=w1big, w2big=w2big, pe2=pe2, w1kv=w1kv, w2kv=w2kv,
                gm_ln_g=gm_ln_g[l], gm_ln_b=gm_ln_b[l], gm_ws=gm_ws[l], gm_b=gm_b[l],
                pw_big=_pool_weights(pool_w[l]), pool_scale=pool_scale[l])


def kernel(x_prompt, x_sample, cache_kv_cmp, cache_kv_slc, state_kv_win, state_pool, page_table, ln_g, ln_b, ffn_w_in, ffn_w_out, w_in, w_o, cmp_pe, cmp_w1, cmp_w2, gm_ln_g, gm_ln_b, gm_ws, gm_b, pool_w, pool_scale):
    batch, seq, _ = x_prompt.shape
    fi = ffn_w_in.astype(BF16)
    fo = ffn_w_out.astype(BF16)
    n_seq, dec_seq, _ = x_sample.shape
    xp = x_prompt.reshape(batch * seq, D_MODEL)
    xs = x_sample.reshape(n_seq * dec_seq, D_MODEL)
    cmp_t = _pages_by_channel(cache_kv_cmp)
    slc_t = _pages_by_channel(cache_kv_slc)
    win_t = _pages_by_channel(state_kv_win)
    leaf_bufs, win_out, pool_p, new_s = (), None, [], []
    for l in range(DEPTH):
        lw = _layer_weights(l, fi, fo, ln_g, ln_b, w_in, w_o, cmp_pe, cmp_w1, cmp_w2, gm_ln_g, gm_ln_b, gm_ws, gm_b, pool_w, pool_scale)
        xp, leaf_bufs, pool_l = _prompt_layer(xp, lw, leaf_bufs, layer=l, batch=batch, seq=seq, tm=512, tm_ffn=1024)
        xs, st_s, win_out = _sample_layer(xs, lw, cmp_t, slc_t, win_t, win_out, state_pool[l], page_table,
                                          layer=l, n_seq=n_seq, dec_seq=dec_seq)
        pool_p.append(pool_l)
        new_s.append(st_s)
    stk = lambda lst, i: jnp.stack([t[i] for t in lst])
    wb = min(WINDOW, seq)
    win_s = _leaf_rows(win_out.reshape(DEPTH, n_seq, KV_W, win_out.shape[-1]))
    return (xp.reshape(batch, seq, D_MODEL), xs.reshape(n_seq, dec_seq, D_MODEL),
            _leaf_rows(leaf_bufs[0]), stk(new_s, 0), _leaf_rows(leaf_bufs[1]), stk(new_s, 1),
            _leaf_rows(leaf_bufs[2])[:, :, seq - wb:], win_s, jnp.stack(pool_p), stk(new_s, 2), stk(new_s, 3))
```

```python
import functools

import numpy as np
import jax
import jax.numpy as jnp
from jax import lax
from jax.experimental import pallas as pl
from jax.experimental.pallas import tpu as pltpu

F32 = jnp.float32
BF16 = jnp.bfloat16

D_MODEL = 1024
DEPTH = 2
PAST_LEN = 16384
PAGE_SIZE = 128
HEAD_DIM = 64
NSA_W = D_MODEL // 2
GM_W = D_MODEL // 4
POOL_W = D_MODEL // 4
N_HEADS = NSA_W // HEAD_DIM
N_KV_HEADS = 2
GROUP = N_HEADS // N_KV_HEADS
CMP_STRIDE = 16
CMP_LEN = 2 * CMP_STRIDE
SLC_BLOCK = 64
SLC_TOPK = 16
WINDOW = 512
FORCE_SCORE = 1.0e4
ROPE_THETA = 10000.0
SCALE = HEAD_DIM ** -0.5
GM_HEADS = GM_W // HEAD_DIM
GM_CHUNK = 128
POOL_GROUPS = 4
POOL_GW = POOL_W // POOL_GROUPS
POOL_WINDOWS = (2, 4, 8, 16)
POOL_HIST = max(POOL_WINDOWS) - 1
D_FF = 256 * ((8 * D_MODEL // 3 + 255) // 256)
ALPHA = (2 * DEPTH) ** 0.25
LN_EPS = 1e-5
Q_W = N_HEADS * HEAD_DIM
KV_W = 2 * N_KV_HEADS * HEAD_DIM
GATE_W = 3 * N_HEADS
N_IN = Q_W + 3 * KV_W + GATE_W + 2 * GM_W + POOL_W

LANES = 128
KVP = N_KV_HEADS * HEAD_DIM
VMEM_LIMIT = 56 * 1024 * 1024
NEG = -1e30
HALO = 16

_OFF_Q = 0
_OFF_QR = Q_W
_OFF_KV = 2 * Q_W
_OFF_GATE = _OFF_KV + 3 * 3 * KVP
_OFF_UV = _OFF_GATE + LANES
_OFF_P = _OFF_UV + 2 * GM_W
N_EXT = _OFF_P + POOL_W


def _ln_rows(y, g, b):
    mu = jnp.mean(y, axis=-1, keepdims=True)
    d = y - mu
    var = jnp.mean(d * d, axis=-1, keepdims=True)
    return d * lax.rsqrt(var + LN_EPS) * g + b


def _dot(a, b):
    return jnp.dot(a, b, preferred_element_type=F32)


def _dot_t(a, b):
    return lax.dot_general(a, b, (((1,), (1,)), ((), ())), preferred_element_type=F32)


def _ffn_kernel(x_ref, wg_ref, wu_ref, wo_ref, g_ref, b_ref, o_ref, xb_ref, *, n_chunks):
    j = pl.program_id(1)

    @pl.when(j == 0)
    def _():
        xb_ref[...] = x_ref[...].astype(BF16)

    xb = xb_ref[...]
    gate = _dot(xb, wg_ref[...])
    up = _dot(xb, wu_ref[...])
    hid = (gate * jax.nn.sigmoid(gate)) * up
    part = _dot(hid.astype(BF16), wo_ref[...])

    @pl.when(j == 0)
    def _():
        o_ref[...] = part

    if n_chunks > 2:
        @pl.when((j > 0) & (j < n_chunks - 1))
        def _():
            o_ref[...] += part

    @pl.when(j == n_chunks - 1)
    def _():
        y = ALPHA * x_ref[...] + 0.5 * (o_ref[...] + part)
        o_ref[...] = _ln_rows(y, g_ref[...], b_ref[...])


def _ffn_ln(x, w_in_b, w_out_b, g, b, *, tm, sel):
    rows = x.shape[0]
    n_chunks = 2
    fc = D_FF // n_chunks
    l, w = sel
    return pl.pallas_call(
        functools.partial(_ffn_kernel, n_chunks=n_chunks),
        grid=(rows // tm, n_chunks),
        in_specs=[
            pl.BlockSpec((tm, D_MODEL), lambda i, j: (i, 0)),
            pl.BlockSpec((None, None, D_MODEL, fc), lambda i, j: (l, w, 0, j)),
            pl.BlockSpec((None, None, D_MODEL, fc), lambda i, j: (l, w, 0, n_chunks + j)),
            pl.BlockSpec((None, None, fc, D_MODEL), lambda i, j: (l, w, j, 0)),
            pl.BlockSpec((1, D_MODEL), lambda i, j: (0, 0)),
            pl.BlockSpec((1, D_MODEL), lambda i, j: (0, 0)),
        ],
        out_specs=pl.BlockSpec((tm, D_MODEL), lambda i, j: (i, 0)),
        out_shape=jax.ShapeDtypeStruct((rows, D_MODEL), F32),
        scratch_shapes=[pltpu.VMEM((tm, D_MODEL), BF16)],
        compiler_params=pltpu.CompilerParams(
            dimension_semantics=("parallel", "arbitrary"), vmem_limit_bytes=VMEM_LIMIT),
        name="ffn_ln",
    )(x, w_in_b, w_in_b, w_out_b, g.reshape(1, D_MODEL), b.reshape(1, D_MODEL))


def _inproj_kernel(*refs, prompt, n_alias):
    h_ref, w_ref, cos_ref, sin_ref, gmg_ref, gmb_ref = refs[:6]
    outs = refs[6 + n_alias:]
    if prompt:
        (qt_ref, kvcb_ref, leafc_ref, leafs_ref, leafw_ref, ks_ref, vts_ref, kw_ref, vtw_ref,
         gate_ref, gu_ref, gv_ref, p_ref) = outs
        leaves = (leafc_ref, leafs_ref, leafw_ref)
        k_refs = (None, ks_ref, kw_ref)
        vt_refs = (None, vts_ref, vtw_ref)
    else:
        qt_ref, kvc_ref, kvs_ref, kvw_ref, gate_ref, gu_ref, gv_ref, p_ref = outs
        rows_out = (kvc_ref, kvs_ref, kvw_ref)
    hb = h_ref[...].astype(BF16)
    cos = cos_ref[...]
    sin = sin_ref[...]

    zq = _dot(hb, w_ref[:, _OFF_Q:_OFF_KV])
    zk = _dot(hb, w_ref[:, _OFF_KV:_OFF_UV])
    zu = _dot(hb, w_ref[:, _OFF_UV:N_EXT])

    n_sq = hb.shape[0] // LANES
    zeros_half = jnp.zeros((HEAD_DIM, LANES), F32)

    for m in range(N_HEADS // 2):
        c0 = m * LANES
        pair = (zq[:, c0:c0 + LANES] * cos + zq[:, Q_W + c0:Q_W + c0 + LANES] * sin) * SCALE
        kvh = (2 * m) // GROUP
        for c in range(n_sq):
            pt = pair[c * LANES:(c + 1) * LANES].T
            for e in range(2):
                piece = pt[e * HEAD_DIM:(e + 1) * HEAD_DIM]
                both = [piece, zeros_half] if kvh == 0 else [zeros_half, piece]
                qt_ref[2 * m + e, :, c * LANES:(c + 1) * LANES] = jnp.concatenate(both, axis=0).astype(BF16)

    for br in range(3):
        c0 = br * 3 * KVP
        k = zk[:, c0:c0 + KVP] * cos + zk[:, c0 + KVP:c0 + 2 * KVP] * sin
        v = zk[:, c0 + 2 * KVP:c0 + 3 * KVP]
        if not prompt:
            rows_out[br][:, 0:KVP] = k
            rows_out[br][:, KVP:2 * KVP] = v
            continue
        for c in range(n_sq):
            cols = slice(c * LANES, (c + 1) * LANES)
            vt = v[cols].T
            leaves[br][0:KVP, cols] = k[cols].T
            leaves[br][KVP:2 * KVP, cols] = vt
            if br > 0:
                vt_refs[br][c] = vt.astype(BF16)
        if br == 0:
            kvcb_ref[0] = k
            kvcb_ref[1] = v
        else:
            k_refs[br][...] = k.astype(BF16)

    gate_ref[...] = jax.nn.sigmoid(zk[:, _OFF_GATE - _OFF_KV:_OFF_UV - _OFF_KV])
    gu_ref[...] = jax.nn.gelu(zu[:, 0:GM_W])
    gv_ref[...] = _ln_rows(jax.nn.gelu(zu[:, GM_W:2 * GM_W]), gmg_ref[...], gmb_ref[...])
    p_ref[...] = zu[:, 2 * GM_W:2 * GM_W + POOL_W]


def _inproj(h, w_ext, cos, sin, gmg, gmb, *, tm, leaf_bufs=None, layer=0, batch=None):
    rows = h.shape[0]
    n_tab = cos.shape[0] // tm
    prompt = leaf_bufs is not None
    row_spec = lambda w: pl.BlockSpec((tm, w), lambda i: (i, 0))
    tab_spec = pl.BlockSpec((tm, LANES), lambda i: (i % n_tab, 0))
    vec_spec = pl.BlockSpec((1, GM_W), lambda i: (0, 0))
    sds = jax.ShapeDtypeStruct
    in_specs = [row_spec(D_MODEL), pl.BlockSpec((D_MODEL, N_EXT), lambda i: (0, 0)),
                tab_spec, tab_spec, vec_spec, vec_spec]
    args = [h, w_ext, cos, sin, gmg.reshape(1, GM_W), gmb.reshape(1, GM_W)]
    tail_specs = [row_spec(LANES), row_spec(GM_W), row_spec(GM_W), row_spec(POOL_W)]
    tail_shapes = [sds((rows, LANES), F32), sds((rows, GM_W), F32), sds((rows, GM_W), F32), sds((rows, POOL_W), F32)]
    qt_spec = pl.BlockSpec((N_HEADS, LANES, tm), lambda i: (0, 0, i))
    qt_shape = sds((N_HEADS, LANES, rows), BF16)
    aliases = {}
    if prompt:
        seq = rows // batch
        tiles = seq // tm
        leaf_spec = pl.BlockSpec((None, None, KV_W, tm), lambda i: (layer, i // tiles, 0, i % tiles))
        leaf_shape = sds((DEPTH, batch, KV_W, seq), F32)
        sq_spec = pl.BlockSpec((tm // LANES, KVP, LANES), lambda i: (i, 0, 0))
        sq_shape = sds((rows // LANES, KVP, LANES), BF16)
        out_specs = [qt_spec, pl.BlockSpec((2, tm, KVP), lambda i: (0, i, 0)), leaf_spec, leaf_spec, leaf_spec,
                     row_spec(KVP), sq_spec, row_spec(KVP), sq_spec] + tail_specs
        out_shape = [qt_shape, sds((2, rows, KVP), F32), leaf_shape, leaf_shape, leaf_shape,
                     sds((rows, KVP), BF16), sq_shape, sds((rows, KVP), BF16), sq_shape] + tail_shapes
        for n, buf in enumerate(leaf_bufs):
            in_specs.append(pl.BlockSpec(memory_space=pl.ANY))
            args.append(buf)
            aliases[6 + n] = 2 + n
    else:
        out_specs = [qt_spec, row_spec(KV_W), row_spec(KV_W), row_spec(KV_W)] + tail_specs
        out_shape = [qt_shape, sds((rows, KV_W), F32), sds((rows, KV_W), F32), sds((rows, KV_W), F32)] + tail_shapes
    return pl.pallas_call(
        functools.partial(_inproj_kernel, prompt=prompt, n_alias=len(aliases)),
        grid=(rows // tm,),
        in_specs=in_specs,
        out_specs=out_specs,
        out_shape=out_shape,
        input_output_aliases=aliases,
        compiler_params=pltpu.CompilerParams(dimension_semantics=("parallel",), vmem_limit_bytes=VMEM_LIMIT),
        name="inproj",
    )(*args)


def _build_w_ext(w_in):
    half = HEAD_DIM // 2

    def rot(w):
        n = w.shape[1] // HEAD_DIM
        w3 = w.reshape(D_MODEL, n, 2, half)
        return jnp.stack([-w3[:, :, 1], w3[:, :, 0]], axis=2).reshape(D_MODEL, n * HEAD_DIM)

    q = w_in[:, :Q_W]
    cols = [q, rot(q)]
    for br in range(3):
        kv = w_in[:, Q_W + br * KV_W:Q_W + (br + 1) * KV_W]
        k, v = kv[:, :KVP], kv[:, KVP:]
        cols += [k, rot(k), v]
    g0 = Q_W + 3 * KV_W
    cols.append(jnp.pad(w_in[:, g0:g0 + GATE_W], ((0, 0), (0, LANES - GATE_W))))
    cols.append(w_in[:, g0 + GATE_W:])
    return jnp.concatenate(cols, axis=1).astype(BF16)


def _rope_tables(pos):
    half = HEAD_DIM // 2
    inv = ROPE_THETA ** (-jnp.arange(half, dtype=F32) / half)
    ang = pos.astype(F32)[:, None] * inv[None, :]
    cos = jnp.tile(jnp.cos(ang), (1, LANES // half))
    sin = jnp.tile(jnp.sin(ang), (1, LANES // half))
    return cos, sin


def _compress_kernel(x_ref, pe_ref, w1_ref, w2_ref, kc_ref, vct_ref):
    nsub = kc_ref.shape[0]
    outs = []
    for kv in range(2):
        xr = jnp.concatenate(
            [x_ref[kv, pl.ds(j, nsub, stride=CMP_STRIDE), :].astype(BF16) for j in range(CMP_STRIDE)], axis=1)
        f = _dot(xr, w1_ref[kv])
        per = _dot(pe_ref[kv].astype(BF16), w1_ref[kv])
        pe_term = per[0:1, 0:KVP] + per[1:2, KVP:2 * KVP]
        nxt = jnp.concatenate([f[1:, KVP:2 * KVP], jnp.zeros((1, KVP), F32)], axis=0)
        hid = jax.nn.gelu(f[:, 0:KVP] + nxt + pe_term)
        outs.append(_dot(hid.astype(BF16), w2_ref[kv]))
    kc_ref[...] = outs[0].astype(BF16)
    for c in range(nsub // LANES):
        vct_ref[:, c * LANES:(c + 1) * LANES] = outs[1][c * LANES:(c + 1) * LANES].T.astype(BF16)


def _compress_prompt(kv_rows, pe2, w1kv, w2kv, *, batch):
    seq = kv_rows.shape[1] // batch
    nsub = seq // CMP_STRIDE
    width = CMP_STRIDE * KVP
    return pl.pallas_call(
        _compress_kernel,
        grid=(batch,),
        in_specs=[pl.BlockSpec((2, seq, KVP), lambda b: (0, b, 0)),
                  pl.BlockSpec((2, 8, width), lambda b: (0, 0, 0)),
                  pl.BlockSpec((2, width, 2 * KVP), lambda b: (0, 0, 0)),
                  pl.BlockSpec((2, KVP, KVP), lambda b: (0, 0, 0))],
        out_specs=[pl.BlockSpec((None, nsub, KVP), lambda b: (b, 0, 0)),
                   pl.BlockSpec((None, KVP, nsub), lambda b: (b, 0, 0))],
        out_shape=[jax.ShapeDtypeStruct((batch, nsub, KVP), BF16), jax.ShapeDtypeStruct((batch, KVP, nsub), BF16)],
        compiler_params=pltpu.CompilerParams(dimension_semantics=("parallel",), vmem_limit_bytes=VMEM_LIMIT),
        name="compress_prompt",
    )(kv_rows, pe2, w1kv, w2kv)


def _top_blocks_cols(score, blk):
    sel = jnp.zeros(score.shape, F32)
    for _ in range(SLC_TOPK):
        m = jnp.max(score, axis=0, keepdims=True)
        first = jnp.min(jnp.where(score == m, blk, 1e9), axis=0, keepdims=True)
        hit = blk == first
        sel = jnp.where(hit, 1.0, sel)
        score = jnp.where(hit, -jnp.inf, score)
    return sel


def _top_blocks_idx(score, blk):
    sel = jnp.zeros(score.shape, F32)
    idx = jnp.zeros((score.shape[0], LANES), jnp.int32)
    lane = lax.broadcasted_iota(jnp.int32, (1, LANES), 1)
    big = jnp.int32(1 << 20)
    for it in range(SLC_TOPK):
        m = jnp.max(score, axis=-1, keepdims=True)
        first = jnp.min(jnp.where(score == m, blk, big), axis=-1, keepdims=True)
        hit = blk == first
        sel = jnp.where(hit, 1.0, sel)
        idx = jnp.where(lane == it, first, idx)
        score = jnp.where(hit, -jnp.inf, score)
    return sel, idx


def _softmax_rows(s, mask):
    s = jnp.where(mask, s, NEG)
    m = jnp.max(s, axis=-1, keepdims=True)
    e = jnp.exp(s - m)
    return jnp.where(mask, e / jnp.sum(e, axis=-1, keepdims=True), 0.0)


def _nsa_prompt_kernel(qt_ref, kc_ref, vct_ref, ks_ref, vts_ref, kw_ref, vtw_ref, gate_ref, selmap_ref, expand_ref,
                       o_ref, *, tq, tk, seq, n_sel_blocks):
    t0 = pl.program_id(1) * tq
    nsub = kc_ref.shape[0]
    nb = 8 * ((n_sel_blocks + 7) // 8)
    span = min(WINDOW + tq, seq)
    qpos = t0 + lax.broadcasted_iota(jnp.int32, (1, tq), 1)
    rep = lambda x, n: jnp.concatenate([x] * n, axis=1)
    qt = jnp.concatenate([qt_ref[h] for h in range(N_HEADS)], axis=1)

    cmp_end = CMP_STRIDE * lax.broadcasted_iota(jnp.int32, (nsub, 1), 0) + (CMP_LEN - 1)
    c_bias = jnp.where(cmp_end <= qpos, 0.0, NEG)
    s = _dot(kc_ref[...], qt) + rep(c_bias, N_HEADS)
    e = jnp.exp(s - jnp.max(s, axis=0, keepdims=True))
    inv_c = jnp.where(rep(qpos >= CMP_LEN - 1, N_HEADS), 1.0 / jnp.sum(e, axis=0, keepdims=True), 0.0)
    eb = e.astype(BF16)
    o_c = _dot(vct_ref[...], eb) * inv_c
    imp_h = _dot(selmap_ref[...], eb)[0:nb] * inv_c

    imp = []
    for k in range(N_KV_HEADS):
        acc = None
        for g in range(GROUP):
            part = imp_h[:, (GROUP * k + g) * tq:(GROUP * k + g + 1) * tq]
            acc = part if acc is None else acc + part
        imp.append(acc)
    imp = jnp.concatenate(imp, axis=1)
    blk = lax.broadcasted_iota(jnp.int32, (nb, 1), 0)
    cur = rep(qpos // SLC_BLOCK, N_KV_HEADS)
    forced = (blk == 0) | (blk == cur) | (blk == cur - 1)
    score = jnp.where(forced, FORCE_SCORE, jnp.where(blk <= cur, imp, -1.0))
    if nb > n_sel_blocks:
        score = jnp.where(blk < n_sel_blocks, score, -jnp.inf)
    sel = _top_blocks_cols(score, blk.astype(F32))
    sel_m1 = jnp.concatenate([sel - 1.0, jnp.zeros((LANES - nb, N_KV_HEADS * tq), F32)], axis=0).astype(BF16)

    start = pl.multiple_of(jnp.maximum(t0 + tq - span, 0), tq)
    dist = qpos - (start + lax.broadcasted_iota(jnp.int32, (span, 1), 0))
    w_bias = jnp.where(dist >= 0, jnp.where(dist <= WINDOW, 0.0, NEG), NEG)
    s = _dot(kw_ref[pl.ds(start, span), :], qt) + rep(w_bias, N_HEADS)
    e = jnp.exp(s - jnp.max(s, axis=0, keepdims=True))
    inv_w = 1.0 / jnp.sum(e, axis=0, keepdims=True)
    c0 = start // LANES
    vt = jnp.concatenate([vtw_ref[c0 + c] for c in range(span // LANES)], axis=1)
    o_w = _dot(vt, e.astype(BF16)) * inv_w

    def tile(kt, carry, diagonal):
        m_i, l_i, acc = carry
        r0 = pl.multiple_of(kt * tk, tk)
        bias = _dot(expand_ref[pl.ds(r0, tk), :], sel_m1)
        if diagonal:
            kpos = r0 + lax.broadcasted_iota(jnp.int32, (tk, 1), 0)
            bias = jnp.where(kpos <= rep(qpos, N_KV_HEADS), bias, NEG)
        s = _dot(ks_ref[pl.ds(r0, tk), :], qt)
        s = jnp.concatenate([s[:, h * tq:(h + 1) * tq] + bias[:, (h // GROUP) * tq:(h // GROUP + 1) * tq]
                             for h in range(N_HEADS)], axis=1)
        m_new = jnp.maximum(m_i, jnp.max(s, axis=0, keepdims=True))
        a = jnp.exp(m_i - m_new)
        e = jnp.exp(s - m_new)
        l_new = a * l_i + jnp.sum(e, axis=0, keepdims=True)
        c0 = kt * (tk // LANES)
        vt = jnp.concatenate([vts_ref[c0 + c] for c in range(tk // LANES)], axis=1)
        return m_new, l_new, a * acc + _dot(vt, e.astype(BF16))

    n_kt = (t0 + tq + tk - 1) // tk
    init = (jnp.full((1, N_HEADS * tq), NEG, F32), jnp.zeros((1, N_HEADS * tq), F32),
            jnp.zeros((KVP, N_HEADS * tq), F32))
    carry = lax.fori_loop(0, n_kt - 1, lambda kt, c: tile(kt, c, False), init)
    _, l_s, acc_s = tile(n_kt - 1, carry, True)
    o_s = acc_s * (1.0 / l_s)

    gt = gate_ref[...].T
    parts = []
    for h in range(N_HEADS):
        rows = slice((h // GROUP) * HEAD_DIM, (h // GROUP + 1) * HEAD_DIM)
        cols = slice(h * tq, (h + 1) * tq)
        parts.append(gt[3 * h:3 * h + 1] * o_c[rows, cols] + gt[3 * h + 1:3 * h + 2] * o_s[rows, cols]
                     + gt[3 * h + 2:3 * h + 3] * o_w[rows, cols])
    ot = jnp.concatenate(parts, axis=0)
    for m in range(NSA_W // LANES):
        o_ref[:, m * LANES:(m + 1) * LANES] = ot[m * LANES:(m + 1) * LANES].T.astype(o_ref.dtype)


def _sel_map_t(nc_rows, n_cmp, ns):
    c0 = CMP_STRIDE * np.arange(nc_rows)[None, :]
    s0 = SLC_BLOCK * np.arange(LANES)[:, None]
    ov = np.clip(np.minimum(c0 + CMP_LEN, s0 + SLC_BLOCK) - np.maximum(c0, s0), 0, None) / CMP_LEN
    ov = ov * (np.arange(nc_rows)[None, :] < n_cmp) * (np.arange(LANES)[:, None] < ns)
    return jnp.asarray(ov, dtype=BF16)


def _expand_map(seq):
    e = (np.arange(LANES)[None, :] == (np.arange(seq) // SLC_BLOCK)[:, None]).astype(np.float32) * -NEG
    return jnp.asarray(e, dtype=BF16)


def _nsa_prompt(qt, kc, vct, ks, vts, kw, vtw, gates, *, batch, seq, tq=128, tk=512):
    tk = min(tk, seq)
    nq = seq // tq
    nsub = seq // CMP_STRIDE
    ns = seq // SLC_BLOCK
    assert ns <= LANES and seq % tk == 0 and tk % tq == 0 and tq == LANES
    kern = functools.partial(_nsa_prompt_kernel, tq=tq, tk=tk, seq=seq, n_sel_blocks=ns)
    per_batch = lambda shp: pl.BlockSpec(shp, lambda b, i: (b,) + (0,) * (len(shp) - 1))
    const = lambda shp: pl.BlockSpec(shp, lambda b, i: (0,) * len(shp))
    return pl.pallas_call(
        kern,
        grid=(batch, nq),
        in_specs=[pl.BlockSpec((N_HEADS, KVP, tq), lambda b, i: (0, 0, b * nq + i)),
                  per_batch((None, nsub, KVP)), per_batch((None, KVP, nsub)),
                  per_batch((seq, KVP)), per_batch((seq // LANES, KVP, LANES)),
                  per_batch((seq, KVP)), per_batch((seq // LANES, KVP, LANES)),
                  pl.BlockSpec((tq, LANES), lambda b, i: (b * nq + i, 0)),
                  const((LANES, nsub)), const((seq, LANES))],
        out_specs=pl.BlockSpec((tq, NSA_W), lambda b, i: (b * nq + i, 0)),
        out_shape=jax.ShapeDtypeStruct((batch * seq, NSA_W), BF16),
        compiler_params=pltpu.CompilerParams(
            dimension_semantics=("parallel", "arbitrary"), vmem_limit_bytes=VMEM_LIMIT),
        name="nsa_prompt",
    )(qt, kc, vct, ks, vts, kw, vtw, gates, _sel_map_t(nsub, nsub - 1, ns), _expand_map(seq))


def _pool_windows(z_ext, tm):
    s2 = z_ext[1:] + z_ext[:-1]
    s4 = s2[2:] + s2[:-2]
    s8 = s4[4:] + s4[:-4]
    s16 = s8[8:] + s8[:-8]
    return (s2[HALO - 1:HALO - 1 + tm], s4[HALO - 3:HALO - 3 + tm], s8[HALO - 7:HALO - 7 + tm],
            s16[HALO - 15:HALO - 15 + tm])


def _mix_out_kernel(*refs, tm, tiles_per_seq, pool_in_kernel):
    if pool_in_kernel:
        (x_ref, nsa_ref, gu_ref, gv_ref, p_ref, halo_ref, ws_ref, gb_ref, pw_ref, ps_ref, wo_ref,
         g_ref, b_ref, o_ref) = refs
    else:
        (x_ref, nsa_ref, gu_ref, gv_ref, d_ref, ws_ref, gb_ref, pw_ref, ps_ref, wo_ref,
         g_ref, b_ref, o_ref) = refs
    lane = lax.broadcasted_iota(jnp.int32, (1, GM_W), 1)

    parts = []
    for c in range(tm // GM_CHUNK):
        v = gv_ref[c * GM_CHUNK:(c + 1) * GM_CHUNK, :]
        stacked = jnp.concatenate(
            [jnp.where(lane // HEAD_DIM == h, v, 0.0) for h in range(GM_HEADS)], axis=0).astype(BF16)
        s = _dot(ws_ref[...], stacked) + gb_ref[...]
        parts.append(gu_ref[c * GM_CHUNK:(c + 1) * GM_CHUNK, :] * s)
    o_gm = parts[0] if len(parts) == 1 else jnp.concatenate(parts, axis=0)

    if pool_in_kernel:
        first_tile = (pl.program_id(0) % tiles_per_seq) == 0
        halo = jnp.where(first_tile, 0.0, halo_ref[...])
        z = p_ref[...]
        wins = _pool_windows(jnp.concatenate([halo, z], axis=0), tm)
        pos = (pl.program_id(0) % tiles_per_seq) * tm + lax.broadcasted_iota(jnp.int32, (tm, 1), 0)
        grp = lane // POOL_GW
        wsum = jnp.where(grp == 0, wins[0], jnp.where(grp == 1, wins[1], jnp.where(grp == 2, wins[2], wins[3])))
        width = jnp.where(grp == 0, POOL_WINDOWS[0], jnp.where(grp == 1, POOL_WINDOWS[1],
                          jnp.where(grp == 2, POOL_WINDOWS[2], POOL_WINDOWS[3])))
        cnt = jnp.minimum(width, pos + 1).astype(F32)
        d = wsum / cnt - z
    else:
        d = d_ref[...]
    o_pool = _dot(d.astype(BF16), pw_ref[...]) * ps_ref[...]

    mixed = jnp.concatenate([nsa_ref[...], o_gm.astype(BF16), o_pool.astype(BF16)], axis=1)
    y = ALPHA * x_ref[...] + _dot(mixed, wo_ref[...])
    o_ref[...] = _ln_rows(y, g_ref[...], b_ref[...])


def _mix_out(x, o_nsa, gu, gv, p_or_d, ws_cat, gb_full, pw_big, ps, w_o_b, g, b, *, tm, seq, pool_in_kernel):
    rows = x.shape[0]
    tiles_per_seq = max(seq // tm, 1)
    row_spec = lambda w: pl.BlockSpec((tm, w), lambda i: (i, 0))
    const = lambda shp: pl.BlockSpec(shp, lambda i: (0,) * len(shp))
    in_specs = [row_spec(D_MODEL), row_spec(NSA_W), row_spec(GM_W), row_spec(GM_W), row_spec(POOL_W)]
    args = [x, o_nsa, gu, gv, p_or_d]
    if pool_in_kernel:
        in_specs.append(pl.BlockSpec((HALO, POOL_W), lambda i: (jnp.maximum(i * (tm // HALO) - 1, 0), 0)))
        args.append(p_or_d)
    in_specs += [const((GM_CHUNK, GM_HEADS * GM_CHUNK)), const((GM_CHUNK, GM_W)), const((POOL_W, POOL_W)),
                 const((1, POOL_W)), const((D_MODEL, D_MODEL)), const((1, D_MODEL)), const((1, D_MODEL))]
    args += [ws_cat, gb_full, pw_big, ps.reshape(1, POOL_W), w_o_b, g.reshape(1, D_MODEL), b.reshape(1, D_MODEL)]
    kern = functools.partial(_mix_out_kernel, tm=tm, tiles_per_seq=tiles_per_seq, pool_in_kernel=pool_in_kernel)
    return pl.pallas_call(
        kern,
        grid=(rows // tm,),
        in_specs=in_specs,
        out_specs=row_spec(D_MODEL),
        out_shape=jax.ShapeDtypeStruct((rows, D_MODEL), F32),
        compiler_params=pltpu.CompilerParams(dimension_semantics=("parallel",), vmem_limit_bytes=VMEM_LIMIT),
        name="mix_out_prompt" if pool_in_kernel else "mix_out_sample",
    )(*args)


def _gmlp_weights(ws, gb, chunk_rows, reps):
    wm = jnp.tril(ws[:, :chunk_rows, :chunk_rows])
    bias = gb[:, :chunk_rows]
    if reps > 1:
        eye = jnp.eye(reps, dtype=F32)
        wm = jnp.einsum('hts,ab->hatbs', wm, eye).reshape(GM_HEADS, reps * chunk_rows, reps * chunk_rows)
        bias = jnp.tile(bias, (1, reps))
    ws_cat = wm.transpose(1, 0, 2).reshape(GM_CHUNK, GM_HEADS * GM_CHUNK).astype(BF16)
    gb_full = jnp.repeat(bias.T, HEAD_DIM, axis=1)
    return ws_cat, gb_full


def _pool_weights(pw):
    eye = jnp.eye(POOL_GROUPS, dtype=F32)
    return jnp.einsum('gce,gq->gcqe', pw, eye).reshape(POOL_W, POOL_W).astype(BF16)


PAGES_PER_STEP = 32
SUBS_PER_PAGE = PAGE_SIZE // CMP_STRIDE


def _cmp_sample_kernel(pt_ref, cache_ref, q_ref, pe_ref, w1_ref, w2_ref, selmap_ref, oc_ref, idx_ref,
                       pbuf, sem, xt_ref, fs_ref, *, layer, n_chunks, n_seq, dec_seq, past_len, n_sel_blocks):
    b = pl.program_id(0)
    c = pl.program_id(1)
    step = b * n_chunks + c
    slot = lax.rem(step, 2)
    pps = PAGES_PER_STEP

    def page_copies(sb, sc, sl):
        return [pltpu.make_async_copy(cache_ref.at[layer, pt_ref[sb, sc * pps + p]], pbuf.at[sl, p], sem.at[sl])
                for p in range(pps)]

    @pl.when(step == 0)
    def _():
        for cp in page_copies(b, c, slot):
            cp.start()

    @pl.when(step + 1 < n_seq * n_chunks)
    def _():
        wrap = c + 1 == n_chunks
        for cp in page_copies(jnp.where(wrap, b + 1, b), jnp.where(wrap, 0, c + 1), 1 - slot):
            cp.start()

    for cp in page_copies(b, c, slot):
        cp.wait()

    def to_rows(p, carry):
        r0 = pl.multiple_of(p * PAGE_SIZE, PAGE_SIZE)
        for kv in range(2):
            xt_ref[kv, pl.ds(r0, PAGE_SIZE), :] = pbuf[slot, p, kv].T
        return carry

    lax.fori_loop(0, pps, to_rows, 0, unroll=8)

    subs = pps * SUBS_PER_PAGE
    s0 = pl.multiple_of(c * subs, subs)
    for kv in range(2):
        xr = jnp.concatenate(
            [xt_ref[kv, pl.ds(j, subs, stride=CMP_STRIDE), :].astype(BF16) for j in range(CMP_STRIDE)], axis=1)
        fs_ref[kv, pl.ds(s0, subs), :] = _dot(xr, w1_ref[kv])

    @pl.when(c == n_chunks - 1)
    def _():
        nsub = n_chunks * subs
        kcv = []
        for kv in range(2):
            f = fs_ref[kv]
            per = _dot(pe_ref[kv].astype(BF16), w1_ref[kv])
            pe_term = per[0:1, 0:KVP] + per[1:2, KVP:2 * KVP]
            nxt = jnp.concatenate([f[1:, KVP:2 * KVP], jnp.zeros((1, KVP), F32)], axis=0)
            hid = jax.nn.gelu(f[:, 0:KVP] + nxt + pe_term)
            kcv.append(_dot(hid.astype(BF16), w2_ref[kv]).astype(BF16))
        n_kt = N_KV_HEADS * dec_seq
        rows = GROUP * n_kt
        q = q_ref[...]
        qpos = past_len + lax.rem(lax.broadcasted_iota(jnp.int32, (rows, 1), 0), dec_seq)
        cmp_end = CMP_STRIDE * lax.broadcasted_iota(jnp.int32, (1, nsub), 1) + (CMP_LEN - 1)
        p = _softmax_rows(_dot_t(q, kcv[0]), cmp_end <= qpos).astype(BF16)
        oc_ref[...] = _dot(p, kcv[1])
        imp_g = _dot(p, selmap_ref[...])
        imp = imp_g[0:n_kt]
        for g in range(1, GROUP):
            imp = imp + imp_g[g * n_kt:(g + 1) * n_kt]
        blk = lax.broadcasted_iota(jnp.int32, (1, imp.shape[1]), 1)
        cur = qpos[0:n_kt] // SLC_BLOCK
        forced = (blk == 0) | (blk == cur) | (blk == cur - 1)
        score = jnp.where(forced, FORCE_SCORE, jnp.where(blk <= cur, imp, -1.0))
        score = jnp.where(blk < n_sel_blocks, score, -jnp.inf)
        idx_ref[...] = _top_blocks_idx(score, blk)[1]


def _cmp_sample(page_table, cache_t, q_gkt, pe2, w1kv, w2kv, *, layer, dec_seq, past_len):
    n_seq, n_pages = page_table.shape
    n_chunks = n_pages // PAGES_PER_STEP
    nsub = n_pages * SUBS_PER_PAGE
    ns = (past_len + dec_seq + SLC_BLOCK - 1) // SLC_BLOCK
    ns_pad = LANES * ((ns + LANES - 1) // LANES)
    n_kt = N_KV_HEADS * dec_seq
    rows = GROUP * n_kt
    c0 = CMP_STRIDE * np.arange(nsub)[:, None]
    s0 = SLC_BLOCK * np.arange(ns_pad)[None, :]
    ov = np.clip(np.minimum(c0 + CMP_LEN, s0 + SLC_BLOCK) - np.maximum(c0, s0), 0, None) / CMP_LEN
    ov = ov * (np.arange(nsub)[:, None] < nsub - 1) * (np.arange(ns_pad)[None, :] < ns)
    selmap = jnp.asarray(ov, dtype=BF16)
    width = CMP_STRIDE * KVP
    kern = functools.partial(_cmp_sample_kernel, layer=layer, n_chunks=n_chunks, n_seq=n_seq, dec_seq=dec_seq,
                             past_len=past_len, n_sel_blocks=ns)
    const = lambda shp: pl.BlockSpec(shp, lambda b, c, pt: (0,) * len(shp))
    grid_spec = pltpu.PrefetchScalarGridSpec(
        num_scalar_prefetch=1,
        grid=(n_seq, n_chunks),
        in_specs=[pl.BlockSpec(memory_space=pl.ANY),
                  pl.BlockSpec((None, rows, LANES), lambda b, c, pt: (b, 0, 0)),
                  const((2, 8, width)), const((2, width, 2 * KVP)), const((2, KVP, KVP)), const((nsub, ns_pad))],
        out_specs=[pl.BlockSpec((None, rows, LANES), lambda b, c, pt: (b, 0, 0)),
                   pl.BlockSpec((None, n_kt, LANES), lambda b, c, pt: (b, 0, 0))],
        scratch_shapes=[pltpu.VMEM((2, PAGES_PER_STEP, 2, KVP, PAGE_SIZE), F32),
                        pltpu.SemaphoreType.DMA((2,)),
                        pltpu.VMEM((2, PAGES_PER_STEP * PAGE_SIZE, KVP), F32),
                        pltpu.VMEM((2, nsub, 2 * KVP), F32)])
    return pl.pallas_call(
        kern,
        grid_spec=grid_spec,
        out_shape=[jax.ShapeDtypeStruct((n_seq, rows, LANES), F32),
                   jax.ShapeDtypeStruct((n_seq, n_kt, LANES), jnp.int32)],
        compiler_params=pltpu.CompilerParams(
            dimension_semantics=("arbitrary", "arbitrary"), vmem_limit_bytes=VMEM_LIMIT),
        name="cmp_sample",
    )(page_table, cache_t, q_gkt, pe2, w1kv, w2kv, selmap)


def _compress_weights_kv(pe, w1, w2):
    eye = jnp.eye(N_KV_HEADS, dtype=F32)
    w1r = w1.reshape(2, 2, CMP_STRIDE, HEAD_DIM, HEAD_DIM)
    w1kv = jnp.einsum('ksjde,hg->kjhdsge', w1r, eye).reshape(2, CMP_STRIDE * KVP, 2 * KVP)
    w2kv = jnp.einsum('ked,hg->khegd', w2, eye).reshape(2, KVP, KVP)
    per = pe.reshape(2, 2, CMP_STRIDE, HEAD_DIM)
    per = jnp.broadcast_to(per[:, :, :, None, :], (2, 2, CMP_STRIDE, N_KV_HEADS, HEAD_DIM))
    pe2 = jnp.pad(per.reshape(2, 2, CMP_STRIDE * KVP), ((0, 0), (0, 6), (0, 0)))
    return pe2, w1kv.astype(BF16), w2kv.astype(BF16)


def _slc_sample_kernel(*refs, layer, n_seq, n_pages, dec_seq, past_len, n_alias):
    (pt_ref, idx_sm_ref, cache_ref, q_ref, idxv_ref, knew_ref, win_ref, wnew_ref, oc_ref, gate_ref, expand_ref,
     wnewt_ref) = refs[:12]
    o_ref, wout_ref, kbuf, vbuf, sem = refs[12 + n_alias:]
    b = pl.program_id(0)
    slot = lax.rem(b, 2)
    n_kt = N_KV_HEADS * dec_seq
    rows = GROUP * n_kt
    n_past_blocks = past_len // SLC_BLOCK
    per_head = dec_seq * SLC_TOPK

    def tile_copies(sb, sl, k, i):
        kt = k * dec_seq + i // SLC_TOPK
        s = lax.rem(i, SLC_TOPK)
        j = idx_sm_ref[(sb * n_kt + kt) * SLC_TOPK + s]
        phys = pt_ref[sb, jnp.minimum(lax.shift_right_logical(j, 1), n_pages - 1)]
        return [pltpu.make_async_copy(cache_ref.at[layer, phys, kv, pl.ds(k * HEAD_DIM, HEAD_DIM), :],
                                      buf.at[sl, kt, s], sem.at[sl]) for kv, buf in ((0, kbuf), (1, vbuf))]

    def start_all(sb, sl):
        for k in range(N_KV_HEADS):
            def body(i, carry):
                for cp in tile_copies(sb, sl, k, i):
                    cp.start()
                return carry
            lax.fori_loop(0, per_head, body, 0)

    @pl.when(b == 0)
    def _():
        start_all(b, slot)

    @pl.when(b + 1 < n_seq)
    def _():
        start_all(b + 1, 1 - slot)

    for k in range(N_KV_HEADS):
        def wait_body(i, carry):
            for cp in tile_copies(b, slot, k, i):
                cp.wait()
            return carry
        lax.fori_loop(0, per_head, wait_body, 0)

    q = q_ref[...]
    qb = q.astype(BF16)
    gates = gate_ref[...]
    t_row = lax.rem(lax.broadcasted_iota(jnp.int32, (rows, 1), 0) // GROUP, dec_seq)
    t_new = lax.broadcasted_iota(jnp.int32, (1, dec_seq), 1)
    new_ok = t_new <= t_row

    wb = win_ref.shape[2]
    kpos = past_len - wb + lax.broadcasted_iota(jnp.int32, (1, wb), 1)
    dist = past_len + t_row - kpos
    w_ok = (dist >= 0) & (dist <= WINDOW) & (kpos >= 0)
    s_w = jnp.where(w_ok, _dot(qb, win_ref[0].astype(BF16)), NEG)
    s_n = jnp.where(new_ok, _dot_t(qb, wnew_ref[:, 0:KVP].astype(BF16)), NEG)
    m = jnp.maximum(jnp.max(s_w, axis=-1, keepdims=True), jnp.max(s_n, axis=-1, keepdims=True))
    e_w = jnp.where(w_ok, jnp.exp(s_w - m), 0.0)
    e_n = jnp.where(new_ok, jnp.exp(s_n - m), 0.0)
    den = jnp.sum(e_w, axis=-1, keepdims=True) + jnp.sum(e_n, axis=-1, keepdims=True)
    o_w = (_dot_t(e_w.astype(BF16), win_ref[1].astype(BF16))
           + _dot(e_n.astype(BF16), wnew_ref[:, KVP:2 * KVP].astype(BF16))) / den

    idxv = idxv_ref[...]
    lane16 = lax.broadcasted_iota(jnp.int32, (1, LANES), 1) < SLC_TOPK
    half = jnp.where(lane16 & ((idxv & 1) == 1), 1.0, 0.0).astype(BF16)
    live = jnp.where(lane16 & (idxv < n_past_blocks), 1.0, 0.0).astype(BF16)
    half_x = _dot(half, expand_ref[...])
    live_x = _dot(live, expand_ref[...])
    col = lax.broadcasted_iota(jnp.int32, (1, SLC_TOPK * PAGE_SIZE), 1)
    col_half = (lax.rem(col, PAGE_SIZE) // SLC_BLOCK).astype(F32)
    tile_ok = (live_x > 0.5) & (half_x == col_half)
    s_new = _dot_t(qb, knew_ref[:, 0:KVP].astype(BF16))
    zeros_half = jnp.zeros((GROUP, HEAD_DIM), F32)
    o_parts = []
    for kt in range(n_kt):
        k = kt // dec_seq
        r0 = kt * GROUP
        qk = q[r0:r0 + GROUP, k * HEAD_DIM:(k + 1) * HEAD_DIM].astype(BF16)
        kcat = jnp.concatenate([kbuf[slot, kt, s] for s in range(SLC_TOPK)], axis=1).astype(BF16)
        vcat = jnp.concatenate([vbuf[slot, kt, s] for s in range(SLC_TOPK)], axis=1).astype(BF16)
        ok = tile_ok[kt:kt + 1]
        nok = new_ok[r0:r0 + GROUP]
        s_s = jnp.where(ok, _dot(qk, kcat), NEG)
        s_n = jnp.where(nok, s_new[r0:r0 + GROUP], NEG)
        m = jnp.maximum(jnp.max(s_s, axis=-1, keepdims=True), jnp.max(s_n, axis=-1, keepdims=True))
        e_s = jnp.where(ok, jnp.exp(s_s - m), 0.0)
        e_n = jnp.where(nok, jnp.exp(s_n - m), 0.0)
        den = jnp.sum(e_s, axis=-1, keepdims=True) + jnp.sum(e_n, axis=-1, keepdims=True)
        v_new = knew_ref[:, KVP + k * HEAD_DIM:KVP + (k + 1) * HEAD_DIM].astype(BF16)
        o = (_dot_t(e_s.astype(BF16), vcat) + _dot(e_n.astype(BF16), v_new)) / den
        o_parts.append(jnp.concatenate([o, zeros_half] if k == 0 else [zeros_half, o], axis=1))
    o_s = jnp.concatenate(o_parts, axis=0)

    o_ref[...] = gates[:, 0:1] * oc_ref[...] + gates[:, 1:2] * o_s + gates[:, 2:3] * o_w

    lane_w = lax.broadcasted_iota(jnp.int32, (1, wb), 1)
    for kv in range(2):
        moved = pltpu.roll(win_ref[kv], wb - dec_seq, 1)
        for t in range(dec_seq):
            moved = jnp.where(lane_w == wb - dec_seq + t, wnewt_ref[kv * KVP:(kv + 1) * KVP, t:t + 1], moved)
        wout_ref[kv] = moved


def _slc_sample(page_table, idx_flat, cache_t, q_ktg, idxv, kvs_new, win_t, kvw_new, o_c, gates_r, win_out,
                *, layer, dec_seq, past_len):
    n_seq, n_pages = page_table.shape
    n_kt = N_KV_HEADS * dec_seq
    rows = GROUP * n_kt
    wb = win_t.shape[-1]
    cols = SLC_TOPK * PAGE_SIZE
    expand = jnp.asarray((np.arange(LANES)[:, None] == (np.arange(cols) // PAGE_SIZE)[None, :]).astype(np.float32),
                         dtype=BF16)
    n_alias = 0 if win_out is None else 1
    kern = functools.partial(_slc_sample_kernel, layer=layer, n_seq=n_seq, n_pages=n_pages, dec_seq=dec_seq,
                             past_len=past_len, n_alias=n_alias)
    per_seq = lambda r, w: pl.BlockSpec((None, r, w), lambda b, pt, ix: (b, 0, 0))
    state_spec = pl.BlockSpec((None, None, 2, KVP, wb), lambda b, pt, ix: (layer, b, 0, 0, 0))
    in_specs = [pl.BlockSpec(memory_space=pl.ANY),
                per_seq(rows, LANES), per_seq(n_kt, LANES), per_seq(dec_seq, KV_W), state_spec,
                per_seq(dec_seq, KV_W), per_seq(rows, LANES), per_seq(rows, LANES),
                pl.BlockSpec((LANES, cols), lambda b, pt, ix: (0, 0)), per_seq(KV_W, dec_seq)]
    args = [page_table, idx_flat, cache_t, q_ktg, idxv, kvs_new, win_t, kvw_new, o_c, gates_r, expand,
            jnp.swapaxes(kvw_new, 1, 2)]
    if n_alias:
        in_specs.append(pl.BlockSpec(memory_space=pl.ANY))
        args.append(win_out)
    grid_spec = pltpu.PrefetchScalarGridSpec(
        num_scalar_prefetch=2,
        grid=(n_seq,),
        in_specs=in_specs,
        out_specs=[per_seq(rows, LANES), state_spec],
        scratch_shapes=[pltpu.VMEM((2, n_kt, SLC_TOPK, HEAD_DIM, PAGE_SIZE), F32),
                        pltpu.VMEM((2, n_kt, SLC_TOPK, HEAD_DIM, PAGE_SIZE), F32),
                        pltpu.SemaphoreType.DMA((2,))])
    return pl.pallas_call(
        kern,
        grid_spec=grid_spec,
        out_shape=[jax.ShapeDtypeStruct((n_seq, rows, LANES), F32), jax.ShapeDtypeStruct(win_t.shape, F32)],
        input_output_aliases={len(args) - 1: 1} if n_alias else {},
        compiler_params=pltpu.CompilerParams(dimension_semantics=("arbitrary",), vmem_limit_bytes=VMEM_LIMIT),
        name="slc_sample",
    )(*args)


def _pool_sample_kernel(z_ref, d_ref, *, dec_seq, pos0):
    lane = lax.broadcasted_iota(jnp.int32, (1, POOL_W), 1)
    grp = lane // POOL_GW
    for t in range(dec_seq):
        cur = z_ref[:, POOL_HIST + t, :]
        acc = cur
        sums = {}
        for back in range(1, max(POOL_WINDOWS)):
            acc = acc + z_ref[:, POOL_HIST + t - back, :]
            if back + 1 in POOL_WINDOWS:
                sums[back + 1] = acc
        d = None
        for g, w in enumerate(POOL_WINDOWS):
            val = sums[w] / float(min(w, pos0 + t + 1)) - cur
            d = val if d is None else jnp.where(grp == g, val, d)
        d_ref[:, t, :] = d


def _pool_sample(z_ext, *, dec_seq, pos0):
    n_seq = z_ext.shape[0]
    return pl.pallas_call(
        functools.partial(_pool_sample_kernel, dec_seq=dec_seq, pos0=pos0),
        out_shape=jax.ShapeDtypeStruct((n_seq, dec_seq, POOL_W), F32),
        name="pool_sample",
    )(z_ext)


def _sample_layer(x, lw, cmp_t, slc_t, win_t, win_out, pool_state, page_table, *, layer, n_seq, dec_seq):
    rows = n_seq * dec_seq
    n_kt = N_KV_HEADS * dec_seq
    x = _ffn_ln(x, lw['ffn_in'], lw['ffn_out'], lw['ln_g'][0], lw['ln_b'][0], sel=(lw['layer'], 0),tm=rows)
    cos, sin = _rope_tables(PAST_LEN + jnp.arange(rows) % dec_seq)
    qt, kvc, kvs, kvw, gates, gu, gv, p = _inproj(
        x, lw['w_ext'], cos, sin, lw['gm_ln_g'], lw['gm_ln_b'], tm=rows)

    qf = jnp.swapaxes(qt.astype(F32), 1, 2).reshape(N_KV_HEADS, GROUP, n_seq, dec_seq, LANES)
    q_gkt = qf.transpose(2, 1, 0, 3, 4).reshape(n_seq, GROUP * n_kt, LANES).astype(BF16)
    q_ktg = qf.transpose(2, 0, 3, 1, 4).reshape(n_seq, GROUP * n_kt, LANES)
    o_c, idxv = _cmp_sample(page_table, cmp_t, q_gkt, lw['pe2'], lw['w1kv'], lw['w2kv'],
                            layer=layer, dec_seq=dec_seq, past_len=PAST_LEN)
    o_c = o_c.reshape(n_seq, GROUP, N_KV_HEADS, dec_seq, LANES).transpose(0, 2, 3, 1, 4).reshape(n_seq, GROUP * n_kt, LANES)
    gates_r = gates[:, :GATE_W].reshape(n_seq, dec_seq, N_KV_HEADS, GROUP, 3).transpose(0, 2, 1, 3, 4)
    gates_r = jnp.pad(gates_r.reshape(n_seq, GROUP * n_kt, 3), ((0, 0), (0, 0), (0, LANES - 3)))
    idx_flat = idxv[:, :, :SLC_TOPK].reshape(-1)
    o, win_out = _slc_sample(page_table, idx_flat, slc_t, q_ktg, idxv, kvs.reshape(n_seq, dec_seq, KV_W), win_t,
                             kvw.reshape(n_seq, dec_seq, KV_W), o_c, gates_r, win_out,
                             layer=layer, dec_seq=dec_seq, past_len=PAST_LEN)
    o = o.reshape(n_seq, N_KV_HEADS, dec_seq, GROUP, LANES)
    o_nsa = jnp.stack([o[:, k, :, :, k * HEAD_DIM:(k + 1) * HEAD_DIM] for k in range(N_KV_HEADS)], axis=2)
    o_nsa = o_nsa.reshape(rows, NSA_W).astype(BF16)

    z_ext = jnp.concatenate([pool_state, p.reshape(n_seq, dec_seq, POOL_W)], axis=1)
    d = _pool_sample(z_ext, dec_seq=dec_seq, pos0=PAST_LEN).reshape(rows, POOL_W)
    ws_cat, gb_full = _gmlp_weights(lw['gm_ws'], lw['gm_b'], dec_seq, GM_CHUNK // dec_seq)
    x = _mix_out(x, o_nsa, gu, gv, d, ws_cat, gb_full, lw['pw_big'], lw['pool_scale'], lw['w_o'],
                 lw['ln_g'][1], lw['ln_b'][1], tm=rows, seq=dec_seq, pool_in_kernel=False)
    x = _ffn_ln(x, lw['ffn_in'], lw['ffn_out'], lw['ln_g'][2], lw['ln_b'][2], sel=(lw['layer'], 1),tm=rows)
    shp = (n_seq, dec_seq, 2, N_KV_HEADS, HEAD_DIM)
    new = (kvc.reshape(shp), kvs.reshape(shp), z_ext[:, dec_seq:], gv.reshape(n_seq, dec_seq, GM_W))
    return x, new, win_out


def _pages_by_channel(cache):
    nd = cache.ndim
    t = jnp.transpose(cache, tuple(range(nd - 4)) + (nd - 3, nd - 2, nd - 1, nd - 4))
    return t.reshape(t.shape[:-3] + (KVP, t.shape[-1]))
def _prompt_layer(x, lw, leaf_bufs, *, layer, batch, seq, tm, tm_ffn):
    x = _ffn_ln(x, lw['ffn_in'], lw['ffn_out'], lw['ln_g'][0], lw['ln_b'][0], sel=(lw['layer'], 0),tm=tm_ffn)
    cos, sin = _rope_tables(jnp.arange(seq))
    qt, kvc, leaf_c, leaf_s, leaf_w, ks, vts, kw, vtw, gates, gu, gv, p = _inproj(
        x, lw['w_ext'], cos, sin, lw['gm_ln_g'], lw['gm_ln_b'], tm=tm, leaf_bufs=leaf_bufs, layer=layer, batch=batch)
    kc, vct = _compress_prompt(kvc, lw['pe2'], lw['w1kv'], lw['w2kv'], batch=batch)
    o_nsa = _nsa_prompt(qt, kc, vct, ks, vts, kw, vtw, gates, batch=batch, seq=seq)
    ws_cat, gb_full = _gmlp_weights(lw['gm_ws'], lw['gm_b'], GM_CHUNK, 1)
    x = _mix_out(x, o_nsa, gu, gv, p, ws_cat, gb_full, lw['pw_big'], lw['pool_scale'], lw['w_o'],
                 lw['ln_g'][1], lw['ln_b'][1], tm=tm, seq=seq, pool_in_kernel=True)
    x = _ffn_ln(x, lw['ffn_in'], lw['ffn_out'], lw['ln_g'][2], lw['ln_b'][2], sel=(lw['layer'], 1),tm=tm_ffn)
    return x, (leaf_c, leaf_s, leaf_w), p.reshape(batch, seq, POOL_W)[:, seq - POOL_HIST:]


def _leaf_rows(buf):
    d, b, _, t = buf.shape
    return buf.reshape(d, b, 2, N_KV_HEADS, HEAD_DIM, t).transpose(0, 1, 5, 2, 3, 4)


def _layer_weights(l, ffn_in_b, ffn_out_b, ln_g, ln_b, w_in, w_o, cmp_pe, cmp_w1, cmp_w2,
                   gm_ln_g, gm_ln_b, gm_ws, gm_b, pool_w, pool_scale):
    pe2, w1kv, w2kv = _compress_weights_kv(cmp_pe[l], cmp_w1[l], cmp_w2[l])
    return dict(layer=l, ffn_in=ffn_in_b, ffn_out=ffn_out_b, ln_g=ln_g[l], ln_b=ln_b[l],
                w_ext=_build_w_ext(w_in[l]), w_o=w_o[l].astype(BF16),
                pe2=pe2, w1kv=w1kv, w2kv=w2kv,
                gm_ln_g=gm_ln_g[l], gm_ln_b=gm_ln_b[l], gm_ws=gm_ws[l], gm_b=gm_b[l],
                pw_big=_pool_weights(pool_w[l]), pool_scale=pool_scale[l])


def kernel(x_prompt, x_sample, cache_kv_cmp, cache_kv_slc, state_kv_win, state_pool, page_table, ln_g, ln_b, ffn_w_in, ffn_w_out, w_in, w_o, cmp_pe, cmp_w1, cmp_w2, gm_ln_g, gm_ln_b, gm_ws, gm_b, pool_w, pool_scale):
    batch, seq, _ = x_prompt.shape
    fi = ffn_w_in.astype(BF16)
    fo = ffn_w_out.astype(BF16)
    n_seq, dec_seq, _ = x_sample.shape
    xp = x_prompt.reshape(batch * seq, D_MODEL)
    xs = x_sample.reshape(n_seq * dec_seq, D_MODEL)
    cmp_t = _pages_by_channel(cache_kv_cmp)
    slc_t = _pages_by_channel(cache_kv_slc)
    win_t = _pages_by_channel(state_kv_win)
    leaf_bufs = tuple(jnp.zeros((DEPTH, batch, KV_W, seq), F32) for _ in range(3))
    win_out = jnp.zeros(win_t.shape, F32)
    pool_p, new_s = [], []
    for l in range(DEPTH):
        lw = _layer_weights(l, fi, fo, ln_g, ln_b, w_in, w_o, cmp_pe, cmp_w1, cmp_w2, gm_ln_g, gm_ln_b, gm_ws, gm_b, pool_w, pool_scale)
        xp, leaf_bufs, pool_l = _prompt_layer(xp, lw, leaf_bufs, layer=l, batch=batch, seq=seq, tm=512, tm_ffn=1024)
        xs, st_s, win_out = _sample_layer(xs, lw, cmp_t, slc_t, win_t, win_out, state_pool[l], page_table,
                                          layer=l, n_seq=n_seq, dec_seq=dec_seq)
        pool_p.append(pool_l)
        new_s.append(st_s)
    stk = lambda lst, i: jnp.stack([t[i] for t in lst])
    wb = min(WINDOW, seq)
    win_s = _leaf_rows(win_out.reshape(DEPTH, n_seq, KV_W, win_out.shape[-1]))
    return (xp.reshape(batch, seq, D_MODEL), xs.reshape(n_seq, dec_seq, D_MODEL),
            _leaf_rows(leaf_bufs[0]), stk(new_s, 0), _leaf_rows(leaf_bufs[1]), stk(new_s, 1),
            _leaf_rows(leaf_bufs[2])[:, :, seq - wb:], win_s, jnp.stack(pool_p), stk(new_s, 2), stk(new_s, 3))
```

```python
import functools

import numpy as np
import jax
import jax.numpy as jnp
from jax import lax
from jax.experimental import pallas as pl
from jax.experimental.pallas import tpu as pltpu

F32 = jnp.float32
BF16 = jnp.bfloat16

D_MODEL = 1024
DEPTH = 2
PAST_LEN = 16384
PAGE_SIZE = 128
HEAD_DIM = 64
NSA_W = D_MODEL // 2
GM_W = D_MODEL // 4
POOL_W = D_MODEL // 4
N_HEADS = NSA_W // HEAD_DIM
N_KV_HEADS = 2
GROUP = N_HEADS // N_KV_HEADS
CMP_STRIDE = 16
CMP_LEN = 2 * CMP_STRIDE
SLC_BLOCK = 64
SLC_TOPK = 16
WINDOW = 512
FORCE_SCORE = 1.0e4
ROPE_THETA = 10000.0
SCALE = HEAD_DIM ** -0.5
GM_HEADS = GM_W // HEAD_DIM
GM_CHUNK = 128
POOL_GROUPS = 4
POOL_GW = POOL_W // POOL_GROUPS
POOL_WINDOWS = (2, 4, 8, 16)
POOL_HIST = max(POOL_WINDOWS) - 1
D_FF = 256 * ((8 * D_MODEL // 3 + 255) // 256)
ALPHA = (2 * DEPTH) ** 0.25
LN_EPS = 1e-5
Q_W = N_HEADS * HEAD_DIM
KV_W = 2 * N_KV_HEADS * HEAD_DIM
GATE_W = 3 * N_HEADS
N_IN = Q_W + 3 * KV_W + GATE_W + 2 * GM_W + POOL_W

LANES = 128
KVP = N_KV_HEADS * HEAD_DIM
VMEM_LIMIT = 56 * 1024 * 1024
NEG = -1e30
LOG2E = 1.4426950408889634
HALO = 16

_OFF_Q = 0
_OFF_QR = Q_W
_OFF_KV = 2 * Q_W
_OFF_GATE = _OFF_KV + 3 * 3 * KVP
_OFF_UV = _OFF_GATE + LANES
_OFF_P = _OFF_UV + 2 * GM_W
N_EXT = _OFF_P + POOL_W


def _ln_rows(y, g, b):
    mu = jnp.mean(y, axis=-1, keepdims=True)
    d = y - mu
    var = jnp.mean(d * d, axis=-1, keepdims=True)
    return d * lax.rsqrt(var + LN_EPS) * g + b


def _dot(a, b):
    return jnp.dot(a, b, preferred_element_type=F32)


def _dot_t(a, b):
    return lax.dot_general(a, b, (((1,), (1,)), ((), ())), preferred_element_type=F32)


def _ffn_kernel(x_ref, wg_ref, wu_ref, wo_ref, g_ref, b_ref, o_ref, xb_ref, *, n_chunks):
    j = pl.program_id(1)

    @pl.when(j == 0)
    def _():
        xb_ref[...] = x_ref[...].astype(BF16)

    xb = xb_ref[...]
    gate = _dot(xb, wg_ref[...])
    up = _dot(xb, wu_ref[...])
    hid = (gate * jax.nn.sigmoid(gate)) * up
    part = _dot(hid.astype(BF16), wo_ref[...])

    @pl.when(j == 0)
    def _():
        o_ref[...] = part

    if n_chunks > 2:
        @pl.when((j > 0) & (j < n_chunks - 1))
        def _():
            o_ref[...] += part

    @pl.when(j == n_chunks - 1)
    def _():
        y = ALPHA * x_ref[...] + 0.5 * (o_ref[...] + part)
        o_ref[...] = _ln_rows(y, g_ref[...], b_ref[...])


def _ffn_ln(x, w_in_b, w_out_b, g, b, *, tm, sel):
    rows = x.shape[0]
    n_chunks = 2
    fc = D_FF // n_chunks
    l, w = sel
    return pl.pallas_call(
        functools.partial(_ffn_kernel, n_chunks=n_chunks),
        grid=(rows // tm, n_chunks),
        in_specs=[
            pl.BlockSpec((tm, D_MODEL), lambda i, j: (i, 0)),
            pl.BlockSpec((None, None, D_MODEL, fc), lambda i, j: (l, w, 0, j)),
            pl.BlockSpec((None, None, D_MODEL, fc), lambda i, j: (l, w, 0, n_chunks + j)),
            pl.BlockSpec((None, None, fc, D_MODEL), lambda i, j: (l, w, j, 0)),
            pl.BlockSpec((1, D_MODEL), lambda i, j: (0, 0)),
            pl.BlockSpec((1, D_MODEL), lambda i, j: (0, 0)),
        ],
        out_specs=pl.BlockSpec((tm, D_MODEL), lambda i, j: (i, 0)),
        out_shape=jax.ShapeDtypeStruct((rows, D_MODEL), F32),
        scratch_shapes=[pltpu.VMEM((tm, D_MODEL), BF16)],
        compiler_params=pltpu.CompilerParams(
            dimension_semantics=("parallel", "arbitrary"), vmem_limit_bytes=VMEM_LIMIT),
        name="ffn_ln",
    )(x, w_in_b, w_in_b, w_out_b, g.reshape(1, D_MODEL), b.reshape(1, D_MODEL))


def _inproj_kernel(*refs, prompt, n_alias, q_scale):
    h_ref, w_ref, cos_ref, sin_ref, gmg_ref, gmb_ref = refs[:6]
    outs = refs[6 + n_alias:]
    if prompt:
        (qt_ref, kvcb_ref, leafc_ref, leafs_ref, leafw_ref, ks_ref, vts_ref, kw_ref, vtw_ref,
         gate_ref, gu_ref, gv_ref, p_ref) = outs
        leaves = (leafc_ref, leafs_ref, leafw_ref)
        k_refs = (None, ks_ref, kw_ref)
        vt_refs = (None, vts_ref, vtw_ref)
    else:
        qt_ref, kvc_ref, kvs_ref, kvw_ref, gate_ref, gu_ref, gv_ref, p_ref = outs
        rows_out = (kvc_ref, kvs_ref, kvw_ref)
    hb = h_ref[...].astype(BF16)
    cos = cos_ref[...]
    sin = sin_ref[...]

    zq = _dot(hb, w_ref[:, _OFF_Q:_OFF_KV])
    zk = _dot(hb, w_ref[:, _OFF_KV:_OFF_UV])
    zu = _dot(hb, w_ref[:, _OFF_UV:N_EXT])

    n_sq = hb.shape[0] // LANES
    zeros_half = jnp.zeros((HEAD_DIM, LANES), F32)

    for m in range(N_HEADS // 2):
        c0 = m * LANES
        pair = (zq[:, c0:c0 + LANES] * cos + zq[:, Q_W + c0:Q_W + c0 + LANES] * sin) * q_scale
        kvh = (2 * m) // GROUP
        for c in range(n_sq):
            pt = pair[c * LANES:(c + 1) * LANES].T
            for e in range(2):
                piece = pt[e * HEAD_DIM:(e + 1) * HEAD_DIM]
                both = [piece, zeros_half] if kvh == 0 else [zeros_half, piece]
                qt_ref[2 * m + e, :, c * LANES:(c + 1) * LANES] = jnp.concatenate(both, axis=0).astype(BF16)

    for br in range(3):
        c0 = br * 3 * KVP
        k = zk[:, c0:c0 + KVP] * cos + zk[:, c0 + KVP:c0 + 2 * KVP] * sin
        v = zk[:, c0 + 2 * KVP:c0 + 3 * KVP]
        if not prompt:
            rows_out[br][:, 0:KVP] = k
            rows_out[br][:, KVP:2 * KVP] = v
            continue
        for c in range(n_sq):
            cols = slice(c * LANES, (c + 1) * LANES)
            vt = v[cols].T
            leaves[br][0:KVP, cols] = k[cols].T
            leaves[br][KVP:2 * KVP, cols] = vt
            if br > 0:
                vt_refs[br][c] = vt.astype(BF16)
        if br == 0:
            kvcb_ref[0] = k
            kvcb_ref[1] = v
        else:
            k_refs[br][...] = k.astype(BF16)

    gate_ref[...] = jax.nn.sigmoid(zk[:, _OFF_GATE - _OFF_KV:_OFF_UV - _OFF_KV])
    gu_ref[...] = jax.nn.gelu(zu[:, 0:GM_W])
    gv_ref[...] = _ln_rows(jax.nn.gelu(zu[:, GM_W:2 * GM_W]), gmg_ref[...], gmb_ref[...])
    p_ref[...] = zu[:, 2 * GM_W:2 * GM_W + POOL_W]


def _inproj(h, w_ext, cos, sin, gmg, gmb, *, tm, leaf_bufs=None, layer=0, batch=None):
    rows = h.shape[0]
    n_tab = cos.shape[0] // tm
    prompt = leaf_bufs is not None
    row_spec = lambda w: pl.BlockSpec((tm, w), lambda i: (i, 0))
    tab_spec = pl.BlockSpec((tm, LANES), lambda i: (i % n_tab, 0))
    vec_spec = pl.BlockSpec((1, GM_W), lambda i: (0, 0))
    sds = jax.ShapeDtypeStruct
    in_specs = [row_spec(D_MODEL), pl.BlockSpec((D_MODEL, N_EXT), lambda i: (0, 0)),
                tab_spec, tab_spec, vec_spec, vec_spec]
    args = [h, w_ext, cos, sin, gmg.reshape(1, GM_W), gmb.reshape(1, GM_W)]
    tail_specs = [row_spec(LANES), row_spec(GM_W), row_spec(GM_W), row_spec(POOL_W)]
    tail_shapes = [sds((rows, LANES), F32), sds((rows, GM_W), F32), sds((rows, GM_W), F32), sds((rows, POOL_W), F32)]
    qt_spec = pl.BlockSpec((N_HEADS, LANES, tm), lambda i: (0, 0, i))
    qt_shape = sds((N_HEADS, LANES, rows), BF16)
    aliases = {}
    if prompt:
        seq = rows // batch
        tiles = seq // tm
        leaf_spec = pl.BlockSpec((None, None, KV_W, tm), lambda i: (layer, i // tiles, 0, i % tiles))
        leaf_shape = sds((DEPTH, batch, KV_W, seq), F32)
        sq_spec = pl.BlockSpec((tm // LANES, KVP, LANES), lambda i: (i, 0, 0))
        sq_shape = sds((rows // LANES, KVP, LANES), BF16)
        out_specs = [qt_spec, pl.BlockSpec((2, tm, KVP), lambda i: (0, i, 0)), leaf_spec, leaf_spec, leaf_spec,
                     row_spec(KVP), sq_spec, row_spec(KVP), sq_spec] + tail_specs
        out_shape = [qt_shape, sds((2, rows, KVP), F32), leaf_shape, leaf_shape, leaf_shape,
                     sds((rows, KVP), BF16), sq_shape, sds((rows, KVP), BF16), sq_shape] + tail_shapes
        for n, buf in enumerate(leaf_bufs):
            in_specs.append(pl.BlockSpec(memory_space=pl.ANY))
            args.append(buf)
            aliases[6 + n] = 2 + n
    else:
        out_specs = [qt_spec, row_spec(KV_W), row_spec(KV_W), row_spec(KV_W)] + tail_specs
        out_shape = [qt_shape, sds((rows, KV_W), F32), sds((rows, KV_W), F32), sds((rows, KV_W), F32)] + tail_shapes
    return pl.pallas_call(
        functools.partial(_inproj_kernel, prompt=prompt, n_alias=len(aliases),
                          q_scale=SCALE * LOG2E if prompt else SCALE),
        grid=(rows // tm,),
        in_specs=in_specs,
        out_specs=out_specs,
        out_shape=out_shape,
        input_output_aliases=aliases,
        compiler_params=pltpu.CompilerParams(dimension_semantics=("parallel",), vmem_limit_bytes=VMEM_LIMIT),
        name="inproj",
    )(*args)


def _build_w_ext(w_in):
    half = HEAD_DIM // 2

    def rot(w):
        n = w.shape[1] // HEAD_DIM
        w3 = w.reshape(D_MODEL, n, 2, half)
        return jnp.stack([-w3[:, :, 1], w3[:, :, 0]], axis=2).reshape(D_MODEL, n * HEAD_DIM)

    q = w_in[:, :Q_W]
    cols = [q, rot(q)]
    for br in range(3):
        kv = w_in[:, Q_W + br * KV_W:Q_W + (br + 1) * KV_W]
        k, v = kv[:, :KVP], kv[:, KVP:]
        cols += [k, rot(k), v]
    g0 = Q_W + 3 * KV_W
    cols.append(jnp.pad(w_in[:, g0:g0 + GATE_W], ((0, 0), (0, LANES - GATE_W))))
    cols.append(w_in[:, g0 + GATE_W:])
    return jnp.concatenate(cols, axis=1).astype(BF16)


def _rope_tables(pos):
    half = HEAD_DIM // 2
    inv = ROPE_THETA ** (-jnp.arange(half, dtype=F32) / half)
    ang = pos.astype(F32)[:, None] * inv[None, :]
    cos = jnp.tile(jnp.cos(ang), (1, LANES // half))
    sin = jnp.tile(jnp.sin(ang), (1, LANES // half))
    return cos, sin


def _compress_kernel(x_ref, pe_ref, w1_ref, w2_ref, kc_ref, vct_ref):
    nsub = kc_ref.shape[0]
    outs = []
    for kv in range(2):
        xr = jnp.concatenate(
            [x_ref[kv, pl.ds(j, nsub, stride=CMP_STRIDE), :].astype(BF16) for j in range(CMP_STRIDE)], axis=1)
        f = _dot(xr, w1_ref[kv])
        per = _dot(pe_ref[kv].astype(BF16), w1_ref[kv])
        pe_term = per[0:1, 0:KVP] + per[1:2, KVP:2 * KVP]
        nxt = jnp.concatenate([f[1:, KVP:2 * KVP], jnp.zeros((1, KVP), F32)], axis=0)
        hid = jax.nn.gelu(f[:, 0:KVP] + nxt + pe_term)
        outs.append(_dot(hid.astype(BF16), w2_ref[kv]))
    kc_ref[...] = outs[0].astype(BF16)
    for c in range(nsub // LANES):
        vct_ref[:, c * LANES:(c + 1) * LANES] = outs[1][c * LANES:(c + 1) * LANES].T.astype(BF16)


def _compress_prompt(kv_rows, pe2, w1kv, w2kv, *, batch):
    seq = kv_rows.shape[1] // batch
    nsub = seq // CMP_STRIDE
    width = CMP_STRIDE * KVP
    return pl.pallas_call(
        _compress_kernel,
        grid=(batch,),
        in_specs=[pl.BlockSpec((2, seq, KVP), lambda b: (0, b, 0)),
                  pl.BlockSpec((2, 8, width), lambda b: (0, 0, 0)),
                  pl.BlockSpec((2, width, 2 * KVP), lambda b: (0, 0, 0)),
                  pl.BlockSpec((2, KVP, KVP), lambda b: (0, 0, 0))],
        out_specs=[pl.BlockSpec((None, nsub, KVP), lambda b: (b, 0, 0)),
                   pl.BlockSpec((None, KVP, nsub), lambda b: (b, 0, 0))],
        out_shape=[jax.ShapeDtypeStruct((batch, nsub, KVP), BF16), jax.ShapeDtypeStruct((batch, KVP, nsub), BF16)],
        compiler_params=pltpu.CompilerParams(dimension_semantics=("parallel",), vmem_limit_bytes=VMEM_LIMIT),
        name="compress_prompt",
    )(kv_rows, pe2, w1kv, w2kv)


def _top_blocks_cols(score, blk):
    sel = jnp.zeros(score.shape, F32)
    for _ in range(SLC_TOPK):
        m = jnp.max(score, axis=0, keepdims=True)
        first = jnp.min(jnp.where(score == m, blk, 1e9), axis=0, keepdims=True)
        hit = blk == first
        sel = jnp.where(hit, 1.0, sel)
        score = jnp.where(hit, -jnp.inf, score)
    return sel


def _top_blocks_idx(score, blk):
    sel = jnp.zeros(score.shape, F32)
    idx = jnp.zeros((score.shape[0], LANES), jnp.int32)
    lane = lax.broadcasted_iota(jnp.int32, (1, LANES), 1)
    big = jnp.int32(1 << 20)
    for it in range(SLC_TOPK):
        m = jnp.max(score, axis=-1, keepdims=True)
        first = jnp.min(jnp.where(score == m, blk, big), axis=-1, keepdims=True)
        hit = blk == first
        sel = jnp.where(hit, 1.0, sel)
        idx = jnp.where(lane == it, first, idx)
        score = jnp.where(hit, -jnp.inf, score)
    return sel, idx


def _softmax_rows(s, mask):
    s = jnp.where(mask, s, NEG)
    m = jnp.max(s, axis=-1, keepdims=True)
    e = jnp.exp(s - m)
    return jnp.where(mask, e / jnp.sum(e, axis=-1, keepdims=True), 0.0)


def _nsa_prompt_kernel(qt_ref, kc_ref, vct_ref, ks_ref, vts_ref, kw_ref, vtw_ref, gate_ref, selmap_ref, expand_ref,
                       o_ref, *, tq, tk, seq, n_sel_blocks):
    t0 = pl.program_id(1) * tq
    nsub = kc_ref.shape[0]
    nb = 8 * ((n_sel_blocks + 7) // 8)
    span = min(WINDOW + tq, seq)
    qpos = t0 + lax.broadcasted_iota(jnp.int32, (1, tq), 1)
    rep = lambda x, n: jnp.concatenate([x] * n, axis=1)
    qt = jnp.concatenate([qt_ref[h] for h in range(N_HEADS)], axis=1)

    cmp_end = CMP_STRIDE * lax.broadcasted_iota(jnp.int32, (nsub, 1), 0) + (CMP_LEN - 1)
    c_bias = jnp.where(cmp_end <= qpos, 0.0, NEG)
    s = _dot(kc_ref[...], qt) + rep(c_bias, N_HEADS)
    e = jnp.exp2(s - jnp.max(s, axis=0, keepdims=True))
    inv_c = jnp.where(rep(qpos >= CMP_LEN - 1, N_HEADS), 1.0 / jnp.sum(e, axis=0, keepdims=True), 0.0)
    eb = e.astype(BF16)
    o_c = _dot(vct_ref[...], eb) * inv_c
    imp_h = _dot(selmap_ref[...], eb)[0:nb] * inv_c

    imp = []
    for k in range(N_KV_HEADS):
        acc = None
        for g in range(GROUP):
            part = imp_h[:, (GROUP * k + g) * tq:(GROUP * k + g + 1) * tq]
            acc = part if acc is None else acc + part
        imp.append(acc)
    imp = jnp.concatenate(imp, axis=1)
    blk = lax.broadcasted_iota(jnp.int32, (nb, 1), 0)
    cur = rep(qpos // SLC_BLOCK, N_KV_HEADS)
    forced = (blk == 0) | (blk == cur) | (blk == cur - 1)
    score = jnp.where(forced, FORCE_SCORE, jnp.where(blk <= cur, imp, -1.0))
    if nb > n_sel_blocks:
        score = jnp.where(blk < n_sel_blocks, score, -jnp.inf)
    sel = _top_blocks_cols(score, blk.astype(F32))
    sel_m1 = jnp.concatenate([sel - 1.0, jnp.zeros((LANES - nb, N_KV_HEADS * tq), F32)], axis=0).astype(BF16)

    start = pl.multiple_of(jnp.maximum(t0 + tq - span, 0), tq)
    dist = qpos - (start + lax.broadcasted_iota(jnp.int32, (span, 1), 0))
    w_bias = jnp.where(dist >= 0, jnp.where(dist <= WINDOW, 0.0, NEG), NEG)
    s = _dot(kw_ref[pl.ds(start, span), :], qt) + rep(w_bias, N_HEADS)
    e = jnp.exp2(s - jnp.max(s, axis=0, keepdims=True))
    inv_w = 1.0 / jnp.sum(e, axis=0, keepdims=True)
    c0 = start // LANES
    vt = jnp.concatenate([vtw_ref[c0 + c] for c in range(span // LANES)], axis=1)
    o_w = _dot(vt, e.astype(BF16)) * inv_w

    def tile(kt, carry, diagonal):
        m_i, l_i, acc = carry
        r0 = pl.multiple_of(kt * tk, tk)
        bias = _dot(expand_ref[pl.ds(r0, tk), :], sel_m1)
        if diagonal:
            kpos = r0 + lax.broadcasted_iota(jnp.int32, (tk, 1), 0)
            bias = jnp.where(kpos <= rep(qpos, N_KV_HEADS), bias, NEG)
        s = _dot(ks_ref[pl.ds(r0, tk), :], qt)
        s = jnp.concatenate([s[:, h * tq:(h + 1) * tq] + bias[:, (h // GROUP) * tq:(h // GROUP + 1) * tq]
                             for h in range(N_HEADS)], axis=1)
        m_new = jnp.maximum(m_i, jnp.max(s, axis=0, keepdims=True))
        a = jnp.exp2(m_i - m_new)
        e = jnp.exp2(s - m_new)
        l_new = a * l_i + jnp.sum(e, axis=0, keepdims=True)
        c0 = kt * (tk // LANES)
        vt = jnp.concatenate([vts_ref[c0 + c] for c in range(tk // LANES)], axis=1)
        return m_new, l_new, a * acc + _dot(vt, e.astype(BF16))

    n_kt = (t0 + tq + tk - 1) // tk
    init = (jnp.full((1, N_HEADS * tq), NEG, F32), jnp.zeros((1, N_HEADS * tq), F32),
            jnp.zeros((KVP, N_HEADS * tq), F32))
    carry = lax.fori_loop(0, n_kt - 1, lambda kt, c: tile(kt, c, False), init)
    _, l_s, acc_s = tile(n_kt - 1, carry, True)
    o_s = acc_s * (1.0 / l_s)

    gt = gate_ref[...].T
    parts = []
    for h in range(N_HEADS):
        rows = slice((h // GROUP) * HEAD_DIM, (h // GROUP + 1) * HEAD_DIM)
        cols = slice(h * tq, (h + 1) * tq)
        parts.append(gt[3 * h:3 * h + 1] * o_c[rows, cols] + gt[3 * h + 1:3 * h + 2] * o_s[rows, cols]
                     + gt[3 * h + 2:3 * h + 3] * o_w[rows, cols])
    ot = jnp.concatenate(parts, axis=0)
    for m in range(NSA_W // LANES):
        o_ref[:, m * LANES:(m + 1) * LANES] = ot[m * LANES:(m + 1) * LANES].T.astype(o_ref.dtype)


def _sel_map_t(nc_rows, n_cmp, ns):
    c0 = CMP_STRIDE * np.arange(nc_rows)[None, :]
    s0 = SLC_BLOCK * np.arange(LANES)[:, None]
    ov = np.clip(np.minimum(c0 + CMP_LEN, s0 + SLC_BLOCK) - np.maximum(c0, s0), 0, None) / CMP_LEN
    ov = ov * (np.arange(nc_rows)[None, :] < n_cmp) * (np.arange(LANES)[:, None] < ns)
    return jnp.asarray(ov, dtype=BF16)


def _expand_map(seq):
    e = (np.arange(LANES)[None, :] == (np.arange(seq) // SLC_BLOCK)[:, None]).astype(np.float32) * -NEG
    return jnp.asarray(e, dtype=BF16)


def _nsa_prompt(qt, kc, vct, ks, vts, kw, vtw, gates, *, batch, seq, tq=128, tk=512):
    tk = min(tk, seq)
    nq = seq // tq
    nsub = seq // CMP_STRIDE
    ns = seq // SLC_BLOCK
    assert ns <= LANES and seq % tk == 0 and tk % tq == 0 and tq == LANES
    kern = functools.partial(_nsa_prompt_kernel, tq=tq, tk=tk, seq=seq, n_sel_blocks=ns)
    per_batch = lambda shp: pl.BlockSpec(shp, lambda b, i: (b,) + (0,) * (len(shp) - 1))
    const = lambda shp: pl.BlockSpec(shp, lambda b, i: (0,) * len(shp))
    return pl.pallas_call(
        kern,
        grid=(batch, nq),
        in_specs=[pl.BlockSpec((N_HEADS, KVP, tq), lambda b, i: (0, 0, b * nq + i)),
                  per_batch((None, nsub, KVP)), per_batch((None, KVP, nsub)),
                  per_batch((seq, KVP)), per_batch((seq // LANES, KVP, LANES)),
                  per_batch((seq, KVP)), per_batch((seq // LANES, KVP, LANES)),
                  pl.BlockSpec((tq, LANES), lambda b, i: (b * nq + i, 0)),
                  const((LANES, nsub)), const((seq, LANES))],
        out_specs=pl.BlockSpec((tq, NSA_W), lambda b, i: (b * nq + i, 0)),
        out_shape=jax.ShapeDtypeStruct((batch * seq, NSA_W), BF16),
        compiler_params=pltpu.CompilerParams(
            dimension_semantics=("parallel", "arbitrary"), vmem_limit_bytes=VMEM_LIMIT),
        name="nsa_prompt",
    )(qt, kc, vct, ks, vts, kw, vtw, gates, _sel_map_t(nsub, nsub - 1, ns), _expand_map(seq))


def _pool_windows(z_ext, tm):
    s2 = z_ext[1:] + z_ext[:-1]
    s4 = s2[2:] + s2[:-2]
    s8 = s4[4:] + s4[:-4]
    s16 = s8[8:] + s8[:-8]
    return (s2[HALO - 1:HALO - 1 + tm], s4[HALO - 3:HALO - 3 + tm], s8[HALO - 7:HALO - 7 + tm],
            s16[HALO - 15:HALO - 15 + tm])


def _mix_out_kernel(*refs, tm, tiles_per_seq, pool_in_kernel):
    if pool_in_kernel:
        (x_ref, nsa_ref, gu_ref, gv_ref, p_ref, halo_ref, ws_ref, gb_ref, pw_ref, ps_ref, wo_ref,
         g_ref, b_ref, o_ref) = refs
    else:
        (x_ref, nsa_ref, gu_ref, gv_ref, d_ref, ws_ref, gb_ref, pw_ref, ps_ref, wo_ref,
         g_ref, b_ref, o_ref) = refs
    lane = lax.broadcasted_iota(jnp.int32, (1, GM_W), 1)

    parts = []
    for c in range(tm // GM_CHUNK):
        v = gv_ref[c * GM_CHUNK:(c + 1) * GM_CHUNK, :]
        stacked = jnp.concatenate(
            [jnp.where(lane // HEAD_DIM == h, v, 0.0) for h in range(GM_HEADS)], axis=0).astype(BF16)
        s = _dot(ws_ref[...], stacked) + gb_ref[...]
        parts.append(gu_ref[c * GM_CHUNK:(c + 1) * GM_CHUNK, :] * s)
    o_gm = parts[0] if len(parts) == 1 else jnp.concatenate(parts, axis=0)

    if pool_in_kernel:
        first_tile = (pl.program_id(0) % tiles_per_seq) == 0
        halo = jnp.where(first_tile, 0.0, halo_ref[...])
        z = p_ref[...]
        wins = _pool_windows(jnp.concatenate([halo, z], axis=0), tm)
        pos = (pl.program_id(0) % tiles_per_seq) * tm + lax.broadcasted_iota(jnp.int32, (tm, 1), 0)
        grp = lane // POOL_GW
        wsum = jnp.where(grp == 0, wins[0], jnp.where(grp == 1, wins[1], jnp.where(grp == 2, wins[2], wins[3])))
        width = jnp.where(grp == 0, POOL_WINDOWS[0], jnp.where(grp == 1, POOL_WINDOWS[1],
                          jnp.where(grp == 2, POOL_WINDOWS[2], POOL_WINDOWS[3])))
        cnt = jnp.minimum(width, pos + 1).astype(F32)
        d = wsum / cnt - z
    else:
        d = d_ref[...]
    o_pool = _dot(d.astype(BF16), pw_ref[...]) * ps_ref[...]

    mixed = jnp.concatenate([nsa_ref[...], o_gm.astype(BF16), o_pool.astype(BF16)], axis=1)
    y = ALPHA * x_ref[...] + _dot(mixed, wo_ref[...])
    o_ref[...] = _ln_rows(y, g_ref[...], b_ref[...])


def _mix_out(x, o_nsa, gu, gv, p_or_d, ws_cat, gb_full, pw_big, ps, w_o_b, g, b, *, tm, seq, pool_in_kernel):
    rows = x.shape[0]
    tiles_per_seq = max(seq // tm, 1)
    row_spec = lambda w: pl.BlockSpec((tm, w), lambda i: (i, 0))
    const = lambda shp: pl.BlockSpec(shp, lambda i: (0,) * len(shp))
    in_specs = [row_spec(D_MODEL), row_spec(NSA_W), row_spec(GM_W), row_spec(GM_W), row_spec(POOL_W)]
    args = [x, o_nsa, gu, gv, p_or_d]
    if pool_in_kernel:
        in_specs.append(pl.BlockSpec((HALO, POOL_W), lambda i: (jnp.maximum(i * (tm // HALO) - 1, 0), 0)))
        args.append(p_or_d)
    in_specs += [const((GM_CHUNK, GM_HEADS * GM_CHUNK)), const((GM_CHUNK, GM_W)), const((POOL_W, POOL_W)),
                 const((1, POOL_W)), const((D_MODEL, D_MODEL)), const((1, D_MODEL)), const((1, D_MODEL))]
    args += [ws_cat, gb_full, pw_big, ps.reshape(1, POOL_W), w_o_b, g.reshape(1, D_MODEL), b.reshape(1, D_MODEL)]
    kern = functools.partial(_mix_out_kernel, tm=tm, tiles_per_seq=tiles_per_seq, pool_in_kernel=pool_in_kernel)
    return pl.pallas_call(
        kern,
        grid=(rows // tm,),
        in_specs=in_specs,
        out_specs=row_spec(D_MODEL),
        out_shape=jax.ShapeDtypeStruct((rows, D_MODEL), F32),
        compiler_params=pltpu.CompilerParams(dimension_semantics=("parallel",), vmem_limit_bytes=VMEM_LIMIT),
        name="mix_out_prompt" if pool_in_kernel else "mix_out_sample",
    )(*args)


def _gmlp_weights(ws, gb, chunk_rows, reps):
    wm = jnp.tril(ws[:, :chunk_rows, :chunk_rows])
    bias = gb[:, :chunk_rows]
    if reps > 1:
        eye = jnp.eye(reps, dtype=F32)
        wm = jnp.einsum('hts,ab->hatbs', wm, eye).reshape(GM_HEADS, reps * chunk_rows, reps * chunk_rows)
        bias = jnp.tile(bias, (1, reps))
    ws_cat = wm.transpose(1, 0, 2).reshape(GM_CHUNK, GM_HEADS * GM_CHUNK).astype(BF16)
    gb_full = jnp.repeat(bias.T, HEAD_DIM, axis=1)
    return ws_cat, gb_full


def _pool_weights(pw):
    eye = jnp.eye(POOL_GROUPS, dtype=F32)
    return jnp.einsum('gce,gq->gcqe', pw, eye).reshape(POOL_W, POOL_W).astype(BF16)


PAGES_PER_STEP = 32
SUBS_PER_PAGE = PAGE_SIZE // CMP_STRIDE


def _cmp_sample_kernel(pt_ref, cache_ref, q_ref, pe_ref, w1_ref, w2_ref, selmap_ref, perm_ref, oc_ref, idx_ref,
                       pbuf, sem, xr_ref, fs_ref, *, layer, n_chunks, n_seq, dec_seq, past_len, n_sel_blocks):
    b = pl.program_id(0)
    c = pl.program_id(1)
    step = b * n_chunks + c
    slot = lax.rem(step, 2)
    pps = PAGES_PER_STEP

    def page_copies(sb, sc, sl):
        return [pltpu.make_async_copy(cache_ref.at[layer, pt_ref[sb, sc * pps + p]], pbuf.at[sl, p], sem.at[sl])
                for p in range(pps)]

    @pl.when(step == 0)
    def _():
        for cp in page_copies(b, c, slot):
            cp.start()

    @pl.when(step + 1 < n_seq * n_chunks)
    def _():
        wrap = c + 1 == n_chunks
        for cp in page_copies(jnp.where(wrap, b + 1, b), jnp.where(wrap, 0, c + 1), 1 - slot):
            cp.start()

    for cp in page_copies(b, c, slot):
        cp.wait()

    def to_rows(p, carry):
        r0 = pl.multiple_of(p * SUBS_PER_PAGE, SUBS_PER_PAGE)
        for kv in range(2):
            rows = _dot_t(perm_ref[...], pbuf[slot, p, kv].astype(BF16))
            for j in range(CMP_STRIDE):
                xr_ref[kv, pl.ds(r0, SUBS_PER_PAGE), j * KVP:(j + 1) * KVP] = rows[j * SUBS_PER_PAGE:(j + 1) * SUBS_PER_PAGE]
        return carry

    lax.fori_loop(0, pps, to_rows, 0, unroll=True)

    subs = pps * SUBS_PER_PAGE
    s0 = pl.multiple_of(c * subs, subs)
    for kv in range(2):
        fs_ref[kv, pl.ds(s0, subs), :] = _dot(xr_ref[kv].astype(BF16), w1_ref[kv])

    @pl.when(c == n_chunks - 1)
    def _():
        nsub = n_chunks * subs
        kcv = []
        for kv in range(2):
            f = fs_ref[kv]
            per = _dot(pe_ref[kv].astype(BF16), w1_ref[kv])
            pe_term = per[0:1, 0:KVP] + per[1:2, KVP:2 * KVP]
            nxt = jnp.concatenate([f[1:, KVP:2 * KVP], jnp.zeros((1, KVP), F32)], axis=0)
            hid = jax.nn.gelu(f[:, 0:KVP] + nxt + pe_term)
            kcv.append(_dot(hid.astype(BF16), w2_ref[kv]).astype(BF16))
        n_kt = N_KV_HEADS * dec_seq
        rows = GROUP * n_kt
        q = q_ref[...]
        qpos = past_len + lax.rem(lax.broadcasted_iota(jnp.int32, (rows, 1), 0), dec_seq)
        cmp_end = CMP_STRIDE * lax.broadcasted_iota(jnp.int32, (1, nsub), 1) + (CMP_LEN - 1)
        p = _softmax_rows(_dot_t(q, kcv[0]), cmp_end <= qpos).astype(BF16)
        oc_ref[...] = _dot(p, kcv[1])
        imp_g = _dot(p, selmap_ref[...])
        imp = imp_g[0:n_kt]
        for g in range(1, GROUP):
            imp = imp + imp_g[g * n_kt:(g + 1) * n_kt]
        blk = lax.broadcasted_iota(jnp.int32, (1, imp.shape[1]), 1)
        cur = qpos[0:n_kt] // SLC_BLOCK
        forced = (blk == 0) | (blk == cur) | (blk == cur - 1)
        score = jnp.where(forced, FORCE_SCORE, jnp.where(blk <= cur, imp, -1.0))
        score = jnp.where(blk < n_sel_blocks, score, -jnp.inf)
        idx_ref[...] = _top_blocks_idx(score, blk)[1]


def _cmp_sample(page_table, cache_t, q_gkt, pe2, w1kv, w2kv, *, layer, dec_seq, past_len):
    n_seq, n_pages = page_table.shape
    n_chunks = n_pages // PAGES_PER_STEP
    nsub = n_pages * SUBS_PER_PAGE
    ns = (past_len + dec_seq + SLC_BLOCK - 1) // SLC_BLOCK
    ns_pad = LANES * ((ns + LANES - 1) // LANES)
    n_kt = N_KV_HEADS * dec_seq
    rows = GROUP * n_kt
    c0 = CMP_STRIDE * np.arange(nsub)[:, None]
    s0 = SLC_BLOCK * np.arange(ns_pad)[None, :]
    ov = np.clip(np.minimum(c0 + CMP_LEN, s0 + SLC_BLOCK) - np.maximum(c0, s0), 0, None) / CMP_LEN
    ov = ov * (np.arange(nsub)[:, None] < nsub - 1) * (np.arange(ns_pad)[None, :] < ns)
    selmap = jnp.asarray(ov, dtype=BF16)
    pos = np.arange(PAGE_SIZE)
    perm = jnp.asarray((pos[None, :] == (CMP_STRIDE * (pos % SUBS_PER_PAGE) + pos // SUBS_PER_PAGE)[:, None])
                       .astype(np.float32), dtype=BF16)
    width = CMP_STRIDE * KVP
    kern = functools.partial(_cmp_sample_kernel, layer=layer, n_chunks=n_chunks, n_seq=n_seq, dec_seq=dec_seq,
                             past_len=past_len, n_sel_blocks=ns)
    const = lambda shp: pl.BlockSpec(shp, lambda b, c, pt: (0,) * len(shp))
    grid_spec = pltpu.PrefetchScalarGridSpec(
        num_scalar_prefetch=1,
        grid=(n_seq, n_chunks),
        in_specs=[pl.BlockSpec(memory_space=pl.ANY),
                  pl.BlockSpec((None, rows, LANES), lambda b, c, pt: (b, 0, 0)),
                  const((2, 8, width)), const((2, width, 2 * KVP)), const((2, KVP, KVP)), const((nsub, ns_pad)),
                  const((PAGE_SIZE, PAGE_SIZE))],
        out_specs=[pl.BlockSpec((None, rows, LANES), lambda b, c, pt: (b, 0, 0)),
                   pl.BlockSpec((None, n_kt, LANES), lambda b, c, pt: (b, 0, 0))],
        scratch_shapes=[pltpu.VMEM((2, PAGES_PER_STEP, 2, KVP, PAGE_SIZE), F32),
                        pltpu.SemaphoreType.DMA((2,)),
                        pltpu.VMEM((2, PAGES_PER_STEP * SUBS_PER_PAGE, CMP_STRIDE * KVP), F32),
                        pltpu.VMEM((2, nsub, 2 * KVP), F32)])
    return pl.pallas_call(
        kern,
        grid_spec=grid_spec,
        out_shape=[jax.ShapeDtypeStruct((n_seq, rows, LANES), F32),
                   jax.ShapeDtypeStruct((n_seq, n_kt, LANES), jnp.int32)],
        compiler_params=pltpu.CompilerParams(
            dimension_semantics=("arbitrary", "arbitrary"), vmem_limit_bytes=VMEM_LIMIT),
        name="cmp_sample",
    )(page_table, cache_t, q_gkt, pe2, w1kv, w2kv, selmap, perm)


def _compress_weights_kv(pe, w1, w2):
    eye = jnp.eye(N_KV_HEADS, dtype=F32)
    w1r = w1.reshape(2, 2, CMP_STRIDE, HEAD_DIM, HEAD_DIM)
    w1kv = jnp.einsum('ksjde,hg->kjhdsge', w1r, eye).reshape(2, CMP_STRIDE * KVP, 2 * KVP)
    w2kv = jnp.einsum('ked,hg->khegd', w2, eye).reshape(2, KVP, KVP)
    per = pe.reshape(2, 2, CMP_STRIDE, HEAD_DIM)
    per = jnp.broadcast_to(per[:, :, :, None, :], (2, 2, CMP_STRIDE, N_KV_HEADS, HEAD_DIM))
    pe2 = jnp.pad(per.reshape(2, 2, CMP_STRIDE * KVP), ((0, 0), (0, 6), (0, 0)))
    return pe2, w1kv.astype(BF16), w2kv.astype(BF16)


def _slc_sample_kernel(*refs, layer, n_seq, n_pages, dec_seq, past_len, n_alias):
    (pt_ref, idx_sm_ref, cache_ref, q_ref, idxv_ref, knew_ref, win_ref, wnew_ref, oc_ref, gate_ref, expand_ref,
     wnewt_ref) = refs[:12]
    o_ref, wout_ref, kbuf, vbuf, sem = refs[12 + n_alias:]
    b = pl.program_id(0)
    slot = lax.rem(b, 2)
    n_kt = N_KV_HEADS * dec_seq
    rows = GROUP * n_kt
    n_past_blocks = past_len // SLC_BLOCK
    per_head = dec_seq * SLC_TOPK

    def tile_copies(sb, sl, k, i):
        kt = k * dec_seq + i // SLC_TOPK
        s = lax.rem(i, SLC_TOPK)
        j = idx_sm_ref[(sb * n_kt + kt) * SLC_TOPK + s]
        phys = pt_ref[sb, jnp.minimum(lax.shift_right_logical(j, 1), n_pages - 1)]
        return [pltpu.make_async_copy(cache_ref.at[layer, phys, kv, pl.ds(k * HEAD_DIM, HEAD_DIM), :],
                                      buf.at[sl, kt, s], sem.at[sl]) for kv, buf in ((0, kbuf), (1, vbuf))]

    def start_all(sb, sl):
        for k in range(N_KV_HEADS):
            def body(i, carry):
                for cp in tile_copies(sb, sl, k, i):
                    cp.start()
                return carry
            lax.fori_loop(0, per_head, body, 0)

    @pl.when(b == 0)
    def _():
        start_all(b, slot)

    @pl.when(b + 1 < n_seq)
    def _():
        start_all(b + 1, 1 - slot)

    for k in range(N_KV_HEADS):
        def wait_body(i, carry):
            for cp in tile_copies(b, slot, k, i):
                cp.wait()
            return carry
        lax.fori_loop(0, per_head, wait_body, 0)

    q = q_ref[...]
    qb = q.astype(BF16)
    gates = gate_ref[...]
    t_row = lax.rem(lax.broadcasted_iota(jnp.int32, (rows, 1), 0) // GROUP, dec_seq)
    t_new = lax.broadcasted_iota(jnp.int32, (1, dec_seq), 1)
    new_ok = t_new <= t_row

    wb = win_ref.shape[2]
    kpos = past_len - wb + lax.broadcasted_iota(jnp.int32, (1, wb), 1)
    dist = past_len + t_row - kpos
    w_ok = (dist >= 0) & (dist <= WINDOW) & (kpos >= 0)
    s_w = jnp.where(w_ok, _dot(qb, win_ref[0].astype(BF16)), NEG)
    s_n = jnp.where(new_ok, _dot_t(qb, wnew_ref[:, 0:KVP].astype(BF16)), NEG)
    m = jnp.maximum(jnp.max(s_w, axis=-1, keepdims=True), jnp.max(s_n, axis=-1, keepdims=True))
    e_w = jnp.where(w_ok, jnp.exp(s_w - m), 0.0)
    e_n = jnp.where(new_ok, jnp.exp(s_n - m), 0.0)
    den = jnp.sum(e_w, axis=-1, keepdims=True) + jnp.sum(e_n, axis=-1, keepdims=True)
    o_w = (_dot_t(e_w.astype(BF16), win_ref[1].astype(BF16))
           + _dot(e_n.astype(BF16), wnew_ref[:, KVP:2 * KVP].astype(BF16))) / den

    idxv = idxv_ref[...]
    lane16 = lax.broadcasted_iota(jnp.int32, (1, LANES), 1) < SLC_TOPK
    half = jnp.where(lane16 & ((idxv & 1) == 1), 1.0, 0.0).astype(BF16)
    live = jnp.where(lane16 & (idxv < n_past_blocks), 1.0, 0.0).astype(BF16)
    half_x = _dot(half, expand_ref[...])
    live_x = _dot(live, expand_ref[...])
    col = lax.broadcasted_iota(jnp.int32, (1, SLC_TOPK * PAGE_SIZE), 1)
    col_half = (lax.rem(col, PAGE_SIZE) // SLC_BLOCK).astype(F32)
    tile_ok = (live_x > 0.5) & (half_x == col_half)
    s_new = _dot_t(qb, knew_ref[:, 0:KVP].astype(BF16))
    zeros_half = jnp.zeros((GROUP, HEAD_DIM), F32)
    o_parts = []
    for kt in range(n_kt):
        k = kt // dec_seq
        r0 = kt * GROUP
        qk = q[r0:r0 + GROUP, k * HEAD_DIM:(k + 1) * HEAD_DIM].astype(BF16)
        kcat = jnp.concatenate([kbuf[slot, kt, s] for s in range(SLC_TOPK)], axis=1).astype(BF16)
        vcat = jnp.concatenate([vbuf[slot, kt, s] for s in range(SLC_TOPK)], axis=1).astype(BF16)
        ok = tile_ok[kt:kt + 1]
        nok = new_ok[r0:r0 + GROUP]
        s_s = jnp.where(ok, _dot(qk, kcat), NEG)
        s_n = jnp.where(nok, s_new[r0:r0 + GROUP], NEG)
        m = jnp.maximum(jnp.max(s_s, axis=-1, keepdims=True), jnp.max(s_n, axis=-1, keepdims=True))
        e_s = jnp.where(ok, jnp.exp(s_s - m), 0.0)
        e_n = jnp.where(nok, jnp.exp(s_n - m), 0.0)
        den = jnp.sum(e_s, axis=-1, keepdims=True) + jnp.sum(e_n, axis=-1, keepdims=True)
        v_new = knew_ref[:, KVP + k * HEAD_DIM:KVP + (k + 1) * HEAD_DIM].astype(BF16)
        o = (_dot_t(e_s.astype(BF16), vcat) + _dot(e_n.astype(BF16), v_new)) / den
        o_parts.append(jnp.concatenate([o, zeros_half] if k == 0 else [zeros_half, o], axis=1))
    o_s = jnp.concatenate(o_parts, axis=0)

    o_ref[...] = gates[:, 0:1] * oc_ref[...] + gates[:, 1:2] * o_s + gates[:, 2:3] * o_w

    lane_w = lax.broadcasted_iota(jnp.int32, (1, wb), 1)
    for kv in range(2):
        moved = pltpu.roll(win_ref[kv], wb - dec_seq, 1)
        for t in range(dec_seq):
            moved = jnp.where(lane_w == wb - dec_seq + t, wnewt_ref[kv * KVP:(kv + 1) * KVP, t:t + 1], moved)
        wout_ref[kv] = moved


def _slc_sample(page_table, idx_flat, cache_t, q_ktg, idxv, kvs_new, win_t, kvw_new, o_c, gates_r, win_out,
                *, layer, dec_seq, past_len):
    n_seq, n_pages = page_table.shape
    n_kt = N_KV_HEADS * dec_seq
    rows = GROUP * n_kt
    wb = win_t.shape[-1]
    cols = SLC_TOPK * PAGE_SIZE
    expand = jnp.asarray((np.arange(LANES)[:, None] == (np.arange(cols) // PAGE_SIZE)[None, :]).astype(np.float32),
                         dtype=BF16)
    n_alias = 0 if win_out is None else 1
    kern = functools.partial(_slc_sample_kernel, layer=layer, n_seq=n_seq, n_pages=n_pages, dec_seq=dec_seq,
                             past_len=past_len, n_alias=n_alias)
    per_seq = lambda r, w: pl.BlockSpec((None, r, w), lambda b, pt, ix: (b, 0, 0))
    state_spec = pl.BlockSpec((None, None, 2, KVP, wb), lambda b, pt, ix: (layer, b, 0, 0, 0))
    in_specs = [pl.BlockSpec(memory_space=pl.ANY),
                per_seq(rows, LANES), per_seq(n_kt, LANES), per_seq(dec_seq, KV_W), state_spec,
                per_seq(dec_seq, KV_W), per_seq(rows, LANES), per_seq(rows, LANES),
                pl.BlockSpec((LANES, cols), lambda b, pt, ix: (0, 0)), per_seq(KV_W, dec_seq)]
    args = [page_table, idx_flat, cache_t, q_ktg, idxv, kvs_new, win_t, kvw_new, o_c, gates_r, expand,
            jnp.swapaxes(kvw_new, 1, 2)]
    if n_alias:
        in_specs.append(pl.BlockSpec(memory_space=pl.ANY))
        args.append(win_out)
    grid_spec = pltpu.PrefetchScalarGridSpec(
        num_scalar_prefetch=2,
        grid=(n_seq,),
        in_specs=in_specs,
        out_specs=[per_seq(rows, LANES), state_spec],
        scratch_shapes=[pltpu.VMEM((2, n_kt, SLC_TOPK, HEAD_DIM, PAGE_SIZE), F32),
                        pltpu.VMEM((2, n_kt, SLC_TOPK, HEAD_DIM, PAGE_SIZE), F32),
                        pltpu.SemaphoreType.DMA((2,))])
    return pl.pallas_call(
        kern,
        grid_spec=grid_spec,
        out_shape=[jax.ShapeDtypeStruct((n_seq, rows, LANES), F32), jax.ShapeDtypeStruct(win_t.shape, F32)],
        input_output_aliases={len(args) - 1: 1} if n_alias else {},
        compiler_params=pltpu.CompilerParams(dimension_semantics=("arbitrary",), vmem_limit_bytes=VMEM_LIMIT),
        name="slc_sample",
    )(*args)


def _pool_sample_kernel(z_ref, d_ref, *, dec_seq, pos0):
    lane = lax.broadcasted_iota(jnp.int32, (1, POOL_W), 1)
    grp = lane // POOL_GW
    for t in range(dec_seq):
        cur = z_ref[:, POOL_HIST + t, :]
        acc = cur
        sums = {}
        for back in range(1, max(POOL_WINDOWS)):
            acc = acc + z_ref[:, POOL_HIST + t - back, :]
            if back + 1 in POOL_WINDOWS:
                sums[back + 1] = acc
        d = None
        for g, w in enumerate(POOL_WINDOWS):
            val = sums[w] / float(min(w, pos0 + t + 1)) - cur
            d = val if d is None else jnp.where(grp == g, val, d)
        d_ref[:, t, :] = d


def _pool_sample(z_ext, *, dec_seq, pos0):
    n_seq = z_ext.shape[0]
    return pl.pallas_call(
        functools.partial(_pool_sample_kernel, dec_seq=dec_seq, pos0=pos0),
        out_shape=jax.ShapeDtypeStruct((n_seq, dec_seq, POOL_W), F32),
        name="pool_sample",
    )(z_ext)


def _sample_layer(x, lw, cmp_t, slc_t, win_t, win_out, pool_state, page_table, *, layer, n_seq, dec_seq):
    rows = n_seq * dec_seq
    n_kt = N_KV_HEADS * dec_seq
    x = _ffn_ln(x, lw['ffn_in'], lw['ffn_out'], lw['ln_g'][0], lw['ln_b'][0], sel=(lw['layer'], 0),tm=rows)
    cos, sin = _rope_tables(PAST_LEN + jnp.arange(rows) % dec_seq)
    qt, kvc, kvs, kvw, gates, gu, gv, p = _inproj(
        x, lw['w_ext'], cos, sin, lw['gm_ln_g'], lw['gm_ln_b'], tm=rows)

    qf = jnp.swapaxes(qt.astype(F32), 1, 2).reshape(N_KV_HEADS, GROUP, n_seq, dec_seq, LANES)
    q_gkt = qf.transpose(2, 1, 0, 3, 4).reshape(n_seq, GROUP * n_kt, LANES).astype(BF16)
    q_ktg = qf.transpose(2, 0, 3, 1, 4).reshape(n_seq, GROUP * n_kt, LANES)
    o_c, idxv = _cmp_sample(page_table, cmp_t, q_gkt, lw['pe2'], lw['w1kv'], lw['w2kv'],
                            layer=layer, dec_seq=dec_seq, past_len=PAST_LEN)
    o_c = o_c.reshape(n_seq, GROUP, N_KV_HEADS, dec_seq, LANES).transpose(0, 2, 3, 1, 4).reshape(n_seq, GROUP * n_kt, LANES)
    gates_r = gates[:, :GATE_W].reshape(n_seq, dec_seq, N_KV_HEADS, GROUP, 3).transpose(0, 2, 1, 3, 4)
    gates_r = jnp.pad(gates_r.reshape(n_seq, GROUP * n_kt, 3), ((0, 0), (0, 0), (0, LANES - 3)))
    idx_flat = idxv[:, :, :SLC_TOPK].reshape(-1)
    o, win_out = _slc_sample(page_table, idx_flat, slc_t, q_ktg, idxv, kvs.reshape(n_seq, dec_seq, KV_W), win_t,
                             kvw.reshape(n_seq, dec_seq, KV_W), o_c, gates_r, win_out,
                             layer=layer, dec_seq=dec_seq, past_len=PAST_LEN)
    o = o.reshape(n_seq, N_KV_HEADS, dec_seq, GROUP, LANES)
    o_nsa = jnp.stack([o[:, k, :, :, k * HEAD_DIM:(k + 1) * HEAD_DIM] for k in range(N_KV_HEADS)], axis=2)
    o_nsa = o_nsa.reshape(rows, NSA_W).astype(BF16)

    z_ext = jnp.concatenate([pool_state, p.reshape(n_seq, dec_seq, POOL_W)], axis=1)
    d = _pool_sample(z_ext, dec_seq=dec_seq, pos0=PAST_LEN).reshape(rows, POOL_W)
    ws_cat, gb_full = _gmlp_weights(lw['gm_ws'], lw['gm_b'], dec_seq, GM_CHUNK // dec_seq)
    x = _mix_out(x, o_nsa, gu, gv, d, ws_cat, gb_full, lw['pw_big'], lw['pool_scale'], lw['w_o'],
                 lw['ln_g'][1], lw['ln_b'][1], tm=rows, seq=dec_seq, pool_in_kernel=False)
    x = _ffn_ln(x, lw['ffn_in'], lw['ffn_out'], lw['ln_g'][2], lw['ln_b'][2], sel=(lw['layer'], 1),tm=rows)
    shp = (n_seq, dec_seq, 2, N_KV_HEADS, HEAD_DIM)
    new = (kvc.reshape(shp), kvs.reshape(shp), z_ext[:, dec_seq:], gv.reshape(n_seq, dec_seq, GM_W))
    return x, new, win_out


def _pages_by_channel(cache):
    nd = cache.ndim
    t = jnp.transpose(cache, tuple(range(nd - 4)) + (nd - 3, nd - 2, nd - 1, nd - 4))
    return t.reshape(t.shape[:-3] + (KVP, t.shape[-1]))
def _prompt_layer(x, lw, leaf_bufs, *, layer, batch, seq, tm, tm_ffn):
    x = _ffn_ln(x, lw['ffn_in'], lw['ffn_out'], lw['ln_g'][0], lw['ln_b'][0], sel=(lw['layer'], 0),tm=tm_ffn)
    cos, sin = _rope_tables(jnp.arange(seq))
    qt, kvc, leaf_c, leaf_s, leaf_w, ks, vts, kw, vtw, gates, gu, gv, p = _inproj(
        x, lw['w_ext'], cos, sin, lw['gm_ln_g'], lw['gm_ln_b'], tm=tm, leaf_bufs=leaf_bufs, layer=layer, batch=batch)
    kc, vct = _compress_prompt(kvc, lw['pe2'], lw['w1kv'], lw['w2kv'], batch=batch)
    o_nsa = _nsa_prompt(qt, kc, vct, ks, vts, kw, vtw, gates, batch=batch, seq=seq)
    ws_cat, gb_full = _gmlp_weights(lw['gm_ws'], lw['gm_b'], GM_CHUNK, 1)
    x = _mix_out(x, o_nsa, gu, gv, p, ws_cat, gb_full, lw['pw_big'], lw['pool_scale'], lw['w_o'],
                 lw['ln_g'][1], lw['ln_b'][1], tm=tm, seq=seq, pool_in_kernel=True)
    x = _ffn_ln(x, lw['ffn_in'], lw['ffn_out'], lw['ln_g'][2], lw['ln_b'][2], sel=(lw['layer'], 1),tm=tm_ffn)
    return x, (leaf_c, leaf_s, leaf_w), p.reshape(batch, seq, POOL_W)[:, seq - POOL_HIST:]


def _leaf_rows(buf):
    d, b, _, t = buf.shape
    return buf.reshape(d, b, 2, N_KV_HEADS, HEAD_DIM, t).transpose(0, 1, 5, 2, 3, 4)


def _layer_weights(l, ffn_in_b, ffn_out_b, ln_g, ln_b, w_in, w_o, cmp_pe, cmp_w1, cmp_w2,
                   gm_ln_g, gm_ln_b, gm_ws, gm_b, pool_w, pool_scale):
    pe2, w1kv, w2kv = _compress_weights_kv(cmp_pe[l], cmp_w1[l], cmp_w2[l])
    return dict(layer=l, ffn_in=ffn_in_b, ffn_out=ffn_out_b, ln_g=ln_g[l], ln_b=ln_b[l],
                w_ext=_build_w_ext(w_in[l]), w_o=w_o[l].astype(BF16),
                pe2=pe2, w1kv=w1kv, w2kv=w2kv,
                gm_ln_g=gm_ln_g[l], gm_ln_b=gm_ln_b[l], gm_ws=gm_ws[l], gm_b=gm_b[l],
                pw_big=_pool_weights(pool_w[l]), pool_scale=pool_scale[l])


def kernel(x_prompt, x_sample, cache_kv_cmp, cache_kv_slc, state_kv_win, state_pool, page_table, ln_g, ln_b, ffn_w_in, ffn_w_out, w_in, w_o, cmp_pe, cmp_w1, cmp_w2, gm_ln_g, gm_ln_b, gm_ws, gm_b, pool_w, pool_scale):
    batch, seq, _ = x_prompt.shape
    fi = ffn_w_in.astype(BF16)
    fo = ffn_w_out.astype(BF16)
    n_seq, dec_seq, _ = x_sample.shape
    xp = x_prompt.reshape(batch * seq, D_MODEL)
    xs = x_sample.reshape(n_seq * dec_seq, D_MODEL)
    cmp_t = _pages_by_channel(cache_kv_cmp)
    slc_t = _pages_by_channel(cache_kv_slc)
    win_t = _pages_by_channel(state_kv_win)
    leaf_bufs = tuple(jnp.zeros((DEPTH, batch, KV_W, seq), F32) for _ in range(3))
    win_out = jnp.zeros(win_t.shape, F32)
    pool_p, new_s = [], []
    for l in range(DEPTH):
        lw = _layer_weights(l, fi, fo, ln_g, ln_b, w_in, w_o, cmp_pe, cmp_w1, cmp_w2, gm_ln_g, gm_ln_b, gm_ws, gm_b, pool_w, pool_scale)
        xp, leaf_bufs, pool_l = _prompt_layer(xp, lw, leaf_bufs, layer=l, batch=batch, seq=seq, tm=512, tm_ffn=1024)
        xs, st_s, win_out = _sample_layer(xs, lw, cmp_t, slc_t, win_t, win_out, state_pool[l], page_table,
                                          layer=l, n_seq=n_seq, dec_seq=dec_seq)
        pool_p.append(pool_l)
        new_s.append(st_s)
    stk = lambda lst, i: jnp.stack([t[i] for t in lst])
    wb = min(WINDOW, seq)
    win_s = _leaf_rows(win_out.reshape(DEPTH, n_seq, KV_W, win_out.shape[-1]))
    return (xp.reshape(batch, seq, D_MODEL), xs.reshape(n_seq, dec_seq, D_MODEL),
            _leaf_rows(leaf_bufs[0]), stk(new_s, 0), _leaf_rows(leaf_bufs[1]), stk(new_s, 1),
            _leaf_rows(leaf_bufs[2])[:, :, seq - wb:], win_s, jnp.stack(pool_p), stk(new_s, 2), stk(new_s, 3))
```

```python
import functools

import numpy as np
import jax
import jax.numpy as jnp
from jax import lax
from jax.experimental import pallas as pl
from jax.experimental.pallas import tpu as pltpu

F32 = jnp.float32
BF16 = jnp.bfloat16

D_MODEL = 1024
DEPTH = 2
PAST_LEN = 16384
PAGE_SIZE = 128
HEAD_DIM = 64
NSA_W = D_MODEL // 2
GM_W = D_MODEL // 4
POOL_W = D_MODEL // 4
N_HEADS = NSA_W // HEAD_DIM
N_KV_HEADS = 2
GROUP = N_HEADS // N_KV_HEADS
CMP_STRIDE = 16
CMP_LEN = 2 * CMP_STRIDE
SLC_BLOCK = 64
SLC_TOPK = 16
WINDOW = 512
FORCE_SCORE = 1.0e4
ROPE_THETA = 10000.0
SCALE = HEAD_DIM ** -0.5
GM_HEADS = GM_W // HEAD_DIM
GM_CHUNK = 128
POOL_GROUPS = 4
POOL_GW = POOL_W // POOL_GROUPS
POOL_WINDOWS = (2, 4, 8, 16)
POOL_HIST = max(POOL_WINDOWS) - 1
D_FF = 256 * ((8 * D_MODEL // 3 + 255) // 256)
ALPHA = (2 * DEPTH) ** 0.25
LN_EPS = 1e-5
Q_W = N_HEADS * HEAD_DIM
KV_W = 2 * N_KV_HEADS * HEAD_DIM
GATE_W = 3 * N_HEADS
N_IN = Q_W + 3 * KV_W + GATE_W + 2 * GM_W + POOL_W

LANES = 128
KVP = N_KV_HEADS * HEAD_DIM
VMEM_LIMIT = 56 * 1024 * 1024
NEG = -1e30
LOG2E = 1.4426950408889634
HALO = 16

_OFF_Q = 0
_OFF_QR = Q_W
_OFF_KV = 2 * Q_W
_OFF_GATE = _OFF_KV + 3 * 3 * KVP
_OFF_UV = _OFF_GATE + LANES
_OFF_P = _OFF_UV + 2 * GM_W
N_EXT = _OFF_P + POOL_W


def _ln_rows(y, g, b):
    mu = jnp.mean(y, axis=-1, keepdims=True)
    d = y - mu
    var = jnp.mean(d * d, axis=-1, keepdims=True)
    return d * lax.rsqrt(var + LN_EPS) * g + b


def _dot(a, b):
    return jnp.dot(a, b, preferred_element_type=F32)


def _dot_t(a, b):
    return lax.dot_general(a, b, (((1,), (1,)), ((), ())), preferred_element_type=F32)


def _ffn_kernel(x_ref, wg_ref, wu_ref, wo_ref, g_ref, b_ref, o_ref, xb_ref, *, n_chunks):
    j = pl.program_id(1)

    @pl.when(j == 0)
    def _():
        xb_ref[...] = x_ref[...].astype(BF16)

    xb = xb_ref[...]
    gate = _dot(xb, wg_ref[...])
    up = _dot(xb, wu_ref[...])
    hid = (gate * jax.nn.sigmoid(gate)) * up
    part = _dot(hid.astype(BF16), wo_ref[...])

    @pl.when(j == 0)
    def _():
        o_ref[...] = part

    if n_chunks > 2:
        @pl.when((j > 0) & (j < n_chunks - 1))
        def _():
            o_ref[...] += part

    @pl.when(j == n_chunks - 1)
    def _():
        y = ALPHA * x_ref[...] + 0.5 * (o_ref[...] + part)
        o_ref[...] = _ln_rows(y, g_ref[...], b_ref[...])


def _ffn_ln(x, w_in_b, w_out_b, g, b, *, tm, sel):
    rows = x.shape[0]
    n_chunks = 2
    fc = D_FF // n_chunks
    l, w = sel
    return pl.pallas_call(
        functools.partial(_ffn_kernel, n_chunks=n_chunks),
        grid=(rows // tm, n_chunks),
        in_specs=[
            pl.BlockSpec((tm, D_MODEL), lambda i, j: (i, 0)),
            pl.BlockSpec((None, None, D_MODEL, fc), lambda i, j: (l, w, 0, j)),
            pl.BlockSpec((None, None, D_MODEL, fc), lambda i, j: (l, w, 0, n_chunks + j)),
            pl.BlockSpec((None, None, fc, D_MODEL), lambda i, j: (l, w, j, 0)),
            pl.BlockSpec((1, D_MODEL), lambda i, j: (0, 0)),
            pl.BlockSpec((1, D_MODEL), lambda i, j: (0, 0)),
        ],
        out_specs=pl.BlockSpec((tm, D_MODEL), lambda i, j: (i, 0)),
        out_shape=jax.ShapeDtypeStruct((rows, D_MODEL), F32),
        scratch_shapes=[pltpu.VMEM((tm, D_MODEL), BF16)],
        compiler_params=pltpu.CompilerParams(
            dimension_semantics=("parallel", "arbitrary"), vmem_limit_bytes=VMEM_LIMIT),
        name="ffn_ln",
    )(x, w_in_b, w_in_b, w_out_b, g.reshape(1, D_MODEL), b.reshape(1, D_MODEL))


def _inproj_kernel(*refs, prompt, n_alias, q_scale):
    h_ref, w_ref, cos_ref, sin_ref, gmg_ref, gmb_ref = refs[:6]
    outs = refs[6 + n_alias:]
    if prompt:
        (qt_ref, kvcb_ref, leafc_ref, leafs_ref, leafw_ref, ks_ref, vts_ref, kw_ref, vtw_ref,
         gate_ref, gu_ref, gv_ref, p_ref) = outs
        leaves = (leafc_ref, leafs_ref, leafw_ref)
        k_refs = (None, ks_ref, kw_ref)
        vt_refs = (None, vts_ref, vtw_ref)
    else:
        qt_ref, kvc_ref, kvs_ref, kvw_ref, gate_ref, gu_ref, gv_ref, p_ref = outs
        rows_out = (kvc_ref, kvs_ref, kvw_ref)
    hb = h_ref[...].astype(BF16)
    cos = cos_ref[...]
    sin = sin_ref[...]

    zq = _dot(hb, w_ref[:, _OFF_Q:_OFF_KV])
    zk = _dot(hb, w_ref[:, _OFF_KV:_OFF_UV])
    zu = _dot(hb, w_ref[:, _OFF_UV:N_EXT])

    n_sq = hb.shape[0] // LANES
    zeros_half = jnp.zeros((HEAD_DIM, LANES), F32)

    for m in range(N_HEADS // 2):
        c0 = m * LANES
        pair = (zq[:, c0:c0 + LANES] * cos + zq[:, Q_W + c0:Q_W + c0 + LANES] * sin) * q_scale
        kvh = (2 * m) // GROUP
        for c in range(n_sq):
            pt = pair[c * LANES:(c + 1) * LANES].T
            for e in range(2):
                piece = pt[e * HEAD_DIM:(e + 1) * HEAD_DIM]
                both = [piece, zeros_half] if kvh == 0 else [zeros_half, piece]
                qt_ref[2 * m + e, :, c * LANES:(c + 1) * LANES] = jnp.concatenate(both, axis=0).astype(BF16)

    for br in range(3):
        c0 = br * 3 * KVP
        k = zk[:, c0:c0 + KVP] * cos + zk[:, c0 + KVP:c0 + 2 * KVP] * sin
        v = zk[:, c0 + 2 * KVP:c0 + 3 * KVP]
        if not prompt:
            rows_out[br][:, 0:KVP] = k
            rows_out[br][:, KVP:2 * KVP] = v
            continue
        for c in range(n_sq):
            cols = slice(c * LANES, (c + 1) * LANES)
            vt = v[cols].T
            leaves[br][0:KVP, cols] = k[cols].T
            leaves[br][KVP:2 * KVP, cols] = vt
            if br > 0:
                vt_refs[br][c] = vt.astype(BF16)
        if br == 0:
            kvcb_ref[0] = k
            kvcb_ref[1] = v
        else:
            k_refs[br][...] = k.astype(BF16)

    gate_ref[...] = jax.nn.sigmoid(zk[:, _OFF_GATE - _OFF_KV:_OFF_UV - _OFF_KV])
    gu_ref[...] = jax.nn.gelu(zu[:, 0:GM_W])
    gv_ref[...] = _ln_rows(jax.nn.gelu(zu[:, GM_W:2 * GM_W]), gmg_ref[...], gmb_ref[...])
    p_ref[...] = zu[:, 2 * GM_W:2 * GM_W + POOL_W]


def _inproj(h, w_ext, cos, sin, gmg, gmb, *, tm, leaf_bufs=None, layer=0, batch=None):
    rows = h.shape[0]
    n_tab = cos.shape[0] // tm
    prompt = leaf_bufs is not None
    row_spec = lambda w: pl.BlockSpec((tm, w), lambda i: (i, 0))
    tab_spec = pl.BlockSpec((tm, LANES), lambda i: (i % n_tab, 0))
    vec_spec = pl.BlockSpec((1, GM_W), lambda i: (0, 0))
    sds = jax.ShapeDtypeStruct
    in_specs = [row_spec(D_MODEL), pl.BlockSpec((D_MODEL, N_EXT), lambda i: (0, 0)),
                tab_spec, tab_spec, vec_spec, vec_spec]
    args = [h, w_ext, cos, sin, gmg.reshape(1, GM_W), gmb.reshape(1, GM_W)]
    tail_specs = [row_spec(LANES), row_spec(GM_W), row_spec(GM_W), row_spec(POOL_W)]
    tail_shapes = [sds((rows, LANES), F32), sds((rows, GM_W), F32), sds((rows, GM_W), F32), sds((rows, POOL_W), F32)]
    qt_spec = pl.BlockSpec((N_HEADS, LANES, tm), lambda i: (0, 0, i))
    qt_shape = sds((N_HEADS, LANES, rows), BF16)
    aliases = {}
    if prompt:
        seq = rows // batch
        tiles = seq // tm
        leaf_spec = pl.BlockSpec((None, None, KV_W, tm), lambda i: (layer, i // tiles, 0, i % tiles))
        leaf_shape = sds((DEPTH, batch, KV_W, seq), F32)
        sq_spec = pl.BlockSpec((tm // LANES, KVP, LANES), lambda i: (i, 0, 0))
        sq_shape = sds((rows // LANES, KVP, LANES), BF16)
        out_specs = [qt_spec, pl.BlockSpec((2, tm, KVP), lambda i: (0, i, 0)), leaf_spec, leaf_spec, leaf_spec,
                     row_spec(KVP), sq_spec, row_spec(KVP), sq_spec] + tail_specs
        out_shape = [qt_shape, sds((2, rows, KVP), F32), leaf_shape, leaf_shape, leaf_shape,
                     sds((rows, KVP), BF16), sq_shape, sds((rows, KVP), BF16), sq_shape] + tail_shapes
        for n, buf in enumerate(leaf_bufs):
            in_specs.append(pl.BlockSpec(memory_space=pl.ANY))
            args.append(buf)
            aliases[6 + n] = 2 + n
    else:
        out_specs = [qt_spec, row_spec(KV_W), row_spec(KV_W), row_spec(KV_W)] + tail_specs
        out_shape = [qt_shape, sds((rows, KV_W), F32), sds((rows, KV_W), F32), sds((rows, KV_W), F32)] + tail_shapes
    return pl.pallas_call(
        functools.partial(_inproj_kernel, prompt=prompt, n_alias=len(aliases),
                          q_scale=SCALE * LOG2E if prompt else SCALE),
        grid=(rows // tm,),
        in_specs=in_specs,
        out_specs=out_specs,
        out_shape=out_shape,
        input_output_aliases=aliases,
        compiler_params=pltpu.CompilerParams(dimension_semantics=("parallel",), vmem_limit_bytes=VMEM_LIMIT),
        name="inproj",
    )(*args)


def _build_w_ext(w_in):
    half = HEAD_DIM // 2

    def rot(w):
        n = w.shape[1] // HEAD_DIM
        w3 = w.reshape(D_MODEL, n, 2, half)
        return jnp.stack([-w3[:, :, 1], w3[:, :, 0]], axis=2).reshape(D_MODEL, n * HEAD_DIM)

    q = w_in[:, :Q_W]
    cols = [q, rot(q)]
    for br in range(3):
        kv = w_in[:, Q_W + br * KV_W:Q_W + (br + 1) * KV_W]
        k, v = kv[:, :KVP], kv[:, KVP:]
        cols += [k, rot(k), v]
    g0 = Q_W + 3 * KV_W
    cols.append(jnp.pad(w_in[:, g0:g0 + GATE_W], ((0, 0), (0, LANES - GATE_W))))
    cols.append(w_in[:, g0 + GATE_W:])
    return jnp.concatenate(cols, axis=1).astype(BF16)


def _rope_tables(pos):
    half = HEAD_DIM // 2
    inv = ROPE_THETA ** (-jnp.arange(half, dtype=F32) / half)
    ang = pos.astype(F32)[:, None] * inv[None, :]
    cos = jnp.tile(jnp.cos(ang), (1, LANES // half))
    sin = jnp.tile(jnp.sin(ang), (1, LANES // half))
    return cos, sin


def _compress_kernel(x_ref, pe_ref, w1_ref, w2_ref, kc_ref, vct_ref):
    nsub = kc_ref.shape[0]
    outs = []
    for kv in range(2):
        xr = jnp.concatenate(
            [x_ref[kv, pl.ds(j, nsub, stride=CMP_STRIDE), :].astype(BF16) for j in range(CMP_STRIDE)], axis=1)
        f = _dot(xr, w1_ref[kv])
        per = _dot(pe_ref[kv].astype(BF16), w1_ref[kv])
        pe_term = per[0:1, 0:KVP] + per[1:2, KVP:2 * KVP]
        nxt = jnp.concatenate([f[1:, KVP:2 * KVP], jnp.zeros((1, KVP), F32)], axis=0)
        hid = jax.nn.gelu(f[:, 0:KVP] + nxt + pe_term)
        outs.append(_dot(hid.astype(BF16), w2_ref[kv]))
    kc_ref[...] = outs[0].astype(BF16)
    for c in range(nsub // LANES):
        vct_ref[:, c * LANES:(c + 1) * LANES] = outs[1][c * LANES:(c + 1) * LANES].T.astype(BF16)


def _compress_prompt(kv_rows, pe2, w1kv, w2kv, *, batch):
    seq = kv_rows.shape[1] // batch
    nsub = seq // CMP_STRIDE
    width = CMP_STRIDE * KVP
    return pl.pallas_call(
        _compress_kernel,
        grid=(batch,),
        in_specs=[pl.BlockSpec((2, seq, KVP), lambda b: (0, b, 0)),
                  pl.BlockSpec((2, 8, width), lambda b: (0, 0, 0)),
                  pl.BlockSpec((2, width, 2 * KVP), lambda b: (0, 0, 0)),
                  pl.BlockSpec((2, KVP, KVP), lambda b: (0, 0, 0))],
        out_specs=[pl.BlockSpec((None, nsub, KVP), lambda b: (b, 0, 0)),
                   pl.BlockSpec((None, KVP, nsub), lambda b: (b, 0, 0))],
        out_shape=[jax.ShapeDtypeStruct((batch, nsub, KVP), BF16), jax.ShapeDtypeStruct((batch, KVP, nsub), BF16)],
        compiler_params=pltpu.CompilerParams(dimension_semantics=("parallel",), vmem_limit_bytes=VMEM_LIMIT),
        name="compress_prompt",
    )(kv_rows, pe2, w1kv, w2kv)


def _top_blocks_cols(score, blk):
    sel = jnp.zeros(score.shape, F32)
    for _ in range(SLC_TOPK):
        m = jnp.max(score, axis=0, keepdims=True)
        first = jnp.min(jnp.where(score == m, blk, 1e9), axis=0, keepdims=True)
        hit = blk == first
        sel = jnp.where(hit, 1.0, sel)
        score = jnp.where(hit, -jnp.inf, score)
    return sel


def _top_blocks_idx(score, blk):
    sel = jnp.zeros(score.shape, F32)
    idx = jnp.zeros((score.shape[0], LANES), jnp.int32)
    lane = lax.broadcasted_iota(jnp.int32, (1, LANES), 1)
    big = jnp.int32(1 << 20)
    for it in range(SLC_TOPK):
        m = jnp.max(score, axis=-1, keepdims=True)
        first = jnp.min(jnp.where(score == m, blk, big), axis=-1, keepdims=True)
        hit = blk == first
        sel = jnp.where(hit, 1.0, sel)
        idx = jnp.where(lane == it, first, idx)
        score = jnp.where(hit, -jnp.inf, score)
    return sel, idx


def _softmax_rows(s, mask):
    s = jnp.where(mask, s, NEG)
    m = jnp.max(s, axis=-1, keepdims=True)
    e = jnp.exp(s - m)
    return jnp.where(mask, e / jnp.sum(e, axis=-1, keepdims=True), 0.0)


def _nsa_prompt_kernel(qt_ref, kc_ref, vct_ref, ks_ref, vts_ref, kw_ref, vtw_ref, gate_ref, selmap_ref, expand_ref,
                       o_ref, *, tq, tk, seq, n_sel_blocks):
    t0 = pl.program_id(1) * tq
    nsub = kc_ref.shape[0]
    nb = 8 * ((n_sel_blocks + 7) // 8)
    span = min(WINDOW + tq, seq)
    qpos = t0 + lax.broadcasted_iota(jnp.int32, (1, tq), 1)
    rep = lambda x, n: jnp.concatenate([x] * n, axis=1)
    qt = jnp.concatenate([qt_ref[h] for h in range(N_HEADS)], axis=1)

    cmp_end = CMP_STRIDE * lax.broadcasted_iota(jnp.int32, (nsub, 1), 0) + (CMP_LEN - 1)
    c_bias = jnp.where(cmp_end <= qpos, 0.0, NEG)
    s = _dot(kc_ref[...], qt) + rep(c_bias, N_HEADS)
    e = jnp.exp2(s - jnp.max(s, axis=0, keepdims=True))
    inv_c = jnp.where(rep(qpos >= CMP_LEN - 1, N_HEADS), 1.0 / jnp.sum(e, axis=0, keepdims=True), 0.0)
    eb = e.astype(BF16)
    o_c = _dot(vct_ref[...], eb) * inv_c
    imp_h = _dot(selmap_ref[...], eb)[0:nb] * inv_c

    imp = []
    for k in range(N_KV_HEADS):
        acc = None
        for g in range(GROUP):
            part = imp_h[:, (GROUP * k + g) * tq:(GROUP * k + g + 1) * tq]
            acc = part if acc is None else acc + part
        imp.append(acc)
    imp = jnp.concatenate(imp, axis=1)
    blk = lax.broadcasted_iota(jnp.int32, (nb, 1), 0)
    cur = rep(qpos // SLC_BLOCK, N_KV_HEADS)
    forced = (blk == 0) | (blk == cur) | (blk == cur - 1)
    score = jnp.where(forced, FORCE_SCORE, jnp.where(blk <= cur, imp, -1.0))
    if nb > n_sel_blocks:
        score = jnp.where(blk < n_sel_blocks, score, -jnp.inf)
    sel = _top_blocks_cols(score, blk.astype(F32))
    sel_m1 = jnp.concatenate([sel - 1.0, jnp.zeros((LANES - nb, N_KV_HEADS * tq), F32)], axis=0).astype(BF16)

    start = pl.multiple_of(jnp.maximum(t0 + tq - span, 0), tq)
    dist = qpos - (start + lax.broadcasted_iota(jnp.int32, (span, 1), 0))
    w_bias = jnp.where(dist >= 0, jnp.where(dist <= WINDOW, 0.0, NEG), NEG)
    s = _dot(kw_ref[pl.ds(start, span), :], qt) + rep(w_bias, N_HEADS)
    e = jnp.exp2(s - jnp.max(s, axis=0, keepdims=True))
    inv_w = 1.0 / jnp.sum(e, axis=0, keepdims=True)
    c0 = start // LANES
    vt = jnp.concatenate([vtw_ref[c0 + c] for c in range(span // LANES)], axis=1)
    o_w = _dot(vt, e.astype(BF16)) * inv_w

    def tile(kt, carry, diagonal, nk=tk):
        m_i, l_i, acc = carry
        r0 = pl.multiple_of(kt * tk, tk)
        bias = _dot(expand_ref[pl.ds(r0, nk), :], sel_m1)
        if diagonal:
            kpos = r0 + lax.broadcasted_iota(jnp.int32, (nk, 1), 0)
            bias = jnp.where(kpos <= rep(qpos, N_KV_HEADS), bias, NEG)
        s = _dot(ks_ref[pl.ds(r0, nk), :], qt)
        s = jnp.concatenate([s[:, h * tq:(h + 1) * tq] + bias[:, (h // GROUP) * tq:(h // GROUP + 1) * tq]
                             for h in range(N_HEADS)], axis=1)
        m_new = jnp.maximum(m_i, jnp.max(s, axis=0, keepdims=True))
        a = jnp.exp2(m_i - m_new)
        e = jnp.exp2(s - m_new)
        l_new = a * l_i + jnp.sum(e, axis=0, keepdims=True)
        c0 = kt * (tk // LANES)
        vt = jnp.concatenate([vts_ref[c0 + c] for c in range(nk // LANES)], axis=1)
        return m_new, l_new, a * acc + _dot(vt, e.astype(BF16))

    n_kt = (t0 + tq + tk - 1) // tk
    init = (jnp.full((1, N_HEADS * tq), NEG, F32), jnp.zeros((1, N_HEADS * tq), F32),
            jnp.zeros((KVP, N_HEADS * tq), F32))
    carry = lax.fori_loop(0, n_kt - 1, lambda kt, c: tile(kt, c, False), init)
    diag = [functools.partial(tile, n_kt - 1, diagonal=True, nk=(v + 1) * tq) for v in range(tk // tq)]
    _, l_s, acc_s = lax.switch(lax.rem(pl.program_id(1), tk // tq), diag, carry)
    o_s = acc_s * (1.0 / l_s)

    gt = gate_ref[...].T
    parts = []
    for h in range(N_HEADS):
        rows = slice((h // GROUP) * HEAD_DIM, (h // GROUP + 1) * HEAD_DIM)
        cols = slice(h * tq, (h + 1) * tq)
        parts.append(gt[3 * h:3 * h + 1] * o_c[rows, cols] + gt[3 * h + 1:3 * h + 2] * o_s[rows, cols]
                     + gt[3 * h + 2:3 * h + 3] * o_w[rows, cols])
    ot = jnp.concatenate(parts, axis=0)
    for m in range(NSA_W // LANES):
        o_ref[:, m * LANES:(m + 1) * LANES] = ot[m * LANES:(m + 1) * LANES].T.astype(o_ref.dtype)


def _sel_map_t(nc_rows, n_cmp, ns):
    c0 = CMP_STRIDE * np.arange(nc_rows)[None, :]
    s0 = SLC_BLOCK * np.arange(LANES)[:, None]
    ov = np.clip(np.minimum(c0 + CMP_LEN, s0 + SLC_BLOCK) - np.maximum(c0, s0), 0, None) / CMP_LEN
    ov = ov * (np.arange(nc_rows)[None, :] < n_cmp) * (np.arange(LANES)[:, None] < ns)
    return jnp.asarray(ov, dtype=BF16)


def _expand_map(seq):
    e = (np.arange(LANES)[None, :] == (np.arange(seq) // SLC_BLOCK)[:, None]).astype(np.float32) * -NEG
    return jnp.asarray(e, dtype=BF16)


def _nsa_prompt(qt, kc, vct, ks, vts, kw, vtw, gates, *, batch, seq, tq=128, tk=512):
    tk = min(tk, seq)
    nq = seq // tq
    nsub = seq // CMP_STRIDE
    ns = seq // SLC_BLOCK
    assert ns <= LANES and seq % tk == 0 and tk % tq == 0 and tq == LANES
    kern = functools.partial(_nsa_prompt_kernel, tq=tq, tk=tk, seq=seq, n_sel_blocks=ns)
    per_batch = lambda shp: pl.BlockSpec(shp, lambda b, i: (b,) + (0,) * (len(shp) - 1))
    const = lambda shp: pl.BlockSpec(shp, lambda b, i: (0,) * len(shp))
    return pl.pallas_call(
        kern,
        grid=(batch, nq),
        in_specs=[pl.BlockSpec((N_HEADS, KVP, tq), lambda b, i: (0, 0, b * nq + i)),
                  per_batch((None, nsub, KVP)), per_batch((None, KVP, nsub)),
                  per_batch((seq, KVP)), per_batch((seq // LANES, KVP, LANES)),
                  per_batch((seq, KVP)), per_batch((seq // LANES, KVP, LANES)),
                  pl.BlockSpec((tq, LANES), lambda b, i: (b * nq + i, 0)),
                  const((LANES, nsub)), const((seq, LANES))],
        out_specs=pl.BlockSpec((tq, NSA_W), lambda b, i: (b * nq + i, 0)),
        out_shape=jax.ShapeDtypeStruct((batch * seq, NSA_W), BF16),
        compiler_params=pltpu.CompilerParams(
            dimension_semantics=("parallel", "arbitrary"), vmem_limit_bytes=VMEM_LIMIT),
        name="nsa_prompt",
    )(qt, kc, vct, ks, vts, kw, vtw, gates, _sel_map_t(nsub, nsub - 1, ns), _expand_map(seq))


def _pool_windows(z_ext, tm):
    s2 = z_ext[1:] + z_ext[:-1]
    s4 = s2[2:] + s2[:-2]
    s8 = s4[4:] + s4[:-4]
    s16 = s8[8:] + s8[:-8]
    return (s2[HALO - 1:HALO - 1 + tm], s4[HALO - 3:HALO - 3 + tm], s8[HALO - 7:HALO - 7 + tm],
            s16[HALO - 15:HALO - 15 + tm])


def _mix_out_kernel(*refs, tm, tiles_per_seq, pool_in_kernel):
    if pool_in_kernel:
        (x_ref, nsa_ref, gu_ref, gv_ref, p_ref, halo_ref, ws_ref, gb_ref, pw_ref, ps_ref, wo_ref,
         g_ref, b_ref, o_ref) = refs
    else:
        (x_ref, nsa_ref, gu_ref, gv_ref, d_ref, ws_ref, gb_ref, pw_ref, ps_ref, wo_ref,
         g_ref, b_ref, o_ref) = refs
    lane = lax.broadcasted_iota(jnp.int32, (1, GM_W), 1)

    parts = []
    for c in range(tm // GM_CHUNK):
        v = gv_ref[c * GM_CHUNK:(c + 1) * GM_CHUNK, :]
        stacked = jnp.concatenate(
            [jnp.where(lane // HEAD_DIM == h, v, 0.0) for h in range(GM_HEADS)], axis=0).astype(BF16)
        s = _dot(ws_ref[...], stacked) + gb_ref[...]
        parts.append(gu_ref[c * GM_CHUNK:(c + 1) * GM_CHUNK, :] * s)
    o_gm = parts[0] if len(parts) == 1 else jnp.concatenate(parts, axis=0)

    if pool_in_kernel:
        first_tile = (pl.program_id(0) % tiles_per_seq) == 0
        halo = jnp.where(first_tile, 0.0, halo_ref[...])
        z = p_ref[...]
        wins = _pool_windows(jnp.concatenate([halo, z], axis=0), tm)
        pos = (pl.program_id(0) % tiles_per_seq) * tm + lax.broadcasted_iota(jnp.int32, (tm, 1), 0)
        grp = lane // POOL_GW
        wsum = jnp.where(grp == 0, wins[0], jnp.where(grp == 1, wins[1], jnp.where(grp == 2, wins[2], wins[3])))
        width = jnp.where(grp == 0, POOL_WINDOWS[0], jnp.where(grp == 1, POOL_WINDOWS[1],
                          jnp.where(grp == 2, POOL_WINDOWS[2], POOL_WINDOWS[3])))
        cnt = jnp.minimum(width, pos + 1).astype(F32)
        d = wsum / cnt - z
    else:
        d = d_ref[...]
    o_pool = _dot(d.astype(BF16), pw_ref[...]) * ps_ref[...]

    mixed = jnp.concatenate([nsa_ref[...], o_gm.astype(BF16), o_pool.astype(BF16)], axis=1)
    y = ALPHA * x_ref[...] + _dot(mixed, wo_ref[...])
    o_ref[...] = _ln_rows(y, g_ref[...], b_ref[...])


def _mix_out(x, o_nsa, gu, gv, p_or_d, ws_cat, gb_full, pw_big, ps, w_o_b, g, b, *, tm, seq, pool_in_kernel):
    rows = x.shape[0]
    tiles_per_seq = max(seq // tm, 1)
    row_spec = lambda w: pl.BlockSpec((tm, w), lambda i: (i, 0))
    const = lambda shp: pl.BlockSpec(shp, lambda i: (0,) * len(shp))
    in_specs = [row_spec(D_MODEL), row_spec(NSA_W), row_spec(GM_W), row_spec(GM_W), row_spec(POOL_W)]
    args = [x, o_nsa, gu, gv, p_or_d]
    if pool_in_kernel:
        in_specs.append(pl.BlockSpec((HALO, POOL_W), lambda i: (jnp.maximum(i * (tm // HALO) - 1, 0), 0)))
        args.append(p_or_d)
    in_specs += [const((GM_CHUNK, GM_HEADS * GM_CHUNK)), const((GM_CHUNK, GM_W)), const((POOL_W, POOL_W)),
                 const((1, POOL_W)), const((D_MODEL, D_MODEL)), const((1, D_MODEL)), const((1, D_MODEL))]
    args += [ws_cat, gb_full, pw_big, ps.reshape(1, POOL_W), w_o_b, g.reshape(1, D_MODEL), b.reshape(1, D_MODEL)]
    kern = functools.partial(_mix_out_kernel, tm=tm, tiles_per_seq=tiles_per_seq, pool_in_kernel=pool_in_kernel)
    return pl.pallas_call(
        kern,
        grid=(rows // tm,),
        in_specs=in_specs,
        out_specs=row_spec(D_MODEL),
        out_shape=jax.ShapeDtypeStruct((rows, D_MODEL), F32),
        compiler_params=pltpu.CompilerParams(dimension_semantics=("parallel",), vmem_limit_bytes=VMEM_LIMIT),
        name="mix_out_prompt" if pool_in_kernel else "mix_out_sample",
    )(*args)


def _gmlp_weights(ws, gb, chunk_rows, reps):
    wm = jnp.tril(ws[:, :chunk_rows, :chunk_rows])
    bias = gb[:, :chunk_rows]
    if reps > 1:
        eye = jnp.eye(reps, dtype=F32)
        wm = jnp.einsum('hts,ab->hatbs', wm, eye).reshape(GM_HEADS, reps * chunk_rows, reps * chunk_rows)
        bias = jnp.tile(bias, (1, reps))
    ws_cat = wm.transpose(1, 0, 2).reshape(GM_CHUNK, GM_HEADS * GM_CHUNK).astype(BF16)
    gb_full = jnp.repeat(bias.T, HEAD_DIM, axis=1)
    return ws_cat, gb_full


def _pool_weights(pw):
    eye = jnp.eye(POOL_GROUPS, dtype=F32)
    return jnp.einsum('gce,gq->gcqe', pw, eye).reshape(POOL_W, POOL_W).astype(BF16)


PAGES_PER_STEP = 32
SUBS_PER_PAGE = PAGE_SIZE // CMP_STRIDE


def _cmp_sample_kernel(pt_ref, cache_ref, q_ref, pe_ref, w1_ref, w2_ref, selmap_ref, perm_ref, oc_ref, idx_ref,
                       pbuf, sem, xr_ref, fs_ref, *, layer, n_chunks, n_seq, dec_seq, past_len, n_sel_blocks):
    b = pl.program_id(0)
    c = pl.program_id(1)
    step = b * n_chunks + c
    slot = lax.rem(step, 2)
    pps = PAGES_PER_STEP

    def page_copies(sb, sc, sl):
        return [pltpu.make_async_copy(cache_ref.at[layer, pt_ref[sb, sc * pps + p]], pbuf.at[sl, p], sem.at[sl])
                for p in range(pps)]

    @pl.when(step == 0)
    def _():
        for cp in page_copies(b, c, slot):
            cp.start()

    @pl.when(step + 1 < n_seq * n_chunks)
    def _():
        wrap = c + 1 == n_chunks
        for cp in page_copies(jnp.where(wrap, b + 1, b), jnp.where(wrap, 0, c + 1), 1 - slot):
            cp.start()

    for cp in page_copies(b, c, slot):
        cp.wait()

    def to_rows(p, carry):
        r0 = pl.multiple_of(p * SUBS_PER_PAGE, SUBS_PER_PAGE)
        for kv in range(2):
            rows = _dot_t(perm_ref[...], pbuf[slot, p, kv].astype(BF16))
            for j in range(CMP_STRIDE):
                xr_ref[kv, pl.ds(r0, SUBS_PER_PAGE), j * KVP:(j + 1) * KVP] = rows[j * SUBS_PER_PAGE:(j + 1) * SUBS_PER_PAGE]
        return carry

    lax.fori_loop(0, pps, to_rows, 0, unroll=True)

    subs = pps * SUBS_PER_PAGE
    s0 = pl.multiple_of(c * subs, subs)
    for kv in range(2):
        fs_ref[kv, pl.ds(s0, subs), :] = _dot(xr_ref[kv].astype(BF16), w1_ref[kv])

    @pl.when(c == n_chunks - 1)
    def _():
        nsub = n_chunks * subs
        kcv = []
        for kv in range(2):
            f = fs_ref[kv]
            per = _dot(pe_ref[kv].astype(BF16), w1_ref[kv])
            pe_term = per[0:1, 0:KVP] + per[1:2, KVP:2 * KVP]
            nxt = jnp.concatenate([f[1:, KVP:2 * KVP], jnp.zeros((1, KVP), F32)], axis=0)
            hid = jax.nn.gelu(f[:, 0:KVP] + nxt + pe_term)
            kcv.append(_dot(hid.astype(BF16), w2_ref[kv]).astype(BF16))
        n_kt = N_KV_HEADS * dec_seq
        rows = GROUP * n_kt
        q = q_ref[...]
        qpos = past_len + lax.rem(lax.broadcasted_iota(jnp.int32, (rows, 1), 0), dec_seq)
        cmp_end = CMP_STRIDE * lax.broadcasted_iota(jnp.int32, (1, nsub), 1) + (CMP_LEN - 1)
        p = _softmax_rows(_dot_t(q, kcv[0]), cmp_end <= qpos).astype(BF16)
        oc_ref[...] = _dot(p, kcv[1])
        imp_g = _dot(p, selmap_ref[...])
        imp = imp_g[0:n_kt]
        for g in range(1, GROUP):
            imp = imp + imp_g[g * n_kt:(g + 1) * n_kt]
        blk = lax.broadcasted_iota(jnp.int32, (1, imp.shape[1]), 1)
        cur = qpos[0:n_kt] // SLC_BLOCK
        forced = (blk == 0) | (blk == cur) | (blk == cur - 1)
        score = jnp.where(forced, FORCE_SCORE, jnp.where(blk <= cur, imp, -1.0))
        score = jnp.where(blk < n_sel_blocks, score, -jnp.inf)
        idx_ref[...] = _top_blocks_idx(score, blk)[1]


def _cmp_sample(page_table, cache_t, q_gkt, pe2, w1kv, w2kv, *, layer, dec_seq, past_len):
    n_seq, n_pages = page_table.shape
    n_chunks = n_pages // PAGES_PER_STEP
    nsub = n_pages * SUBS_PER_PAGE
    ns = (past_len + dec_seq + SLC_BLOCK - 1) // SLC_BLOCK
    ns_pad = LANES * ((ns + LANES - 1) // LANES)
    n_kt = N_KV_HEADS * dec_seq
    rows = GROUP * n_kt
    c0 = CMP_STRIDE * np.arange(nsub)[:, None]
    s0 = SLC_BLOCK * np.arange(ns_pad)[None, :]
    ov = np.clip(np.minimum(c0 + CMP_LEN, s0 + SLC_BLOCK) - np.maximum(c0, s0), 0, None) / CMP_LEN
    ov = ov * (np.arange(nsub)[:, None] < nsub - 1) * (np.arange(ns_pad)[None, :] < ns)
    selmap = jnp.asarray(ov, dtype=BF16)
    pos = np.arange(PAGE_SIZE)
    perm = jnp.asarray((pos[None, :] == (CMP_STRIDE * (pos % SUBS_PER_PAGE) + pos // SUBS_PER_PAGE)[:, None])
                       .astype(np.float32), dtype=BF16)
    width = CMP_STRIDE * KVP
    kern = functools.partial(_cmp_sample_kernel, layer=layer, n_chunks=n_chunks, n_seq=n_seq, dec_seq=dec_seq,
                             past_len=past_len, n_sel_blocks=ns)
    const = lambda shp: pl.BlockSpec(shp, lambda b, c, pt: (0,) * len(shp))
    grid_spec = pltpu.PrefetchScalarGridSpec(
        num_scalar_prefetch=1,
        grid=(n_seq, n_chunks),
        in_specs=[pl.BlockSpec(memory_space=pl.ANY),
                  pl.BlockSpec((None, rows, LANES), lambda b, c, pt: (b, 0, 0)),
                  const((2, 8, width)), const((2, width, 2 * KVP)), const((2, KVP, KVP)), const((nsub, ns_pad)),
                  const((PAGE_SIZE, PAGE_SIZE))],
        out_specs=[pl.BlockSpec((None, rows, LANES), lambda b, c, pt: (b, 0, 0)),
                   pl.BlockSpec((None, n_kt, LANES), lambda b, c, pt: (b, 0, 0))],
        scratch_shapes=[pltpu.VMEM((2, PAGES_PER_STEP, 2, KVP, PAGE_SIZE), F32),
                        pltpu.SemaphoreType.DMA((2,)),
                        pltpu.VMEM((2, PAGES_PER_STEP * SUBS_PER_PAGE, CMP_STRIDE * KVP), F32),
                        pltpu.VMEM((2, nsub, 2 * KVP), F32)])
    return pl.pallas_call(
        kern,
        grid_spec=grid_spec,
        out_shape=[jax.ShapeDtypeStruct((n_seq, rows, LANES), F32),
                   jax.ShapeDtypeStruct((n_seq, n_kt, LANES), jnp.int32)],
        compiler_params=pltpu.CompilerParams(
            dimension_semantics=("arbitrary", "arbitrary"), vmem_limit_bytes=VMEM_LIMIT),
        name="cmp_sample",
    )(page_table, cache_t, q_gkt, pe2, w1kv, w2kv, selmap, perm)


def _compress_weights_kv(pe, w1, w2):
    eye = jnp.eye(N_KV_HEADS, dtype=F32)
    w1r = w1.reshape(2, 2, CMP_STRIDE, HEAD_DIM, HEAD_DIM)
    w1kv = jnp.einsum('ksjde,hg->kjhdsge', w1r, eye).reshape(2, CMP_STRIDE * KVP, 2 * KVP)
    w2kv = jnp.einsum('ked,hg->khegd', w2, eye).reshape(2, KVP, KVP)
    per = pe.reshape(2, 2, CMP_STRIDE, HEAD_DIM)
    per = jnp.broadcast_to(per[:, :, :, None, :], (2, 2, CMP_STRIDE, N_KV_HEADS, HEAD_DIM))
    pe2 = jnp.pad(per.reshape(2, 2, CMP_STRIDE * KVP), ((0, 0), (0, 6), (0, 0)))
    return pe2, w1kv.astype(BF16), w2kv.astype(BF16)


def _slc_sample_kernel(*refs, layer, n_seq, n_pages, dec_seq, past_len, n_alias):
    (pt_ref, idx_sm_ref, cache_ref, q_ref, idxv_ref, knew_ref, win_ref, wnew_ref, oc_ref, gate_ref, expand_ref,
     wnewt_ref) = refs[:12]
    o_ref, wout_ref, kvbuf, sem = refs[12 + n_alias:]
    b = pl.program_id(0)
    slot = lax.rem(b, 2)
    n_kt = N_KV_HEADS * dec_seq
    rows = GROUP * n_kt
    n_past_blocks = past_len // SLC_BLOCK
    per_head = dec_seq * SLC_TOPK

    def tile_copies(sb, sl, k, i):
        kt = k * dec_seq + i // SLC_TOPK
        s = lax.rem(i, SLC_TOPK)
        j = idx_sm_ref[(sb * n_kt + kt) * SLC_TOPK + s]
        phys = pt_ref[sb, jnp.minimum(lax.shift_right_logical(j, 1), n_pages - 1)]
        return [pltpu.make_async_copy(cache_ref.at[layer, phys, :, pl.ds(k * HEAD_DIM, HEAD_DIM), :],
                                      kvbuf.at[sl, kt, s], sem.at[sl])]

    def start_all(sb, sl):
        for k in range(N_KV_HEADS):
            def body(i, carry):
                for cp in tile_copies(sb, sl, k, i):
                    cp.start()
                return carry
            lax.fori_loop(0, per_head, body, 0, unroll=4)

    @pl.when(b == 0)
    def _():
        start_all(b, slot)

    @pl.when(b + 1 < n_seq)
    def _():
        start_all(b + 1, 1 - slot)

    for k in range(N_KV_HEADS):
        def wait_body(i, carry):
            for cp in tile_copies(b, slot, k, i):
                cp.wait()
            return carry
        lax.fori_loop(0, per_head, wait_body, 0, unroll=4)

    q = q_ref[...]
    qb = q.astype(BF16)
    gates = gate_ref[...]
    t_row = lax.rem(lax.broadcasted_iota(jnp.int32, (rows, 1), 0) // GROUP, dec_seq)
    t_new = lax.broadcasted_iota(jnp.int32, (1, dec_seq), 1)
    new_ok = t_new <= t_row

    wb = win_ref.shape[2]
    kpos = past_len - wb + lax.broadcasted_iota(jnp.int32, (1, wb), 1)
    dist = past_len + t_row - kpos
    w_ok = (dist >= 0) & (dist <= WINDOW) & (kpos >= 0)
    s_w = jnp.where(w_ok, _dot(qb, win_ref[0].astype(BF16)), NEG)
    s_n = jnp.where(new_ok, _dot_t(qb, wnew_ref[:, 0:KVP].astype(BF16)), NEG)
    m = jnp.maximum(jnp.max(s_w, axis=-1, keepdims=True), jnp.max(s_n, axis=-1, keepdims=True))
    e_w = jnp.where(w_ok, jnp.exp(s_w - m), 0.0)
    e_n = jnp.where(new_ok, jnp.exp(s_n - m), 0.0)
    den = jnp.sum(e_w, axis=-1, keepdims=True) + jnp.sum(e_n, axis=-1, keepdims=True)
    o_w = (_dot_t(e_w.astype(BF16), win_ref[1].astype(BF16))
           + _dot(e_n.astype(BF16), wnew_ref[:, KVP:2 * KVP].astype(BF16))) / den

    idxv = idxv_ref[...]
    lane16 = lax.broadcasted_iota(jnp.int32, (1, LANES), 1) < SLC_TOPK
    half = jnp.where(lane16 & ((idxv & 1) == 1), 1.0, 0.0).astype(BF16)
    live = jnp.where(lane16 & (idxv < n_past_blocks), 1.0, 0.0).astype(BF16)
    half_x = _dot(half, expand_ref[...])
    live_x = _dot(live, expand_ref[...])
    col = lax.broadcasted_iota(jnp.int32, (1, SLC_TOPK * PAGE_SIZE), 1)
    col_half = (lax.rem(col, PAGE_SIZE) // SLC_BLOCK).astype(F32)
    tile_ok = (live_x > 0.5) & (half_x == col_half)
    s_new = _dot_t(qb, knew_ref[:, 0:KVP].astype(BF16))
    zeros_half = jnp.zeros((GROUP, HEAD_DIM), F32)
    o_parts = []
    for kt in range(n_kt):
        k = kt // dec_seq
        r0 = kt * GROUP
        qk = q[r0:r0 + GROUP, k * HEAD_DIM:(k + 1) * HEAD_DIM].astype(BF16)
        kcat = jnp.concatenate([kvbuf[slot, kt, s, 0] for s in range(SLC_TOPK)], axis=1).astype(BF16)
        vcat = jnp.concatenate([kvbuf[slot, kt, s, 1] for s in range(SLC_TOPK)], axis=1).astype(BF16)
        ok = tile_ok[kt:kt + 1]
        nok = new_ok[r0:r0 + GROUP]
        s_s = jnp.where(ok, _dot(qk, kcat), NEG)
        s_n = jnp.where(nok, s_new[r0:r0 + GROUP], NEG)
        m = jnp.maximum(jnp.max(s_s, axis=-1, keepdims=True), jnp.max(s_n, axis=-1, keepdims=True))
        e_s = jnp.where(ok, jnp.exp(s_s - m), 0.0)
        e_n = jnp.where(nok, jnp.exp(s_n - m), 0.0)
        den = jnp.sum(e_s, axis=-1, keepdims=True) + jnp.sum(e_n, axis=-1, keepdims=True)
        v_new = knew_ref[:, KVP + k * HEAD_DIM:KVP + (k + 1) * HEAD_DIM].astype(BF16)
        o = (_dot_t(e_s.astype(BF16), vcat) + _dot(e_n.astype(BF16), v_new)) / den
        o_parts.append(jnp.concatenate([o, zeros_half] if k == 0 else [zeros_half, o], axis=1))
    o_s = jnp.concatenate(o_parts, axis=0)

    o_ref[...] = gates[:, 0:1] * oc_ref[...] + gates[:, 1:2] * o_s + gates[:, 2:3] * o_w

    lane_w = lax.broadcasted_iota(jnp.int32, (1, wb), 1)
    for kv in range(2):
        moved = pltpu.roll(win_ref[kv], wb - dec_seq, 1)
        for t in range(dec_seq):
            moved = jnp.where(lane_w == wb - dec_seq + t, wnewt_ref[kv * KVP:(kv + 1) * KVP, t:t + 1], moved)
        wout_ref[kv] = moved


def _slc_sample(page_table, idx_flat, cache_t, q_ktg, idxv, kvs_new, win_t, kvw_new, o_c, gates_r, win_out,
                *, layer, dec_seq, past_len):
    n_seq, n_pages = page_table.shape
    n_kt = N_KV_HEADS * dec_seq
    rows = GROUP * n_kt
    wb = win_t.shape[-1]
    cols = SLC_TOPK * PAGE_SIZE
    expand = jnp.asarray((np.arange(LANES)[:, None] == (np.arange(cols) // PAGE_SIZE)[None, :]).astype(np.float32),
                         dtype=BF16)
    n_alias = 0 if win_out is None else 1
    kern = functools.partial(_slc_sample_kernel, layer=layer, n_seq=n_seq, n_pages=n_pages, dec_seq=dec_seq,
                             past_len=past_len, n_alias=n_alias)
    per_seq = lambda r, w: pl.BlockSpec((None, r, w), lambda b, pt, ix: (b, 0, 0))
    state_spec = pl.BlockSpec((None, None, 2, KVP, wb), lambda b, pt, ix: (layer, b, 0, 0, 0))
    in_specs = [pl.BlockSpec(memory_space=pl.ANY),
                per_seq(rows, LANES), per_seq(n_kt, LANES), per_seq(dec_seq, KV_W), state_spec,
                per_seq(dec_seq, KV_W), per_seq(rows, LANES), per_seq(rows, LANES),
                pl.BlockSpec((LANES, cols), lambda b, pt, ix: (0, 0)), per_seq(KV_W, dec_seq)]
    args = [page_table, idx_flat, cache_t, q_ktg, idxv, kvs_new, win_t, kvw_new, o_c, gates_r, expand,
            jnp.swapaxes(kvw_new, 1, 2)]
    if n_alias:
        in_specs.append(pl.BlockSpec(memory_space=pl.ANY))
        args.append(win_out)
    grid_spec = pltpu.PrefetchScalarGridSpec(
        num_scalar_prefetch=2,
        grid=(n_seq,),
        in_specs=in_specs,
        out_specs=[per_seq(rows, LANES), state_spec],
        scratch_shapes=[pltpu.VMEM((2, n_kt, SLC_TOPK, 2, HEAD_DIM, PAGE_SIZE), F32),
                        pltpu.SemaphoreType.DMA((2,))])
    return pl.pallas_call(
        kern,
        grid_spec=grid_spec,
        out_shape=[jax.ShapeDtypeStruct((n_seq, rows, LANES), F32), jax.ShapeDtypeStruct(win_t.shape, F32)],
        input_output_aliases={len(args) - 1: 1} if n_alias else {},
        compiler_params=pltpu.CompilerParams(dimension_semantics=("arbitrary",), vmem_limit_bytes=VMEM_LIMIT),
        name="slc_sample",
    )(*args)


def _pool_sample_kernel(z_ref, d_ref, *, dec_seq, pos0):
    lane = lax.broadcasted_iota(jnp.int32, (1, POOL_W), 1)
    grp = lane // POOL_GW
    for t in range(dec_seq):
        cur = z_ref[:, POOL_HIST + t, :]
        acc = cur
        sums = {}
        for back in range(1, max(POOL_WINDOWS)):
            acc = acc + z_ref[:, POOL_HIST + t - back, :]
            if back + 1 in POOL_WINDOWS:
                sums[back + 1] = acc
        d = None
        for g, w in enumerate(POOL_WINDOWS):
            val = sums[w] / float(min(w, pos0 + t + 1)) - cur
            d = val if d is None else jnp.where(grp == g, val, d)
        d_ref[:, t, :] = d


def _pool_sample(z_ext, *, dec_seq, pos0):
    n_seq = z_ext.shape[0]
    return pl.pallas_call(
        functools.partial(_pool_sample_kernel, dec_seq=dec_seq, pos0=pos0),
        out_shape=jax.ShapeDtypeStruct((n_seq, dec_seq, POOL_W), F32),
        name="pool_sample",
    )(z_ext)


def _sample_layer(x, lw, cmp_t, slc_t, win_t, win_out, pool_state, page_table, *, layer, n_seq, dec_seq):
    rows = n_seq * dec_seq
    n_kt = N_KV_HEADS * dec_seq
    x = _ffn_ln(x, lw['ffn_in'], lw['ffn_out'], lw['ln_g'][0], lw['ln_b'][0], sel=(lw['layer'], 0),tm=rows)
    cos, sin = _rope_tables(PAST_LEN + jnp.arange(rows) % dec_seq)
    qt, kvc, kvs, kvw, gates, gu, gv, p = _inproj(
        x, lw['w_ext'], cos, sin, lw['gm_ln_g'], lw['gm_ln_b'], tm=rows)

    qf = jnp.swapaxes(qt.astype(F32), 1, 2).reshape(N_KV_HEADS, GROUP, n_seq, dec_seq, LANES)
    q_gkt = qf.transpose(2, 1, 0, 3, 4).reshape(n_seq, GROUP * n_kt, LANES).astype(BF16)
    q_ktg = qf.transpose(2, 0, 3, 1, 4).reshape(n_seq, GROUP * n_kt, LANES)
    o_c, idxv = _cmp_sample(page_table, cmp_t, q_gkt, lw['pe2'], lw['w1kv'], lw['w2kv'],
                            layer=layer, dec_seq=dec_seq, past_len=PAST_LEN)
    o_c = o_c.reshape(n_seq, GROUP, N_KV_HEADS, dec_seq, LANES).transpose(0, 2, 3, 1, 4).reshape(n_seq, GROUP * n_kt, LANES)
    gates_r = gates[:, :GATE_W].reshape(n_seq, dec_seq, N_KV_HEADS, GROUP, 3).transpose(0, 2, 1, 3, 4)
    gates_r = jnp.pad(gates_r.reshape(n_seq, GROUP * n_kt, 3), ((0, 0), (0, 0), (0, LANES - 3)))
    idx_flat = idxv[:, :, :SLC_TOPK].reshape(-1)
    o, win_out = _slc_sample(page_table, idx_flat, slc_t, q_ktg, idxv, kvs.reshape(n_seq, dec_seq, KV_W), win_t,
                             kvw.reshape(n_seq, dec_seq, KV_W), o_c, gates_r, win_out,
                             layer=layer, dec_seq=dec_seq, past_len=PAST_LEN)
    o = o.reshape(n_seq, N_KV_HEADS, dec_seq, GROUP, LANES)
    o_nsa = jnp.stack([o[:, k, :, :, k * HEAD_DIM:(k + 1) * HEAD_DIM] for k in range(N_KV_HEADS)], axis=2)
    o_nsa = o_nsa.reshape(rows, NSA_W).astype(BF16)

    z_ext = jnp.concatenate([pool_state, p.reshape(n_seq, dec_seq, POOL_W)], axis=1)
    d = _pool_sample(z_ext, dec_seq=dec_seq, pos0=PAST_LEN).reshape(rows, POOL_W)
    ws_cat, gb_full = _gmlp_weights(lw['gm_ws'], lw['gm_b'], dec_seq, GM_CHUNK // dec_seq)
    x = _mix_out(x, o_nsa, gu, gv, d, ws_cat, gb_full, lw['pw_big'], lw['pool_scale'], lw['w_o'],
                 lw['ln_g'][1], lw['ln_b'][1], tm=rows, seq=dec_seq, pool_in_kernel=False)
    x = _ffn_ln(x, lw['ffn_in'], lw['ffn_out'], lw['ln_g'][2], lw['ln_b'][2], sel=(lw['layer'], 1),tm=rows)
    shp = (n_seq, dec_seq, 2, N_KV_HEADS, HEAD_DIM)
    new = (kvc.reshape(shp), kvs.reshape(shp), z_ext[:, dec_seq:], gv.reshape(n_seq, dec_seq, GM_W))
    return x, new, win_out


def _pages_by_channel(cache):
    nd = cache.ndim
    t = jnp.transpose(cache, tuple(range(nd - 4)) + (nd - 3, nd - 2, nd - 1, nd - 4))
    return t.reshape(t.shape[:-3] + (KVP, t.shape[-1]))
def _prompt_layer(x, lw, leaf_bufs, *, layer, batch, seq, tm, tm_ffn):
    x = _ffn_ln(x, lw['ffn_in'], lw['ffn_out'], lw['ln_g'][0], lw['ln_b'][0], sel=(lw['layer'], 0),tm=tm_ffn)
    cos, sin = _rope_tables(jnp.arange(seq))
    qt, kvc, leaf_c, leaf_s, leaf_w, ks, vts, kw, vtw, gates, gu, gv, p = _inproj(
        x, lw['w_ext'], cos, sin, lw['gm_ln_g'], lw['gm_ln_b'], tm=tm, leaf_bufs=leaf_bufs, layer=layer, batch=batch)
    kc, vct = _compress_prompt(kvc, lw['pe2'], lw['w1kv'], lw['w2kv'], batch=batch)
    o_nsa = _nsa_prompt(qt, kc, vct, ks, vts, kw, vtw, gates, batch=batch, seq=seq)
    ws_cat, gb_full = _gmlp_weights(lw['gm_ws'], lw['gm_b'], GM_CHUNK, 1)
    x = _mix_out(x, o_nsa, gu, gv, p, ws_cat, gb_full, lw['pw_big'], lw['pool_scale'], lw['w_o'],
                 lw['ln_g'][1], lw['ln_b'][1], tm=tm, seq=seq, pool_in_kernel=True)
    x = _ffn_ln(x, lw['ffn_in'], lw['ffn_out'], lw['ln_g'][2], lw['ln_b'][2], sel=(lw['layer'], 1),tm=tm_ffn)
    return x, (leaf_c, leaf_s, leaf_w), p.reshape(batch, seq, POOL_W)[:, seq - POOL_HIST:]


def _leaf_rows(buf):
    d, b, _, t = buf.shape
    return buf.reshape(d, b, 2, N_KV_HEADS, HEAD_DIM, t).transpose(0, 1, 5, 2, 3, 4)


def _layer_weights(l, ffn_in_b, ffn_out_b, ln_g, ln_b, w_in, w_o, cmp_pe, cmp_w1, cmp_w2,
                   gm_ln_g, gm_ln_b, gm_ws, gm_b, pool_w, pool_scale):
    pe2, w1kv, w2kv = _compress_weights_kv(cmp_pe[l], cmp_w1[l], cmp_w2[l])
    return dict(layer=l, ffn_in=ffn_in_b, ffn_out=ffn_out_b, ln_g=ln_g[l], ln_b=ln_b[l],
                w_ext=_build_w_ext(w_in[l]), w_o=w_o[l].astype(BF16),
                pe2=pe2, w1kv=w1kv, w2kv=w2kv,
                gm_ln_g=gm_ln_g[l], gm_ln_b=gm_ln_b[l], gm_ws=gm_ws[l], gm_b=gm_b[l],
                pw_big=_pool_weights(pool_w[l]), pool_scale=pool_scale[l])


def kernel(x_prompt, x_sample, cache_kv_cmp, cache_kv_slc, state_kv_win, state_pool, page_table, ln_g, ln_b, ffn_w_in, ffn_w_out, w_in, w_o, cmp_pe, cmp_w1, cmp_w2, gm_ln_g, gm_ln_b, gm_ws, gm_b, pool_w, pool_scale):
    batch, seq, _ = x_prompt.shape
    fi = ffn_w_in.astype(BF16)
    fo = ffn_w_out.astype(BF16)
    n_seq, dec_seq, _ = x_sample.shape
    xp = x_prompt.reshape(batch * seq, D_MODEL)
    xs = x_sample.reshape(n_seq * dec_seq, D_MODEL)
    cmp_t = _pages_by_channel(cache_kv_cmp)
    slc_t = _pages_by_channel(cache_kv_slc)
    win_t = _pages_by_channel(state_kv_win)
    leaf_bufs = tuple(jnp.zeros((DEPTH, batch, KV_W, seq), F32) for _ in range(3))
    win_out = jnp.zeros(win_t.shape, F32)
    pool_p, new_s = [], []
    for l in range(DEPTH):
        lw = _layer_weights(l, fi, fo, ln_g, ln_b, w_in, w_o, cmp_pe, cmp_w1, cmp_w2, gm_ln_g, gm_ln_b, gm_ws, gm_b, pool_w, pool_scale)
        xp, leaf_bufs, pool_l = _prompt_layer(xp, lw, leaf_bufs, layer=l, batch=batch, seq=seq, tm=512, tm_ffn=1024)
        xs, st_s, win_out = _sample_layer(xs, lw, cmp_t, slc_t, win_t, win_out, state_pool[l], page_table,
                                          layer=l, n_seq=n_seq, dec_seq=dec_seq)
        pool_p.append(pool_l)
        new_s.append(st_s)
    stk = lambda lst, i: jnp.stack([t[i] for t in lst])
    wb = min(WINDOW, seq)
    win_s = _leaf_rows(win_out.reshape(DEPTH, n_seq, KV_W, win_out.shape[-1]))
    return (xp.reshape(batch, seq, D_MODEL), xs.reshape(n_seq, dec_seq, D_MODEL),
            _leaf_rows(leaf_bufs[0]), stk(new_s, 0), _leaf_rows(leaf_bufs[1]), stk(new_s, 1),
            _leaf_rows(leaf_bufs[2])[:, :, seq - wb:], win_s, jnp.stack(pool_p), stk(new_s, 2), stk(new_s, 3))
```

```python
import functools

import numpy as np
import jax
import jax.numpy as jnp
from jax import lax
from jax.experimental import pallas as pl
from jax.experimental.pallas import tpu as pltpu

F32 = jnp.float32
BF16 = jnp.bfloat16

D_MODEL = 1024
DEPTH = 2
PAST_LEN = 16384
PAGE_SIZE = 128
HEAD_DIM = 64
NSA_W = D_MODEL // 2
GM_W = D_MODEL // 4
POOL_W = D_MODEL // 4
N_HEADS = NSA_W // HEAD_DIM
N_KV_HEADS = 2
GROUP = N_HEADS // N_KV_HEADS
CMP_STRIDE = 16
CMP_LEN = 2 * CMP_STRIDE
SLC_BLOCK = 64
SLC_TOPK = 16
WINDOW = 512
FORCE_SCORE = 1.0e4
ROPE_THETA = 10000.0
SCALE = HEAD_DIM ** -0.5
GM_HEADS = GM_W // HEAD_DIM
GM_CHUNK = 128
POOL_GROUPS = 4
POOL_GW = POOL_W // POOL_GROUPS
POOL_WINDOWS = (2, 4, 8, 16)
POOL_HIST = max(POOL_WINDOWS) - 1
D_FF = 256 * ((8 * D_MODEL // 3 + 255) // 256)
ALPHA = (2 * DEPTH) ** 0.25
LN_EPS = 1e-5
Q_W = N_HEADS * HEAD_DIM
KV_W = 2 * N_KV_HEADS * HEAD_DIM
GATE_W = 3 * N_HEADS
N_IN = Q_W + 3 * KV_W + GATE_W + 2 * GM_W + POOL_W

LANES = 128
KVP = N_KV_HEADS * HEAD_DIM
VMEM_LIMIT = 56 * 1024 * 1024
NEG = -1e30
LOG2E = 1.4426950408889634
HALO = 16

_OFF_Q = 0
_OFF_KV = Q_W
_OFF_GATE = _OFF_KV + 3 * KV_W
_OFF_UV = _OFF_GATE + 2 * LANES
_OFF_P = _OFF_UV + 2 * GM_W
N_EXT = _OFF_P + POOL_W


def _ln_rows(y, g, b):
    mu = jnp.mean(y, axis=-1, keepdims=True)
    d = y - mu
    var = jnp.mean(d * d, axis=-1, keepdims=True)
    return d * lax.rsqrt(var + LN_EPS) * g + b


def _dot(a, b):
    return jnp.dot(a, b, preferred_element_type=F32)


def _dot_t(a, b):
    return lax.dot_general(a, b, (((1,), (1,)), ((), ())), preferred_element_type=F32)


def _ffn_kernel(x_ref, wg_ref, wu_ref, wo_ref, g_ref, b_ref, o_ref, xb_ref, *, n_chunks):
    j = pl.program_id(1)

    @pl.when(j == 0)
    def _():
        xb_ref[...] = x_ref[...].astype(BF16)

    xb = xb_ref[...]
    gate = _dot(xb, wg_ref[...])
    up = _dot(xb, wu_ref[...])
    hid = (gate * jax.nn.sigmoid(gate)) * up
    part = _dot(hid.astype(BF16), wo_ref[...])

    @pl.when(j == 0)
    def _():
        o_ref[...] = part

    if n_chunks > 2:
        @pl.when((j > 0) & (j < n_chunks - 1))
        def _():
            o_ref[...] += part

    @pl.when(j == n_chunks - 1)
    def _():
        y = ALPHA * x_ref[...] + 0.5 * (o_ref[...] + part)
        o_ref[...] = _ln_rows(y, g_ref[...], b_ref[...])


def _ffn_ln(x, w_in_b, w_out_b, g, b, *, tm, sel):
    rows = x.shape[0]
    n_chunks = 2
    fc = D_FF // n_chunks
    l, w = sel
    return pl.pallas_call(
        functools.partial(_ffn_kernel, n_chunks=n_chunks),
        grid=(rows // tm, n_chunks),
        in_specs=[
            pl.BlockSpec((tm, D_MODEL), lambda i, j: (i, 0)),
            pl.BlockSpec((None, None, D_MODEL, fc), lambda i, j: (l, w, 0, j)),
            pl.BlockSpec((None, None, D_MODEL, fc), lambda i, j: (l, w, 0, n_chunks + j)),
            pl.BlockSpec((None, None, fc, D_MODEL), lambda i, j: (l, w, j, 0)),
            pl.BlockSpec((1, D_MODEL), lambda i, j: (0, 0)),
            pl.BlockSpec((1, D_MODEL), lambda i, j: (0, 0)),
        ],
        out_specs=pl.BlockSpec((tm, D_MODEL), lambda i, j: (i, 0)),
        out_shape=jax.ShapeDtypeStruct((rows, D_MODEL), F32),
        scratch_shapes=[pltpu.VMEM((tm, D_MODEL), BF16)],
        compiler_params=pltpu.CompilerParams(
            dimension_semantics=("parallel", "arbitrary"), vmem_limit_bytes=VMEM_LIMIT),
        name="ffn_ln",
    )(x, w_in_b, w_in_b, w_out_b, g.reshape(1, D_MODEL), b.reshape(1, D_MODEL))


def _inproj_kernel(*refs, prompt, n_alias, q_scale):
    h_ref, w_ref, cos_ref, sin_ref, gmg_ref, gmb_ref = refs[:6]
    outs = refs[6 + n_alias:]
    if prompt:
        (qt_ref, kvcb_ref, leafc_ref, leafs_ref, leafw_ref, ks_ref, vts_ref, kw_ref, vtw_ref,
         gate_ref, gu_ref, gv_ref, p_ref) = outs
        leaves = (leafc_ref, leafs_ref, leafw_ref)
        k_refs = (None, ks_ref, kw_ref)
        vt_refs = (None, vts_ref, vtw_ref)
    else:
        qt_ref, kvc_ref, kvs_ref, kvw_ref, gate_ref, gu_ref, gv_ref, p_ref = outs
        rows_out = (kvc_ref, kvs_ref, kvw_ref)
    hb = h_ref[...].astype(BF16)
    cos = cos_ref[...]
    sin = sin_ref[...]

    zq = _dot(hb, w_ref[:, _OFF_Q:_OFF_KV])
    zk = _dot(hb, w_ref[:, _OFF_KV:_OFF_UV])
    zu = _dot(hb, w_ref[:, _OFF_UV:N_EXT])

    n_sq = hb.shape[0] // LANES
    zeros_half = jnp.zeros((HEAD_DIM, LANES), F32)
    first_half = lax.rem(lax.broadcasted_iota(jnp.int32, (1, LANES), 1), HEAD_DIM) < HEAD_DIM // 2

    def rotary(x):
        turned = jnp.where(first_half, -pltpu.roll(x, LANES - HEAD_DIM // 2, 1), pltpu.roll(x, HEAD_DIM // 2, 1))
        return x * cos + turned * sin

    for m in range(N_HEADS // 2):
        c0 = m * LANES
        pair = rotary(zq[:, c0:c0 + LANES]) * q_scale
        kvh = (2 * m) // GROUP
        for c in range(n_sq):
            pt = pair[c * LANES:(c + 1) * LANES].T
            for e in range(2):
                piece = pt[e * HEAD_DIM:(e + 1) * HEAD_DIM]
                both = [piece, zeros_half] if kvh == 0 else [zeros_half, piece]
                qt_ref[2 * m + e, :, c * LANES:(c + 1) * LANES] = jnp.concatenate(both, axis=0).astype(BF16)

    for br in range(3):
        c0 = br * KV_W
        k = rotary(zk[:, c0:c0 + KVP])
        v = zk[:, c0 + KVP:c0 + 2 * KVP]
        if not prompt:
            rows_out[br][:, 0:KVP] = k
            rows_out[br][:, KVP:2 * KVP] = v
            continue
        for c in range(n_sq):
            cols = slice(c * LANES, (c + 1) * LANES)
            vt = v[cols].T
            leaves[br][0:KVP, cols] = k[cols].T
            leaves[br][KVP:2 * KVP, cols] = vt
            if br > 0:
                vt_refs[br][c] = vt.astype(BF16)
        if br == 0:
            kvcb_ref[0] = k
            kvcb_ref[1] = v
        else:
            k_refs[br][...] = k.astype(BF16)

    gate_ref[...] = jax.nn.sigmoid(zk[:, _OFF_GATE - _OFF_KV:_OFF_GATE - _OFF_KV + LANES])
    gu_ref[...] = jax.nn.gelu(zu[:, 0:GM_W])
    gv_ref[...] = _ln_rows(jax.nn.gelu(zu[:, GM_W:2 * GM_W]), gmg_ref[...], gmb_ref[...])
    p_ref[...] = zu[:, 2 * GM_W:2 * GM_W + POOL_W]


def _inproj(h, w_ext, cos, sin, gmg, gmb, *, tm, leaf_bufs=None, layer=0, batch=None):
    rows = h.shape[0]
    n_tab = cos.shape[0] // tm
    prompt = leaf_bufs is not None
    row_spec = lambda w: pl.BlockSpec((tm, w), lambda i: (i, 0))
    tab_spec = pl.BlockSpec((tm, LANES), lambda i: (i % n_tab, 0))
    vec_spec = pl.BlockSpec((1, GM_W), lambda i: (0, 0))
    sds = jax.ShapeDtypeStruct
    in_specs = [row_spec(D_MODEL), pl.BlockSpec((D_MODEL, N_EXT), lambda i: (0, 0)),
                tab_spec, tab_spec, vec_spec, vec_spec]
    args = [h, w_ext, cos, sin, gmg.reshape(1, GM_W), gmb.reshape(1, GM_W)]
    tail_specs = [row_spec(LANES), row_spec(GM_W), row_spec(GM_W), row_spec(POOL_W)]
    tail_shapes = [sds((rows, LANES), F32), sds((rows, GM_W), F32), sds((rows, GM_W), F32), sds((rows, POOL_W), F32)]
    qt_spec = pl.BlockSpec((N_HEADS, LANES, tm), lambda i: (0, 0, i))
    qt_shape = sds((N_HEADS, LANES, rows), BF16)
    aliases = {}
    if prompt:
        seq = rows // batch
        tiles = seq // tm
        leaf_spec = pl.BlockSpec((None, None, KV_W, tm), lambda i: (layer, i // tiles, 0, i % tiles))
        leaf_shape = sds((DEPTH, batch, KV_W, seq), F32)
        sq_spec = pl.BlockSpec((tm // LANES, KVP, LANES), lambda i: (i, 0, 0))
        sq_shape = sds((rows // LANES, KVP, LANES), BF16)
        out_specs = [qt_spec, pl.BlockSpec((2, tm, KVP), lambda i: (0, i, 0)), leaf_spec, leaf_spec, leaf_spec,
                     row_spec(KVP), sq_spec, row_spec(KVP), sq_spec] + tail_specs
        out_shape = [qt_shape, sds((2, rows, KVP), F32), leaf_shape, leaf_shape, leaf_shape,
                     sds((rows, KVP), BF16), sq_shape, sds((rows, KVP), BF16), sq_shape] + tail_shapes
        for n, buf in enumerate(leaf_bufs):
            in_specs.append(pl.BlockSpec(memory_space=pl.ANY))
            args.append(buf)
            aliases[6 + n] = 2 + n
    else:
        out_specs = [qt_spec, row_spec(KV_W), row_spec(KV_W), row_spec(KV_W)] + tail_specs
        out_shape = [qt_shape, sds((rows, KV_W), F32), sds((rows, KV_W), F32), sds((rows, KV_W), F32)] + tail_shapes
    return pl.pallas_call(
        functools.partial(_inproj_kernel, prompt=prompt, n_alias=len(aliases),
                          q_scale=SCALE * LOG2E if prompt else SCALE),
        grid=(rows // tm,),
        in_specs=in_specs,
        out_specs=out_specs,
        out_shape=out_shape,
        input_output_aliases=aliases,
        compiler_params=pltpu.CompilerParams(dimension_semantics=("parallel",), vmem_limit_bytes=VMEM_LIMIT),
        name="inproj",
    )(*args)


def _build_w_ext(w_in):
    g0 = Q_W + 3 * KV_W
    gate = jnp.pad(w_in[:, g0:g0 + GATE_W], ((0, 0), (0, 2 * LANES - GATE_W)))
    return jnp.concatenate([w_in[:, :g0], gate, w_in[:, g0 + GATE_W:]], axis=1).astype(BF16)


def _rope_tables(pos):
    half = HEAD_DIM // 2
    inv = ROPE_THETA ** (-jnp.arange(half, dtype=F32) / half)
    ang = pos.astype(F32)[:, None] * inv[None, :]
    cos = jnp.tile(jnp.cos(ang), (1, LANES // half))
    sin = jnp.tile(jnp.sin(ang), (1, LANES // half))
    return cos, sin


def _compress_kernel(x_ref, pe_ref, w1_ref, w2_ref, kc_ref, vct_ref):
    nsub = kc_ref.shape[0]
    outs = []
    for kv in range(2):
        xr = jnp.concatenate(
            [x_ref[kv, pl.ds(j, nsub, stride=CMP_STRIDE), :].astype(BF16) for j in range(CMP_STRIDE)], axis=1)
        f = _dot(xr, w1_ref[kv])
        per = _dot(pe_ref[kv].astype(BF16), w1_ref[kv])
        pe_term = per[0:1, 0:KVP] + per[1:2, KVP:2 * KVP]
        nxt = jnp.concatenate([f[1:, KVP:2 * KVP], jnp.zeros((1, KVP), F32)], axis=0)
        hid = jax.nn.gelu(f[:, 0:KVP] + nxt + pe_term)
        outs.append(_dot(hid.astype(BF16), w2_ref[kv]))
    kc_ref[...] = outs[0].astype(BF16)
    for c in range(nsub // LANES):
        vct_ref[:, c * LANES:(c + 1) * LANES] = outs[1][c * LANES:(c + 1) * LANES].T.astype(BF16)


def _compress_prompt(kv_rows, pe2, w1kv, w2kv, *, batch):
    seq = kv_rows.shape[1] // batch
    nsub = seq // CMP_STRIDE
    width = CMP_STRIDE * KVP
    return pl.pallas_call(
        _compress_kernel,
        grid=(batch,),
        in_specs=[pl.BlockSpec((2, seq, KVP), lambda b: (0, b, 0)),
                  pl.BlockSpec((2, 8, width), lambda b: (0, 0, 0)),
                  pl.BlockSpec((2, width, 2 * KVP), lambda b: (0, 0, 0)),
                  pl.BlockSpec((2, KVP, KVP), lambda b: (0, 0, 0))],
        out_specs=[pl.BlockSpec((None, nsub, KVP), lambda b: (b, 0, 0)),
                   pl.BlockSpec((None, KVP, nsub), lambda b: (b, 0, 0))],
        out_shape=[jax.ShapeDtypeStruct((batch, nsub, KVP), BF16), jax.ShapeDtypeStruct((batch, KVP, nsub), BF16)],
        compiler_params=pltpu.CompilerParams(dimension_semantics=("parallel",), vmem_limit_bytes=VMEM_LIMIT),
        name="compress_prompt",
    )(kv_rows, pe2, w1kv, w2kv)


def _top_blocks_cols(score, blk):
    sel = jnp.zeros(score.shape, F32)
    for _ in range(SLC_TOPK):
        m = jnp.max(score, axis=0, keepdims=True)
        first = jnp.min(jnp.where(score == m, blk, 1e9), axis=0, keepdims=True)
        hit = blk == first
        sel = jnp.where(hit, 1.0, sel)
        score = jnp.where(hit, -jnp.inf, score)
    return sel


def _top_blocks_idx(score, blk):
    sel = jnp.zeros(score.shape, F32)
    idx = jnp.zeros((score.shape[0], LANES), jnp.int32)
    lane = lax.broadcasted_iota(jnp.int32, (1, LANES), 1)
    big = jnp.int32(1 << 20)
    for it in range(SLC_TOPK):
        m = jnp.max(score, axis=-1, keepdims=True)
        first = jnp.min(jnp.where(score == m, blk, big), axis=-1, keepdims=True)
        hit = blk == first
        sel = jnp.where(hit, 1.0, sel)
        idx = jnp.where(lane == it, first, idx)
        score = jnp.where(hit, -jnp.inf, score)
    return sel, idx


def _softmax_rows(s, mask):
    s = jnp.where(mask, s, NEG)
    m = jnp.max(s, axis=-1, keepdims=True)
    e = jnp.exp(s - m)
    return jnp.where(mask, e / jnp.sum(e, axis=-1, keepdims=True), 0.0)


def _nsa_prompt_kernel(qt_ref, kc_ref, vct_ref, ks_ref, vts_ref, kw_ref, vtw_ref, gate_ref, selmap_ref, expand_ref,
                       o_ref, *, tq, tk, seq, n_sel_blocks):
    t0 = pl.program_id(1) * tq
    nsub = kc_ref.shape[0]
    nb = 8 * ((n_sel_blocks + 7) // 8)
    span = min(WINDOW + tq, seq)
    qpos = t0 + lax.broadcasted_iota(jnp.int32, (1, tq), 1)
    rep = lambda x, n: jnp.concatenate([x] * n, axis=1)
    qt = jnp.concatenate([qt_ref[h] for h in range(N_HEADS)], axis=1)

    cmp_end = CMP_STRIDE * lax.broadcasted_iota(jnp.int32, (nsub, 1), 0) + (CMP_LEN - 1)
    c_bias = jnp.where(cmp_end <= qpos, 0.0, NEG)
    s = _dot(kc_ref[...], qt) + rep(c_bias, N_HEADS)
    e = jnp.exp2(s - jnp.max(s, axis=0, keepdims=True))
    inv_c = jnp.where(rep(qpos >= CMP_LEN - 1, N_HEADS), 1.0 / jnp.sum(e, axis=0, keepdims=True), 0.0)
    eb = e.astype(BF16)
    o_c = _dot(vct_ref[...], eb) * inv_c
    imp_h = _dot(selmap_ref[...], eb)[0:nb] * inv_c

    imp = []
    for k in range(N_KV_HEADS):
        acc = None
        for g in range(GROUP):
            part = imp_h[:, (GROUP * k + g) * tq:(GROUP * k + g + 1) * tq]
            acc = part if acc is None else acc + part
        imp.append(acc)
    imp = jnp.concatenate(imp, axis=1)
    blk = lax.broadcasted_iota(jnp.int32, (nb, 1), 0)
    cur = rep(qpos // SLC_BLOCK, N_KV_HEADS)
    forced = (blk == 0) | (blk == cur) | (blk == cur - 1)
    score = jnp.where(forced, FORCE_SCORE, jnp.where(blk <= cur, imp, -1.0))
    if nb > n_sel_blocks:
        score = jnp.where(blk < n_sel_blocks, score, -jnp.inf)
    sel = _top_blocks_cols(score, blk.astype(F32))
    sel_m1 = jnp.concatenate([sel - 1.0, jnp.zeros((LANES - nb, N_KV_HEADS * tq), F32)], axis=0).astype(BF16)

    start = pl.multiple_of(jnp.maximum(t0 + tq - span, 0), tq)
    dist = qpos - (start + lax.broadcasted_iota(jnp.int32, (span, 1), 0))
    w_bias = jnp.where(dist >= 0, jnp.where(dist <= WINDOW, 0.0, NEG), NEG)
    s = _dot(kw_ref[pl.ds(start, span), :], qt) + rep(w_bias, N_HEADS)
    e = jnp.exp2(s - jnp.max(s, axis=0, keepdims=True))
    inv_w = 1.0 / jnp.sum(e, axis=0, keepdims=True)
    c0 = start // LANES
    vt = jnp.concatenate([vtw_ref[c0 + c] for c in range(span // LANES)], axis=1)
    o_w = _dot(vt, e.astype(BF16)) * inv_w

    def tile(kt, carry, diagonal, nk=tk):
        m_i, l_i, acc = carry
        r0 = pl.multiple_of(kt * tk, tk)
        bias = _dot(expand_ref[pl.ds(r0, nk), :], sel_m1)
        if diagonal:
            kpos = r0 + lax.broadcasted_iota(jnp.int32, (nk, 1), 0)
            bias = jnp.where(kpos <= rep(qpos, N_KV_HEADS), bias, NEG)
        s = _dot(ks_ref[pl.ds(r0, nk), :], qt)
        s = jnp.concatenate([s[:, h * tq:(h + 1) * tq] + bias[:, (h // GROUP) * tq:(h // GROUP + 1) * tq]
                             for h in range(N_HEADS)], axis=1)
        m_new = jnp.maximum(m_i, jnp.max(s, axis=0, keepdims=True))
        a = jnp.exp2(m_i - m_new)
        e = jnp.exp2(s - m_new)
        l_new = a * l_i + jnp.sum(e, axis=0, keepdims=True)
        c0 = kt * (tk // LANES)
        vt = jnp.concatenate([vts_ref[c0 + c] for c in range(nk // LANES)], axis=1)
        return m_new, l_new, a * acc + _dot(vt, e.astype(BF16))

    n_kt = (t0 + tq + tk - 1) // tk
    init = (jnp.full((1, N_HEADS * tq), NEG, F32), jnp.zeros((1, N_HEADS * tq), F32),
            jnp.zeros((KVP, N_HEADS * tq), F32))
    carry = lax.fori_loop(0, n_kt - 1, lambda kt, c: tile(kt, c, False), init)
    diag = [functools.partial(tile, n_kt - 1, diagonal=True, nk=(v + 1) * tq) for v in range(tk // tq)]
    _, l_s, acc_s = lax.switch(lax.rem(pl.program_id(1), tk // tq), diag, carry)
    o_s = acc_s * (1.0 / l_s)

    gt = gate_ref[...].T
    parts = []
    for h in range(N_HEADS):
        rows = slice((h // GROUP) * HEAD_DIM, (h // GROUP + 1) * HEAD_DIM)
        cols = slice(h * tq, (h + 1) * tq)
        parts.append(gt[3 * h:3 * h + 1] * o_c[rows, cols] + gt[3 * h + 1:3 * h + 2] * o_s[rows, cols]
                     + gt[3 * h + 2:3 * h + 3] * o_w[rows, cols])
    ot = jnp.concatenate(parts, axis=0)
    for m in range(NSA_W // LANES):
        o_ref[:, m * LANES:(m + 1) * LANES] = ot[m * LANES:(m + 1) * LANES].T.astype(o_ref.dtype)


def _sel_map_t(nc_rows, n_cmp, ns):
    c0 = CMP_STRIDE * np.arange(nc_rows)[None, :]
    s0 = SLC_BLOCK * np.arange(LANES)[:, None]
    ov = np.clip(np.minimum(c0 + CMP_LEN, s0 + SLC_BLOCK) - np.maximum(c0, s0), 0, None) / CMP_LEN
    ov = ov * (np.arange(nc_rows)[None, :] < n_cmp) * (np.arange(LANES)[:, None] < ns)
    return jnp.asarray(ov, dtype=BF16)


def _expand_map(seq):
    e = (np.arange(LANES)[None, :] == (np.arange(seq) // SLC_BLOCK)[:, None]).astype(np.float32) * -NEG
    return jnp.asarray(e, dtype=BF16)


def _nsa_prompt(qt, kc, vct, ks, vts, kw, vtw, gates, *, batch, seq, tq=128, tk=512):
    tk = min(tk, seq)
    nq = seq // tq
    nsub = seq // CMP_STRIDE
    ns = seq // SLC_BLOCK
    assert ns <= LANES and seq % tk == 0 and tk % tq == 0 and tq == LANES
    kern = functools.partial(_nsa_prompt_kernel, tq=tq, tk=tk, seq=seq, n_sel_blocks=ns)
    per_batch = lambda shp: pl.BlockSpec(shp, lambda b, i: (b,) + (0,) * (len(shp) - 1))
    const = lambda shp: pl.BlockSpec(shp, lambda b, i: (0,) * len(shp))
    return pl.pallas_call(
        kern,
        grid=(batch, nq),
        in_specs=[pl.BlockSpec((N_HEADS, KVP, tq), lambda b, i: (0, 0, b * nq + i)),
                  per_batch((None, nsub, KVP)), per_batch((None, KVP, nsub)),
                  per_batch((seq, KVP)), per_batch((seq // LANES, KVP, LANES)),
                  per_batch((seq, KVP)), per_batch((seq // LANES, KVP, LANES)),
                  pl.BlockSpec((tq, LANES), lambda b, i: (b * nq + i, 0)),
                  const((LANES, nsub)), const((seq, LANES))],
        out_specs=pl.BlockSpec((tq, NSA_W), lambda b, i: (b * nq + i, 0)),
        out_shape=jax.ShapeDtypeStruct((batch * seq, NSA_W), BF16),
        compiler_params=pltpu.CompilerParams(
            dimension_semantics=("parallel", "arbitrary"), vmem_limit_bytes=VMEM_LIMIT),
        name="nsa_prompt",
    )(qt, kc, vct, ks, vts, kw, vtw, gates, _sel_map_t(nsub, nsub - 1, ns), _expand_map(seq))


def _pool_windows(z_ext, tm):
    s2 = z_ext[1:] + z_ext[:-1]
    s4 = s2[2:] + s2[:-2]
    s8 = s4[4:] + s4[:-4]
    s16 = s8[8:] + s8[:-8]
    return (s2[HALO - 1:HALO - 1 + tm], s4[HALO - 3:HALO - 3 + tm], s8[HALO - 7:HALO - 7 + tm],
            s16[HALO - 15:HALO - 15 + tm])


def _mix_out_kernel(*refs, tm, tiles_per_seq, pool_in_kernel):
    if pool_in_kernel:
        (x_ref, nsa_ref, gu_ref, gv_ref, p_ref, halo_ref, ws_ref, gb_ref, pw_ref, ps_ref, wo_ref,
         g_ref, b_ref, o_ref) = refs
    else:
        (x_ref, nsa_ref, gu_ref, gv_ref, d_ref, ws_ref, gb_ref, pw_ref, ps_ref, wo_ref,
         g_ref, b_ref, o_ref) = refs
    lane = lax.broadcasted_iota(jnp.int32, (1, GM_W), 1)

    parts = []
    for c in range(tm // GM_CHUNK):
        v = gv_ref[c * GM_CHUNK:(c + 1) * GM_CHUNK, :]
        stacked = jnp.concatenate(
            [jnp.where(lane // HEAD_DIM == h, v, 0.0) for h in range(GM_HEADS)], axis=0).astype(BF16)
        s = _dot(ws_ref[...], stacked) + gb_ref[...]
        parts.append(gu_ref[c * GM_CHUNK:(c + 1) * GM_CHUNK, :] * s)
    o_gm = parts[0] if len(parts) == 1 else jnp.concatenate(parts, axis=0)

    if pool_in_kernel:
        first_tile = (pl.program_id(0) % tiles_per_seq) == 0
        halo = jnp.where(first_tile, 0.0, halo_ref[...])
        z = p_ref[...]
        wins = _pool_windows(jnp.concatenate([halo, z], axis=0), tm)
        pos = (pl.program_id(0) % tiles_per_seq) * tm + lax.broadcasted_iota(jnp.int32, (tm, 1), 0)
        grp = lane // POOL_GW
        wsum = jnp.where(grp == 0, wins[0], jnp.where(grp == 1, wins[1], jnp.where(grp == 2, wins[2], wins[3])))
        width = jnp.where(grp == 0, POOL_WINDOWS[0], jnp.where(grp == 1, POOL_WINDOWS[1],
                          jnp.where(grp == 2, POOL_WINDOWS[2], POOL_WINDOWS[3])))
        cnt = jnp.minimum(width, pos + 1).astype(F32)
        d = wsum / cnt - z
    else:
        d = d_ref[...]
    o_pool = _dot(d.astype(BF16), pw_ref[...]) * ps_ref[...]

    mixed = jnp.concatenate([nsa_ref[...], o_gm.astype(BF16), o_pool.astype(BF16)], axis=1)
    y = ALPHA * x_ref[...] + _dot(mixed, wo_ref[...])
    o_ref[...] = _ln_rows(y, g_ref[...], b_ref[...])


def _mix_out(x, o_nsa, gu, gv, p_or_d, ws_cat, gb_full, pw_big, ps, w_o_b, g, b, *, tm, seq, pool_in_kernel):
    rows = x.shape[0]
    tiles_per_seq = max(seq // tm, 1)
    row_spec = lambda w: pl.BlockSpec((tm, w), lambda i: (i, 0))
    const = lambda shp: pl.BlockSpec(shp, lambda i: (0,) * len(shp))
    in_specs = [row_spec(D_MODEL), row_spec(NSA_W), row_spec(GM_W), row_spec(GM_W), row_spec(POOL_W)]
    args = [x, o_nsa, gu, gv, p_or_d]
    if pool_in_kernel:
        in_specs.append(pl.BlockSpec((HALO, POOL_W), lambda i: (jnp.maximum(i * (tm // HALO) - 1, 0), 0)))
        args.append(p_or_d)
    in_specs += [const((GM_CHUNK, GM_HEADS * GM_CHUNK)), const((GM_CHUNK, GM_W)), const((POOL_W, POOL_W)),
                 const((1, POOL_W)), const((D_MODEL, D_MODEL)), const((1, D_MODEL)), const((1, D_MODEL))]
    args += [ws_cat, gb_full, pw_big, ps.reshape(1, POOL_W), w_o_b, g.reshape(1, D_MODEL), b.reshape(1, D_MODEL)]
    kern = functools.partial(_mix_out_kernel, tm=tm, tiles_per_seq=tiles_per_seq, pool_in_kernel=pool_in_kernel)
    return pl.pallas_call(
        kern,
        grid=(rows // tm,),
        in_specs=in_specs,
        out_specs=row_spec(D_MODEL),
        out_shape=jax.ShapeDtypeStruct((rows, D_MODEL), F32),
        compiler_params=pltpu.CompilerParams(dimension_semantics=("parallel",), vmem_limit_bytes=VMEM_LIMIT),
        name="mix_out_prompt" if pool_in_kernel else "mix_out_sample",
    )(*args)


def _gmlp_weights(ws, gb, chunk_rows, reps):
    wm = jnp.tril(ws[:, :chunk_rows, :chunk_rows])
    bias = gb[:, :chunk_rows]
    if reps > 1:
        eye = jnp.eye(reps, dtype=F32)
        wm = jnp.einsum('hts,ab->hatbs', wm, eye).reshape(GM_HEADS, reps * chunk_rows, reps * chunk_rows)
        bias = jnp.tile(bias, (1, reps))
    ws_cat = wm.transpose(1, 0, 2).reshape(GM_CHUNK, GM_HEADS * GM_CHUNK).astype(BF16)
    gb_full = jnp.repeat(bias.T, HEAD_DIM, axis=1)
    return ws_cat, gb_full


def _pool_weights(pw):
    eye = jnp.eye(POOL_GROUPS, dtype=F32)
    return jnp.einsum('gce,gq->gcqe', pw, eye).reshape(POOL_W, POOL_W).astype(BF16)


PAGES_PER_STEP = 32
SUBS_PER_PAGE = PAGE_SIZE // CMP_STRIDE


def _cmp_sample_kernel(pt_ref, cache_ref, q_ref, pe_ref, w1_ref, w2_ref, selmap_ref, perm_ref, oc_ref, idx_ref,
                       pbuf, sem, xr_ref, fs_ref, *, layer, n_chunks, n_seq, dec_seq, past_len, n_sel_blocks):
    b = pl.program_id(0)
    c = pl.program_id(1)
    step = b * n_chunks + c
    slot = lax.rem(step, 2)
    pps = PAGES_PER_STEP

    def page_copies(sb, sc, sl):
        return [pltpu.make_async_copy(cache_ref.at[layer, pt_ref[sb, sc * pps + p]], pbuf.at[sl, p], sem.at[sl])
                for p in range(pps)]

    @pl.when(step == 0)
    def _():
        for cp in page_copies(b, c, slot):
            cp.start()

    @pl.when(step + 1 < n_seq * n_chunks)
    def _():
        wrap = c + 1 == n_chunks
        for cp in page_copies(jnp.where(wrap, b + 1, b), jnp.where(wrap, 0, c + 1), 1 - slot):
            cp.start()

    for cp in page_copies(b, c, slot):
        cp.wait()

    def to_rows(p, carry):
        r0 = pl.multiple_of(p * SUBS_PER_PAGE, SUBS_PER_PAGE)
        for kv in range(2):
            rows = _dot_t(perm_ref[...], pbuf[slot, p, kv].astype(BF16))
            for j in range(CMP_STRIDE):
                xr_ref[kv, pl.ds(r0, SUBS_PER_PAGE), j * KVP:(j + 1) * KVP] = rows[j * SUBS_PER_PAGE:(j + 1) * SUBS_PER_PAGE]
        return carry

    lax.fori_loop(0, pps, to_rows, 0, unroll=True)

    subs = pps * SUBS_PER_PAGE
    s0 = pl.multiple_of(c * subs, subs)
    for kv in range(2):
        fs_ref[kv, pl.ds(s0, subs), :] = _dot(xr_ref[kv].astype(BF16), w1_ref[kv])

    @pl.when(c == n_chunks - 1)
    def _():
        nsub = n_chunks * subs
        kcv = []
        for kv in range(2):
            f = fs_ref[kv]
            per = _dot(pe_ref[kv].astype(BF16), w1_ref[kv])
            pe_term = per[0:1, 0:KVP] + per[1:2, KVP:2 * KVP]
            nxt = jnp.concatenate([f[1:, KVP:2 * KVP], jnp.zeros((1, KVP), F32)], axis=0)
            hid = jax.nn.gelu(f[:, 0:KVP] + nxt + pe_term)
            kcv.append(_dot(hid.astype(BF16), w2_ref[kv]).astype(BF16))
        n_kt = N_KV_HEADS * dec_seq
        rows = GROUP * n_kt
        q = q_ref[...]
        qpos = past_len + lax.rem(lax.broadcasted_iota(jnp.int32, (rows, 1), 0), dec_seq)
        cmp_end = CMP_STRIDE * lax.broadcasted_iota(jnp.int32, (1, nsub), 1) + (CMP_LEN - 1)
        p = _softmax_rows(_dot_t(q, kcv[0]), cmp_end <= qpos).astype(BF16)
        oc_ref[...] = _dot(p, kcv[1])
        imp_g = _dot(p, selmap_ref[...])
        imp = imp_g[0:n_kt]
        for g in range(1, GROUP):
            imp = imp + imp_g[g * n_kt:(g + 1) * n_kt]
        blk = lax.broadcasted_iota(jnp.int32, (1, imp.shape[1]), 1)
        cur = qpos[0:n_kt] // SLC_BLOCK
        forced = (blk == 0) | (blk == cur) | (blk == cur - 1)
        score = jnp.where(forced, FORCE_SCORE, jnp.where(blk <= cur, imp, -1.0))
        score = jnp.where(blk < n_sel_blocks, score, -jnp.inf)
        idx_ref[...] = _top_blocks_idx(score, blk)[1]


def _cmp_sample(page_table, cache_t, q_gkt, pe2, w1kv, w2kv, *, layer, dec_seq, past_len):
    n_seq, n_pages = page_table.shape
    n_chunks = n_pages // PAGES_PER_STEP
    nsub = n_pages * SUBS_PER_PAGE
    ns = (past_len + dec_seq + SLC_BLOCK - 1) // SLC_BLOCK
    ns_pad = LANES * ((ns + LANES - 1) // LANES)
    n_kt = N_KV_HEADS * dec_seq
    rows = GROUP * n_kt
    c0 = CMP_STRIDE * np.arange(nsub)[:, None]
    s0 = SLC_BLOCK * np.arange(ns_pad)[None, :]
    ov = np.clip(np.minimum(c0 + CMP_LEN, s0 + SLC_BLOCK) - np.maximum(c0, s0), 0, None) / CMP_LEN
    ov = ov * (np.arange(nsub)[:, None] < nsub - 1) * (np.arange(ns_pad)[None, :] < ns)
    selmap = jnp.asarray(ov, dtype=BF16)
    pos = np.arange(PAGE_SIZE)
    perm = jnp.asarray((pos[None, :] == (CMP_STRIDE * (pos % SUBS_PER_PAGE) + pos // SUBS_PER_PAGE)[:, None])
                       .astype(np.float32), dtype=BF16)
    width = CMP_STRIDE * KVP
    kern = functools.partial(_cmp_sample_kernel, layer=layer, n_chunks=n_chunks, n_seq=n_seq, dec_seq=dec_seq,
                             past_len=past_len, n_sel_blocks=ns)
    const = lambda shp: pl.BlockSpec(shp, lambda b, c, pt: (0,) * len(shp))
    grid_spec = pltpu.PrefetchScalarGridSpec(
        num_scalar_prefetch=1,
        grid=(n_seq, n_chunks),
        in_specs=[pl.BlockSpec(memory_space=pl.ANY),
                  pl.BlockSpec((None, rows, LANES), lambda b, c, pt: (b, 0, 0)),
                  const((2, 8, width)), const((2, width, 2 * KVP)), const((2, KVP, KVP)), const((nsub, ns_pad)),
                  const((PAGE_SIZE, PAGE_SIZE))],
        out_specs=[pl.BlockSpec((None, rows, LANES), lambda b, c, pt: (b, 0, 0)),
                   pl.BlockSpec((None, n_kt, LANES), lambda b, c, pt: (b, 0, 0))],
        scratch_shapes=[pltpu.VMEM((2, PAGES_PER_STEP, 2, KVP, PAGE_SIZE), F32),
                        pltpu.SemaphoreType.DMA((2,)),
                        pltpu.VMEM((2, PAGES_PER_STEP * SUBS_PER_PAGE, CMP_STRIDE * KVP), F32),
                        pltpu.VMEM((2, nsub, 2 * KVP), F32)])
    return pl.pallas_call(
        kern,
        grid_spec=grid_spec,
        out_shape=[jax.ShapeDtypeStruct((n_seq, rows, LANES), F32),
                   jax.ShapeDtypeStruct((n_seq, n_kt, LANES), jnp.int32)],
        compiler_params=pltpu.CompilerParams(
            dimension_semantics=("arbitrary", "arbitrary"), vmem_limit_bytes=VMEM_LIMIT),
        name="cmp_sample",
    )(page_table, cache_t, q_gkt, pe2, w1kv, w2kv, selmap, perm)


def _compress_weights_kv(pe, w1, w2):
    eye = jnp.eye(N_KV_HEADS, dtype=F32)
    w1r = w1.reshape(2, 2, CMP_STRIDE, HEAD_DIM, HEAD_DIM)
    w1kv = jnp.einsum('ksjde,hg->kjhdsge', w1r, eye).reshape(2, CMP_STRIDE * KVP, 2 * KVP)
    w2kv = jnp.einsum('ked,hg->khegd', w2, eye).reshape(2, KVP, KVP)
    per = pe.reshape(2, 2, CMP_STRIDE, HEAD_DIM)
    per = jnp.broadcast_to(per[:, :, :, None, :], (2, 2, CMP_STRIDE, N_KV_HEADS, HEAD_DIM))
    pe2 = jnp.pad(per.reshape(2, 2, CMP_STRIDE * KVP), ((0, 0), (0, 6), (0, 0)))
    return pe2, w1kv.astype(BF16), w2kv.astype(BF16)


def _slc_sample_kernel(*refs, layer, n_seq, n_pages, dec_seq, past_len, n_alias):
    (pt_ref, idx_sm_ref, cache_ref, q_ref, idxv_ref, knew_ref, win_ref, wnew_ref, oc_ref, gate_ref, expand_ref,
     wnewt_ref) = refs[:12]
    o_ref, wout_ref, kvbuf, sem = refs[12 + n_alias:]
    b = pl.program_id(0)
    slot = lax.rem(b, 2)
    n_kt = N_KV_HEADS * dec_seq
    rows = GROUP * n_kt
    n_past_blocks = past_len // SLC_BLOCK
    per_head = dec_seq * SLC_TOPK

    def tile_copies(sb, sl, k, i):
        kt = k * dec_seq + i // SLC_TOPK
        s = lax.rem(i, SLC_TOPK)
        j = idx_sm_ref[(sb * n_kt + kt) * SLC_TOPK + s]
        phys = pt_ref[sb, jnp.minimum(lax.shift_right_logical(j, 1), n_pages - 1)]
        return [pltpu.make_async_copy(cache_ref.at[layer, phys, :, pl.ds(k * HEAD_DIM, HEAD_DIM), :],
                                      kvbuf.at[sl, kt, s], sem.at[sl])]

    def start_all(sb, sl):
        for k in range(N_KV_HEADS):
            def body(i, carry):
                for cp in tile_copies(sb, sl, k, i):
                    cp.start()
                return carry
            lax.fori_loop(0, per_head, body, 0, unroll=4)

    @pl.when(b == 0)
    def _():
        start_all(b, slot)

    @pl.when(b + 1 < n_seq)
    def _():
        start_all(b + 1, 1 - slot)

    for k in range(N_KV_HEADS):
        def wait_body(i, carry):
            for cp in tile_copies(b, slot, k, i):
                cp.wait()
            return carry
        lax.fori_loop(0, per_head, wait_body, 0, unroll=4)

    q = q_ref[...]
    qb = q.astype(BF16)
    gates = gate_ref[...]
    t_row = lax.rem(lax.broadcasted_iota(jnp.int32, (rows, 1), 0) // GROUP, dec_seq)
    t_new = lax.broadcasted_iota(jnp.int32, (1, dec_seq), 1)
    new_ok = t_new <= t_row

    wb = win_ref.shape[2]
    kpos = past_len - wb + lax.broadcasted_iota(jnp.int32, (1, wb), 1)
    dist = past_len + t_row - kpos
    w_ok = (dist >= 0) & (dist <= WINDOW) & (kpos >= 0)
    s_w = jnp.where(w_ok, _dot(qb, win_ref[0].astype(BF16)), NEG)
    s_n = jnp.where(new_ok, _dot_t(qb, wnew_ref[:, 0:KVP].astype(BF16)), NEG)
    m = jnp.maximum(jnp.max(s_w, axis=-1, keepdims=True), jnp.max(s_n, axis=-1, keepdims=True))
    e_w = jnp.where(w_ok, jnp.exp(s_w - m), 0.0)
    e_n = jnp.where(new_ok, jnp.exp(s_n - m), 0.0)
    den = jnp.sum(e_w, axis=-1, keepdims=True) + jnp.sum(e_n, axis=-1, keepdims=True)
    o_w = (_dot_t(e_w.astype(BF16), win_ref[1].astype(BF16))
           + _dot(e_n.astype(BF16), wnew_ref[:, KVP:2 * KVP].astype(BF16))) / den

    idxv = idxv_ref[...]
    lane16 = lax.broadcasted_iota(jnp.int32, (1, LANES), 1) < SLC_TOPK
    half = jnp.where(lane16 & ((idxv & 1) == 1), 1.0, 0.0).astype(BF16)
    live = jnp.where(lane16 & (idxv < n_past_blocks), 1.0, 0.0).astype(BF16)
    half_x = _dot(half, expand_ref[...])
    live_x = _dot(live, expand_ref[...])
    col = lax.broadcasted_iota(jnp.int32, (1, SLC_TOPK * PAGE_SIZE), 1)
    col_half = (lax.rem(col, PAGE_SIZE) // SLC_BLOCK).astype(F32)
    tile_ok = (live_x > 0.5) & (half_x == col_half)
    s_new = _dot_t(qb, knew_ref[:, 0:KVP].astype(BF16))
    zeros_half = jnp.zeros((GROUP, HEAD_DIM), F32)
    o_parts = []
    for kt in range(n_kt):
        k = kt // dec_seq
        r0 = kt * GROUP
        qk = q[r0:r0 + GROUP, k * HEAD_DIM:(k + 1) * HEAD_DIM].astype(BF16)
        kcat = jnp.concatenate([kvbuf[slot, kt, s, 0] for s in range(SLC_TOPK)], axis=1).astype(BF16)
        vcat = jnp.concatenate([kvbuf[slot, kt, s, 1] for s in range(SLC_TOPK)], axis=1).astype(BF16)
        ok = tile_ok[kt:kt + 1]
        nok = new_ok[r0:r0 + GROUP]
        s_s = jnp.where(ok, _dot(qk, kcat), NEG)
        s_n = jnp.where(nok, s_new[r0:r0 + GROUP], NEG)
        m = jnp.maximum(jnp.max(s_s, axis=-1, keepdims=True), jnp.max(s_n, axis=-1, keepdims=True))
        e_s = jnp.where(ok, jnp.exp(s_s - m), 0.0)
        e_n = jnp.where(nok, jnp.exp(s_n - m), 0.0)
        den = jnp.sum(e_s, axis=-1, keepdims=True) + jnp.sum(e_n, axis=-1, keepdims=True)
        v_new = knew_ref[:, KVP + k * HEAD_DIM:KVP + (k + 1) * HEAD_DIM].astype(BF16)
        o = (_dot_t(e_s.astype(BF16), vcat) + _dot(e_n.astype(BF16), v_new)) / den
        o_parts.append(jnp.concatenate([o, zeros_half] if k == 0 else [zeros_half, o], axis=1))
    o_s = jnp.concatenate(o_parts, axis=0)

    o_ref[...] = gates[:, 0:1] * oc_ref[...] + gates[:, 1:2] * o_s + gates[:, 2:3] * o_w

    lane_w = lax.broadcasted_iota(jnp.int32, (1, wb), 1)
    for kv in range(2):
        moved = pltpu.roll(win_ref[kv], wb - dec_seq, 1)
        for t in range(dec_seq):
            moved = jnp.where(lane_w == wb - dec_seq + t, wnewt_ref[kv * KVP:(kv + 1) * KVP, t:t + 1], moved)
        wout_ref[kv] = moved


def _slc_sample(page_table, idx_flat, cache_t, q_ktg, idxv, kvs_new, win_t, kvw_new, o_c, gates_r, win_out,
                *, layer, dec_seq, past_len):
    n_seq, n_pages = page_table.shape
    n_kt = N_KV_HEADS * dec_seq
    rows = GROUP * n_kt
    wb = win_t.shape[-1]
    cols = SLC_TOPK * PAGE_SIZE
    expand = jnp.asarray((np.arange(LANES)[:, None] == (np.arange(cols) // PAGE_SIZE)[None, :]).astype(np.float32),
                         dtype=BF16)
    n_alias = 0 if win_out is None else 1
    kern = functools.partial(_slc_sample_kernel, layer=layer, n_seq=n_seq, n_pages=n_pages, dec_seq=dec_seq,
                             past_len=past_len, n_alias=n_alias)
    per_seq = lambda r, w: pl.BlockSpec((None, r, w), lambda b, pt, ix: (b, 0, 0))
    state_spec = pl.BlockSpec((None, None, 2, KVP, wb), lambda b, pt, ix: (layer, b, 0, 0, 0))
    in_specs = [pl.BlockSpec(memory_space=pl.ANY),
                per_seq(rows, LANES), per_seq(n_kt, LANES), per_seq(dec_seq, KV_W), state_spec,
                per_seq(dec_seq, KV_W), per_seq(rows, LANES), per_seq(rows, LANES),
                pl.BlockSpec((LANES, cols), lambda b, pt, ix: (0, 0)), per_seq(KV_W, dec_seq)]
    args = [page_table, idx_flat, cache_t, q_ktg, idxv, kvs_new, win_t, kvw_new, o_c, gates_r, expand,
            jnp.swapaxes(kvw_new, 1, 2)]
    if n_alias:
        in_specs.append(pl.BlockSpec(memory_space=pl.ANY))
        args.append(win_out)
    grid_spec = pltpu.PrefetchScalarGridSpec(
        num_scalar_prefetch=2,
        grid=(n_seq,),
        in_specs=in_specs,
        out_specs=[per_seq(rows, LANES), state_spec],
        scratch_shapes=[pltpu.VMEM((2, n_kt, SLC_TOPK, 2, HEAD_DIM, PAGE_SIZE), F32),
                        pltpu.SemaphoreType.DMA((2,))])
    return pl.pallas_call(
        kern,
        grid_spec=grid_spec,
        out_shape=[jax.ShapeDtypeStruct((n_seq, rows, LANES), F32), jax.ShapeDtypeStruct(win_t.shape, F32)],
        input_output_aliases={len(args) - 1: 1} if n_alias else {},
        compiler_params=pltpu.CompilerParams(dimension_semantics=("arbitrary",), vmem_limit_bytes=VMEM_LIMIT),
        name="slc_sample",
    )(*args)


def _pool_sample_kernel(z_ref, d_ref, *, dec_seq, pos0):
    lane = lax.broadcasted_iota(jnp.int32, (1, POOL_W), 1)
    grp = lane // POOL_GW
    for t in range(dec_seq):
        cur = z_ref[:, POOL_HIST + t, :]
        acc = cur
        sums = {}
        for back in range(1, max(POOL_WINDOWS)):
            acc = acc + z_ref[:, POOL_HIST + t - back, :]
            if back + 1 in POOL_WINDOWS:
                sums[back + 1] = acc
        d = None
        for g, w in enumerate(POOL_WINDOWS):
            val = sums[w] / float(min(w, pos0 + t + 1)) - cur
            d = val if d is None else jnp.where(grp == g, val, d)
        d_ref[:, t, :] = d


def _pool_sample(z_ext, *, dec_seq, pos0):
    n_seq = z_ext.shape[0]
    return pl.pallas_call(
        functools.partial(_pool_sample_kernel, dec_seq=dec_seq, pos0=pos0),
        out_shape=jax.ShapeDtypeStruct((n_seq, dec_seq, POOL_W), F32),
        name="pool_sample",
    )(z_ext)


def _sample_layer(x, lw, cmp_t, slc_t, win_t, win_out, pool_state, page_table, *, layer, n_seq, dec_seq):
    rows = n_seq * dec_seq
    n_kt = N_KV_HEADS * dec_seq
    x = _ffn_ln(x, lw['ffn_in'], lw['ffn_out'], lw['ln_g'][0], lw['ln_b'][0], sel=(lw['layer'], 0),tm=rows)
    cos, sin = _rope_tables(PAST_LEN + jnp.arange(rows) % dec_seq)
    qt, kvc, kvs, kvw, gates, gu, gv, p = _inproj(
        x, lw['w_ext'], cos, sin, lw['gm_ln_g'], lw['gm_ln_b'], tm=rows)

    qf = jnp.swapaxes(qt.astype(F32), 1, 2).reshape(N_KV_HEADS, GROUP, n_seq, dec_seq, LANES)
    q_gkt = qf.transpose(2, 1, 0, 3, 4).reshape(n_seq, GROUP * n_kt, LANES).astype(BF16)
    q_ktg = qf.transpose(2, 0, 3, 1, 4).reshape(n_seq, GROUP * n_kt, LANES)
    o_c, idxv = _cmp_sample(page_table, cmp_t, q_gkt, lw['pe2'], lw['w1kv'], lw['w2kv'],
                            layer=layer, dec_seq=dec_seq, past_len=PAST_LEN)
    o_c = o_c.reshape(n_seq, GROUP, N_KV_HEADS, dec_seq, LANES).transpose(0, 2, 3, 1, 4).reshape(n_seq, GROUP * n_kt, LANES)
    gates_r = gates[:, :GATE_W].reshape(n_seq, dec_seq, N_KV_HEADS, GROUP, 3).transpose(0, 2, 1, 3, 4)
    gates_r = jnp.pad(gates_r.reshape(n_seq, GROUP * n_kt, 3), ((0, 0), (0, 0), (0, LANES - 3)))
    idx_flat = idxv[:, :, :SLC_TOPK].reshape(-1)
    o, win_out = _slc_sample(page_table, idx_flat, slc_t, q_ktg, idxv, kvs.reshape(n_seq, dec_seq, KV_W), win_t,
                             kvw.reshape(n_seq, dec_seq, KV_W), o_c, gates_r, win_out,
                             layer=layer, dec_seq=dec_seq, past_len=PAST_LEN)
    o = o.reshape(n_seq, N_KV_HEADS, dec_seq, GROUP, LANES)
    o_nsa = jnp.stack([o[:, k, :, :, k * HEAD_DIM:(k + 1) * HEAD_DIM] for k in range(N_KV_HEADS)], axis=2)
    o_nsa = o_nsa.reshape(rows, NSA_W).astype(BF16)

    z_ext = jnp.concatenate([pool_state, p.reshape(n_seq, dec_seq, POOL_W)], axis=1)
    d = _pool_sample(z_ext, dec_seq=dec_seq, pos0=PAST_LEN).reshape(rows, POOL_W)
    ws_cat, gb_full = _gmlp_weights(lw['gm_ws'], lw['gm_b'], dec_seq, GM_CHUNK // dec_seq)
    x = _mix_out(x, o_nsa, gu, gv, d, ws_cat, gb_full, lw['pw_big'], lw['pool_scale'], lw['w_o'],
                 lw['ln_g'][1], lw['ln_b'][1], tm=rows, seq=dec_seq, pool_in_kernel=False)
    x = _ffn_ln(x, lw['ffn_in'], lw['ffn_out'], lw['ln_g'][2], lw['ln_b'][2], sel=(lw['layer'], 1),tm=rows)
    shp = (n_seq, dec_seq, 2, N_KV_HEADS, HEAD_DIM)
    new = (kvc.reshape(shp), kvs.reshape(shp), z_ext[:, dec_seq:], gv.reshape(n_seq, dec_seq, GM_W))
    return x, new, win_out


def _pages_by_channel(cache):
    nd = cache.ndim
    t = jnp.transpose(cache, tuple(range(nd - 4)) + (nd - 3, nd - 2, nd - 1, nd - 4))
    return t.reshape(t.shape[:-3] + (KVP, t.shape[-1]))
def _prompt_layer(x, lw, leaf_bufs, *, layer, batch, seq, tm, tm_ffn):
    x = _ffn_ln(x, lw['ffn_in'], lw['ffn_out'], lw['ln_g'][0], lw['ln_b'][0], sel=(lw['layer'], 0),tm=tm_ffn)
    cos, sin = _rope_tables(jnp.arange(seq))
    qt, kvc, leaf_c, leaf_s, leaf_w, ks, vts, kw, vtw, gates, gu, gv, p = _inproj(
        x, lw['w_ext'], cos, sin, lw['gm_ln_g'], lw['gm_ln_b'], tm=tm, leaf_bufs=leaf_bufs, layer=layer, batch=batch)
    kc, vct = _compress_prompt(kvc, lw['pe2'], lw['w1kv'], lw['w2kv'], batch=batch)
    o_nsa = _nsa_prompt(qt, kc, vct, ks, vts, kw, vtw, gates, batch=batch, seq=seq)
    ws_cat, gb_full = _gmlp_weights(lw['gm_ws'], lw['gm_b'], GM_CHUNK, 1)
    x = _mix_out(x, o_nsa, gu, gv, p, ws_cat, gb_full, lw['pw_big'], lw['pool_scale'], lw['w_o'],
                 lw['ln_g'][1], lw['ln_b'][1], tm=tm, seq=seq, pool_in_kernel=True)
    x = _ffn_ln(x, lw['ffn_in'], lw['ffn_out'], lw['ln_g'][2], lw['ln_b'][2], sel=(lw['layer'], 1),tm=tm_ffn)
    return x, (leaf_c, leaf_s, leaf_w), p.reshape(batch, seq, POOL_W)[:, seq - POOL_HIST:]


def _leaf_rows(buf):
    d, b, _, t = buf.shape
    return buf.reshape(d, b, 2, N_KV_HEADS, HEAD_DIM, t).transpose(0, 1, 5, 2, 3, 4)


def _layer_weights(l, ffn_in_b, ffn_out_b, ln_g, ln_b, w_in, w_o, cmp_pe, cmp_w1, cmp_w2,
                   gm_ln_g, gm_ln_b, gm_ws, gm_b, pool_w, pool_scale):
    pe2, w1kv, w2kv = _compress_weights_kv(cmp_pe[l], cmp_w1[l], cmp_w2[l])
    return dict(layer=l, ffn_in=ffn_in_b, ffn_out=ffn_out_b, ln_g=ln_g[l], ln_b=ln_b[l],
                w_ext=_build_w_ext(w_in[l]), w_o=w_o[l].astype(BF16),
                pe2=pe2, w1kv=w1kv, w2kv=w2kv,
                gm_ln_g=gm_ln_g[l], gm_ln_b=gm_ln_b[l], gm_ws=gm_ws[l], gm_b=gm_b[l],
                pw_big=_pool_weights(pool_w[l]), pool_scale=pool_scale[l])


def kernel(x_prompt, x_sample, cache_kv_cmp, cache_kv_slc, state_kv_win, state_pool, page_table, ln_g, ln_b, ffn_w_in, ffn_w_out, w_in, w_o, cmp_pe, cmp_w1, cmp_w2, gm_ln_g, gm_ln_b, gm_ws, gm_b, pool_w, pool_scale):
    batch, seq, _ = x_prompt.shape
    fi = ffn_w_in.astype(BF16)
    fo = ffn_w_out.astype(BF16)
    n_seq, dec_seq, _ = x_sample.shape
    xp = x_prompt.reshape(batch * seq, D_MODEL)
    xs = x_sample.reshape(n_seq * dec_seq, D_MODEL)
    cmp_t = _pages_by_channel(cache_kv_cmp)
    slc_t = _pages_by_channel(cache_kv_slc)
    win_t = _pages_by_channel(state_kv_win)
    leaf_bufs = tuple(jnp.zeros((DEPTH, batch, KV_W, seq), F32) for _ in range(3))
    win_out = jnp.zeros(win_t.shape, F32)
    pool_p, new_s = [], []
    for l in range(DEPTH):
        lw = _layer_weights(l, fi, fo, ln_g, ln_b, w_in, w_o, cmp_pe, cmp_w1, cmp_w2, gm_ln_g, gm_ln_b, gm_ws, gm_b, pool_w, pool_scale)
        xp, leaf_bufs, pool_l = _prompt_layer(xp, lw, leaf_bufs, layer=l, batch=batch, seq=seq, tm=512, tm_ffn=1024)
        xs, st_s, win_out = _sample_layer(xs, lw, cmp_t, slc_t, win_t, win_out, state_pool[l], page_table,
                                          layer=l, n_seq=n_seq, dec_seq=dec_seq)
        pool_p.append(pool_l)
        new_s.append(st_s)
    stk = lambda lst, i: jnp.stack([t[i] for t in lst])
    wb = min(WINDOW, seq)
    win_s = _leaf_rows(win_out.reshape(DEPTH, n_seq, KV_W, win_out.shape[-1]))
    return (xp.reshape(batch, seq, D_MODEL), xs.reshape(n_seq, dec_seq, D_MODEL),
            _leaf_rows(leaf_bufs[0]), stk(new_s, 0), _leaf_rows(leaf_bufs[1]), stk(new_s, 1),
            _leaf_rows(leaf_bufs[2])[:, :, seq - wb:], win_s, jnp.stack(pool_p), stk(new_s, 2), stk(new_s, 3))
```

```python
import functools

import numpy as np
import jax
import jax.numpy as jnp
from jax import lax
from jax.experimental import pallas as pl
from jax.experimental.pallas import tpu as pltpu

F32 = jnp.float32
BF16 = jnp.bfloat16

D_MODEL = 1024
DEPTH = 2
PAST_LEN = 16384
PAGE_SIZE = 128
HEAD_DIM = 64
NSA_W = D_MODEL // 2
GM_W = D_MODEL // 4
POOL_W = D_MODEL // 4
N_HEADS = NSA_W // HEAD_DIM
N_KV_HEADS = 2
GROUP = N_HEADS // N_KV_HEADS
CMP_STRIDE = 16
CMP_LEN = 2 * CMP_STRIDE
SLC_BLOCK = 64
SLC_TOPK = 16
WINDOW = 512
FORCE_SCORE = 1.0e4
ROPE_THETA = 10000.0
SCALE = HEAD_DIM ** -0.5
GM_HEADS = GM_W // HEAD_DIM
GM_CHUNK = 128
POOL_GROUPS = 4
POOL_GW = POOL_W // POOL_GROUPS
POOL_WINDOWS = (2, 4, 8, 16)
POOL_HIST = max(POOL_WINDOWS) - 1
D_FF = 256 * ((8 * D_MODEL // 3 + 255) // 256)
ALPHA = (2 * DEPTH) ** 0.25
LN_EPS = 1e-5
Q_W = N_HEADS * HEAD_DIM
KV_W = 2 * N_KV_HEADS * HEAD_DIM
GATE_W = 3 * N_HEADS
N_IN = Q_W + 3 * KV_W + GATE_W + 2 * GM_W + POOL_W

LANES = 128
KVP = N_KV_HEADS * HEAD_DIM
VMEM_LIMIT = 56 * 1024 * 1024
NEG = -1e30
LOG2E = 1.4426950408889634
HALO = 16

_OFF_Q = 0
_OFF_KV = Q_W
_OFF_GATE = _OFF_KV + 3 * KV_W
_OFF_UV = _OFF_GATE + 2 * LANES
_OFF_P = _OFF_UV + 2 * GM_W
N_EXT = _OFF_P + POOL_W


def _ln_rows(y, g, b):
    mu = jnp.mean(y, axis=-1, keepdims=True)
    d = y - mu
    var = jnp.mean(d * d, axis=-1, keepdims=True)
    return d * lax.rsqrt(var + LN_EPS) * g + b


def _dot(a, b):
    return jnp.dot(a, b, preferred_element_type=F32)


def _dot_t(a, b):
    return lax.dot_general(a, b, (((1,), (1,)), ((), ())), preferred_element_type=F32)


def _ffn_kernel(x_ref, wg_ref, wu_ref, wo_ref, g_ref, b_ref, o_ref, xb_ref, *, n_chunks):
    j = pl.program_id(1)

    @pl.when(j == 0)
    def _():
        xb_ref[...] = x_ref[...].astype(BF16)

    xb = xb_ref[...]
    gate = _dot(xb, wg_ref[...])
    up = _dot(xb, wu_ref[...])
    hid = (gate * jax.nn.sigmoid(gate)) * up
    part = _dot(hid.astype(BF16), wo_ref[...])

    @pl.when(j == 0)
    def _():
        o_ref[...] = part

    if n_chunks > 2:
        @pl.when((j > 0) & (j < n_chunks - 1))
        def _():
            o_ref[...] += part

    @pl.when(j == n_chunks - 1)
    def _():
        y = ALPHA * x_ref[...] + 0.5 * (o_ref[...] + part)
        o_ref[...] = _ln_rows(y, g_ref[...], b_ref[...])


def _ffn_ln(x, w_in_b, w_out_b, g, b, *, tm, sel):
    rows = x.shape[0]
    n_chunks = 2
    fc = D_FF // n_chunks
    l, w = sel
    return pl.pallas_call(
        functools.partial(_ffn_kernel, n_chunks=n_chunks),
        grid=(rows // tm, n_chunks),
        in_specs=[
            pl.BlockSpec((tm, D_MODEL), lambda i, j: (i, 0)),
            pl.BlockSpec((None, None, D_MODEL, fc), lambda i, j: (l, w, 0, j)),
            pl.BlockSpec((None, None, D_MODEL, fc), lambda i, j: (l, w, 0, n_chunks + j)),
            pl.BlockSpec((None, None, fc, D_MODEL), lambda i, j: (l, w, j, 0)),
            pl.BlockSpec((1, D_MODEL), lambda i, j: (0, 0)),
            pl.BlockSpec((1, D_MODEL), lambda i, j: (0, 0)),
        ],
        out_specs=pl.BlockSpec((tm, D_MODEL), lambda i, j: (i, 0)),
        out_shape=jax.ShapeDtypeStruct((rows, D_MODEL), F32),
        scratch_shapes=[pltpu.VMEM((tm, D_MODEL), BF16)],
        compiler_params=pltpu.CompilerParams(
            dimension_semantics=("parallel", "arbitrary"), vmem_limit_bytes=VMEM_LIMIT),
        name="ffn_ln",
    )(x, w_in_b, w_in_b, w_out_b, g.reshape(1, D_MODEL), b.reshape(1, D_MODEL))


def _inproj_kernel(*refs, prompt, n_alias, q_scale):
    h_ref, w_ref, cos_ref, sin_ref, gmg_ref, gmb_ref = refs[:6]
    outs = refs[6 + n_alias:]
    if prompt:
        (qt_ref, kvcb_ref, leafc_ref, leafs_ref, leafw_ref, ks_ref, vts_ref, kw_ref, vtw_ref,
         gate_ref, gu_ref, gv_ref, p_ref) = outs
        leaves = (leafc_ref, leafs_ref, leafw_ref)
        k_refs = (None, ks_ref, kw_ref)
        vt_refs = (None, vts_ref, vtw_ref)
    else:
        qt_ref, kvc_ref, kvs_ref, kvw_ref, gate_ref, gu_ref, gv_ref, p_ref = outs
        rows_out = (kvc_ref, kvs_ref, kvw_ref)
    hb = h_ref[...].astype(BF16)
    cos = cos_ref[...]
    sin = sin_ref[...]

    zq = _dot(hb, w_ref[:, _OFF_Q:_OFF_KV])
    zk = _dot(hb, w_ref[:, _OFF_KV:_OFF_UV])
    zu = _dot(hb, w_ref[:, _OFF_UV:N_EXT])

    n_sq = hb.shape[0] // LANES
    zeros_half = jnp.zeros((HEAD_DIM, LANES), F32)
    first_half = lax.rem(lax.broadcasted_iota(jnp.int32, (1, LANES), 1), HEAD_DIM) < HEAD_DIM // 2

    def rotary(x):
        turned = jnp.where(first_half, -pltpu.roll(x, LANES - HEAD_DIM // 2, 1), pltpu.roll(x, HEAD_DIM // 2, 1))
        return x * cos + turned * sin

    for m in range(N_HEADS // 2):
        c0 = m * LANES
        pair = rotary(zq[:, c0:c0 + LANES]) * q_scale
        kvh = (2 * m) // GROUP
        for c in range(n_sq):
            pt = pair[c * LANES:(c + 1) * LANES].T
            for e in range(2):
                piece = pt[e * HEAD_DIM:(e + 1) * HEAD_DIM]
                both = [piece, zeros_half] if kvh == 0 else [zeros_half, piece]
                qt_ref[2 * m + e, :, c * LANES:(c + 1) * LANES] = jnp.concatenate(both, axis=0).astype(BF16)

    for br in range(3):
        c0 = br * KV_W
        k = rotary(zk[:, c0:c0 + KVP])
        v = zk[:, c0 + KVP:c0 + 2 * KVP]
        if not prompt:
            rows_out[br][:, 0:KVP] = k
            rows_out[br][:, KVP:2 * KVP] = v
            continue
        for c in range(n_sq):
            cols = slice(c * LANES, (c + 1) * LANES)
            vt = v[cols].T
            leaves[br][0:KVP, cols] = k[cols].T
            leaves[br][KVP:2 * KVP, cols] = vt
            if br > 0:
                vt_refs[br][c] = vt.astype(BF16)
        if br == 0:
            kvcb_ref[0] = k
            kvcb_ref[1] = v
        else:
            k_refs[br][...] = k.astype(BF16)

    gate_ref[...] = jax.nn.sigmoid(zk[:, _OFF_GATE - _OFF_KV:_OFF_GATE - _OFF_KV + LANES])
    gu_ref[...] = jax.nn.gelu(zu[:, 0:GM_W])
    gv_ref[...] = _ln_rows(jax.nn.gelu(zu[:, GM_W:2 * GM_W]), gmg_ref[...], gmb_ref[...])
    p_ref[...] = zu[:, 2 * GM_W:2 * GM_W + POOL_W]


def _inproj(h, w_ext, cos, sin, gmg, gmb, *, tm, leaf_bufs=None, layer=0, batch=None):
    rows = h.shape[0]
    n_tab = cos.shape[0] // tm
    prompt = leaf_bufs is not None
    row_spec = lambda w: pl.BlockSpec((tm, w), lambda i: (i, 0))
    tab_spec = pl.BlockSpec((tm, LANES), lambda i: (i % n_tab, 0))
    vec_spec = pl.BlockSpec((1, GM_W), lambda i: (0, 0))
    sds = jax.ShapeDtypeStruct
    in_specs = [row_spec(D_MODEL), pl.BlockSpec((D_MODEL, N_EXT), lambda i: (0, 0)),
                tab_spec, tab_spec, vec_spec, vec_spec]
    args = [h, w_ext, cos, sin, gmg.reshape(1, GM_W), gmb.reshape(1, GM_W)]
    tail_specs = [row_spec(LANES), row_spec(GM_W), row_spec(GM_W), row_spec(POOL_W)]
    tail_shapes = [sds((rows, LANES), F32), sds((rows, GM_W), F32), sds((rows, GM_W), F32), sds((rows, POOL_W), F32)]
    qt_spec = pl.BlockSpec((N_HEADS, LANES, tm), lambda i: (0, 0, i))
    qt_shape = sds((N_HEADS, LANES, rows), BF16)
    aliases = {}
    if prompt:
        seq = rows // batch
        tiles = seq // tm
        leaf_spec = pl.BlockSpec((None, None, KV_W, tm), lambda i: (layer, i // tiles, 0, i % tiles))
        leaf_shape = sds((DEPTH, batch, KV_W, seq), F32)
        sq_spec = pl.BlockSpec((tm // LANES, KVP, LANES), lambda i: (i, 0, 0))
        sq_shape = sds((rows // LANES, KVP, LANES), BF16)
        out_specs = [qt_spec, pl.BlockSpec((2, tm, KVP), lambda i: (0, i, 0)), leaf_spec, leaf_spec, leaf_spec,
                     row_spec(KVP), sq_spec, row_spec(KVP), sq_spec] + tail_specs
        out_shape = [qt_shape, sds((2, rows, KVP), F32), leaf_shape, leaf_shape, leaf_shape,
                     sds((rows, KVP), BF16), sq_shape, sds((rows, KVP), BF16), sq_shape] + tail_shapes
        for n, buf in enumerate(leaf_bufs):
            in_specs.append(pl.BlockSpec(memory_space=pl.ANY))
            args.append(buf)
            aliases[6 + n] = 2 + n
    else:
        out_specs = [qt_spec, row_spec(KV_W), row_spec(KV_W), row_spec(KV_W)] + tail_specs
        out_shape = [qt_shape, sds((rows, KV_W), F32), sds((rows, KV_W), F32), sds((rows, KV_W), F32)] + tail_shapes
    return pl.pallas_call(
        functools.partial(_inproj_kernel, prompt=prompt, n_alias=len(aliases),
                          q_scale=SCALE * LOG2E if prompt else SCALE),
        grid=(rows // tm,),
        in_specs=in_specs,
        out_specs=out_specs,
        out_shape=out_shape,
        input_output_aliases=aliases,
        compiler_params=pltpu.CompilerParams(dimension_semantics=("parallel",), vmem_limit_bytes=VMEM_LIMIT),
        name="inproj",
    )(*args)


def _build_w_ext(w_in):
    g0 = Q_W + 3 * KV_W
    gate = jnp.pad(w_in[:, g0:g0 + GATE_W], ((0, 0), (0, 2 * LANES - GATE_W)))
    return jnp.concatenate([w_in[:, :g0], gate, w_in[:, g0 + GATE_W:]], axis=1).astype(BF16)


def _rope_tables(pos):
    half = HEAD_DIM // 2
    inv = ROPE_THETA ** (-jnp.arange(half, dtype=F32) / half)
    ang = pos.astype(F32)[:, None] * inv[None, :]
    cos = jnp.tile(jnp.cos(ang), (1, LANES // half))
    sin = jnp.tile(jnp.sin(ang), (1, LANES // half))
    return cos, sin


def _compress_kernel(x_ref, pe_ref, w1_ref, w2_ref, kc_ref, vct_ref):
    nsub = kc_ref.shape[0]
    outs = []
    for kv in range(2):
        xr = jnp.concatenate(
            [x_ref[kv, pl.ds(j, nsub, stride=CMP_STRIDE), :].astype(BF16) for j in range(CMP_STRIDE)], axis=1)
        f = _dot(xr, w1_ref[kv])
        per = _dot(pe_ref[kv].astype(BF16), w1_ref[kv])
        pe_term = per[0:1, 0:KVP] + per[1:2, KVP:2 * KVP]
        nxt = jnp.concatenate([f[1:, KVP:2 * KVP], jnp.zeros((1, KVP), F32)], axis=0)
        hid = jax.nn.gelu(f[:, 0:KVP] + nxt + pe_term)
        outs.append(_dot(hid.astype(BF16), w2_ref[kv]))
    kc_ref[...] = outs[0].astype(BF16)
    for c in range(nsub // LANES):
        vct_ref[:, c * LANES:(c + 1) * LANES] = outs[1][c * LANES:(c + 1) * LANES].T.astype(BF16)


def _compress_prompt(kv_rows, pe2, w1kv, w2kv, *, batch):
    seq = kv_rows.shape[1] // batch
    nsub = seq // CMP_STRIDE
    width = CMP_STRIDE * KVP
    return pl.pallas_call(
        _compress_kernel,
        grid=(batch,),
        in_specs=[pl.BlockSpec((2, seq, KVP), lambda b: (0, b, 0)),
                  pl.BlockSpec((2, 8, width), lambda b: (0, 0, 0)),
                  pl.BlockSpec((2, width, 2 * KVP), lambda b: (0, 0, 0)),
                  pl.BlockSpec((2, KVP, KVP), lambda b: (0, 0, 0))],
        out_specs=[pl.BlockSpec((None, nsub, KVP), lambda b: (b, 0, 0)),
                   pl.BlockSpec((None, KVP, nsub), lambda b: (b, 0, 0))],
        out_shape=[jax.ShapeDtypeStruct((batch, nsub, KVP), BF16), jax.ShapeDtypeStruct((batch, KVP, nsub), BF16)],
        compiler_params=pltpu.CompilerParams(dimension_semantics=("parallel",), vmem_limit_bytes=VMEM_LIMIT),
        name="compress_prompt",
    )(kv_rows, pe2, w1kv, w2kv)


def _top_blocks_cols(score, blk):
    sel = jnp.zeros(score.shape, F32)
    for _ in range(SLC_TOPK):
        m = jnp.max(score, axis=0, keepdims=True)
        first = jnp.min(jnp.where(score == m, blk, 1e9), axis=0, keepdims=True)
        hit = blk == first
        sel = jnp.where(hit, 1.0, sel)
        score = jnp.where(hit, -jnp.inf, score)
    return sel


def _top_blocks_idx(score, blk):
    sel = jnp.zeros(score.shape, F32)
    idx = jnp.zeros((score.shape[0], LANES), jnp.int32)
    lane = lax.broadcasted_iota(jnp.int32, (1, LANES), 1)
    big = jnp.int32(1 << 20)
    for it in range(SLC_TOPK):
        m = jnp.max(score, axis=-1, keepdims=True)
        first = jnp.min(jnp.where(score == m, blk, big), axis=-1, keepdims=True)
        hit = blk == first
        sel = jnp.where(hit, 1.0, sel)
        idx = jnp.where(lane == it, first, idx)
        score = jnp.where(hit, -jnp.inf, score)
    return sel, idx


def _softmax_rows(s, mask):
    s = jnp.where(mask, s, NEG)
    m = jnp.max(s, axis=-1, keepdims=True)
    e = jnp.exp(s - m)
    return jnp.where(mask, e / jnp.sum(e, axis=-1, keepdims=True), 0.0)


def _nsa_prompt_kernel(qt_ref, kc_ref, vct_ref, ks_ref, vts_ref, kw_ref, vtw_ref, gate_ref, selmap_ref, expand_ref,
                       o_ref, *, tq, tk, seq, n_sel_blocks):
    t0 = pl.program_id(1) * tq
    nsub = kc_ref.shape[0]
    nb = 8 * ((n_sel_blocks + 7) // 8)
    span = min(WINDOW + tq, seq)
    qpos = t0 + lax.broadcasted_iota(jnp.int32, (1, tq), 1)
    rep = lambda x, n: jnp.concatenate([x] * n, axis=1)
    qt = jnp.concatenate([qt_ref[h] for h in range(N_HEADS)], axis=1)

    cmp_end = CMP_STRIDE * lax.broadcasted_iota(jnp.int32, (nsub, 1), 0) + (CMP_LEN - 1)
    c_bias = jnp.where(cmp_end <= qpos, 0.0, NEG)
    s = _dot(kc_ref[...], qt) + rep(c_bias, N_HEADS)
    e = jnp.exp2(s - jnp.max(s, axis=0, keepdims=True))
    inv_c = jnp.where(rep(qpos >= CMP_LEN - 1, N_HEADS), 1.0 / jnp.sum(e, axis=0, keepdims=True), 0.0)
    eb = e.astype(BF16)
    o_c = _dot(vct_ref[...], eb) * inv_c
    imp_h = _dot(selmap_ref[...], eb)[0:nb] * inv_c

    imp = []
    for k in range(N_KV_HEADS):
        acc = None
        for g in range(GROUP):
            part = imp_h[:, (GROUP * k + g) * tq:(GROUP * k + g + 1) * tq]
            acc = part if acc is None else acc + part
        imp.append(acc)
    imp = jnp.concatenate(imp, axis=1)
    blk = lax.broadcasted_iota(jnp.int32, (nb, 1), 0)
    cur = rep(qpos // SLC_BLOCK, N_KV_HEADS)
    forced = (blk == 0) | (blk == cur) | (blk == cur - 1)
    score = jnp.where(forced, FORCE_SCORE, jnp.where(blk <= cur, imp, -1.0))
    if nb > n_sel_blocks:
        score = jnp.where(blk < n_sel_blocks, score, -jnp.inf)
    sel = _top_blocks_cols(score, blk.astype(F32))
    sel_m1 = jnp.concatenate([sel - 1.0, jnp.zeros((LANES - nb, N_KV_HEADS * tq), F32)], axis=0).astype(BF16)

    start = pl.multiple_of(jnp.maximum(t0 + tq - span, 0), tq)
    dist = qpos - (start + lax.broadcasted_iota(jnp.int32, (span, 1), 0))
    w_bias = jnp.where(dist >= 0, jnp.where(dist <= WINDOW, 0.0, NEG), NEG)
    s = _dot(kw_ref[pl.ds(start, span), :], qt) + rep(w_bias, N_HEADS)
    e = jnp.exp2(s - jnp.max(s, axis=0, keepdims=True))
    inv_w = 1.0 / jnp.sum(e, axis=0, keepdims=True)
    c0 = start // LANES
    vt = jnp.concatenate([vtw_ref[c0 + c] for c in range(span // LANES)], axis=1)
    o_w = _dot(vt, e.astype(BF16)) * inv_w

    def tile(kt, carry, diagonal, nk=tk):
        m_i, l_i, acc = carry
        r0 = pl.multiple_of(kt * tk, tk)
        bias = _dot(expand_ref[pl.ds(r0, nk), :], sel_m1)
        if diagonal:
            kpos = r0 + lax.broadcasted_iota(jnp.int32, (nk, 1), 0)
            bias = jnp.where(kpos <= rep(qpos, N_KV_HEADS), bias, NEG)
        s = _dot(ks_ref[pl.ds(r0, nk), :], qt)
        s = jnp.concatenate([s[:, h * tq:(h + 1) * tq] + bias[:, (h // GROUP) * tq:(h // GROUP + 1) * tq]
                             for h in range(N_HEADS)], axis=1)
        m_new = jnp.maximum(m_i, jnp.max(s, axis=0, keepdims=True))
        a = jnp.exp2(m_i - m_new)
        e = jnp.exp2(s - m_new)
        l_new = a * l_i + jnp.sum(e, axis=0, keepdims=True)
        c0 = kt * (tk // LANES)
        vt = jnp.concatenate([vts_ref[c0 + c] for c in range(nk // LANES)], axis=1)
        return m_new, l_new, a * acc + _dot(vt, e.astype(BF16))

    n_kt = (t0 + tq + tk - 1) // tk
    init = (jnp.full((1, N_HEADS * tq), NEG, F32), jnp.zeros((1, N_HEADS * tq), F32),
            jnp.zeros((KVP, N_HEADS * tq), F32))
    carry = lax.fori_loop(0, n_kt - 1, lambda kt, c: tile(kt, c, False), init)
    diag = [functools.partial(tile, n_kt - 1, diagonal=True, nk=(v + 1) * tq) for v in range(tk // tq)]
    _, l_s, acc_s = lax.switch(lax.rem(pl.program_id(1), tk // tq), diag, carry)
    o_s = acc_s * (1.0 / l_s)

    gt = gate_ref[...].T
    parts = []
    for h in range(N_HEADS):
        rows = slice((h // GROUP) * HEAD_DIM, (h // GROUP + 1) * HEAD_DIM)
        cols = slice(h * tq, (h + 1) * tq)
        parts.append(gt[3 * h:3 * h + 1] * o_c[rows, cols] + gt[3 * h + 1:3 * h + 2] * o_s[rows, cols]
                     + gt[3 * h + 2:3 * h + 3] * o_w[rows, cols])
    ot = jnp.concatenate(parts, axis=0)
    for m in range(NSA_W // LANES):
        o_ref[:, m * LANES:(m + 1) * LANES] = ot[m * LANES:(m + 1) * LANES].T.astype(o_ref.dtype)


def _sel_map_t(nc_rows, n_cmp, ns):
    c0 = CMP_STRIDE * np.arange(nc_rows)[None, :]
    s0 = SLC_BLOCK * np.arange(LANES)[:, None]
    ov = np.clip(np.minimum(c0 + CMP_LEN, s0 + SLC_BLOCK) - np.maximum(c0, s0), 0, None) / CMP_LEN
    ov = ov * (np.arange(nc_rows)[None, :] < n_cmp) * (np.arange(LANES)[:, None] < ns)
    return jnp.asarray(ov, dtype=BF16)


def _expand_map(seq):
    e = (np.arange(LANES)[None, :] == (np.arange(seq) // SLC_BLOCK)[:, None]).astype(np.float32) * -NEG
    return jnp.asarray(e, dtype=BF16)


def _nsa_prompt(qt, kc, vct, ks, vts, kw, vtw, gates, *, batch, seq, tq=128, tk=1024):
    tk = min(tk, seq)
    nq = seq // tq
    nsub = seq // CMP_STRIDE
    ns = seq // SLC_BLOCK
    assert ns <= LANES and seq % tk == 0 and tk % tq == 0 and tq == LANES
    kern = functools.partial(_nsa_prompt_kernel, tq=tq, tk=tk, seq=seq, n_sel_blocks=ns)
    per_batch = lambda shp: pl.BlockSpec(shp, lambda b, i: (b,) + (0,) * (len(shp) - 1))
    const = lambda shp: pl.BlockSpec(shp, lambda b, i: (0,) * len(shp))
    return pl.pallas_call(
        kern,
        grid=(batch, nq),
        in_specs=[pl.BlockSpec((N_HEADS, KVP, tq), lambda b, i: (0, 0, b * nq + i)),
                  per_batch((None, nsub, KVP)), per_batch((None, KVP, nsub)),
                  per_batch((seq, KVP)), per_batch((seq // LANES, KVP, LANES)),
                  per_batch((seq, KVP)), per_batch((seq // LANES, KVP, LANES)),
                  pl.BlockSpec((tq, LANES), lambda b, i: (b * nq + i, 0)),
                  const((LANES, nsub)), const((seq, LANES))],
        out_specs=pl.BlockSpec((tq, NSA_W), lambda b, i: (b * nq + i, 0)),
        out_shape=jax.ShapeDtypeStruct((batch * seq, NSA_W), BF16),
        compiler_params=pltpu.CompilerParams(
            dimension_semantics=("parallel", "arbitrary"), vmem_limit_bytes=VMEM_LIMIT),
        name="nsa_prompt",
    )(qt, kc, vct, ks, vts, kw, vtw, gates, _sel_map_t(nsub, nsub - 1, ns), _expand_map(seq))


def _pool_windows(z_ext, tm):
    s2 = z_ext[1:] + z_ext[:-1]
    s4 = s2[2:] + s2[:-2]
    s8 = s4[4:] + s4[:-4]
    s16 = s8[8:] + s8[:-8]
    return (s2[HALO - 1:HALO - 1 + tm], s4[HALO - 3:HALO - 3 + tm], s8[HALO - 7:HALO - 7 + tm],
            s16[HALO - 15:HALO - 15 + tm])


def _mix_out_kernel(*refs, tm, tiles_per_seq, pool_in_kernel):
    if pool_in_kernel:
        (x_ref, nsa_ref, gu_ref, gv_ref, p_ref, halo_ref, ws_ref, gb_ref, pw_ref, ps_ref, wo_ref,
         g_ref, b_ref, o_ref) = refs
    else:
        (x_ref, nsa_ref, gu_ref, gv_ref, d_ref, ws_ref, gb_ref, pw_ref, ps_ref, wo_ref,
         g_ref, b_ref, o_ref) = refs
    lane = lax.broadcasted_iota(jnp.int32, (1, GM_W), 1)

    parts = []
    for c in range(tm // GM_CHUNK):
        v = gv_ref[c * GM_CHUNK:(c + 1) * GM_CHUNK, :]
        stacked = jnp.concatenate(
            [jnp.where(lane // HEAD_DIM == h, v, 0.0) for h in range(GM_HEADS)], axis=0).astype(BF16)
        s = _dot(ws_ref[...], stacked) + gb_ref[...]
        parts.append(gu_ref[c * GM_CHUNK:(c + 1) * GM_CHUNK, :] * s)
    o_gm = parts[0] if len(parts) == 1 else jnp.concatenate(parts, axis=0)

    if pool_in_kernel:
        first_tile = (pl.program_id(0) % tiles_per_seq) == 0
        halo = jnp.where(first_tile, 0.0, halo_ref[...])
        z = p_ref[...]
        wins = _pool_windows(jnp.concatenate([halo, z], axis=0), tm)
        pos = (pl.program_id(0) % tiles_per_seq) * tm + lax.broadcasted_iota(jnp.int32, (tm, 1), 0)
        grp = lane // POOL_GW
        wsum = jnp.where(grp == 0, wins[0], jnp.where(grp == 1, wins[1], jnp.where(grp == 2, wins[2], wins[3])))
        width = jnp.where(grp == 0, POOL_WINDOWS[0], jnp.where(grp == 1, POOL_WINDOWS[1],
                          jnp.where(grp == 2, POOL_WINDOWS[2], POOL_WINDOWS[3])))
        cnt = jnp.minimum(width, pos + 1).astype(F32)
        d = wsum / cnt - z
    else:
        d = d_ref[...]
    o_pool = _dot(d.astype(BF16), pw_ref[...]) * ps_ref[...]

    mixed = jnp.concatenate([nsa_ref[...], o_gm.astype(BF16), o_pool.astype(BF16)], axis=1)
    y = ALPHA * x_ref[...] + _dot(mixed, wo_ref[...])
    o_ref[...] = _ln_rows(y, g_ref[...], b_ref[...])


def _mix_out(x, o_nsa, gu, gv, p_or_d, ws_cat, gb_full, pw_big, ps, w_o_b, g, b, *, tm, seq, pool_in_kernel):
    rows = x.shape[0]
    tiles_per_seq = max(seq // tm, 1)
    row_spec = lambda w: pl.BlockSpec((tm, w), lambda i: (i, 0))
    const = lambda shp: pl.BlockSpec(shp, lambda i: (0,) * len(shp))
    in_specs = [row_spec(D_MODEL), row_spec(NSA_W), row_spec(GM_W), row_spec(GM_W), row_spec(POOL_W)]
    args = [x, o_nsa, gu, gv, p_or_d]
    if pool_in_kernel:
        in_specs.append(pl.BlockSpec((HALO, POOL_W), lambda i: (jnp.maximum(i * (tm // HALO) - 1, 0), 0)))
        args.append(p_or_d)
    in_specs += [const((GM_CHUNK, GM_HEADS * GM_CHUNK)), const((GM_CHUNK, GM_W)), const((POOL_W, POOL_W)),
                 const((1, POOL_W)), const((D_MODEL, D_MODEL)), const((1, D_MODEL)), const((1, D_MODEL))]
    args += [ws_cat, gb_full, pw_big, ps.reshape(1, POOL_W), w_o_b, g.reshape(1, D_MODEL), b.reshape(1, D_MODEL)]
    kern = functools.partial(_mix_out_kernel, tm=tm, tiles_per_seq=tiles_per_seq, pool_in_kernel=pool_in_kernel)
    return pl.pallas_call(
        kern,
        grid=(rows // tm,),
        in_specs=in_specs,
        out_specs=row_spec(D_MODEL),
        out_shape=jax.ShapeDtypeStruct((rows, D_MODEL), F32),
        compiler_params=pltpu.CompilerParams(dimension_semantics=("parallel",), vmem_limit_bytes=VMEM_LIMIT),
        name="mix_out_prompt" if pool_in_kernel else "mix_out_sample",
    )(*args)


def _gmlp_weights(ws, gb, chunk_rows, reps):
    wm = jnp.tril(ws[:, :chunk_rows, :chunk_rows])
    bias = gb[:, :chunk_rows]
    if reps > 1:
        eye = jnp.eye(reps, dtype=F32)
        wm = jnp.einsum('hts,ab->hatbs', wm, eye).reshape(GM_HEADS, reps * chunk_rows, reps * chunk_rows)
        bias = jnp.tile(bias, (1, reps))
    ws_cat = wm.transpose(1, 0, 2).reshape(GM_CHUNK, GM_HEADS * GM_CHUNK).astype(BF16)
    gb_full = jnp.repeat(bias.T, HEAD_DIM, axis=1)
    return ws_cat, gb_full


def _pool_weights(pw):
    eye = jnp.eye(POOL_GROUPS, dtype=F32)
    return jnp.einsum('gce,gq->gcqe', pw, eye).reshape(POOL_W, POOL_W).astype(BF16)


PAGES_PER_STEP = 32
SUBS_PER_PAGE = PAGE_SIZE // CMP_STRIDE


def _cmp_sample_kernel(pt_ref, cache_ref, q_ref, pe_ref, w1_ref, w2_ref, selmap_ref, perm_ref, oc_ref, idx_ref,
                       pbuf, sem, xr_ref, fs_ref, *, layer, n_chunks, n_seq, dec_seq, past_len, n_sel_blocks):
    b = pl.program_id(0)
    c = pl.program_id(1)
    step = b * n_chunks + c
    slot = lax.rem(step, 2)
    pps = PAGES_PER_STEP

    def page_copies(sb, sc, sl):
        return [pltpu.make_async_copy(cache_ref.at[layer, pt_ref[sb, sc * pps + p]], pbuf.at[sl, p], sem.at[sl])
                for p in range(pps)]

    @pl.when(step == 0)
    def _():
        for cp in page_copies(b, c, slot):
            cp.start()

    @pl.when(step + 1 < n_seq * n_chunks)
    def _():
        wrap = c + 1 == n_chunks
        for cp in page_copies(jnp.where(wrap, b + 1, b), jnp.where(wrap, 0, c + 1), 1 - slot):
            cp.start()

    for cp in page_copies(b, c, slot):
        cp.wait()

    def to_rows(p, carry):
        r0 = pl.multiple_of(p * SUBS_PER_PAGE, SUBS_PER_PAGE)
        for kv in range(2):
            rows = _dot_t(perm_ref[...], pbuf[slot, p, kv].astype(BF16))
            for j in range(CMP_STRIDE):
                xr_ref[kv, pl.ds(r0, SUBS_PER_PAGE), j * KVP:(j + 1) * KVP] = rows[j * SUBS_PER_PAGE:(j + 1) * SUBS_PER_PAGE]
        return carry

    lax.fori_loop(0, pps, to_rows, 0, unroll=True)

    subs = pps * SUBS_PER_PAGE
    s0 = pl.multiple_of(c * subs, subs)
    for kv in range(2):
        fs_ref[kv, pl.ds(s0, subs), :] = _dot(xr_ref[kv].astype(BF16), w1_ref[kv])

    @pl.when(c == n_chunks - 1)
    def _():
        nsub = n_chunks * subs
        kcv = []
        for kv in range(2):
            f = fs_ref[kv]
            per = _dot(pe_ref[kv].astype(BF16), w1_ref[kv])
            pe_term = per[0:1, 0:KVP] + per[1:2, KVP:2 * KVP]
            nxt = jnp.concatenate([f[1:, KVP:2 * KVP], jnp.zeros((1, KVP), F32)], axis=0)
            hid = jax.nn.gelu(f[:, 0:KVP] + nxt + pe_term)
            kcv.append(_dot(hid.astype(BF16), w2_ref[kv]).astype(BF16))
        n_kt = N_KV_HEADS * dec_seq
        rows = GROUP * n_kt
        q = q_ref[...]
        qpos = past_len + lax.rem(lax.broadcasted_iota(jnp.int32, (rows, 1), 0), dec_seq)
        cmp_end = CMP_STRIDE * lax.broadcasted_iota(jnp.int32, (1, nsub), 1) + (CMP_LEN - 1)
        p = _softmax_rows(_dot_t(q, kcv[0]), cmp_end <= qpos).astype(BF16)
        oc_ref[...] = _dot(p, kcv[1])
        imp_g = _dot(p, selmap_ref[...])
        imp = imp_g[0:n_kt]
        for g in range(1, GROUP):
            imp = imp + imp_g[g * n_kt:(g + 1) * n_kt]
        blk = lax.broadcasted_iota(jnp.int32, (1, imp.shape[1]), 1)
        cur = qpos[0:n_kt] // SLC_BLOCK
        forced = (blk == 0) | (blk == cur) | (blk == cur - 1)
        score = jnp.where(forced, FORCE_SCORE, jnp.where(blk <= cur, imp, -1.0))
        score = jnp.where(blk < n_sel_blocks, score, -jnp.inf)
        idx_ref[...] = _top_blocks_idx(score, blk)[1]


def _cmp_sample(page_table, cache_t, q_gkt, pe2, w1kv, w2kv, *, layer, dec_seq, past_len):
    n_seq, n_pages = page_table.shape
    n_chunks = n_pages // PAGES_PER_STEP
    nsub = n_pages * SUBS_PER_PAGE
    ns = (past_len + dec_seq + SLC_BLOCK - 1) // SLC_BLOCK
    ns_pad = LANES * ((ns + LANES - 1) // LANES)
    n_kt = N_KV_HEADS * dec_seq
    rows = GROUP * n_kt
    c0 = CMP_STRIDE * np.arange(nsub)[:, None]
    s0 = SLC_BLOCK * np.arange(ns_pad)[None, :]
    ov = np.clip(np.minimum(c0 + CMP_LEN, s0 + SLC_BLOCK) - np.maximum(c0, s0), 0, None) / CMP_LEN
    ov = ov * (np.arange(nsub)[:, None] < nsub - 1) * (np.arange(ns_pad)[None, :] < ns)
    selmap = jnp.asarray(ov, dtype=BF16)
    pos = np.arange(PAGE_SIZE)
    perm = jnp.asarray((pos[None, :] == (CMP_STRIDE * (pos % SUBS_PER_PAGE) + pos // SUBS_PER_PAGE)[:, None])
                       .astype(np.float32), dtype=BF16)
    width = CMP_STRIDE * KVP
    kern = functools.partial(_cmp_sample_kernel, layer=layer, n_chunks=n_chunks, n_seq=n_seq, dec_seq=dec_seq,
                             past_len=past_len, n_sel_blocks=ns)
    const = lambda shp: pl.BlockSpec(shp, lambda b, c, pt: (0,) * len(shp))
    grid_spec = pltpu.PrefetchScalarGridSpec(
        num_scalar_prefetch=1,
        grid=(n_seq, n_chunks),
        in_specs=[pl.BlockSpec(memory_space=pl.ANY),
                  pl.BlockSpec((None, rows, LANES), lambda b, c, pt: (b, 0, 0)),
                  const((2, 8, width)), const((2, width, 2 * KVP)), const((2, KVP, KVP)), const((nsub, ns_pad)),
                  const((PAGE_SIZE, PAGE_SIZE))],
        out_specs=[pl.BlockSpec((None, rows, LANES), lambda b, c, pt: (b, 0, 0)),
                   pl.BlockSpec((None, n_kt, LANES), lambda b, c, pt: (b, 0, 0))],
        scratch_shapes=[pltpu.VMEM((2, PAGES_PER_STEP, 2, KVP, PAGE_SIZE), F32),
                        pltpu.SemaphoreType.DMA((2,)),
                        pltpu.VMEM((2, PAGES_PER_STEP * SUBS_PER_PAGE, CMP_STRIDE * KVP), F32),
                        pltpu.VMEM((2, nsub, 2 * KVP), F32)])
    return pl.pallas_call(
        kern,
        grid_spec=grid_spec,
        out_shape=[jax.ShapeDtypeStruct((n_seq, rows, LANES), F32),
                   jax.ShapeDtypeStruct((n_seq, n_kt, LANES), jnp.int32)],
        compiler_params=pltpu.CompilerParams(
            dimension_semantics=("arbitrary", "arbitrary"), vmem_limit_bytes=VMEM_LIMIT),
        name="cmp_sample",
    )(page_table, cache_t, q_gkt, pe2, w1kv, w2kv, selmap, perm)


def _compress_weights_kv(pe, w1, w2):
    eye = jnp.eye(N_KV_HEADS, dtype=F32)
    w1r = w1.reshape(2, 2, CMP_STRIDE, HEAD_DIM, HEAD_DIM)
    w1kv = jnp.einsum('ksjde,hg->kjhdsge', w1r, eye).reshape(2, CMP_STRIDE * KVP, 2 * KVP)
    w2kv = jnp.einsum('ked,hg->khegd', w2, eye).reshape(2, KVP, KVP)
    per = pe.reshape(2, 2, CMP_STRIDE, HEAD_DIM)
    per = jnp.broadcast_to(per[:, :, :, None, :], (2, 2, CMP_STRIDE, N_KV_HEADS, HEAD_DIM))
    pe2 = jnp.pad(per.reshape(2, 2, CMP_STRIDE * KVP), ((0, 0), (0, 6), (0, 0)))
    return pe2, w1kv.astype(BF16), w2kv.astype(BF16)


def _slc_sample_kernel(*refs, layer, n_seq, n_pages, dec_seq, past_len, n_alias):
    (pt_ref, idx_sm_ref, cache_ref, q_ref, idxv_ref, knew_ref, win_ref, wnew_ref, oc_ref, gate_ref, expand_ref,
     wnewt_ref) = refs[:12]
    o_ref, wout_ref, kvbuf, sem = refs[12 + n_alias:]
    b = pl.program_id(0)
    slot = lax.rem(b, 2)
    n_kt = N_KV_HEADS * dec_seq
    rows = GROUP * n_kt
    n_past_blocks = past_len // SLC_BLOCK
    per_head = dec_seq * SLC_TOPK

    def tile_copies(sb, sl, k, i):
        kt = k * dec_seq + i // SLC_TOPK
        s = lax.rem(i, SLC_TOPK)
        j = idx_sm_ref[(sb * n_kt + kt) * SLC_TOPK + s]
        phys = pt_ref[sb, jnp.minimum(lax.shift_right_logical(j, 1), n_pages - 1)]
        return [pltpu.make_async_copy(cache_ref.at[layer, phys, :, pl.ds(k * HEAD_DIM, HEAD_DIM), :],
                                      kvbuf.at[sl, kt, s], sem.at[sl])]

    def start_all(sb, sl):
        for k in range(N_KV_HEADS):
            def body(i, carry):
                for cp in tile_copies(sb, sl, k, i):
                    cp.start()
                return carry
            lax.fori_loop(0, per_head, body, 0, unroll=4)

    @pl.when(b == 0)
    def _():
        start_all(b, slot)

    @pl.when(b + 1 < n_seq)
    def _():
        start_all(b + 1, 1 - slot)

    for k in range(N_KV_HEADS):
        def wait_body(i, carry):
            for cp in tile_copies(b, slot, k, i):
                cp.wait()
            return carry
        lax.fori_loop(0, per_head, wait_body, 0, unroll=4)

    q = q_ref[...]
    qb = q.astype(BF16)
    gates = gate_ref[...]
    t_row = lax.rem(lax.broadcasted_iota(jnp.int32, (rows, 1), 0) // GROUP, dec_seq)
    t_new = lax.broadcasted_iota(jnp.int32, (1, dec_seq), 1)
    new_ok = t_new <= t_row

    wb = win_ref.shape[2]
    kpos = past_len - wb + lax.broadcasted_iota(jnp.int32, (1, wb), 1)
    dist = past_len + t_row - kpos
    w_ok = (dist >= 0) & (dist <= WINDOW) & (kpos >= 0)
    s_w = jnp.where(w_ok, _dot(qb, win_ref[0].astype(BF16)), NEG)
    s_n = jnp.where(new_ok, _dot_t(qb, wnew_ref[:, 0:KVP].astype(BF16)), NEG)
    m = jnp.maximum(jnp.max(s_w, axis=-1, keepdims=True), jnp.max(s_n, axis=-1, keepdims=True))
    e_w = jnp.where(w_ok, jnp.exp(s_w - m), 0.0)
    e_n = jnp.where(new_ok, jnp.exp(s_n - m), 0.0)
    den = jnp.sum(e_w, axis=-1, keepdims=True) + jnp.sum(e_n, axis=-1, keepdims=True)
    o_w = (_dot_t(e_w.astype(BF16), win_ref[1].astype(BF16))
           + _dot(e_n.astype(BF16), wnew_ref[:, KVP:2 * KVP].astype(BF16))) / den

    idxv = idxv_ref[...]
    lane16 = lax.broadcasted_iota(jnp.int32, (1, LANES), 1) < SLC_TOPK
    half = jnp.where(lane16 & ((idxv & 1) == 1), 1.0, 0.0).astype(BF16)
    live = jnp.where(lane16 & (idxv < n_past_blocks), 1.0, 0.0).astype(BF16)
    half_x = _dot(half, expand_ref[...])
    live_x = _dot(live, expand_ref[...])
    col = lax.broadcasted_iota(jnp.int32, (1, SLC_TOPK * PAGE_SIZE), 1)
    col_half = (lax.rem(col, PAGE_SIZE) // SLC_BLOCK).astype(F32)
    tile_ok = (live_x > 0.5) & (half_x == col_half)
    s_new = _dot_t(qb, knew_ref[:, 0:KVP].astype(BF16))
    zeros_half = jnp.zeros((GROUP, HEAD_DIM), F32)
    o_parts = []
    for kt in range(n_kt):
        k = kt // dec_seq
        r0 = kt * GROUP
        qk = q[r0:r0 + GROUP, k * HEAD_DIM:(k + 1) * HEAD_DIM].astype(BF16)
        kcat = jnp.concatenate([kvbuf[slot, kt, s, 0] for s in range(SLC_TOPK)], axis=1).astype(BF16)
        vcat = jnp.concatenate([kvbuf[slot, kt, s, 1] for s in range(SLC_TOPK)], axis=1).astype(BF16)
        ok = tile_ok[kt:kt + 1]
        nok = new_ok[r0:r0 + GROUP]
        s_s = jnp.where(ok, _dot(qk, kcat), NEG)
        s_n = jnp.where(nok, s_new[r0:r0 + GROUP], NEG)
        m = jnp.maximum(jnp.max(s_s, axis=-1, keepdims=True), jnp.max(s_n, axis=-1, keepdims=True))
        e_s = jnp.where(ok, jnp.exp(s_s - m), 0.0)
        e_n = jnp.where(nok, jnp.exp(s_n - m), 0.0)
        den = jnp.sum(e_s, axis=-1, keepdims=True) + jnp.sum(e_n, axis=-1, keepdims=True)
        v_new = knew_ref[:, KVP + k * HEAD_DIM:KVP + (k + 1) * HEAD_DIM].astype(BF16)
        o = (_dot_t(e_s.astype(BF16), vcat) + _dot(e_n.astype(BF16), v_new)) / den
        o_parts.append(jnp.concatenate([o, zeros_half] if k == 0 else [zeros_half, o], axis=1))
    o_s = jnp.concatenate(o_parts, axis=0)

    o_ref[...] = gates[:, 0:1] * oc_ref[...] + gates[:, 1:2] * o_s + gates[:, 2:3] * o_w

    lane_w = lax.broadcasted_iota(jnp.int32, (1, wb), 1)
    for kv in range(2):
        moved = pltpu.roll(win_ref[kv], wb - dec_seq, 1)
        for t in range(dec_seq):
            moved = jnp.where(lane_w == wb - dec_seq + t, wnewt_ref[kv * KVP:(kv + 1) * KVP, t:t + 1], moved)
        wout_ref[kv] = moved


def _slc_sample(page_table, idx_flat, cache_t, q_ktg, idxv, kvs_new, win_t, kvw_new, o_c, gates_r, win_out,
                *, layer, dec_seq, past_len):
    n_seq, n_pages = page_table.shape
    n_kt = N_KV_HEADS * dec_seq
    rows = GROUP * n_kt
    wb = win_t.shape[-1]
    cols = SLC_TOPK * PAGE_SIZE
    expand = jnp.asarray((np.arange(LANES)[:, None] == (np.arange(cols) // PAGE_SIZE)[None, :]).astype(np.float32),
                         dtype=BF16)
    n_alias = 0 if win_out is None else 1
    kern = functools.partial(_slc_sample_kernel, layer=layer, n_seq=n_seq, n_pages=n_pages, dec_seq=dec_seq,
                             past_len=past_len, n_alias=n_alias)
    per_seq = lambda r, w: pl.BlockSpec((None, r, w), lambda b, pt, ix: (b, 0, 0))
    state_spec = pl.BlockSpec((None, None, 2, KVP, wb), lambda b, pt, ix: (layer, b, 0, 0, 0))
    in_specs = [pl.BlockSpec(memory_space=pl.ANY),
                per_seq(rows, LANES), per_seq(n_kt, LANES), per_seq(dec_seq, KV_W), state_spec,
                per_seq(dec_seq, KV_W), per_seq(rows, LANES), per_seq(rows, LANES),
                pl.BlockSpec((LANES, cols), lambda b, pt, ix: (0, 0)), per_seq(KV_W, dec_seq)]
    args = [page_table, idx_flat, cache_t, q_ktg, idxv, kvs_new, win_t, kvw_new, o_c, gates_r, expand,
            jnp.swapaxes(kvw_new, 1, 2)]
    if n_alias:
        in_specs.append(pl.BlockSpec(memory_space=pl.ANY))
        args.append(win_out)
    grid_spec = pltpu.PrefetchScalarGridSpec(
        num_scalar_prefetch=2,
        grid=(n_seq,),
        in_specs=in_specs,
        out_specs=[per_seq(rows, LANES), state_spec],
        scratch_shapes=[pltpu.VMEM((2, n_kt, SLC_TOPK, 2, HEAD_DIM, PAGE_SIZE), F32),
                        pltpu.SemaphoreType.DMA((2,))])
    return pl.pallas_call(
        kern,
        grid_spec=grid_spec,
        out_shape=[jax.ShapeDtypeStruct((n_seq, rows, LANES), F32), jax.ShapeDtypeStruct(win_t.shape, F32)],
        input_output_aliases={len(args) - 1: 1} if n_alias else {},
        compiler_params=pltpu.CompilerParams(dimension_semantics=("arbitrary",), vmem_limit_bytes=VMEM_LIMIT),
        name="slc_sample",
    )(*args)


def _pool_sample_kernel(z_ref, d_ref, *, dec_seq, pos0):
    lane = lax.broadcasted_iota(jnp.int32, (1, POOL_W), 1)
    grp = lane // POOL_GW
    for t in range(dec_seq):
        cur = z_ref[:, POOL_HIST + t, :]
        acc = cur
        sums = {}
        for back in range(1, max(POOL_WINDOWS)):
            acc = acc + z_ref[:, POOL_HIST + t - back, :]
            if back + 1 in POOL_WINDOWS:
                sums[back + 1] = acc
        d = None
        for g, w in enumerate(POOL_WINDOWS):
            val = sums[w] / float(min(w, pos0 + t + 1)) - cur
            d = val if d is None else jnp.where(grp == g, val, d)
        d_ref[:, t, :] = d


def _pool_sample(z_ext, *, dec_seq, pos0):
    n_seq = z_ext.shape[0]
    return pl.pallas_call(
        functools.partial(_pool_sample_kernel, dec_seq=dec_seq, pos0=pos0),
        out_shape=jax.ShapeDtypeStruct((n_seq, dec_seq, POOL_W), F32),
        name="pool_sample",
    )(z_ext)


def _sample_layer(x, lw, cmp_t, slc_t, win_t, win_out, pool_state, page_table, *, layer, n_seq, dec_seq):
    rows = n_seq * dec_seq
    n_kt = N_KV_HEADS * dec_seq
    x = _ffn_ln(x, lw['ffn_in'], lw['ffn_out'], lw['ln_g'][0], lw['ln_b'][0], sel=(lw['layer'], 0),tm=rows)
    cos, sin = _rope_tables(PAST_LEN + jnp.arange(rows) % dec_seq)
    qt, kvc, kvs, kvw, gates, gu, gv, p = _inproj(
        x, lw['w_ext'], cos, sin, lw['gm_ln_g'], lw['gm_ln_b'], tm=rows)

    qf = jnp.swapaxes(qt.astype(F32), 1, 2).reshape(N_KV_HEADS, GROUP, n_seq, dec_seq, LANES)
    q_gkt = qf.transpose(2, 1, 0, 3, 4).reshape(n_seq, GROUP * n_kt, LANES).astype(BF16)
    q_ktg = qf.transpose(2, 0, 3, 1, 4).reshape(n_seq, GROUP * n_kt, LANES)
    o_c, idxv = _cmp_sample(page_table, cmp_t, q_gkt, lw['pe2'], lw['w1kv'], lw['w2kv'],
                            layer=layer, dec_seq=dec_seq, past_len=PAST_LEN)
    o_c = o_c.reshape(n_seq, GROUP, N_KV_HEADS, dec_seq, LANES).transpose(0, 2, 3, 1, 4).reshape(n_seq, GROUP * n_kt, LANES)
    gates_r = gates[:, :GATE_W].reshape(n_seq, dec_seq, N_KV_HEADS, GROUP, 3).transpose(0, 2, 1, 3, 4)
    gates_r = jnp.pad(gates_r.reshape(n_seq, GROUP * n_kt, 3), ((0, 0), (0, 0), (0, LANES - 3)))
    idx_flat = idxv[:, :, :SLC_TOPK].reshape(-1)
    o, win_out = _slc_sample(page_table, idx_flat, slc_t, q_ktg, idxv, kvs.reshape(n_seq, dec_seq, KV_W), win_t,
                             kvw.reshape(n_seq, dec_seq, KV_W), o_c, gates_r, win_out,
                             layer=layer, dec_seq=dec_seq, past_len=PAST_LEN)
    o = o.reshape(n_seq, N_KV_HEADS, dec_seq, GROUP, LANES)
    o_nsa = jnp.stack([o[:, k, :, :, k * HEAD_DIM:(k + 1) * HEAD_DIM] for k in range(N_KV_HEADS)], axis=2)
    o_nsa = o_nsa.reshape(rows, NSA_W).astype(BF16)

    z_ext = jnp.concatenate([pool_state, p.reshape(n_seq, dec_seq, POOL_W)], axis=1)
    d = _pool_sample(z_ext, dec_seq=dec_seq, pos0=PAST_LEN).reshape(rows, POOL_W)
    ws_cat, gb_full = _gmlp_weights(lw['gm_ws'], lw['gm_b'], dec_seq, GM_CHUNK // dec_seq)
    x = _mix_out(x, o_nsa, gu, gv, d, ws_cat, gb_full, lw['pw_big'], lw['pool_scale'], lw['w_o'],
                 lw['ln_g'][1], lw['ln_b'][1], tm=rows, seq=dec_seq, pool_in_kernel=False)
    x = _ffn_ln(x, lw['ffn_in'], lw['ffn_out'], lw['ln_g'][2], lw['ln_b'][2], sel=(lw['layer'], 1),tm=rows)
    shp = (n_seq, dec_seq, 2, N_KV_HEADS, HEAD_DIM)
    new = (kvc.reshape(shp), kvs.reshape(shp), z_ext[:, dec_seq:], gv.reshape(n_seq, dec_seq, GM_W))
    return x, new, win_out


def _pages_by_channel(cache):
    nd = cache.ndim
    t = jnp.transpose(cache, tuple(range(nd - 4)) + (nd - 3, nd - 2, nd - 1, nd - 4))
    return t.reshape(t.shape[:-3] + (KVP, t.shape[-1]))
def _prompt_layer(x, lw, leaf_bufs, *, layer, batch, seq, tm, tm_ffn):
    x = _ffn_ln(x, lw['ffn_in'], lw['ffn_out'], lw['ln_g'][0], lw['ln_b'][0], sel=(lw['layer'], 0),tm=tm_ffn)
    cos, sin = _rope_tables(jnp.arange(seq))
    qt, kvc, leaf_c, leaf_s, leaf_w, ks, vts, kw, vtw, gates, gu, gv, p = _inproj(
        x, lw['w_ext'], cos, sin, lw['gm_ln_g'], lw['gm_ln_b'], tm=tm, leaf_bufs=leaf_bufs, layer=layer, batch=batch)
    kc, vct = _compress_prompt(kvc, lw['pe2'], lw['w1kv'], lw['w2kv'], batch=batch)
    o_nsa = _nsa_prompt(qt, kc, vct, ks, vts, kw, vtw, gates, batch=batch, seq=seq)
    ws_cat, gb_full = _gmlp_weights(lw['gm_ws'], lw['gm_b'], GM_CHUNK, 1)
    x = _mix_out(x, o_nsa, gu, gv, p, ws_cat, gb_full, lw['pw_big'], lw['pool_scale'], lw['w_o'],
                 lw['ln_g'][1], lw['ln_b'][1], tm=tm, seq=seq, pool_in_kernel=True)
    x = _ffn_ln(x, lw['ffn_in'], lw['ffn_out'], lw['ln_g'][2], lw['ln_b'][2], sel=(lw['layer'], 1),tm=tm_ffn)
    return x, (leaf_c, leaf_s, leaf_w), p.reshape(batch, seq, POOL_W)[:, seq - POOL_HIST:]


def _leaf_rows(buf):
    d, b, _, t = buf.shape
    return buf.reshape(d, b, 2, N_KV_HEADS, HEAD_DIM, t).transpose(0, 1, 5, 2, 3, 4)


def _layer_weights(l, ffn_in_b, ffn_out_b, ln_g, ln_b, w_in, w_o, cmp_pe, cmp_w1, cmp_w2,
                   gm_ln_g, gm_ln_b, gm_ws, gm_b, pool_w, pool_scale):
    pe2, w1kv, w2kv = _compress_weights_kv(cmp_pe[l], cmp_w1[l], cmp_w2[l])
    return dict(layer=l, ffn_in=ffn_in_b, ffn_out=ffn_out_b, ln_g=ln_g[l], ln_b=ln_b[l],
                w_ext=_build_w_ext(w_in[l]), w_o=w_o[l].astype(BF16),
                pe2=pe2, w1kv=w1kv, w2kv=w2kv,
                gm_ln_g=gm_ln_g[l], gm_ln_b=gm_ln_b[l], gm_ws=gm_ws[l], gm_b=gm_b[l],
                pw_big=_pool_weights(pool_w[l]), pool_scale=pool_scale[l])


def kernel(x_prompt, x_sample, cache_kv_cmp, cache_kv_slc, state_kv_win, state_pool, page_table, ln_g, ln_b, ffn_w_in, ffn_w_out, w_in, w_o, cmp_pe, cmp_w1, cmp_w2, gm_ln_g, gm_ln_b, gm_ws, gm_b, pool_w, pool_scale):
    batch, seq, _ = x_prompt.shape
    fi = ffn_w_in.astype(BF16)
    fo = ffn_w_out.astype(BF16)
    n_seq, dec_seq, _ = x_sample.shape
    xp = x_prompt.reshape(batch * seq, D_MODEL)
    xs = x_sample.reshape(n_seq * dec_seq, D_MODEL)
    cmp_t = _pages_by_channel(cache_kv_cmp)
    slc_t = _pages_by_channel(cache_kv_slc)
    win_t = _pages_by_channel(state_kv_win)
    leaf_bufs = tuple(jnp.zeros((DEPTH, batch, KV_W, seq), F32) for _ in range(3))
    win_out = jnp.zeros(win_t.shape, F32)
    pool_p, new_s = [], []
    for l in range(DEPTH):
        lw = _layer_weights(l, fi, fo, ln_g, ln_b, w_in, w_o, cmp_pe, cmp_w1, cmp_w2, gm_ln_g, gm_ln_b, gm_ws, gm_b, pool_w, pool_scale)
        xp, leaf_bufs, pool_l = _prompt_layer(xp, lw, leaf_bufs, layer=l, batch=batch, seq=seq, tm=512, tm_ffn=1024)
        xs, st_s, win_out = _sample_layer(xs, lw, cmp_t, slc_t, win_t, win_out, state_pool[l], page_table,
                                          layer=l, n_seq=n_seq, dec_seq=dec_seq)
        pool_p.append(pool_l)
        new_s.append(st_s)
    stk = lambda lst, i: jnp.stack([t[i] for t in lst])
    wb = min(WINDOW, seq)
    win_s = _leaf_rows(win_out.reshape(DEPTH, n_seq, KV_W, win_out.shape[-1]))
    return (xp.reshape(batch, seq, D_MODEL), xs.reshape(n_seq, dec_seq, D_MODEL),
            _leaf_rows(leaf_bufs[0]), stk(new_s, 0), _leaf_rows(leaf_bufs[1]), stk(new_s, 1),
            _leaf_rows(leaf_bufs[2])[:, :, seq - wb:], win_s, jnp.stack(pool_p), stk(new_s, 2), stk(new_s, 3))
```

```python
import functools

import numpy as np
import jax
import jax.numpy as jnp
from jax import lax
from jax.experimental import pallas as pl
from jax.experimental.pallas import tpu as pltpu

F32 = jnp.float32
BF16 = jnp.bfloat16

D_MODEL = 1024
DEPTH = 2
PAST_LEN = 16384
PAGE_SIZE = 128
HEAD_DIM = 64
NSA_W = D_MODEL // 2
GM_W = D_MODEL // 4
POOL_W = D_MODEL // 4
N_HEADS = NSA_W // HEAD_DIM
N_KV_HEADS = 2
GROUP = N_HEADS // N_KV_HEADS
CMP_STRIDE = 16
CMP_LEN = 2 * CMP_STRIDE
SLC_BLOCK = 64
SLC_TOPK = 16
WINDOW = 512
FORCE_SCORE = 1.0e4
ROPE_THETA = 10000.0
SCALE = HEAD_DIM ** -0.5
GM_HEADS = GM_W // HEAD_DIM
GM_CHUNK = 128
POOL_GROUPS = 4
POOL_GW = POOL_W // POOL_GROUPS
POOL_WINDOWS = (2, 4, 8, 16)
POOL_HIST = max(POOL_WINDOWS) - 1
D_FF = 256 * ((8 * D_MODEL // 3 + 255) // 256)
ALPHA = (2 * DEPTH) ** 0.25
LN_EPS = 1e-5
Q_W = N_HEADS * HEAD_DIM
KV_W = 2 * N_KV_HEADS * HEAD_DIM
GATE_W = 3 * N_HEADS
N_IN = Q_W + 3 * KV_W + GATE_W + 2 * GM_W + POOL_W

LANES = 128
KVP = N_KV_HEADS * HEAD_DIM
VMEM_LIMIT = 56 * 1024 * 1024
NEG = -1e30
LOG2E = 1.4426950408889634
HALO = 16

_OFF_Q = 0
_OFF_KV = Q_W
_OFF_GATE = _OFF_KV + 3 * KV_W
_OFF_UV = _OFF_GATE + 2 * LANES
_OFF_P = _OFF_UV + 2 * GM_W
N_EXT = _OFF_P + POOL_W


def _ln_rows(y, g, b):
    mu = jnp.mean(y, axis=-1, keepdims=True)
    d = y - mu
    var = jnp.mean(d * d, axis=-1, keepdims=True)
    return d * lax.rsqrt(var + LN_EPS) * g + b


def _dot(a, b):
    return jnp.dot(a, b, preferred_element_type=F32)


def _dot_t(a, b):
    return lax.dot_general(a, b, (((1,), (1,)), ((), ())), preferred_element_type=F32)


def _ffn_kernel(x_ref, wg_ref, wu_ref, wo_ref, g_ref, b_ref, o_ref, xb_ref, *, n_chunks):
    j = pl.program_id(1)

    @pl.when(j == 0)
    def _():
        xb_ref[...] = x_ref[...].astype(BF16)

    xb = xb_ref[...]
    gate = _dot(xb, wg_ref[...])
    up = _dot(xb, wu_ref[...])
    hid = (gate * jax.nn.sigmoid(gate)) * up
    part = _dot(hid.astype(BF16), wo_ref[...])

    @pl.when(j == 0)
    def _():
        o_ref[...] = part

    if n_chunks > 2:
        @pl.when((j > 0) & (j < n_chunks - 1))
        def _():
            o_ref[...] += part

    @pl.when(j == n_chunks - 1)
    def _():
        y = ALPHA * x_ref[...] + 0.5 * (o_ref[...] + part)
        o_ref[...] = _ln_rows(y, g_ref[...], b_ref[...])


def _ffn_ln(x, w_in_b, w_out_b, g, b, *, tm, sel):
    rows = x.shape[0]
    n_chunks = 2
    fc = D_FF // n_chunks
    l, w = sel
    return pl.pallas_call(
        functools.partial(_ffn_kernel, n_chunks=n_chunks),
        grid=(rows // tm, n_chunks),
        in_specs=[
            pl.BlockSpec((tm, D_MODEL), lambda i, j: (i, 0)),
            pl.BlockSpec((None, None, D_MODEL, fc), lambda i, j: (l, w, 0, j)),
            pl.BlockSpec((None, None, D_MODEL, fc), lambda i, j: (l, w, 0, n_chunks + j)),
            pl.BlockSpec((None, None, fc, D_MODEL), lambda i, j: (l, w, j, 0)),
            pl.BlockSpec((1, D_MODEL), lambda i, j: (0, 0)),
            pl.BlockSpec((1, D_MODEL), lambda i, j: (0, 0)),
        ],
        out_specs=pl.BlockSpec((tm, D_MODEL), lambda i, j: (i, 0)),
        out_shape=jax.ShapeDtypeStruct((rows, D_MODEL), F32),
        scratch_shapes=[pltpu.VMEM((tm, D_MODEL), BF16)],
        compiler_params=pltpu.CompilerParams(
            dimension_semantics=("parallel", "arbitrary"), vmem_limit_bytes=VMEM_LIMIT),
        name="ffn_ln",
    )(x, w_in_b, w_in_b, w_out_b, g.reshape(1, D_MODEL), b.reshape(1, D_MODEL))


def _inproj_kernel(*refs, prompt, n_alias, q_scale):
    h_ref, w_ref, cos_ref, sin_ref, gmg_ref, gmb_ref = refs[:6]
    outs = refs[6 + n_alias:]
    if prompt:
        (qt_ref, kvcb_ref, leafc_ref, leafs_ref, leafw_ref, ks_ref, vts_ref, kw_ref, vtw_ref,
         gate_ref, gu_ref, gv_ref, p_ref) = outs
        leaves = (leafc_ref, leafs_ref, leafw_ref)
        k_refs = (None, ks_ref, kw_ref)
        vt_refs = (None, vts_ref, vtw_ref)
    else:
        qt_ref, kvc_ref, kvs_ref, kvw_ref, gate_ref, gu_ref, gv_ref, p_ref = outs
        rows_out = (kvc_ref, kvs_ref, kvw_ref)
    hb = h_ref[...].astype(BF16)
    cos = cos_ref[...]
    sin = sin_ref[...]

    zq = _dot(hb, w_ref[:, _OFF_Q:_OFF_KV])
    zk = _dot(hb, w_ref[:, _OFF_KV:_OFF_UV])
    zu = _dot(hb, w_ref[:, _OFF_UV:N_EXT])

    n_sq = hb.shape[0] // LANES
    zeros_half = jnp.zeros((HEAD_DIM, LANES), F32)
    first_half = lax.rem(lax.broadcasted_iota(jnp.int32, (1, LANES), 1), HEAD_DIM) < HEAD_DIM // 2

    def rotary(x):
        turned = jnp.where(first_half, -pltpu.roll(x, LANES - HEAD_DIM // 2, 1), pltpu.roll(x, HEAD_DIM // 2, 1))
        return x * cos + turned * sin

    for m in range(N_HEADS // 2):
        c0 = m * LANES
        pair = rotary(zq[:, c0:c0 + LANES]) * q_scale
        kvh = (2 * m) // GROUP
        for c in range(n_sq):
            pt = pair[c * LANES:(c + 1) * LANES].T
            for e in range(2):
                piece = pt[e * HEAD_DIM:(e + 1) * HEAD_DIM]
                both = [piece, zeros_half] if kvh == 0 else [zeros_half, piece]
                qt_ref[2 * m + e, :, c * LANES:(c + 1) * LANES] = jnp.concatenate(both, axis=0).astype(BF16)

    for br in range(3):
        c0 = br * KV_W
        k = rotary(zk[:, c0:c0 + KVP])
        v = zk[:, c0 + KVP:c0 + 2 * KVP]
        if not prompt:
            rows_out[br][:, 0:KVP] = k
            rows_out[br][:, KVP:2 * KVP] = v
            continue
        for c in range(n_sq):
            cols = slice(c * LANES, (c + 1) * LANES)
            vt = v[cols].T
            leaves[br][0:KVP, cols] = k[cols].T
            leaves[br][KVP:2 * KVP, cols] = vt
            if br > 0:
                vt_refs[br][c] = vt.astype(BF16)
        if br == 0:
            kvcb_ref[0] = k
            kvcb_ref[1] = v
        else:
            k_refs[br][...] = k.astype(BF16)

    gate_ref[...] = jax.nn.sigmoid(zk[:, _OFF_GATE - _OFF_KV:_OFF_GATE - _OFF_KV + LANES])
    gu_ref[...] = jax.nn.gelu(zu[:, 0:GM_W])
    gv_ref[...] = _ln_rows(jax.nn.gelu(zu[:, GM_W:2 * GM_W]), gmg_ref[...], gmb_ref[...])
    p_ref[...] = zu[:, 2 * GM_W:2 * GM_W + POOL_W]


def _inproj(h, w_ext, cos, sin, gmg, gmb, *, tm, leaf_bufs=None, layer=0, batch=None):
    rows = h.shape[0]
    n_tab = cos.shape[0] // tm
    prompt = leaf_bufs is not None
    row_spec = lambda w: pl.BlockSpec((tm, w), lambda i: (i, 0))
    tab_spec = pl.BlockSpec((tm, LANES), lambda i: (i % n_tab, 0))
    vec_spec = pl.BlockSpec((1, GM_W), lambda i: (0, 0))
    sds = jax.ShapeDtypeStruct
    in_specs = [row_spec(D_MODEL), pl.BlockSpec((D_MODEL, N_EXT), lambda i: (0, 0)),
                tab_spec, tab_spec, vec_spec, vec_spec]
    args = [h, w_ext, cos, sin, gmg.reshape(1, GM_W), gmb.reshape(1, GM_W)]
    tail_specs = [row_spec(LANES), row_spec(GM_W), row_spec(GM_W), row_spec(POOL_W)]
    tail_shapes = [sds((rows, LANES), F32), sds((rows, GM_W), F32), sds((rows, GM_W), F32), sds((rows, POOL_W), F32)]
    qt_spec = pl.BlockSpec((N_HEADS, LANES, tm), lambda i: (0, 0, i))
    qt_shape = sds((N_HEADS, LANES, rows), BF16)
    aliases = {}
    if prompt:
        seq = rows // batch
        tiles = seq // tm
        leaf_spec = pl.BlockSpec((None, None, KV_W, tm), lambda i: (layer, i // tiles, 0, i % tiles))
        leaf_shape = sds((DEPTH, batch, KV_W, seq), F32)
        sq_spec = pl.BlockSpec((tm // LANES, KVP, LANES), lambda i: (i, 0, 0))
        sq_shape = sds((rows // LANES, KVP, LANES), BF16)
        out_specs = [qt_spec, pl.BlockSpec((2, tm, KVP), lambda i: (0, i, 0)), leaf_spec, leaf_spec, leaf_spec,
                     row_spec(KVP), sq_spec, row_spec(KVP), sq_spec] + tail_specs
        out_shape = [qt_shape, sds((2, rows, KVP), F32), leaf_shape, leaf_shape, leaf_shape,
                     sds((rows, KVP), BF16), sq_shape, sds((rows, KVP), BF16), sq_shape] + tail_shapes
        for n, buf in enumerate(leaf_bufs):
            in_specs.append(pl.BlockSpec(memory_space=pl.ANY))
            args.append(buf)
            aliases[6 + n] = 2 + n
    else:
        out_specs = [qt_spec, row_spec(KV_W), row_spec(KV_W), row_spec(KV_W)] + tail_specs
        out_shape = [qt_shape, sds((rows, KV_W), F32), sds((rows, KV_W), F32), sds((rows, KV_W), F32)] + tail_shapes
    return pl.pallas_call(
        functools.partial(_inproj_kernel, prompt=prompt, n_alias=len(aliases),
                          q_scale=SCALE * LOG2E if prompt else SCALE),
        grid=(rows // tm,),
        in_specs=in_specs,
        out_specs=out_specs,
        out_shape=out_shape,
        input_output_aliases=aliases,
        compiler_params=pltpu.CompilerParams(dimension_semantics=("parallel",), vmem_limit_bytes=VMEM_LIMIT),
        name="inproj",
    )(*args)


def _build_w_ext(w_in):
    g0 = Q_W + 3 * KV_W
    gate = jnp.pad(w_in[:, g0:g0 + GATE_W], ((0, 0), (0, 2 * LANES - GATE_W)))
    return jnp.concatenate([w_in[:, :g0], gate, w_in[:, g0 + GATE_W:]], axis=1).astype(BF16)


def _rope_tables(pos):
    half = HEAD_DIM // 2
    inv = ROPE_THETA ** (-jnp.arange(half, dtype=F32) / half)
    ang = pos.astype(F32)[:, None] * inv[None, :]
    cos = jnp.tile(jnp.cos(ang), (1, LANES // half))
    sin = jnp.tile(jnp.sin(ang), (1, LANES // half))
    return cos, sin


def _compress_kernel(x_ref, pe_ref, w1_ref, w2_ref, kc_ref, vct_ref):
    nsub = kc_ref.shape[0]
    outs = []
    for kv in range(2):
        xr = jnp.concatenate(
            [x_ref[kv, pl.ds(j, nsub, stride=CMP_STRIDE), :].astype(BF16) for j in range(CMP_STRIDE)], axis=1)
        f = _dot(xr, w1_ref[kv])
        per = _dot(pe_ref[kv].astype(BF16), w1_ref[kv])
        pe_term = per[0:1, 0:KVP] + per[1:2, KVP:2 * KVP]
        nxt = jnp.concatenate([f[1:, KVP:2 * KVP], jnp.zeros((1, KVP), F32)], axis=0)
        hid = jax.nn.gelu(f[:, 0:KVP] + nxt + pe_term)
        outs.append(_dot(hid.astype(BF16), w2_ref[kv]))
    kc_ref[...] = outs[0].astype(BF16)
    for c in range(nsub // LANES):
        vct_ref[:, c * LANES:(c + 1) * LANES] = outs[1][c * LANES:(c + 1) * LANES].T.astype(BF16)


def _compress_prompt(kv_rows, pe2, w1kv, w2kv, *, batch):
    seq = kv_rows.shape[1] // batch
    nsub = seq // CMP_STRIDE
    width = CMP_STRIDE * KVP
    return pl.pallas_call(
        _compress_kernel,
        grid=(batch,),
        in_specs=[pl.BlockSpec((2, seq, KVP), lambda b: (0, b, 0)),
                  pl.BlockSpec((2, 8, width), lambda b: (0, 0, 0)),
                  pl.BlockSpec((2, width, 2 * KVP), lambda b: (0, 0, 0)),
                  pl.BlockSpec((2, KVP, KVP), lambda b: (0, 0, 0))],
        out_specs=[pl.BlockSpec((None, nsub, KVP), lambda b: (b, 0, 0)),
                   pl.BlockSpec((None, KVP, nsub), lambda b: (b, 0, 0))],
        out_shape=[jax.ShapeDtypeStruct((batch, nsub, KVP), BF16), jax.ShapeDtypeStruct((batch, KVP, nsub), BF16)],
        compiler_params=pltpu.CompilerParams(dimension_semantics=("parallel",), vmem_limit_bytes=VMEM_LIMIT),
        name="compress_prompt",
    )(kv_rows, pe2, w1kv, w2kv)


def _top_blocks_cols(score, blk):
    sel = jnp.zeros(score.shape, F32)
    for _ in range(SLC_TOPK):
        m = jnp.max(score, axis=0, keepdims=True)
        first = jnp.min(jnp.where(score == m, blk, 1e9), axis=0, keepdims=True)
        hit = blk == first
        sel = jnp.where(hit, 1.0, sel)
        score = jnp.where(hit, -jnp.inf, score)
    return sel


def _top_blocks_idx(score, blk):
    sel = jnp.zeros(score.shape, F32)
    idx = jnp.zeros((score.shape[0], LANES), jnp.int32)
    lane = lax.broadcasted_iota(jnp.int32, (1, LANES), 1)
    big = jnp.int32(1 << 20)
    for it in range(SLC_TOPK):
        m = jnp.max(score, axis=-1, keepdims=True)
        first = jnp.min(jnp.where(score == m, blk, big), axis=-1, keepdims=True)
        hit = blk == first
        sel = jnp.where(hit, 1.0, sel)
        idx = jnp.where(lane == it, first, idx)
        score = jnp.where(hit, -jnp.inf, score)
    return sel, idx


def _softmax_rows(s, mask):
    s = jnp.where(mask, s, NEG)
    m = jnp.max(s, axis=-1, keepdims=True)
    e = jnp.exp(s - m)
    return jnp.where(mask, e / jnp.sum(e, axis=-1, keepdims=True), 0.0)


def _nsa_prompt_kernel(qt_ref, kc_ref, vct_ref, ks_ref, vts_ref, kw_ref, vtw_ref, gate_ref, selmap_ref, expand_ref,
                       o_ref, *, tq, tk, seq, n_sel_blocks):
    t0 = pl.program_id(1) * tq
    nsub = kc_ref.shape[0]
    nb = 8 * ((n_sel_blocks + 7) // 8)
    span = min(WINDOW + tq, seq)
    qpos = t0 + lax.broadcasted_iota(jnp.int32, (1, tq), 1)
    rep = lambda x, n: jnp.concatenate([x] * n, axis=1)
    qt = jnp.concatenate([qt_ref[h] for h in range(N_HEADS)], axis=1)

    cmp_end = CMP_STRIDE * lax.broadcasted_iota(jnp.int32, (nsub, 1), 0) + (CMP_LEN - 1)
    c_bias = jnp.where(cmp_end <= qpos, 0.0, NEG)
    s = _dot(kc_ref[...], qt) + rep(c_bias, N_HEADS)
    e = jnp.exp2(s - jnp.max(s, axis=0, keepdims=True))
    inv_c = jnp.where(rep(qpos >= CMP_LEN - 1, N_HEADS), 1.0 / jnp.sum(e, axis=0, keepdims=True), 0.0)
    eb = e.astype(BF16)
    o_c = _dot(vct_ref[...], eb) * inv_c
    imp_h = _dot(selmap_ref[...], eb)[0:nb] * inv_c

    imp = []
    for k in range(N_KV_HEADS):
        acc = None
        for g in range(GROUP):
            part = imp_h[:, (GROUP * k + g) * tq:(GROUP * k + g + 1) * tq]
            acc = part if acc is None else acc + part
        imp.append(acc)
    imp = jnp.concatenate(imp, axis=1)
    blk = lax.broadcasted_iota(jnp.int32, (nb, 1), 0)
    cur = rep(qpos // SLC_BLOCK, N_KV_HEADS)
    forced = (blk == 0) | (blk == cur) | (blk == cur - 1)
    score = jnp.where(forced, FORCE_SCORE, jnp.where(blk <= cur, imp, -1.0))
    if nb > n_sel_blocks:
        score = jnp.where(blk < n_sel_blocks, score, -jnp.inf)
    sel = _top_blocks_cols(score, blk.astype(F32))
    sel_m1 = jnp.concatenate([sel - 1.0, jnp.zeros((LANES - nb, N_KV_HEADS * tq), F32)], axis=0).astype(BF16)

    start = pl.multiple_of(jnp.maximum(t0 + tq - span, 0), tq)
    dist = qpos - (start + lax.broadcasted_iota(jnp.int32, (span, 1), 0))
    w_bias = jnp.where(dist >= 0, jnp.where(dist <= WINDOW, 0.0, NEG), NEG)
    s = _dot(kw_ref[pl.ds(start, span), :], qt) + rep(w_bias, N_HEADS)
    e = jnp.exp2(s - jnp.max(s, axis=0, keepdims=True))
    inv_w = 1.0 / jnp.sum(e, axis=0, keepdims=True)
    c0 = start // LANES
    vt = jnp.concatenate([vtw_ref[c0 + c] for c in range(span // LANES)], axis=1)
    o_w = _dot(vt, e.astype(BF16)) * inv_w

    def tile(kt, carry, diagonal, nk=tk):
        m_i, l_i, acc = carry
        r0 = pl.multiple_of(kt * tk, tk)
        bias = _dot(expand_ref[pl.ds(r0, nk), :], sel_m1)
        if diagonal:
            kpos = r0 + lax.broadcasted_iota(jnp.int32, (nk, 1), 0)
            bias = jnp.where(kpos <= rep(qpos, N_KV_HEADS), bias, NEG)
        s = _dot(ks_ref[pl.ds(r0, nk), :], qt)
        s = jnp.concatenate([s[:, h * tq:(h + 1) * tq] + bias[:, (h // GROUP) * tq:(h // GROUP + 1) * tq]
                             for h in range(N_HEADS)], axis=1)
        m_new = jnp.maximum(m_i, jnp.max(s, axis=0, keepdims=True))
        a = jnp.exp2(m_i - m_new)
        e = jnp.exp2(s - m_new)
        l_new = a * l_i + jnp.sum(e, axis=0, keepdims=True)
        c0 = kt * (tk // LANES)
        vt = jnp.concatenate([vts_ref[c0 + c] for c in range(nk // LANES)], axis=1)
        return m_new, l_new, a * acc + _dot(vt, e.astype(BF16))

    n_kt = (t0 + tq + tk - 1) // tk
    init = (jnp.full((1, N_HEADS * tq), NEG, F32), jnp.zeros((1, N_HEADS * tq), F32),
            jnp.zeros((KVP, N_HEADS * tq), F32))
    carry = lax.fori_loop(0, n_kt - 1, lambda kt, c: tile(kt, c, False), init)
    diag = [functools.partial(tile, n_kt - 1, diagonal=True, nk=(v + 1) * tq) for v in range(tk // tq)]
    _, l_s, acc_s = lax.switch(lax.rem(pl.program_id(1), tk // tq), diag, carry)
    o_s = acc_s * (1.0 / l_s)

    gt = gate_ref[...].T
    parts = []
    for h in range(N_HEADS):
        rows = slice((h // GROUP) * HEAD_DIM, (h // GROUP + 1) * HEAD_DIM)
        cols = slice(h * tq, (h + 1) * tq)
        parts.append(gt[3 * h:3 * h + 1] * o_c[rows, cols] + gt[3 * h + 1:3 * h + 2] * o_s[rows, cols]
                     + gt[3 * h + 2:3 * h + 3] * o_w[rows, cols])
    ot = jnp.concatenate(parts, axis=0)
    for m in range(NSA_W // LANES):
        o_ref[:, m * LANES:(m + 1) * LANES] = ot[m * LANES:(m + 1) * LANES].T.astype(o_ref.dtype)


def _sel_map_t(nc_rows, n_cmp, ns):
    c0 = CMP_STRIDE * np.arange(nc_rows)[None, :]
    s0 = SLC_BLOCK * np.arange(LANES)[:, None]
    ov = np.clip(np.minimum(c0 + CMP_LEN, s0 + SLC_BLOCK) - np.maximum(c0, s0), 0, None) / CMP_LEN
    ov = ov * (np.arange(nc_rows)[None, :] < n_cmp) * (np.arange(LANES)[:, None] < ns)
    return jnp.asarray(ov, dtype=BF16)


def _expand_map(seq):
    e = (np.arange(LANES)[None, :] == (np.arange(seq) // SLC_BLOCK)[:, None]).astype(np.float32) * -NEG
    return jnp.asarray(e, dtype=BF16)


def _nsa_prompt(qt, kc, vct, ks, vts, kw, vtw, gates, *, batch, seq, tq=128, tk=2048):
    tk = min(tk, seq)
    nq = seq // tq
    nsub = seq // CMP_STRIDE
    ns = seq // SLC_BLOCK
    assert ns <= LANES and seq % tk == 0 and tk % tq == 0 and tq == LANES
    kern = functools.partial(_nsa_prompt_kernel, tq=tq, tk=tk, seq=seq, n_sel_blocks=ns)
    per_batch = lambda shp: pl.BlockSpec(shp, lambda b, i: (b,) + (0,) * (len(shp) - 1))
    const = lambda shp: pl.BlockSpec(shp, lambda b, i: (0,) * len(shp))
    return pl.pallas_call(
        kern,
        grid=(batch, nq),
        in_specs=[pl.BlockSpec((N_HEADS, KVP, tq), lambda b, i: (0, 0, b * nq + i)),
                  per_batch((None, nsub, KVP)), per_batch((None, KVP, nsub)),
                  per_batch((seq, KVP)), per_batch((seq // LANES, KVP, LANES)),
                  per_batch((seq, KVP)), per_batch((seq // LANES, KVP, LANES)),
                  pl.BlockSpec((tq, LANES), lambda b, i: (b * nq + i, 0)),
                  const((LANES, nsub)), const((seq, LANES))],
        out_specs=pl.BlockSpec((tq, NSA_W), lambda b, i: (b * nq + i, 0)),
        out_shape=jax.ShapeDtypeStruct((batch * seq, NSA_W), BF16),
        compiler_params=pltpu.CompilerParams(
            dimension_semantics=("parallel", "arbitrary"), vmem_limit_bytes=VMEM_LIMIT),
        name="nsa_prompt",
    )(qt, kc, vct, ks, vts, kw, vtw, gates, _sel_map_t(nsub, nsub - 1, ns), _expand_map(seq))


def _pool_windows(z_ext, tm):
    s2 = z_ext[1:] + z_ext[:-1]
    s4 = s2[2:] + s2[:-2]
    s8 = s4[4:] + s4[:-4]
    s16 = s8[8:] + s8[:-8]
    return (s2[HALO - 1:HALO - 1 + tm], s4[HALO - 3:HALO - 3 + tm], s8[HALO - 7:HALO - 7 + tm],
            s16[HALO - 15:HALO - 15 + tm])


def _mix_out_kernel(*refs, tm, tiles_per_seq, pool_in_kernel):
    if pool_in_kernel:
        (x_ref, nsa_ref, gu_ref, gv_ref, p_ref, halo_ref, ws_ref, gb_ref, pw_ref, ps_ref, wo_ref,
         g_ref, b_ref, o_ref) = refs
    else:
        (x_ref, nsa_ref, gu_ref, gv_ref, d_ref, ws_ref, gb_ref, pw_ref, ps_ref, wo_ref,
         g_ref, b_ref, o_ref) = refs
    lane = lax.broadcasted_iota(jnp.int32, (1, GM_W), 1)

    parts = []
    for c in range(tm // GM_CHUNK):
        v = gv_ref[c * GM_CHUNK:(c + 1) * GM_CHUNK, :]
        stacked = jnp.concatenate(
            [jnp.where(lane // HEAD_DIM == h, v, 0.0) for h in range(GM_HEADS)], axis=0).astype(BF16)
        s = _dot(ws_ref[...], stacked) + gb_ref[...]
        parts.append(gu_ref[c * GM_CHUNK:(c + 1) * GM_CHUNK, :] * s)
    o_gm = parts[0] if len(parts) == 1 else jnp.concatenate(parts, axis=0)

    if pool_in_kernel:
        first_tile = (pl.program_id(0) % tiles_per_seq) == 0
        halo = jnp.where(first_tile, 0.0, halo_ref[...])
        z = p_ref[...]
        wins = _pool_windows(jnp.concatenate([halo, z], axis=0), tm)
        pos = (pl.program_id(0) % tiles_per_seq) * tm + lax.broadcasted_iota(jnp.int32, (tm, 1), 0)
        grp = lane // POOL_GW
        wsum = jnp.where(grp == 0, wins[0], jnp.where(grp == 1, wins[1], jnp.where(grp == 2, wins[2], wins[3])))
        width = jnp.where(grp == 0, POOL_WINDOWS[0], jnp.where(grp == 1, POOL_WINDOWS[1],
                          jnp.where(grp == 2, POOL_WINDOWS[2], POOL_WINDOWS[3])))
        cnt = jnp.minimum(width, pos + 1).astype(F32)
        d = wsum / cnt - z
    else:
        d = d_ref[...]
    o_pool = _dot(d.astype(BF16), pw_ref[...]) * ps_ref[...]

    mixed = jnp.concatenate([nsa_ref[...], o_gm.astype(BF16), o_pool.astype(BF16)], axis=1)
    y = ALPHA * x_ref[...] + _dot(mixed, wo_ref[...])
    o_ref[...] = _ln_rows(y, g_ref[...], b_ref[...])


def _mix_out(x, o_nsa, gu, gv, p_or_d, ws_cat, gb_full, pw_big, ps, w_o_b, g, b, *, tm, seq, pool_in_kernel):
    rows = x.shape[0]
    tiles_per_seq = max(seq // tm, 1)
    row_spec = lambda w: pl.BlockSpec((tm, w), lambda i: (i, 0))
    const = lambda shp: pl.BlockSpec(shp, lambda i: (0,) * len(shp))
    in_specs = [row_spec(D_MODEL), row_spec(NSA_W), row_spec(GM_W), row_spec(GM_W), row_spec(POOL_W)]
    args = [x, o_nsa, gu, gv, p_or_d]
    if pool_in_kernel:
        in_specs.append(pl.BlockSpec((HALO, POOL_W), lambda i: (jnp.maximum(i * (tm // HALO) - 1, 0), 0)))
        args.append(p_or_d)
    in_specs += [const((GM_CHUNK, GM_HEADS * GM_CHUNK)), const((GM_CHUNK, GM_W)), const((POOL_W, POOL_W)),
                 const((1, POOL_W)), const((D_MODEL, D_MODEL)), const((1, D_MODEL)), const((1, D_MODEL))]
    args += [ws_cat, gb_full, pw_big, ps.reshape(1, POOL_W), w_o_b, g.reshape(1, D_MODEL), b.reshape(1, D_MODEL)]
    kern = functools.partial(_mix_out_kernel, tm=tm, tiles_per_seq=tiles_per_seq, pool_in_kernel=pool_in_kernel)
    return pl.pallas_call(
        kern,
        grid=(rows // tm,),
        in_specs=in_specs,
        out_specs=row_spec(D_MODEL),
        out_shape=jax.ShapeDtypeStruct((rows, D_MODEL), F32),
        compiler_params=pltpu.CompilerParams(dimension_semantics=("parallel",), vmem_limit_bytes=VMEM_LIMIT),
        name="mix_out_prompt" if pool_in_kernel else "mix_out_sample",
    )(*args)


def _gmlp_weights(ws, gb, chunk_rows, reps):
    wm = jnp.tril(ws[:, :chunk_rows, :chunk_rows])
    bias = gb[:, :chunk_rows]
    if reps > 1:
        eye = jnp.eye(reps, dtype=F32)
        wm = jnp.einsum('hts,ab->hatbs', wm, eye).reshape(GM_HEADS, reps * chunk_rows, reps * chunk_rows)
        bias = jnp.tile(bias, (1, reps))
    ws_cat = wm.transpose(1, 0, 2).reshape(GM_CHUNK, GM_HEADS * GM_CHUNK).astype(BF16)
    gb_full = jnp.repeat(bias.T, HEAD_DIM, axis=1)
    return ws_cat, gb_full


def _pool_weights(pw):
    eye = jnp.eye(POOL_GROUPS, dtype=F32)
    return jnp.einsum('gce,gq->gcqe', pw, eye).reshape(POOL_W, POOL_W).astype(BF16)


PAGES_PER_STEP = 32
SUBS_PER_PAGE = PAGE_SIZE // CMP_STRIDE


def _cmp_sample_kernel(pt_ref, cache_ref, q_ref, pe_ref, w1_ref, w2_ref, selmap_ref, perm_ref, oc_ref, idx_ref,
                       pbuf, sem, xr_ref, fs_ref, *, layer, n_chunks, n_seq, dec_seq, past_len, n_sel_blocks):
    b = pl.program_id(0)
    c = pl.program_id(1)
    step = b * n_chunks + c
    slot = lax.rem(step, 2)
    pps = PAGES_PER_STEP

    def page_copies(sb, sc, sl):
        return [pltpu.make_async_copy(cache_ref.at[layer, pt_ref[sb, sc * pps + p]], pbuf.at[sl, p], sem.at[sl])
                for p in range(pps)]

    @pl.when(step == 0)
    def _():
        for cp in page_copies(b, c, slot):
            cp.start()

    @pl.when(step + 1 < n_seq * n_chunks)
    def _():
        wrap = c + 1 == n_chunks
        for cp in page_copies(jnp.where(wrap, b + 1, b), jnp.where(wrap, 0, c + 1), 1 - slot):
            cp.start()

    for cp in page_copies(b, c, slot):
        cp.wait()

    def to_rows(p, carry):
        r0 = pl.multiple_of(p * SUBS_PER_PAGE, SUBS_PER_PAGE)
        for kv in range(2):
            rows = _dot_t(perm_ref[...], pbuf[slot, p, kv].astype(BF16))
            for j in range(CMP_STRIDE):
                xr_ref[kv, pl.ds(r0, SUBS_PER_PAGE), j * KVP:(j + 1) * KVP] = rows[j * SUBS_PER_PAGE:(j + 1) * SUBS_PER_PAGE]
        return carry

    lax.fori_loop(0, pps, to_rows, 0, unroll=True)

    subs = pps * SUBS_PER_PAGE
    s0 = pl.multiple_of(c * subs, subs)
    for kv in range(2):
        fs_ref[kv, pl.ds(s0, subs), :] = _dot(xr_ref[kv].astype(BF16), w1_ref[kv])

    @pl.when(c == n_chunks - 1)
    def _():
        nsub = n_chunks * subs
        kcv = []
        for kv in range(2):
            f = fs_ref[kv]
            per = _dot(pe_ref[kv].astype(BF16), w1_ref[kv])
            pe_term = per[0:1, 0:KVP] + per[1:2, KVP:2 * KVP]
            nxt = jnp.concatenate([f[1:, KVP:2 * KVP], jnp.zeros((1, KVP), F32)], axis=0)
            hid = jax.nn.gelu(f[:, 0:KVP] + nxt + pe_term)
            kcv.append(_dot(hid.astype(BF16), w2_ref[kv]).astype(BF16))
        n_kt = N_KV_HEADS * dec_seq
        rows = GROUP * n_kt
        q = q_ref[...]
        qpos = past_len + lax.rem(lax.broadcasted_iota(jnp.int32, (rows, 1), 0), dec_seq)
        cmp_end = CMP_STRIDE * lax.broadcasted_iota(jnp.int32, (1, nsub), 1) + (CMP_LEN - 1)
        p = _softmax_rows(_dot_t(q, kcv[0]), cmp_end <= qpos).astype(BF16)
        oc_ref[...] = _dot(p, kcv[1])
        imp_g = _dot(p, selmap_ref[...])
        imp = imp_g[0:n_kt]
        for g in range(1, GROUP):
            imp = imp + imp_g[g * n_kt:(g + 1) * n_kt]
        blk = lax.broadcasted_iota(jnp.int32, (1, imp.shape[1]), 1)
        cur = qpos[0:n_kt] // SLC_BLOCK
        forced = (blk == 0) | (blk == cur) | (blk == cur - 1)
        score = jnp.where(forced, FORCE_SCORE, jnp.where(blk <= cur, imp, -1.0))
        score = jnp.where(blk < n_sel_blocks, score, -jnp.inf)
        idx_ref[...] = _top_blocks_idx(score, blk)[1]


def _cmp_sample(page_table, cache_t, q_gkt, pe2, w1kv, w2kv, *, layer, dec_seq, past_len):
    n_seq, n_pages = page_table.shape
    n_chunks = n_pages // PAGES_PER_STEP
    nsub = n_pages * SUBS_PER_PAGE
    ns = (past_len + dec_seq + SLC_BLOCK - 1) // SLC_BLOCK
    ns_pad = LANES * ((ns + LANES - 1) // LANES)
    n_kt = N_KV_HEADS * dec_seq
    rows = GROUP * n_kt
    c0 = CMP_STRIDE * np.arange(nsub)[:, None]
    s0 = SLC_BLOCK * np.arange(ns_pad)[None, :]
    ov = np.clip(np.minimum(c0 + CMP_LEN, s0 + SLC_BLOCK) - np.maximum(c0, s0), 0, None) / CMP_LEN
    ov = ov * (np.arange(nsub)[:, None] < nsub - 1) * (np.arange(ns_pad)[None, :] < ns)
    selmap = jnp.asarray(ov, dtype=BF16)
    pos = np.arange(PAGE_SIZE)
    perm = jnp.asarray((pos[None, :] == (CMP_STRIDE * (pos % SUBS_PER_PAGE) + pos // SUBS_PER_PAGE)[:, None])
                       .astype(np.float32), dtype=BF16)
    width = CMP_STRIDE * KVP
    kern = functools.partial(_cmp_sample_kernel, layer=layer, n_chunks=n_chunks, n_seq=n_seq, dec_seq=dec_seq,
                             past_len=past_len, n_sel_blocks=ns)
    const = lambda shp: pl.BlockSpec(shp, lambda b, c, pt: (0,) * len(shp))
    grid_spec = pltpu.PrefetchScalarGridSpec(
        num_scalar_prefetch=1,
        grid=(n_seq, n_chunks),
        in_specs=[pl.BlockSpec(memory_space=pl.ANY),
                  pl.BlockSpec((None, rows, LANES), lambda b, c, pt: (b, 0, 0)),
                  const((2, 8, width)), const((2, width, 2 * KVP)), const((2, KVP, KVP)), const((nsub, ns_pad)),
                  const((PAGE_SIZE, PAGE_SIZE))],
        out_specs=[pl.BlockSpec((None, rows, LANES), lambda b, c, pt: (b, 0, 0)),
                   pl.BlockSpec((None, n_kt, LANES), lambda b, c, pt: (b, 0, 0))],
        scratch_shapes=[pltpu.VMEM((2, PAGES_PER_STEP, 2, KVP, PAGE_SIZE), F32),
                        pltpu.SemaphoreType.DMA((2,)),
                        pltpu.VMEM((2, PAGES_PER_STEP * SUBS_PER_PAGE, CMP_STRIDE * KVP), F32),
                        pltpu.VMEM((2, nsub, 2 * KVP), F32)])
    return pl.pallas_call(
        kern,
        grid_spec=grid_spec,
        out_shape=[jax.ShapeDtypeStruct((n_seq, rows, LANES), F32),
                   jax.ShapeDtypeStruct((n_seq, n_kt, LANES), jnp.int32)],
        compiler_params=pltpu.CompilerParams(
            dimension_semantics=("arbitrary", "arbitrary"), vmem_limit_bytes=VMEM_LIMIT),
        name="cmp_sample",
    )(page_table, cache_t, q_gkt, pe2, w1kv, w2kv, selmap, perm)


def _compress_weights_kv(pe, w1, w2):
    eye = jnp.eye(N_KV_HEADS, dtype=F32)
    w1r = w1.reshape(2, 2, CMP_STRIDE, HEAD_DIM, HEAD_DIM)
    w1kv = jnp.einsum('ksjde,hg->kjhdsge', w1r, eye).reshape(2, CMP_STRIDE * KVP, 2 * KVP)
    w2kv = jnp.einsum('ked,hg->khegd', w2, eye).reshape(2, KVP, KVP)
    per = pe.reshape(2, 2, CMP_STRIDE, HEAD_DIM)
    per = jnp.broadcast_to(per[:, :, :, None, :], (2, 2, CMP_STRIDE, N_KV_HEADS, HEAD_DIM))
    pe2 = jnp.pad(per.reshape(2, 2, CMP_STRIDE * KVP), ((0, 0), (0, 6), (0, 0)))
    return pe2, w1kv.astype(BF16), w2kv.astype(BF16)


def _slc_sample_kernel(*refs, layer, n_seq, n_pages, dec_seq, past_len, n_alias):
    (pt_ref, idx_sm_ref, cache_ref, q_ref, idxv_ref, knew_ref, win_ref, wnew_ref, oc_ref, gate_ref, expand_ref,
     wnewt_ref) = refs[:12]
    o_ref, wout_ref, kvbuf, sem = refs[12 + n_alias:]
    b = pl.program_id(0)
    slot = lax.rem(b, 2)
    n_kt = N_KV_HEADS * dec_seq
    rows = GROUP * n_kt
    n_past_blocks = past_len // SLC_BLOCK
    per_head = dec_seq * SLC_TOPK

    def tile_copies(sb, sl, k, i):
        kt = k * dec_seq + i // SLC_TOPK
        s = lax.rem(i, SLC_TOPK)
        j = idx_sm_ref[(sb * n_kt + kt) * SLC_TOPK + s]
        phys = pt_ref[sb, jnp.minimum(lax.shift_right_logical(j, 1), n_pages - 1)]
        return [pltpu.make_async_copy(cache_ref.at[layer, phys, :, pl.ds(k * HEAD_DIM, HEAD_DIM), :],
                                      kvbuf.at[sl, kt, s], sem.at[sl])]

    def start_all(sb, sl):
        for k in range(N_KV_HEADS):
            def body(i, carry):
                for cp in tile_copies(sb, sl, k, i):
                    cp.start()
                return carry
            lax.fori_loop(0, per_head, body, 0, unroll=4)

    @pl.when(b == 0)
    def _():
        start_all(b, slot)

    @pl.when(b + 1 < n_seq)
    def _():
        start_all(b + 1, 1 - slot)

    for k in range(N_KV_HEADS):
        def wait_body(i, carry):
            for cp in tile_copies(b, slot, k, i):
                cp.wait()
            return carry
        lax.fori_loop(0, per_head, wait_body, 0, unroll=4)

    q = q_ref[...]
    qb = q.astype(BF16)
    gates = gate_ref[...]
    t_row = lax.rem(lax.broadcasted_iota(jnp.int32, (rows, 1), 0) // GROUP, dec_seq)
    t_new = lax.broadcasted_iota(jnp.int32, (1, dec_seq), 1)
    new_ok = t_new <= t_row

    wb = win_ref.shape[2]
    kpos = past_len - wb + lax.broadcasted_iota(jnp.int32, (1, wb), 1)
    dist = past_len + t_row - kpos
    w_ok = (dist >= 0) & (dist <= WINDOW) & (kpos >= 0)
    s_w = jnp.where(w_ok, _dot(qb, win_ref[0].astype(BF16)), NEG)
    s_n = jnp.where(new_ok, _dot_t(qb, wnew_ref[:, 0:KVP].astype(BF16)), NEG)
    m = jnp.maximum(jnp.max(s_w, axis=-1, keepdims=True), jnp.max(s_n, axis=-1, keepdims=True))
    e_w = jnp.where(w_ok, jnp.exp(s_w - m), 0.0)
    e_n = jnp.where(new_ok, jnp.exp(s_n - m), 0.0)
    den = jnp.sum(e_w, axis=-1, keepdims=True) + jnp.sum(e_n, axis=-1, keepdims=True)
    o_w = (_dot_t(e_w.astype(BF16), win_ref[1].astype(BF16))
           + _dot(e_n.astype(BF16), wnew_ref[:, KVP:2 * KVP].astype(BF16))) / den

    idxv = idxv_ref[...]
    lane16 = lax.broadcasted_iota(jnp.int32, (1, LANES), 1) < SLC_TOPK
    half = jnp.where(lane16 & ((idxv & 1) == 1), 1.0, 0.0).astype(BF16)
    live = jnp.where(lane16 & (idxv < n_past_blocks), 1.0, 0.0).astype(BF16)
    half_x = _dot(half, expand_ref[...])
    live_x = _dot(live, expand_ref[...])
    col = lax.broadcasted_iota(jnp.int32, (1, SLC_TOPK * PAGE_SIZE), 1)
    col_half = (lax.rem(col, PAGE_SIZE) // SLC_BLOCK).astype(F32)
    tile_ok = (live_x > 0.5) & (half_x == col_half)
    s_new = _dot_t(qb, knew_ref[:, 0:KVP].astype(BF16))
    zeros_half = jnp.zeros((GROUP, HEAD_DIM), F32)
    o_parts = []
    for kt in range(n_kt):
        k = kt // dec_seq
        r0 = kt * GROUP
        qk = q[r0:r0 + GROUP, k * HEAD_DIM:(k + 1) * HEAD_DIM].astype(BF16)
        kcat = jnp.concatenate([kvbuf[slot, kt, s, 0] for s in range(SLC_TOPK)], axis=1).astype(BF16)
        vcat = jnp.concatenate([kvbuf[slot, kt, s, 1] for s in range(SLC_TOPK)], axis=1).astype(BF16)
        ok = tile_ok[kt:kt + 1]
        nok = new_ok[r0:r0 + GROUP]
        s_s = jnp.where(ok, _dot(qk, kcat), NEG)
        s_n = jnp.where(nok, s_new[r0:r0 + GROUP], NEG)
        m = jnp.maximum(jnp.max(s_s, axis=-1, keepdims=True), jnp.max(s_n, axis=-1, keepdims=True))
        e_s = jnp.where(ok, jnp.exp(s_s - m), 0.0)
        e_n = jnp.where(nok, jnp.exp(s_n - m), 0.0)
        den = jnp.sum(e_s, axis=-1, keepdims=True) + jnp.sum(e_n, axis=-1, keepdims=True)
        v_new = knew_ref[:, KVP + k * HEAD_DIM:KVP + (k + 1) * HEAD_DIM].astype(BF16)
        o = (_dot_t(e_s.astype(BF16), vcat) + _dot(e_n.astype(BF16), v_new)) / den
        o_parts.append(jnp.concatenate([o, zeros_half] if k == 0 else [zeros_half, o], axis=1))
    o_s = jnp.concatenate(o_parts, axis=0)

    o_ref[...] = gates[:, 0:1] * oc_ref[...] + gates[:, 1:2] * o_s + gates[:, 2:3] * o_w

    lane_w = lax.broadcasted_iota(jnp.int32, (1, wb), 1)
    for kv in range(2):
        moved = pltpu.roll(win_ref[kv], wb - dec_seq, 1)
        for t in range(dec_seq):
            moved = jnp.where(lane_w == wb - dec_seq + t, wnewt_ref[kv * KVP:(kv + 1) * KVP, t:t + 1], moved)
        wout_ref[kv] = moved


def _slc_sample(page_table, idx_flat, cache_t, q_ktg, idxv, kvs_new, win_t, kvw_new, o_c, gates_r, win_out,
                *, layer, dec_seq, past_len):
    n_seq, n_pages = page_table.shape
    n_kt = N_KV_HEADS * dec_seq
    rows = GROUP * n_kt
    wb = win_t.shape[-1]
    cols = SLC_TOPK * PAGE_SIZE
    expand = jnp.asarray((np.arange(LANES)[:, None] == (np.arange(cols) // PAGE_SIZE)[None, :]).astype(np.float32),
                         dtype=BF16)
    n_alias = 0 if win_out is None else 1
    kern = functools.partial(_slc_sample_kernel, layer=layer, n_seq=n_seq, n_pages=n_pages, dec_seq=dec_seq,
                             past_len=past_len, n_alias=n_alias)
    per_seq = lambda r, w: pl.BlockSpec((None, r, w), lambda b, pt, ix: (b, 0, 0))
    state_spec = pl.BlockSpec((None, None, 2, KVP, wb), lambda b, pt, ix: (layer, b, 0, 0, 0))
    in_specs = [pl.BlockSpec(memory_space=pl.ANY),
                per_seq(rows, LANES), per_seq(n_kt, LANES), per_seq(dec_seq, KV_W), state_spec,
                per_seq(dec_seq, KV_W), per_seq(rows, LANES), per_seq(rows, LANES),
                pl.BlockSpec((LANES, cols), lambda b, pt, ix: (0, 0)), per_seq(KV_W, dec_seq)]
    args = [page_table, idx_flat, cache_t, q_ktg, idxv, kvs_new, win_t, kvw_new, o_c, gates_r, expand,
            jnp.swapaxes(kvw_new, 1, 2)]
    if n_alias:
        in_specs.append(pl.BlockSpec(memory_space=pl.ANY))
        args.append(win_out)
    grid_spec = pltpu.PrefetchScalarGridSpec(
        num_scalar_prefetch=2,
        grid=(n_seq,),
        in_specs=in_specs,
        out_specs=[per_seq(rows, LANES), state_spec],
        scratch_shapes=[pltpu.VMEM((2, n_kt, SLC_TOPK, 2, HEAD_DIM, PAGE_SIZE), F32),
                        pltpu.SemaphoreType.DMA((2,))])
    return pl.pallas_call(
        kern,
        grid_spec=grid_spec,
        out_shape=[jax.ShapeDtypeStruct((n_seq, rows, LANES), F32), jax.ShapeDtypeStruct(win_t.shape, F32)],
        input_output_aliases={len(args) - 1: 1} if n_alias else {},
        compiler_params=pltpu.CompilerParams(dimension_semantics=("arbitrary",), vmem_limit_bytes=VMEM_LIMIT),
        name="slc_sample",
    )(*args)


def _pool_sample_kernel(z_ref, d_ref, *, dec_seq, pos0):
    lane = lax.broadcasted_iota(jnp.int32, (1, POOL_W), 1)
    grp = lane // POOL_GW
    for t in range(dec_seq):
        cur = z_ref[:, POOL_HIST + t, :]
        acc = cur
        sums = {}
        for back in range(1, max(POOL_WINDOWS)):
            acc = acc + z_ref[:, POOL_HIST + t - back, :]
            if back + 1 in POOL_WINDOWS:
                sums[back + 1] = acc
        d = None
        for g, w in enumerate(POOL_WINDOWS):
            val = sums[w] / float(min(w, pos0 + t + 1)) - cur
            d = val if d is None else jnp.where(grp == g, val, d)
        d_ref[:, t, :] = d


def _pool_sample(z_ext, *, dec_seq, pos0):
    n_seq = z_ext.shape[0]
    return pl.pallas_call(
        functools.partial(_pool_sample_kernel, dec_seq=dec_seq, pos0=pos0),
        out_shape=jax.ShapeDtypeStruct((n_seq, dec_seq, POOL_W), F32),
        name="pool_sample",
    )(z_ext)


def _sample_layer(x, lw, cmp_t, slc_t, win_t, win_out, pool_state, page_table, *, layer, n_seq, dec_seq):
    rows = n_seq * dec_seq
    n_kt = N_KV_HEADS * dec_seq
    x = _ffn_ln(x, lw['ffn_in'], lw['ffn_out'], lw['ln_g'][0], lw['ln_b'][0], sel=(lw['layer'], 0),tm=rows)
    cos, sin = _rope_tables(PAST_LEN + jnp.arange(rows) % dec_seq)
    qt, kvc, kvs, kvw, gates, gu, gv, p = _inproj(
        x, lw['w_ext'], cos, sin, lw['gm_ln_g'], lw['gm_ln_b'], tm=rows)

    qf = jnp.swapaxes(qt.astype(F32), 1, 2).reshape(N_KV_HEADS, GROUP, n_seq, dec_seq, LANES)
    q_gkt = qf.transpose(2, 1, 0, 3, 4).reshape(n_seq, GROUP * n_kt, LANES).astype(BF16)
    q_ktg = qf.transpose(2, 0, 3, 1, 4).reshape(n_seq, GROUP * n_kt, LANES)
    o_c, idxv = _cmp_sample(page_table, cmp_t, q_gkt, lw['pe2'], lw['w1kv'], lw['w2kv'],
                            layer=layer, dec_seq=dec_seq, past_len=PAST_LEN)
    o_c = o_c.reshape(n_seq, GROUP, N_KV_HEADS, dec_seq, LANES).transpose(0, 2, 3, 1, 4).reshape(n_seq, GROUP * n_kt, LANES)
    gates_r = gates[:, :GATE_W].reshape(n_seq, dec_seq, N_KV_HEADS, GROUP, 3).transpose(0, 2, 1, 3, 4)
    gates_r = jnp.pad(gates_r.reshape(n_seq, GROUP * n_kt, 3), ((0, 0), (0, 0), (0, LANES - 3)))
    idx_flat = idxv[:, :, :SLC_TOPK].reshape(-1)
    o, win_out = _slc_sample(page_table, idx_flat, slc_t, q_ktg, idxv, kvs.reshape(n_seq, dec_seq, KV_W), win_t,
                             kvw.reshape(n_seq, dec_seq, KV_W), o_c, gates_r, win_out,
                             layer=layer, dec_seq=dec_seq, past_len=PAST_LEN)
    o = o.reshape(n_seq, N_KV_HEADS, dec_seq, GROUP, LANES)
    o_nsa = jnp.stack([o[:, k, :, :, k * HEAD_DIM:(k + 1) * HEAD_DIM] for k in range(N_KV_HEADS)], axis=2)
    o_nsa = o_nsa.reshape(rows, NSA_W).astype(BF16)

    z_ext = jnp.concatenate([pool_state, p.reshape(n_seq, dec_seq, POOL_W)], axis=1)
    d = _pool_sample(z_ext, dec_seq=dec_seq, pos0=PAST_LEN).reshape(rows, POOL_W)
    ws_cat, gb_full = _gmlp_weights(lw['gm_ws'], lw['gm_b'], dec_seq, GM_CHUNK // dec_seq)
    x = _mix_out(x, o_nsa, gu, gv, d, ws_cat, gb_full, lw['pw_big'], lw['pool_scale'], lw['w_o'],
                 lw['ln_g'][1], lw['ln_b'][1], tm=rows, seq=dec_seq, pool_in_kernel=False)
    x = _ffn_ln(x, lw['ffn_in'], lw['ffn_out'], lw['ln_g'][2], lw['ln_b'][2], sel=(lw['layer'], 1),tm=rows)
    shp = (n_seq, dec_seq, 2, N_KV_HEADS, HEAD_DIM)
    new = (kvc.reshape(shp), kvs.reshape(shp), z_ext[:, dec_seq:], gv.reshape(n_seq, dec_seq, GM_W))
    return x, new, win_out


def _pages_by_channel(cache):
    nd = cache.ndim
    t = jnp.transpose(cache, tuple(range(nd - 4)) + (nd - 3, nd - 2, nd - 1, nd - 4))
    return t.reshape(t.shape[:-3] + (KVP, t.shape[-1]))
def _prompt_layer(x, lw, leaf_bufs, *, layer, batch, seq, tm, tm_ffn):
    x = _ffn_ln(x, lw['ffn_in'], lw['ffn_out'], lw['ln_g'][0], lw['ln_b'][0], sel=(lw['layer'], 0),tm=tm_ffn)
    cos, sin = _rope_tables(jnp.arange(seq))
    qt, kvc, leaf_c, leaf_s, leaf_w, ks, vts, kw, vtw, gates, gu, gv, p = _inproj(
        x, lw['w_ext'], cos, sin, lw['gm_ln_g'], lw['gm_ln_b'], tm=tm, leaf_bufs=leaf_bufs, layer=layer, batch=batch)
    kc, vct = _compress_prompt(kvc, lw['pe2'], lw['w1kv'], lw['w2kv'], batch=batch)
    o_nsa = _nsa_prompt(qt, kc, vct, ks, vts, kw, vtw, gates, batch=batch, seq=seq)
    ws_cat, gb_full = _gmlp_weights(lw['gm_ws'], lw['gm_b'], GM_CHUNK, 1)
    x = _mix_out(x, o_nsa, gu, gv, p, ws_cat, gb_full, lw['pw_big'], lw['pool_scale'], lw['w_o'],
                 lw['ln_g'][1], lw['ln_b'][1], tm=tm, seq=seq, pool_in_kernel=True)
    x = _ffn_ln(x, lw['ffn_in'], lw['ffn_out'], lw['ln_g'][2], lw['ln_b'][2], sel=(lw['layer'], 1),tm=tm_ffn)
    return x, (leaf_c, leaf_s, leaf_w), p.reshape(batch, seq, POOL_W)[:, seq - POOL_HIST:]


def _leaf_rows(buf):
    d, b, _, t = buf.shape
    return buf.reshape(d, b, 2, N_KV_HEADS, HEAD_DIM, t).transpose(0, 1, 5, 2, 3, 4)


def _layer_weights(l, ffn_in_b, ffn_out_b, ln_g, ln_b, w_in, w_o, cmp_pe, cmp_w1, cmp_w2,
                   gm_ln_g, gm_ln_b, gm_ws, gm_b, pool_w, pool_scale):
    pe2, w1kv, w2kv = _compress_weights_kv(cmp_pe[l], cmp_w1[l], cmp_w2[l])
    return dict(layer=l, ffn_in=ffn_in_b, ffn_out=ffn_out_b, ln_g=ln_g[l], ln_b=ln_b[l],
                w_ext=_build_w_ext(w_in[l]), w_o=w_o[l].astype(BF16),
                pe2=pe2, w1kv=w1kv, w2kv=w2kv,
                gm_ln_g=gm_ln_g[l], gm_ln_b=gm_ln_b[l], gm_ws=gm_ws[l], gm_b=gm_b[l],
                pw_big=_pool_weights(pool_w[l]), pool_scale=pool_scale[l])


def kernel(x_prompt, x_sample, cache_kv_cmp, cache_kv_slc, state_kv_win, state_pool, page_table, ln_g, ln_b, ffn_w_in, ffn_w_out, w_in, w_o, cmp_pe, cmp_w1, cmp_w2, gm_ln_g, gm_ln_b, gm_ws, gm_b, pool_w, pool_scale):
    batch, seq, _ = x_prompt.shape
    fi = ffn_w_in.astype(BF16)
    fo = ffn_w_out.astype(BF16)
    n_seq, dec_seq, _ = x_sample.shape
    xp = x_prompt.reshape(batch * seq, D_MODEL)
    xs = x_sample.reshape(n_seq * dec_seq, D_MODEL)
    cmp_t = _pages_by_channel(cache_kv_cmp)
    slc_t = _pages_by_channel(cache_kv_slc)
    win_t = _pages_by_channel(state_kv_win)
    leaf_bufs = tuple(jnp.zeros((DEPTH, batch, KV_W, seq), F32) for _ in range(3))
    win_out = jnp.zeros(win_t.shape, F32)
    pool_p, new_s = [], []
    for l in range(DEPTH):
        lw = _layer_weights(l, fi, fo, ln_g, ln_b, w_in, w_o, cmp_pe, cmp_w1, cmp_w2, gm_ln_g, gm_ln_b, gm_ws, gm_b, pool_w, pool_scale)
        xp, leaf_bufs, pool_l = _prompt_layer(xp, lw, leaf_bufs, layer=l, batch=batch, seq=seq, tm=512, tm_ffn=1024)
        xs, st_s, win_out = _sample_layer(xs, lw, cmp_t, slc_t, win_t, win_out, state_pool[l], page_table,
                                          layer=l, n_seq=n_seq, dec_seq=dec_seq)
        pool_p.append(pool_l)
        new_s.append(st_s)
    stk = lambda lst, i: jnp.stack([t[i] for t in lst])
    wb = min(WINDOW, seq)
    win_s = _leaf_rows(win_out.reshape(DEPTH, n_seq, KV_W, win_out.shape[-1]))
    return (xp.reshape(batch, seq, D_MODEL), xs.reshape(n_seq, dec_seq, D_MODEL),
            _leaf_rows(leaf_bufs[0]), stk(new_s, 0), _leaf_rows(leaf_bufs[1]), stk(new_s, 1),
            _leaf_rows(leaf_bufs[2])[:, :, seq - wb:], win_s, jnp.stack(pool_p), stk(new_s, 2), stk(new_s, 3))
```
